```python
import jax, jax.numpy as jnp
from jax import lax
import numpy as np

D_MODEL = 1024
BATCH = 16
SEQ = 256
DEPTH = 1
DEC_BATCH = 8
DEC_SEQ = 1024
PAST_LEN = 256

GRID_W = 64
HEAD_DIM = 64
ATTN_DIM = D_MODEL // 2
N_HEADS = ATTN_DIM // HEAD_DIM
N_KV = 2
KV_GROUP = N_HEADS // N_KV
KV_DIM = N_KV * HEAD_DIM
CONV_DIM = D_MODEL - ATTN_DIM
CONV_WIDTH = 3
D_FF = -(-(8 * D_MODEL) // (3 * 256)) * 256
IN_DIM = ATTN_DIM + 2 * KV_DIM + 3 * CONV_DIM
ROT_PAIRS = HEAD_DIM // 4
ROPE_THETA = 10000.0
Q_BLOCK = 128
RMS_EPS = 1e-6

kernel_name = "hymba_diffusion_prefix_step"


def _rms(x, g):
    xf = x.astype(jnp.float32)
    y = xf * lax.rsqrt(jnp.mean(xf * xf, axis=-1, keepdims=True) + RMS_EPS)
    return (y * g.astype(jnp.float32)).astype(x.dtype)


def _axial_rope_tables(n_tokens, dtype):
    rows = n_tokens // GRID_W
    row = jnp.repeat(jnp.arange(rows, dtype=jnp.float32), GRID_W)
    col = jnp.tile(jnp.arange(GRID_W, dtype=jnp.float32), rows)
    inv = 1.0 / (ROPE_THETA ** (jnp.arange(ROT_PAIRS, dtype=jnp.float32) / ROT_PAIRS))
    ang = jnp.stack([row[:, None] * inv, col[:, None] * inv], axis=1)
    return jnp.cos(ang).astype(dtype), jnp.sin(ang).astype(dtype)


def _apply_rope(x, cos, sin):
    b, s, h, _ = x.shape
    xr = x.reshape(b, s, h, 2, 2, ROT_PAIRS)
    x1, x2 = xr[..., 0, :], xr[..., 1, :]
    c = cos[None, :, None]
    sn = sin[None, :, None]
    out = jnp.stack([x1 * c - x2 * sn, x1 * sn + x2 * c], axis=-2)
    return out.reshape(b, s, h, HEAD_DIM)


def _block_attention(q, k, v):
    b, s = q.shape[:2]
    nb = s // Q_BLOCK
    qb = q.reshape(b, nb, Q_BLOCK, N_KV, KV_GROUP, HEAD_DIM).transpose(1, 0, 2, 3, 4, 5)
    scale = HEAD_DIM ** -0.5

    def one_block(qblk):
        sc = jnp.einsum("bqkgd,btkd->bkgqt", qblk, k).astype(jnp.float32) * scale
        p = jax.nn.softmax(sc, axis=-1).astype(v.dtype)
        return jnp.einsum("bkgqt,btkd->bqkgd", p, v)

    out = lax.map(one_block, qb)
    return out.transpose(1, 0, 2, 3, 4, 5).reshape(b, s, ATTN_DIM)


def _short_conv(u, w):
    up = jnp.pad(u, ((0, 0), (1, 1), (0, 0)))
    return w[0] * up[:, :-2] + w[1] * up[:, 1:-1] + w[2] * up[:, 2:]


def _mixer(h, rope, ctx_k, ctx_v, w_in, q_norm, k_norm, conv_w, attn_out_norm, conv_out_norm, w_out):
    b, s, _ = h.shape
    proj = h @ w_in
    q, k, v, gb, gc, u = jnp.split(
        proj, np.cumsum([ATTN_DIM, KV_DIM, KV_DIM, CONV_DIM, CONV_DIM]).tolist(), axis=-1)
    q = _rms(q.reshape(b, s, N_HEADS, HEAD_DIM), q_norm)
    k = _rms(k.reshape(b, s, N_KV, HEAD_DIM), k_norm)
    v = v.reshape(b, s, N_KV, HEAD_DIM)
    if rope is None:
        attn = _block_attention(q, k, v)
    else:
        cos, sin = rope
        q_r = _apply_rope(q, cos, sin)
        k_r = _apply_rope(k, cos, sin)
        k_all = jnp.concatenate([ctx_k, k_r], axis=1)
        v_all = jnp.concatenate([ctx_v, v], axis=1)
        attn = _block_attention(q_r, k_all, v_all)
    y_conv = gb * _short_conv(gc * u, conv_w)
    merged = jnp.concatenate([_rms(attn, attn_out_norm), _rms(y_conv, conv_out_norm)], axis=-1)
    return merged @ w_out, k, v


def _layer(x, cond, rope, ctx_k, ctx_v, norm_mix, norm_ffn, w_ada, b_ada, w_in, q_norm, k_norm,
           conv_w, attn_out_norm, conv_out_norm, w_out, w_gate_up, w_down):
    mod = (jax.nn.silu(cond) @ w_ada + b_ada)[:, None, :]
    sh1, sc1, g1, sh2, sc2, g2 = jnp.split(mod, 6, axis=-1)
    h = _rms(x, norm_mix) * (1 + sc1) + sh1
    mix, k, v = _mixer(h, rope, ctx_k, ctx_v, w_in, q_norm, k_norm, conv_w,
                       attn_out_norm, conv_out_norm, w_out)
    x = x + g1 * mix
    h2 = _rms(x, norm_ffn) * (1 + sc2) + sh2
    gate, up = jnp.split(h2 @ w_gate_up, 2, axis=-1)
    x = x + g2 * ((jax.nn.silu(gate) * up) @ w_down)
    return x, k, v


def setup_inputs(seed: int = 0) -> dict:
    key = jax.random.key(seed)
    ks = jax.random.split(key, 20)
    f = jnp.float32
    nrm = lambda k, shape, s: jax.random.normal(k, shape, f) * s
    return {
        "x_prompt": nrm(ks[0], (BATCH, SEQ, D_MODEL), 1.0),
        "x_sample": nrm(ks[1], (DEC_BATCH, DEC_SEQ, D_MODEL), 1.0),
        "c": nrm(ks[2], (DEC_BATCH, D_MODEL), 1.0),
        "cache_k": nrm(ks[3], (DEC_BATCH, DEPTH, PAST_LEN, N_KV, HEAD_DIM), 1.0),
        "cache_v": nrm(ks[4], (DEC_BATCH, DEPTH, PAST_LEN, N_KV, HEAD_DIM), 1.0),
        "c_ctx": nrm(ks[5], (D_MODEL,), 1.0),
        "norm_mix": 1.0 + nrm(ks[6], (DEPTH, D_MODEL), 0.02),
        "norm_ffn": 1.0 + nrm(ks[7], (DEPTH, D_MODEL), 0.02),
        "w_ada": nrm(ks[8], (DEPTH, D_MODEL, 6 * D_MODEL), 0.5 * D_MODEL ** -0.5),
        "b_ada": nrm(ks[9], (DEPTH, 6 * D_MODEL), 0.02),
        "w_in": nrm(ks[10], (DEPTH, D_MODEL, IN_DIM), D_MODEL ** -0.5),
        "q_norm": 1.0 + nrm(ks[11], (DEPTH, HEAD_DIM), 0.02),
        "k_norm": 1.0 + nrm(ks[12], (DEPTH, HEAD_DIM), 0.02),
        "conv_w": nrm(ks[13], (DEPTH, CONV_WIDTH, CONV_DIM), CONV_WIDTH ** -0.5),
        "attn_out_norm": 1.0 + nrm(ks[14], (DEPTH, ATTN_DIM), 0.02),
        "conv_out_norm": 1.0 + nrm(ks[15], (DEPTH, CONV_DIM), 0.02),
        "w_out": nrm(ks[16], (DEPTH, D_MODEL, D_MODEL), D_MODEL ** -0.5),
        "w_gate_up": nrm(ks[17], (DEPTH, D_MODEL, 2 * D_FF), D_MODEL ** -0.5),
        "w_down": nrm(ks[18], (DEPTH, D_FF, D_MODEL), D_FF ** -0.5),
    }


def reference(x_prompt, x_sample, c, cache_k, cache_v, c_ctx, norm_mix, norm_ffn, w_ada, b_ada,
              w_in, q_norm, k_norm, conv_w, attn_out_norm, conv_out_norm, w_out, w_gate_up, w_down):
    rope = _axial_rope_tables(x_sample.shape[1], x_sample.dtype)
    cond_ctx = c_ctx[None, :]
    xp = x_prompt
    xs = x_sample
    new_k, new_v = [], []
    for l in range(DEPTH):
        params = (norm_mix[l], norm_ffn[l], w_ada[l], b_ada[l], w_in[l], q_norm[l], k_norm[l],
                  conv_w[l], attn_out_norm[l], conv_out_norm[l], w_out[l], w_gate_up[l], w_down[l])
        xp, k_l, v_l = _layer(xp, cond_ctx, None, None, None, *params)
        new_k.append(k_l)
        new_v.append(v_l)
        xs, _, _ = _layer(xs, c, rope, cache_k[:, l], cache_v[:, l], *params)
    ctx_k = jnp.stack(new_k, axis=1)
    ctx_v = jnp.stack(new_v, axis=1)
    return (xp, xs, ctx_k, ctx_v)
```

```python
import functools

import numpy as np
import jax
import jax.numpy as jnp
from jax import lax
from jax.experimental import pallas as pl
from jax.experimental.pallas import tpu as pltpu

D_MODEL = 1024
HEAD_DIM = 64
ATTN_DIM = 512
N_HEADS = 8
N_KV = 2
KV_GROUP = N_HEADS // N_KV
KV_DIM = N_KV * HEAD_DIM
CONV_DIM = 512
D_FF = 2816
QK_DIM = ATTN_DIM + KV_DIM
QKV_DIM = ATTN_DIM + 2 * KV_DIM
IN_DIM = QKV_DIM + 3 * CONV_DIM
GRID_W = 64
ROT_PAIRS = HEAD_DIM // 4
ROPE_THETA = 10000.0
RMS_EPS = 1e-6

LANES = 128
MXU_DIM = 256
BLOCK_ROWS = 1024
CHUNK_ROWS = 512
FFN_ROWS = 512
FF_CHUNKS = ((0, 1024), (1024, 1024), (2048, 768))
COND_ROWS = 16
CTX_ROW = 8
ADA_COLS = 1024
VMEM_LIMIT = 56 * 1024 * 1024

F32 = jnp.float32
BF16 = jnp.bfloat16


def _const_spec(shape):
    nd = len(shape)
    return pl.BlockSpec(shape, lambda *_: (0,) * nd, pipeline_mode=pl.Buffered(1))


def _ada_kernel(cond_ref, w_ref, b_ref, out_ref):
    c = cond_ref[...]
    s = (c * jax.nn.sigmoid(c)).astype(BF16)
    out_ref[...] = jnp.dot(s, w_ref[...].astype(BF16), preferred_element_type=F32) + b_ref[...]


def _ada_call(cond, w_ada, b_ada):
    n = w_ada.shape[1]
    return pl.pallas_call(
        _ada_kernel,
        grid=(n // ADA_COLS,),
        in_specs=[
            pl.BlockSpec((COND_ROWS, D_MODEL), lambda j: (0, 0)),
            pl.BlockSpec((D_MODEL, ADA_COLS), lambda j: (0, j)),
            pl.BlockSpec((1, ADA_COLS), lambda j: (0, j)),
        ],
        out_specs=pl.BlockSpec((COND_ROWS, ADA_COLS), lambda j: (0, j)),
        out_shape=jax.ShapeDtypeStruct((COND_ROWS, n), F32),
        compiler_params=pltpu.CompilerParams(
            dimension_semantics=("arbitrary",), vmem_limit_bytes=VMEM_LIMIT),
        name="ada_rows",
    )(cond, w_ada, b_ada)


def _rms_rows(x):
    return x * lax.rsqrt(jnp.mean(x * x, axis=-1, keepdims=True) + RMS_EPS)


def _mixer_kernel(*refs, seq_len, ctx_len, use_rope, emit_kv):
    it = iter(refs)
    x_ref, mod_ref = next(it), next(it)
    if ctx_len:
        ck_ref, cv_ref = next(it), next(it)
    if use_rope:
        cos_ref, sin_ref = next(it), next(it)
    (nmix_ref, win_ref, gsum_ref, qkg_ref, convw_ref, ga_ref, gc_ref, wout_ref) = (
        next(it) for _ in range(8))
    out_ref = next(it)
    if emit_kv:
        ko_ref, vo_ref = next(it), next(it)
    q_s, k_s, v_s, attn_s, t_s, gb_s = (next(it) for _ in range(6))

    n_chunks = BLOCK_ROWS // CHUNK_ROWS
    mod = mod_ref[...]
    shift = mod[:, 0:D_MODEL]
    scale1 = 1.0 + mod[:, D_MODEL:2 * D_MODEL]
    gate = mod[:, 2 * D_MODEL:3 * D_MODEL]

    if ctx_len:
        ck = ck_ref[...]
        cv = cv_ref[...]
        for g in range(N_KV):
            k_s[g, 0:ctx_len, :] = ck[:, g * HEAD_DIM:(g + 1) * HEAD_DIM].astype(BF16)
            v_s[g, 0:ctx_len, :] = cv[:, g * HEAD_DIM:(g + 1) * HEAD_DIM].astype(BF16)

    if use_rope:
        lane = lax.broadcasted_iota(jnp.int32, (CHUNK_ROWS, LANES), 1)
        first_half = (lane % (2 * ROT_PAIRS)) < ROT_PAIRS

    for c in range(n_chunks):
        r0 = c * CHUNK_ROWS
        x = x_ref[r0:r0 + CHUNK_ROWS, :]
        h = (_rms_rows(x) * nmix_ref[...]) * scale1 + shift
        hb = h.astype(BF16)
        qkv = jnp.dot(hb, win_ref[:, 0:QKV_DIM], preferred_element_type=F32)

        groups = []
        for g0 in range(0, QKV_DIM, MXU_DIM):
            sq = qkv[:, g0:g0 + MXU_DIM]
            sq = sq * sq
            hi = sq.astype(BF16)
            lo = (sq - hi.astype(F32)).astype(BF16)
            groups.append(jnp.dot(jnp.concatenate([hi, lo], axis=-1), gsum_ref[...],
                                  preferred_element_type=F32))
        ms = jnp.concatenate(groups, axis=-1)[:, 0:QK_DIM]
        qk = (qkv[:, 0:QK_DIM] * lax.rsqrt(ms + RMS_EPS)) * qkg_ref[...]
        vv = qkv[:, QK_DIM:QKV_DIM]

        if emit_kv:
            ko_ref[r0:r0 + CHUNK_ROWS, :] = qk[:, ATTN_DIM:QK_DIM]
            vo_ref[r0:r0 + CHUNK_ROWS, :] = vv

        for cg in range(QK_DIM // LANES):
            xg = qk[:, cg * LANES:(cg + 1) * LANES]
            if use_rope:
                cs = cos_ref[r0:r0 + CHUNK_ROWS, :]
                sn = sin_ref[r0:r0 + CHUNK_ROWS, :]
                partner = jnp.where(first_half,
                                    pltpu.roll(xg, LANES - ROT_PAIRS, axis=1),
                                    pltpu.roll(xg, ROT_PAIRS, axis=1))
                xg = xg * cs + partner * sn
            if cg < ATTN_DIM // LANES:
                xb = (xg * (HEAD_DIM ** -0.5)).astype(BF16)
                q_s[2 * cg, r0:r0 + CHUNK_ROWS, :] = xb[:, 0:HEAD_DIM]
                q_s[2 * cg + 1, r0:r0 + CHUNK_ROWS, :] = xb[:, HEAD_DIM:LANES]
            else:
                xb = xg.astype(BF16)
                for g in range(N_KV):
                    k_s[g, ctx_len + r0:ctx_len + r0 + CHUNK_ROWS, :] = (
                        xb[:, g * HEAD_DIM:(g + 1) * HEAD_DIM])
        vb = vv.astype(BF16)
        for g in range(N_KV):
            v_s[g, ctx_len + r0:ctx_len + r0 + CHUNK_ROWS, :] = vb[:, g * HEAD_DIM:(g + 1) * HEAD_DIM]

        cvp = jnp.dot(hb, win_ref[:, QKV_DIM:IN_DIM], preferred_element_type=F32)
        gb_s[r0:r0 + CHUNK_ROWS, :] = cvp[:, 0:CONV_DIM]
        t_s[8 + r0:8 + r0 + CHUNK_ROWS, :] = (cvp[:, CONV_DIM:2 * CONV_DIM]
                                              * cvp[:, 2 * CONV_DIM:3 * CONV_DIM])
    t_s[0:8, :] = jnp.zeros((8, CONV_DIM), F32)
    t_s[8 + BLOCK_ROWS:16 + BLOCK_ROWS, :] = jnp.zeros((8, CONV_DIM), F32)

    if ctx_len:
        q_rows, n_keys = 128, ctx_len + seq_len
    else:
        q_rows, n_keys = seq_len, seq_len

    def attn_step(i, carry):
        r0 = pl.multiple_of(i * q_rows, q_rows)
        k0 = 0 if ctx_len else r0
        for g in range(N_KV):
            qs = jnp.concatenate(
                [q_s[KV_GROUP * g + j, pl.ds(r0, q_rows), :] for j in range(KV_GROUP)], axis=0)
            kk = k_s[g, pl.ds(k0, n_keys), :]
            vg = v_s[g, pl.ds(k0, n_keys), :]
            s = lax.dot_general(qs, kk, (((1,), (1,)), ((), ())), preferred_element_type=F32)
            m = jnp.max(s, axis=-1, keepdims=True)
            p = jnp.exp(s - m)
            l = jnp.sum(p, axis=-1, keepdims=True)
            o = jnp.dot(p.astype(BF16), vg, preferred_element_type=F32) / l
            for jj in range(KV_GROUP // 2):
                pair = jnp.concatenate(
                    [o[(2 * jj) * q_rows:(2 * jj + 1) * q_rows],
                     o[(2 * jj + 1) * q_rows:(2 * jj + 2) * q_rows]], axis=-1)
                col = (KV_GROUP * g + 2 * jj) * HEAD_DIM
                attn_s[pl.ds(r0, q_rows), col:col + LANES] = pair
        return carry

    lax.fori_loop(0, BLOCK_ROWS // q_rows, attn_step, 0)

    w0 = convw_ref[0:1, :]
    w1 = convw_ref[1:2, :]
    w2 = convw_ref[2:3, :]
    for c in range(n_chunks):
        r0 = c * CHUNK_ROWS
        pos = (lax.broadcasted_iota(jnp.int32, (CHUNK_ROWS, 1), 0) + r0) % seq_len
        t_prev = jnp.where(pos == 0, 0.0, t_s[7 + r0:7 + r0 + CHUNK_ROWS, :])
        t_mid = t_s[8 + r0:8 + r0 + CHUNK_ROWS, :]
        t_next = jnp.where(pos == seq_len - 1, 0.0, t_s[9 + r0:9 + r0 + CHUNK_ROWS, :])
        y = gb_s[r0:r0 + CHUNK_ROWS, :] * (w0 * t_prev + w1 * t_mid + w2 * t_next)
        yn = _rms_rows(y) * gc_ref[...]
        an = _rms_rows(attn_s[r0:r0 + CHUNK_ROWS, :]) * ga_ref[...]
        merged = jnp.concatenate([an, yn], axis=-1).astype(BF16)
        mix = jnp.dot(merged, wout_ref[...], preferred_element_type=F32)
        out_ref[r0:r0 + CHUNK_ROWS, :] = x_ref[r0:r0 + CHUNK_ROWS, :] + gate * mix


def _mixer_call(x_blocks, mod3, mod_row_of_block, ctx_k, ctx_v, rope, consts, *, seq_len,
                emit_kv):
    n_blocks = x_blocks.shape[0]
    ctx_len = 0 if ctx_k is None else ctx_k.shape[1]
    use_rope = rope is not None
    n_keys_total = ctx_len + BLOCK_ROWS

    blk = lambda cols: pl.BlockSpec((None, BLOCK_ROWS, cols), lambda b: (b, 0, 0))
    args = [x_blocks, mod3]
    in_specs = [blk(D_MODEL),
                pl.BlockSpec((None, 1, 3 * D_MODEL), lambda b: (mod_row_of_block(b), 0, 0))]
    if ctx_len:
        args += [ctx_k, ctx_v]
        in_specs += [pl.BlockSpec((None, ctx_len, KV_DIM), lambda b: (b, 0, 0))] * 2
    if use_rope:
        args += list(rope)
        in_specs += [_const_spec((BLOCK_ROWS, LANES))] * 2
    args += list(consts)
    in_specs += [_const_spec(a.shape) for a in consts]

    out_shape = [jax.ShapeDtypeStruct((n_blocks, BLOCK_ROWS, D_MODEL), F32)]
    out_specs = [blk(D_MODEL)]
    if emit_kv:
        out_shape += [jax.ShapeDtypeStruct((n_blocks, BLOCK_ROWS, KV_DIM), F32)] * 2
        out_specs += [blk(KV_DIM)] * 2

    scratch = [
        pltpu.VMEM((N_HEADS, BLOCK_ROWS, HEAD_DIM), BF16),
        pltpu.VMEM((N_KV, n_keys_total, HEAD_DIM), BF16),
        pltpu.VMEM((N_KV, n_keys_total, HEAD_DIM), BF16),
        pltpu.VMEM((BLOCK_ROWS, ATTN_DIM), F32),
        pltpu.VMEM((BLOCK_ROWS + 16, CONV_DIM), F32),
        pltpu.VMEM((BLOCK_ROWS, CONV_DIM), F32),
    ]
    kern = functools.partial(_mixer_kernel, seq_len=seq_len, ctx_len=ctx_len,
                             use_rope=use_rope, emit_kv=emit_kv)
    return pl.pallas_call(
        kern,
        grid=(n_blocks,),
        in_specs=in_specs,
        out_specs=out_specs,
        out_shape=out_shape,
        scratch_shapes=scratch,
        compiler_params=pltpu.CompilerParams(
            dimension_semantics=("arbitrary",), vmem_limit_bytes=VMEM_LIMIT),
        name="mixer_ctx" if emit_kv else "mixer_latent",
    )(*args)


def _ffn_kernel(x_ref, mod_ref, nffn_ref, wgu_ref, wd_ref, out_ref):
    mod = mod_ref[...]
    shift = mod[:, 0:D_MODEL]
    scale1 = 1.0 + mod[:, D_MODEL:2 * D_MODEL]
    gate = mod[:, 2 * D_MODEL:3 * D_MODEL]
    x = x_ref[...]
    hb = ((_rms_rows(x) * nffn_ref[...]) * scale1 + shift).astype(BF16)
    acc = None
    for c0, cw in FF_CHUNKS:
        gt = jnp.dot(hb, wgu_ref[:, c0:c0 + cw], preferred_element_type=F32)
        up = jnp.dot(hb, wgu_ref[:, D_FF + c0:D_FF + c0 + cw], preferred_element_type=F32)
        act = ((gt * jax.nn.sigmoid(gt)) * up).astype(BF16)
        part = jnp.dot(act, wd_ref[c0:c0 + cw, :], preferred_element_type=F32)
        acc = part if acc is None else acc + part
    out_ref[...] = x + gate * acc


def _ffn_call(x_rows, mod3, mod_row_of_tile, norm_ffn, w_gate_up, w_down, name):
    n_rows = x_rows.shape[0]
    return pl.pallas_call(
        _ffn_kernel,
        grid=(n_rows // FFN_ROWS,),
        in_specs=[
            pl.BlockSpec((FFN_ROWS, D_MODEL), lambda i: (i, 0)),
            pl.BlockSpec((None, 1, 3 * D_MODEL), lambda i: (mod_row_of_tile(i), 0, 1)),
            _const_spec(norm_ffn.shape),
            _const_spec(w_gate_up.shape),
            _const_spec(w_down.shape),
        ],
        out_specs=pl.BlockSpec((FFN_ROWS, D_MODEL), lambda i: (i, 0)),
        out_shape=jax.ShapeDtypeStruct((n_rows, D_MODEL), F32),
        compiler_params=pltpu.CompilerParams(
            dimension_semantics=("arbitrary",), vmem_limit_bytes=VMEM_LIMIT),
        name=name,
    )(x_rows, mod3, norm_ffn, w_gate_up, w_down)


def _rope_tables(n_tokens):
    rows = n_tokens // GRID_W
    row = jnp.repeat(jnp.arange(rows, dtype=F32), GRID_W)
    col = jnp.tile(jnp.arange(GRID_W, dtype=F32), rows)
    inv = 1.0 / (ROPE_THETA ** (jnp.arange(ROT_PAIRS, dtype=F32) / ROT_PAIRS))
    ang = jnp.stack([row[:, None] * inv, col[:, None] * inv], axis=1)
    cos, sin = jnp.cos(ang), jnp.sin(ang)
    cos_h = jnp.concatenate([cos, cos], axis=-1).reshape(n_tokens, HEAD_DIM)
    sin_h = jnp.concatenate([-sin, sin], axis=-1).reshape(n_tokens, HEAD_DIM)
    reps = LANES // HEAD_DIM
    return jnp.tile(cos_h, (1, reps)), jnp.tile(sin_h, (1, reps))


def _group_mean_matrix():
    idx = np.arange(MXU_DIM) // HEAD_DIM
    g = (idx[:, None] == idx[None, :]).astype(np.float32) / HEAD_DIM
    return jnp.asarray(np.concatenate([g, g], axis=0), dtype=BF16)


def kernel(x_prompt, x_sample, c, cache_k, cache_v, c_ctx, norm_mix, norm_ffn, w_ada, b_ada,
           w_in, q_norm, k_norm, conv_w, attn_out_norm, conv_out_norm, w_out, w_gate_up, w_down):
    depth = w_in.shape[0]
    assert depth == 1
    n_prompt, seq, _ = x_prompt.shape
    n_sample, dec_seq, _ = x_sample.shape
    past = cache_k.shape[2]
    assert dec_seq == BLOCK_ROWS and BLOCK_ROWS % seq == 0 and n_sample <= CTX_ROW

    cond = jnp.zeros((COND_ROWS, D_MODEL), F32)
    cond = cond.at[0:n_sample].set(c).at[CTX_ROW].set(c_ctx)
    mod = _ada_call(cond, w_ada[0], b_ada[0][None, :])
    mod3 = mod.reshape(COND_ROWS, 1, 6 * D_MODEL)

    consts = (
        norm_mix[0][None, :],
        w_in[0].astype(BF16),
        _group_mean_matrix(),
        jnp.concatenate([jnp.tile(q_norm[0], N_HEADS), jnp.tile(k_norm[0], N_KV)])[None, :],
        conv_w[0],
        attn_out_norm[0][None, :],
        conv_out_norm[0][None, :],
        w_out[0].astype(BF16),
    )
    nffn = norm_ffn[0][None, :]
    wgu = w_gate_up[0].astype(BF16)
    wd = w_down[0].astype(BF16)

    per_block = BLOCK_ROWS // seq
    xp_blocks = x_prompt.reshape(n_prompt // per_block, BLOCK_ROWS, D_MODEL)
    xp1, k_new, v_new = _mixer_call(xp_blocks, mod3, lambda b: CTX_ROW, None, None, None, consts,
                                    seq_len=seq, emit_kv=True)
    yp = _ffn_call(xp1.reshape(-1, D_MODEL), mod3, lambda i: CTX_ROW, nffn, wgu, wd, "ffn_ctx")

    ck = cache_k[:, 0].reshape(n_sample, past, KV_DIM)
    cv = cache_v[:, 0].reshape(n_sample, past, KV_DIM)
    (xs1,) = _mixer_call(x_sample, mod3, lambda b: b, ck, cv, _rope_tables(dec_seq), consts,
                         seq_len=dec_seq, emit_kv=False)
    tiles_per_seq = dec_seq // FFN_ROWS
    ys = _ffn_call(xs1.reshape(-1, D_MODEL), mod3, lambda i: i // tiles_per_seq, nffn, wgu, wd,
                   "ffn_latent")

    return (yp.reshape(n_prompt, seq, D_MODEL),
            ys.reshape(n_sample, dec_seq, D_MODEL),
            k_new.reshape(n_prompt, 1, seq, N_KV, HEAD_DIM),
            v_new.reshape(n_prompt, 1, seq, N_KV, HEAD_DIM))
```

```python
import functools

import numpy as np
import jax
import jax.numpy as jnp
from jax import lax
from jax.experimental import pallas as pl
from jax.experimental.pallas import tpu as pltpu

D_MODEL = 1024
HEAD_DIM = 64
ATTN_DIM = 512
N_HEADS = 8
N_KV = 2
KV_GROUP = N_HEADS // N_KV
KV_DIM = N_KV * HEAD_DIM
CONV_DIM = 512
D_FF = 2816
QK_DIM = ATTN_DIM + KV_DIM
QKV_DIM = ATTN_DIM + 2 * KV_DIM
IN_DIM = QKV_DIM + 3 * CONV_DIM
GRID_W = 64
ROT_PAIRS = HEAD_DIM // 4
ROPE_THETA = 10000.0
RMS_EPS = 1e-6
Q_SCALE = HEAD_DIM ** -0.5 * 1.4426950408889634
V_ROWS = HEAD_DIM + 16

LANES = 128
MXU_DIM = 256
BLOCK_ROWS = 1024
CHUNK_ROWS = 512
LATENT_Q_ROWS = 128
P_ROWS = 16
FFN_ROWS = 512
FF_CHUNKS = ((0, 1024), (1024, 1024), (2048, 768))
COND_ROWS = 16
CTX_ROW = 8
ADA_COLS = 1024
VMEM_LIMIT = 56 * 1024 * 1024

F32 = jnp.float32
BF16 = jnp.bfloat16


def _const_spec(shape):
    nd = len(shape)
    return pl.BlockSpec(shape, lambda *_: (0,) * nd, pipeline_mode=pl.Buffered(1))


def _ada_kernel(cond_ref, w_ref, b_ref, out_ref):
    c = cond_ref[...]
    s = (c * jax.nn.sigmoid(c)).astype(BF16)
    out_ref[...] = jnp.dot(s, w_ref[...].astype(BF16), preferred_element_type=F32) + b_ref[...]


def _ada_call(cond, w_ada, b_ada):
    n = w_ada.shape[1]
    return pl.pallas_call(
        _ada_kernel,
        grid=(n // ADA_COLS,),
        in_specs=[
            pl.BlockSpec((COND_ROWS, D_MODEL), lambda j: (0, 0)),
            pl.BlockSpec((D_MODEL, ADA_COLS), lambda j: (0, j)),
            pl.BlockSpec((1, ADA_COLS), lambda j: (0, j)),
        ],
        out_specs=pl.BlockSpec((COND_ROWS, ADA_COLS), lambda j: (0, j)),
        out_shape=jax.ShapeDtypeStruct((COND_ROWS, n), F32),
        compiler_params=pltpu.CompilerParams(
            dimension_semantics=("arbitrary",), vmem_limit_bytes=VMEM_LIMIT),
        name="ada_rows",
    )(cond, w_ada, b_ada)


def _rms_rows(x):
    return x * lax.rsqrt(jnp.mean(x * x, axis=-1, keepdims=True) + RMS_EPS)


def _mixer_kernel(*refs, seq_len, ctx_len, use_rope, emit_kv):
    it = iter(refs)
    x_ref, mod_ref = next(it), next(it)
    if ctx_len:
        ck_ref, cv_ref = next(it), next(it)
    if use_rope:
        cos_ref, sin_ref = next(it), next(it)
    (nmix_ref, win_ref, gsum_ref, qkg_ref, convw_ref, ga_ref, gc_ref, wout_ref) = (
        next(it) for _ in range(8))
    out_ref = next(it)
    if emit_kv:
        ko_ref, vo_ref = next(it), next(it)
    (q_s, k_s, vt_s, attn_s, t_s, gb_s, s_ref, p_ref, m_ref) = (next(it) for _ in range(9))

    n_chunks = BLOCK_ROWS // CHUNK_ROWS
    if ctx_len:
        q_rows, n_keys, key_blk = LATENT_Q_ROWS, ctx_len + seq_len, BLOCK_ROWS
    else:
        q_rows, n_keys, key_blk = seq_len, seq_len, seq_len
    n_qb = BLOCK_ROWS // q_rows

    mod = mod_ref[...]
    shift = mod[:, 0:D_MODEL]
    scale1 = 1.0 + mod[:, D_MODEL:2 * D_MODEL]
    gate = mod[:, 2 * D_MODEL:3 * D_MODEL]

    def put_values_t(blk, off, v_rows):
        vt = v_rows.T.astype(BF16)
        for g in range(N_KV):
            vt_s[blk, g, 0:HEAD_DIM, off:off + v_rows.shape[0]] = (
                vt[g * HEAD_DIM:(g + 1) * HEAD_DIM, :])

    tail = (lax.broadcasted_iota(jnp.int32, (V_ROWS - HEAD_DIM, n_keys), 0) == 0).astype(BF16)
    for blk in range(vt_s.shape[0]):
        for g in range(N_KV):
            vt_s[blk, g, HEAD_DIM:V_ROWS, :] = tail

    if ctx_len:
        ck = ck_ref[...]
        for g in range(N_KV):
            k_s[g, 0:ctx_len, :] = ck[:, g * HEAD_DIM:(g + 1) * HEAD_DIM].astype(BF16)
        put_values_t(0, 0, cv_ref[...])

    if use_rope:
        lane = lax.broadcasted_iota(jnp.int32, (CHUNK_ROWS, LANES), 1)
        first_half = (lane % (2 * ROT_PAIRS)) < ROT_PAIRS

    for c in range(n_chunks):
        r0 = c * CHUNK_ROWS
        x = x_ref[r0:r0 + CHUNK_ROWS, :]
        h = (_rms_rows(x) * nmix_ref[...]) * scale1 + shift
        hb = h.astype(BF16)
        qkv = jnp.dot(hb, win_ref[:, 0:QKV_DIM], preferred_element_type=F32)

        groups = []
        for g0 in range(0, QKV_DIM, MXU_DIM):
            sq = qkv[:, g0:g0 + MXU_DIM]
            sq = sq * sq
            hi = sq.astype(BF16)
            lo = (sq - hi.astype(F32)).astype(BF16)
            groups.append(jnp.dot(jnp.concatenate([hi, lo], axis=-1), gsum_ref[...],
                                  preferred_element_type=F32))
        ms = jnp.concatenate(groups, axis=-1)[:, 0:QK_DIM]
        qk = (qkv[:, 0:QK_DIM] * lax.rsqrt(ms + RMS_EPS)) * qkg_ref[...]
        vv = qkv[:, QK_DIM:QKV_DIM]

        if emit_kv:
            ko_ref[r0:r0 + CHUNK_ROWS, :] = qk[:, ATTN_DIM:QK_DIM]
            vo_ref[r0:r0 + CHUNK_ROWS, :] = vv

        for cg in range(QK_DIM // LANES):
            xg = qk[:, cg * LANES:(cg + 1) * LANES]
            if use_rope:
                cs = cos_ref[r0:r0 + CHUNK_ROWS, :]
                sn = sin_ref[r0:r0 + CHUNK_ROWS, :]
                partner = jnp.where(first_half,
                                    pltpu.roll(xg, LANES - ROT_PAIRS, axis=1),
                                    pltpu.roll(xg, ROT_PAIRS, axis=1))
                xg = xg * cs + partner * sn
            if cg < ATTN_DIM // LANES:
                xb = (xg * Q_SCALE).astype(BF16)
                q_s[2 * cg, r0:r0 + CHUNK_ROWS, :] = xb[:, 0:HEAD_DIM]
                q_s[2 * cg + 1, r0:r0 + CHUNK_ROWS, :] = xb[:, HEAD_DIM:LANES]
            else:
                xb = xg.astype(BF16)
                for g in range(N_KV):
                    k_s[g, ctx_len + r0:ctx_len + r0 + CHUNK_ROWS, :] = (
                        xb[:, g * HEAD_DIM:(g + 1) * HEAD_DIM])
        w = min(key_blk, CHUNK_ROWS)
        for r1 in range(r0, r0 + CHUNK_ROWS, w):
            blk, off = (0, ctx_len + r1) if ctx_len else (r1 // key_blk, 0)
            put_values_t(blk, off, vv[r1 - r0:r1 - r0 + w, :])

        cvp = jnp.dot(hb, win_ref[:, QKV_DIM:IN_DIM], preferred_element_type=F32)
        gb_s[r0:r0 + CHUNK_ROWS, :] = cvp[:, 0:CONV_DIM]
        t_s[8 + r0:8 + r0 + CHUNK_ROWS, :] = (cvp[:, CONV_DIM:2 * CONV_DIM]
                                              * cvp[:, 2 * CONV_DIM:3 * CONV_DIM])
    t_s[0:8, :] = jnp.zeros((8, CONV_DIM), F32)
    t_s[8 + BLOCK_ROWS:16 + BLOCK_ROWS, :] = jnp.zeros((8, CONV_DIM), F32)

    def stage_a(qb, g, par):
        r0 = pl.multiple_of(qb * q_rows, q_rows)
        k0 = 0 if ctx_len else r0
        qs = jnp.concatenate(
            [q_s[KV_GROUP * g + j, pl.ds(r0, q_rows), :] for j in range(KV_GROUP)], axis=0)
        kk = k_s[g, pl.ds(k0, n_keys), :]
        s = lax.dot_general(kk, qs, (((1,), (1,)), ((), ())), preferred_element_type=F32)
        s_ref[par, g] = s
        m_ref[par, g] = jnp.max(s, axis=0, keepdims=True)

    def stage_b(g, par):
        m = m_ref[par, g]
        for k1 in range(0, n_keys, P_ROWS):
            p_ref[g, k1:k1 + P_ROWS, :] = jnp.exp2(
                s_ref[par, g, k1:k1 + P_ROWS, :] - m).astype(BF16)

    def stage_c(qb, g, par):
        r0 = pl.multiple_of(qb * q_rows, q_rows)
        blk = 0 if ctx_len else qb
        ot = jnp.dot(vt_s[blk, g], p_ref[g], preferred_element_type=F32)
        ot = ot[0:HEAD_DIM, :] / ot[HEAD_DIM:HEAD_DIM + 1, :]
        for jj in range(KV_GROUP // 2):
            pair_t = jnp.concatenate(
                [ot[:, (2 * jj) * q_rows:(2 * jj + 1) * q_rows],
                 ot[:, (2 * jj + 1) * q_rows:(2 * jj + 2) * q_rows]], axis=0)
            col = (KV_GROUP * g + 2 * jj) * HEAD_DIM
            attn_s[pl.ds(r0, q_rows), col:col + LANES] = pair_t.T

    def step(j, par, do_a=True, do_bc=True):
        for g in range(N_KV):
            if do_a:
                stage_a(j + 1, g, 1 - par)
            if do_bc:
                stage_b(g, par)
                stage_c(j, g, par)

    step(-1, 1, do_bc=False)

    def attn_step(j, carry):
        for par in range(2):
            pl.when(j % 2 == par)(functools.partial(step, j, par))
        return carry

    lax.fori_loop(0, n_qb - 1, attn_step, 0)
    step(n_qb - 1, (n_qb - 1) % 2, do_a=False)

    w0 = convw_ref[0:1, :]
    w1 = convw_ref[1:2, :]
    w2 = convw_ref[2:3, :]
    for c in range(n_chunks):
        r0 = c * CHUNK_ROWS
        pos = (lax.broadcasted_iota(jnp.int32, (CHUNK_ROWS, 1), 0) + r0) % seq_len
        t_prev = jnp.where(pos == 0, 0.0, t_s[7 + r0:7 + r0 + CHUNK_ROWS, :])
        t_mid = t_s[8 + r0:8 + r0 + CHUNK_ROWS, :]
        t_next = jnp.where(pos == seq_len - 1, 0.0, t_s[9 + r0:9 + r0 + CHUNK_ROWS, :])
        y = gb_s[r0:r0 + CHUNK_ROWS, :] * (w0 * t_prev + w1 * t_mid + w2 * t_next)
        yn = _rms_rows(y) * gc_ref[...]
        an = _rms_rows(attn_s[r0:r0 + CHUNK_ROWS, :]) * ga_ref[...]
        merged = jnp.concatenate([an, yn], axis=-1).astype(BF16)
        mix = jnp.dot(merged, wout_ref[...], preferred_element_type=F32)
        out_ref[r0:r0 + CHUNK_ROWS, :] = x_ref[r0:r0 + CHUNK_ROWS, :] + gate * mix


def _mixer_call(x_blocks, mod3, mod_row_of_block, ctx_k, ctx_v, rope, consts, *, seq_len,
                emit_kv):
    n_blocks = x_blocks.shape[0]
    ctx_len = 0 if ctx_k is None else ctx_k.shape[1]
    use_rope = rope is not None
    if ctx_len:
        assert seq_len == BLOCK_ROWS
        q_rows, n_keys, n_key_blocks = LATENT_Q_ROWS, ctx_len + BLOCK_ROWS, 1
    else:
        q_rows, n_keys, n_key_blocks = seq_len, seq_len, BLOCK_ROWS // seq_len
    n_q_cols = KV_GROUP * q_rows

    blk = lambda cols: pl.BlockSpec((None, BLOCK_ROWS, cols), lambda b: (b, 0, 0))
    args = [x_blocks, mod3]
    in_specs = [blk(D_MODEL),
                pl.BlockSpec((None, 1, 3 * D_MODEL), lambda b: (mod_row_of_block(b), 0, 0))]
    if ctx_len:
        args += [ctx_k, ctx_v]
        in_specs += [pl.BlockSpec((None, ctx_len, KV_DIM), lambda b: (b, 0, 0))] * 2
    if use_rope:
        args += list(rope)
        in_specs += [_const_spec((BLOCK_ROWS, LANES))] * 2
    args += list(consts)
    in_specs += [_const_spec(a.shape) for a in consts]

    out_shape = [jax.ShapeDtypeStruct((n_blocks, BLOCK_ROWS, D_MODEL), F32)]
    out_specs = [blk(D_MODEL)]
    if emit_kv:
        out_shape += [jax.ShapeDtypeStruct((n_blocks, BLOCK_ROWS, KV_DIM), F32)] * 2
        out_specs += [blk(KV_DIM)] * 2

    scratch = [
        pltpu.VMEM((N_HEADS, BLOCK_ROWS, HEAD_DIM), BF16),
        pltpu.VMEM((N_KV, ctx_len + BLOCK_ROWS, HEAD_DIM), BF16),
        pltpu.VMEM((n_key_blocks, N_KV, V_ROWS, n_keys), BF16),
        pltpu.VMEM((BLOCK_ROWS, ATTN_DIM), F32),
        pltpu.VMEM((BLOCK_ROWS + 16, CONV_DIM), F32),
        pltpu.VMEM((BLOCK_ROWS, CONV_DIM), F32),
        pltpu.VMEM((2, N_KV, n_keys, n_q_cols), F32),
        pltpu.VMEM((N_KV, n_keys, n_q_cols), BF16),
        pltpu.VMEM((2, N_KV, 1, n_q_cols), F32),
    ]
    kern = functools.partial(_mixer_kernel, seq_len=seq_len, ctx_len=ctx_len,
                             use_rope=use_rope, emit_kv=emit_kv)
    return pl.pallas_call(
        kern,
        grid=(n_blocks,),
        in_specs=in_specs,
        out_specs=out_specs,
        out_shape=out_shape,
        scratch_shapes=scratch,
        compiler_params=pltpu.CompilerParams(
            dimension_semantics=("arbitrary",), vmem_limit_bytes=VMEM_LIMIT),
        name="mixer_ctx" if emit_kv else "mixer_latent",
    )(*args)


def _ffn_kernel(x_ref, mod_ref, nffn_ref, wgu_ref, wd_ref, out_ref):
    mod = mod_ref[...]
    shift = mod[:, 0:D_MODEL]
    scale1 = 1.0 + mod[:, D_MODEL:2 * D_MODEL]
    gate = mod[:, 2 * D_MODEL:3 * D_MODEL]
    x = x_ref[...]
    hb = ((_rms_rows(x) * nffn_ref[...]) * scale1 + shift).astype(BF16)
    acc = None
    for c0, cw in FF_CHUNKS:
        gt = jnp.dot(hb, wgu_ref[:, c0:c0 + cw], preferred_element_type=F32)
        up = jnp.dot(hb, wgu_ref[:, D_FF + c0:D_FF + c0 + cw], preferred_element_type=F32)
        act = ((gt * jax.nn.sigmoid(gt)) * up).astype(BF16)
        part = jnp.dot(act, wd_ref[c0:c0 + cw, :], preferred_element_type=F32)
        acc = part if acc is None else acc + part
    out_ref[...] = x + gate * acc


def _ffn_call(x_rows, mod3, mod_row_of_tile, norm_ffn, w_gate_up, w_down, name):
    n_rows = x_rows.shape[0]
    return pl.pallas_call(
        _ffn_kernel,
        grid=(n_rows // FFN_ROWS,),
        in_specs=[
            pl.BlockSpec((FFN_ROWS, D_MODEL), lambda i: (i, 0)),
            pl.BlockSpec((None, 1, 3 * D_MODEL), lambda i: (mod_row_of_tile(i), 0, 1)),
            _const_spec(norm_ffn.shape),
            _const_spec(w_gate_up.shape),
            _const_spec(w_down.shape),
        ],
        out_specs=pl.BlockSpec((FFN_ROWS, D_MODEL), lambda i: (i, 0)),
        out_shape=jax.ShapeDtypeStruct((n_rows, D_MODEL), F32),
        compiler_params=pltpu.CompilerParams(
            dimension_semantics=("arbitrary",), vmem_limit_bytes=VMEM_LIMIT),
        name=name,
    )(x_rows, mod3, norm_ffn, w_gate_up, w_down)


def _rope_tables(n_tokens):
    rows = n_tokens // GRID_W
    row = jnp.repeat(jnp.arange(rows, dtype=F32), GRID_W)
    col = jnp.tile(jnp.arange(GRID_W, dtype=F32), rows)
    inv = 1.0 / (ROPE_THETA ** (jnp.arange(ROT_PAIRS, dtype=F32) / ROT_PAIRS))
    ang = jnp.stack([row[:, None] * inv, col[:, None] * inv], axis=1)
    cos, sin = jnp.cos(ang), jnp.sin(ang)
    cos_h = jnp.concatenate([cos, cos], axis=-1).reshape(n_tokens, HEAD_DIM)
    sin_h = jnp.concatenate([-sin, sin], axis=-1).reshape(n_tokens, HEAD_DIM)
    reps = LANES // HEAD_DIM
    return jnp.tile(cos_h, (1, reps)), jnp.tile(sin_h, (1, reps))


def _group_mean_matrix():
    idx = np.arange(MXU_DIM) // HEAD_DIM
    g = (idx[:, None] == idx[None, :]).astype(np.float32) / HEAD_DIM
    return jnp.asarray(np.concatenate([g, g], axis=0), dtype=BF16)


def kernel(x_prompt, x_sample, c, cache_k, cache_v, c_ctx, norm_mix, norm_ffn, w_ada, b_ada,
           w_in, q_norm, k_norm, conv_w, attn_out_norm, conv_out_norm, w_out, w_gate_up, w_down):
    depth = w_in.shape[0]
    assert depth == 1
    n_prompt, seq, _ = x_prompt.shape
    n_sample, dec_seq, _ = x_sample.shape
    past = cache_k.shape[2]
    assert dec_seq == BLOCK_ROWS and BLOCK_ROWS % seq == 0 and n_sample <= CTX_ROW

    cond = jnp.zeros((COND_ROWS, D_MODEL), F32)
    cond = cond.at[0:n_sample].set(c).at[CTX_ROW].set(c_ctx)
    mod = _ada_call(cond, w_ada[0], b_ada[0][None, :])
    mod3 = mod.reshape(COND_ROWS, 1, 6 * D_MODEL)

    consts = (
        norm_mix[0][None, :],
        w_in[0].astype(BF16),
        _group_mean_matrix(),
        jnp.concatenate([jnp.tile(q_norm[0], N_HEADS), jnp.tile(k_norm[0], N_KV)])[None, :],
        conv_w[0],
        attn_out_norm[0][None, :],
        conv_out_norm[0][None, :],
        w_out[0].astype(BF16),
    )
    nffn = norm_ffn[0][None, :]
    wgu = w_gate_up[0].astype(BF16)
    wd = w_down[0].astype(BF16)

    per_block = BLOCK_ROWS // seq
    xp_blocks = x_prompt.reshape(n_prompt // per_block, BLOCK_ROWS, D_MODEL)
    xp1, k_new, v_new = _mixer_call(xp_blocks, mod3, lambda b: CTX_ROW, None, None, None, consts,
                                    seq_len=seq, emit_kv=True)
    yp = _ffn_call(xp1.reshape(-1, D_MODEL), mod3, lambda i: CTX_ROW, nffn, wgu, wd, "ffn_ctx")

    ck = cache_k[:, 0].reshape(n_sample, past, KV_DIM)
    cv = cache_v[:, 0].reshape(n_sample, past, KV_DIM)
    (xs1,) = _mixer_call(x_sample, mod3, lambda b: b, ck, cv, _rope_tables(dec_seq), consts,
                         seq_len=dec_seq, emit_kv=False)
    tiles_per_seq = dec_seq // FFN_ROWS
    ys = _ffn_call(xs1.reshape(-1, D_MODEL), mod3, lambda i: i // tiles_per_seq, nffn, wgu, wd,
                   "ffn_latent")

    return (yp.reshape(n_prompt, seq, D_MODEL),
            ys.reshape(n_sample, dec_seq, D_MODEL),
            k_new.reshape(n_prompt, 1, seq, N_KV, HEAD_DIM),
            v_new.reshape(n_prompt, 1, seq, N_KV, HEAD_DIM))
```

```python
import functools

import numpy as np
import jax
import jax.numpy as jnp
from jax import lax
from jax.experimental import pallas as pl
from jax.experimental.pallas import tpu as pltpu

D_MODEL = 1024
HEAD_DIM = 64
ATTN_DIM = 512
N_HEADS = 8
N_KV = 2
KV_GROUP = N_HEADS // N_KV
KV_DIM = N_KV * HEAD_DIM
CONV_DIM = 512
D_FF = 2816
QK_DIM = ATTN_DIM + KV_DIM
QKV_DIM = ATTN_DIM + 2 * KV_DIM
IN_DIM = QKV_DIM + 3 * CONV_DIM
GRID_W = 64
ROT_PAIRS = HEAD_DIM // 4
ROPE_THETA = 10000.0
RMS_EPS = 1e-6
Q_SCALE = HEAD_DIM ** -0.5 * 1.4426950408889634
V_ROWS = HEAD_DIM + 16

LANES = 128
MXU_DIM = 256
BLOCK_ROWS = 1024
CHUNK_ROWS = 512
LATENT_Q_ROWS = 128
P_ROWS = 16
FFN_ROWS = 1024
FFN_SUB_ROWS = 512
FF_CHUNKS = ((0, 1024), (1024, 1024), (2048, 768))
COND_ROWS = 16
CTX_ROW = 8
ADA_COLS = 1024
VMEM_LIMIT = 56 * 1024 * 1024

F32 = jnp.float32
BF16 = jnp.bfloat16


def _const_spec(shape):
    nd = len(shape)
    return pl.BlockSpec(shape, lambda *_: (0,) * nd, pipeline_mode=pl.Buffered(1))


def _ada_kernel(cond_ref, w_ref, b_ref, out_ref):
    c = cond_ref[...]
    s = (c * jax.nn.sigmoid(c)).astype(BF16)
    out_ref[...] = jnp.dot(s, w_ref[...].astype(BF16), preferred_element_type=F32) + b_ref[...]


def _ada_call(cond, w_ada, b_ada):
    n = w_ada.shape[1]
    return pl.pallas_call(
        _ada_kernel,
        grid=(n // ADA_COLS,),
        in_specs=[
            pl.BlockSpec((COND_ROWS, D_MODEL), lambda j: (0, 0)),
            pl.BlockSpec((D_MODEL, ADA_COLS), lambda j: (0, j)),
            pl.BlockSpec((1, ADA_COLS), lambda j: (0, j)),
        ],
        out_specs=pl.BlockSpec((COND_ROWS, ADA_COLS), lambda j: (0, j)),
        out_shape=jax.ShapeDtypeStruct((COND_ROWS, n), F32),
        compiler_params=pltpu.CompilerParams(
            dimension_semantics=("arbitrary",), vmem_limit_bytes=VMEM_LIMIT),
        name="ada_rows",
    )(cond, w_ada, b_ada)


def _rms_rows(x):
    return x * lax.rsqrt(jnp.mean(x * x, axis=-1, keepdims=True) + RMS_EPS)


def _mixer_kernel(*refs, seq_len, ctx_len, use_rope, emit_kv):
    it = iter(refs)
    x_ref, mod_ref = next(it), next(it)
    if ctx_len:
        ck_ref, cv_ref = next(it), next(it)
    if use_rope:
        cos_ref, sin_ref = next(it), next(it)
    (nmix_ref, win_ref, gsum_ref, qkg_ref, convw_ref, ga_ref, gc_ref, wout_ref) = (
        next(it) for _ in range(8))
    out_ref = next(it)
    if emit_kv:
        ko_ref, vo_ref = next(it), next(it)
    (q_s, k_s, vt_s, attn_s, t_s, gb_s, s_ref, p_ref, m_ref) = (next(it) for _ in range(9))

    n_chunks = BLOCK_ROWS // CHUNK_ROWS
    if ctx_len:
        q_rows, n_keys, key_blk = LATENT_Q_ROWS, ctx_len + seq_len, BLOCK_ROWS
    else:
        q_rows, n_keys, key_blk = seq_len, seq_len, seq_len
    n_qb = BLOCK_ROWS // q_rows

    mod = mod_ref[...]
    shift = mod[:, 0:D_MODEL]
    scale1 = 1.0 + mod[:, D_MODEL:2 * D_MODEL]
    gate = mod[:, 2 * D_MODEL:3 * D_MODEL]

    def put_values_t(blk, off, v_rows):
        vt = v_rows.T.astype(BF16)
        for g in range(N_KV):
            vt_s[blk, g, 0:HEAD_DIM, off:off + v_rows.shape[0]] = (
                vt[g * HEAD_DIM:(g + 1) * HEAD_DIM, :])

    tail = (lax.broadcasted_iota(jnp.int32, (V_ROWS - HEAD_DIM, n_keys), 0) == 0).astype(BF16)
    for blk in range(vt_s.shape[0]):
        for g in range(N_KV):
            vt_s[blk, g, HEAD_DIM:V_ROWS, :] = tail

    if ctx_len:
        ck = ck_ref[...]
        for g in range(N_KV):
            k_s[g, 0:ctx_len, :] = ck[:, g * HEAD_DIM:(g + 1) * HEAD_DIM].astype(BF16)
        put_values_t(0, 0, cv_ref[...])

    if use_rope:
        lane = lax.broadcasted_iota(jnp.int32, (CHUNK_ROWS, LANES), 1)
        first_half = (lane % (2 * ROT_PAIRS)) < ROT_PAIRS

    for c in range(n_chunks):
        r0 = c * CHUNK_ROWS
        x = x_ref[r0:r0 + CHUNK_ROWS, :]
        h = (_rms_rows(x) * nmix_ref[...]) * scale1 + shift
        hb = h.astype(BF16)
        qkv = jnp.dot(hb, win_ref[:, 0:QKV_DIM], preferred_element_type=F32)

        groups = []
        for g0 in range(0, QKV_DIM, MXU_DIM):
            sq = qkv[:, g0:g0 + MXU_DIM]
            sq = sq * sq
            hi = sq.astype(BF16)
            lo = (sq - hi.astype(F32)).astype(BF16)
            groups.append(jnp.dot(jnp.concatenate([hi, lo], axis=-1), gsum_ref[...],
                                  preferred_element_type=F32))
        ms = jnp.concatenate(groups, axis=-1)[:, 0:QK_DIM]
        qk = (qkv[:, 0:QK_DIM] * lax.rsqrt(ms + RMS_EPS)) * qkg_ref[...]
        vv = qkv[:, QK_DIM:QKV_DIM]

        if emit_kv:
            ko_ref[r0:r0 + CHUNK_ROWS, :] = qk[:, ATTN_DIM:QK_DIM]
            vo_ref[r0:r0 + CHUNK_ROWS, :] = vv

        for cg in range(QK_DIM // LANES):
            xg = qk[:, cg * LANES:(cg + 1) * LANES]
            if use_rope:
                cs = cos_ref[r0:r0 + CHUNK_ROWS, :]
                sn = sin_ref[r0:r0 + CHUNK_ROWS, :]
                partner = jnp.where(first_half,
                                    pltpu.roll(xg, LANES - ROT_PAIRS, axis=1),
                                    pltpu.roll(xg, ROT_PAIRS, axis=1))
                xg = xg * cs + partner * sn
            if cg < ATTN_DIM // LANES:
                xb = (xg * Q_SCALE).astype(BF16)
                q_s[2 * cg, r0:r0 + CHUNK_ROWS, :] = xb[:, 0:HEAD_DIM]
                q_s[2 * cg + 1, r0:r0 + CHUNK_ROWS, :] = xb[:, HEAD_DIM:LANES]
            else:
                xb = xg.astype(BF16)
                for g in range(N_KV):
                    k_s[g, ctx_len + r0:ctx_len + r0 + CHUNK_ROWS, :] = (
                        xb[:, g * HEAD_DIM:(g + 1) * HEAD_DIM])
        w = min(key_blk, CHUNK_ROWS)
        for r1 in range(r0, r0 + CHUNK_ROWS, w):
            blk, off = (0, ctx_len + r1) if ctx_len else (r1 // key_blk, 0)
            put_values_t(blk, off, vv[r1 - r0:r1 - r0 + w, :])

        cvp = jnp.dot(hb, win_ref[:, QKV_DIM:IN_DIM], preferred_element_type=F32)
        gb_s[r0:r0 + CHUNK_ROWS, :] = cvp[:, 0:CONV_DIM]
        t_s[8 + r0:8 + r0 + CHUNK_ROWS, :] = (cvp[:, CONV_DIM:2 * CONV_DIM]
                                              * cvp[:, 2 * CONV_DIM:3 * CONV_DIM])
    t_s[0:8, :] = jnp.zeros((8, CONV_DIM), F32)
    t_s[8 + BLOCK_ROWS:16 + BLOCK_ROWS, :] = jnp.zeros((8, CONV_DIM), F32)

    def stage_a(qb, g, par):
        r0 = pl.multiple_of(qb * q_rows, q_rows)
        k0 = 0 if ctx_len else r0
        qs = jnp.concatenate(
            [q_s[KV_GROUP * g + j, pl.ds(r0, q_rows), :] for j in range(KV_GROUP)], axis=0)
        kk = k_s[g, pl.ds(k0, n_keys), :]
        s = lax.dot_general(kk, qs, (((1,), (1,)), ((), ())), preferred_element_type=F32)
        s_ref[par, g] = s
        m_ref[par, g] = jnp.max(s, axis=0, keepdims=True)

    def stage_b(g, par):
        m = m_ref[par, g]
        for k1 in range(0, n_keys, P_ROWS):
            p_ref[g, k1:k1 + P_ROWS, :] = jnp.exp2(
                s_ref[par, g, k1:k1 + P_ROWS, :] - m).astype(BF16)

    def stage_c(qb, g, par):
        r0 = pl.multiple_of(qb * q_rows, q_rows)
        blk = 0 if ctx_len else qb
        ot = jnp.dot(vt_s[blk, g], p_ref[g], preferred_element_type=F32)
        ot = ot[0:HEAD_DIM, :] / ot[HEAD_DIM:HEAD_DIM + 1, :]
        for jj in range(KV_GROUP // 2):
            pair_t = jnp.concatenate(
                [ot[:, (2 * jj) * q_rows:(2 * jj + 1) * q_rows],
                 ot[:, (2 * jj + 1) * q_rows:(2 * jj + 2) * q_rows]], axis=0)
            col = (KV_GROUP * g + 2 * jj) * HEAD_DIM
            attn_s[pl.ds(r0, q_rows), col:col + LANES] = pair_t.T

    def step(j, par, do_a=True, do_bc=True):
        for g in range(N_KV):
            if do_a:
                stage_a(j + 1, g, 1 - par)
            if do_bc:
                stage_b(g, par)
                stage_c(j, g, par)

    step(-1, 1, do_bc=False)

    def attn_step(j, carry):
        for par in range(2):
            pl.when(j % 2 == par)(functools.partial(step, j, par))
        return carry

    lax.fori_loop(0, n_qb - 1, attn_step, 0)
    step(n_qb - 1, (n_qb - 1) % 2, do_a=False)

    w0 = convw_ref[0:1, :]
    w1 = convw_ref[1:2, :]
    w2 = convw_ref[2:3, :]
    for c in range(n_chunks):
        r0 = c * CHUNK_ROWS
        pos = (lax.broadcasted_iota(jnp.int32, (CHUNK_ROWS, 1), 0) + r0) % seq_len
        t_prev = jnp.where(pos == 0, 0.0, t_s[7 + r0:7 + r0 + CHUNK_ROWS, :])
        t_mid = t_s[8 + r0:8 + r0 + CHUNK_ROWS, :]
        t_next = jnp.where(pos == seq_len - 1, 0.0, t_s[9 + r0:9 + r0 + CHUNK_ROWS, :])
        y = gb_s[r0:r0 + CHUNK_ROWS, :] * (w0 * t_prev + w1 * t_mid + w2 * t_next)
        yn = _rms_rows(y) * gc_ref[...]
        an = _rms_rows(attn_s[r0:r0 + CHUNK_ROWS, :]) * ga_ref[...]
        merged = jnp.concatenate([an, yn], axis=-1).astype(BF16)
        mix = jnp.dot(merged, wout_ref[...], preferred_element_type=F32)
        out_ref[r0:r0 + CHUNK_ROWS, :] = x_ref[r0:r0 + CHUNK_ROWS, :] + gate * mix


def _mixer_call(x_blocks, mod3, mod_row_of_block, ctx_k, ctx_v, rope, consts, *, seq_len,
                emit_kv):
    n_blocks = x_blocks.shape[0]
    ctx_len = 0 if ctx_k is None else ctx_k.shape[1]
    use_rope = rope is not None
    if ctx_len:
        assert seq_len == BLOCK_ROWS
        q_rows, n_keys, n_key_blocks = LATENT_Q_ROWS, ctx_len + BLOCK_ROWS, 1
    else:
        q_rows, n_keys, n_key_blocks = seq_len, seq_len, BLOCK_ROWS // seq_len
    n_q_cols = KV_GROUP * q_rows

    blk = lambda cols: pl.BlockSpec((None, BLOCK_ROWS, cols), lambda b: (b, 0, 0))
    args = [x_blocks, mod3]
    in_specs = [blk(D_MODEL),
                pl.BlockSpec((None, 1, 3 * D_MODEL), lambda b: (mod_row_of_block(b), 0, 0))]
    if ctx_len:
        args += [ctx_k, ctx_v]
        in_specs += [pl.BlockSpec((None, ctx_len, KV_DIM), lambda b: (b, 0, 0))] * 2
    if use_rope:
        args += list(rope)
        in_specs += [_const_spec((BLOCK_ROWS, LANES))] * 2
    args += list(consts)
    in_specs += [_const_spec(a.shape) for a in consts]

    out_shape = [jax.ShapeDtypeStruct((n_blocks, BLOCK_ROWS, D_MODEL), F32)]
    out_specs = [blk(D_MODEL)]
    if emit_kv:
        out_shape += [jax.ShapeDtypeStruct((n_blocks, BLOCK_ROWS, KV_DIM), F32)] * 2
        out_specs += [blk(KV_DIM)] * 2

    scratch = [
        pltpu.VMEM((N_HEADS, BLOCK_ROWS, HEAD_DIM), BF16),
        pltpu.VMEM((N_KV, ctx_len + BLOCK_ROWS, HEAD_DIM), BF16),
        pltpu.VMEM((n_key_blocks, N_KV, V_ROWS, n_keys), BF16),
        pltpu.VMEM((BLOCK_ROWS, ATTN_DIM), F32),
        pltpu.VMEM((BLOCK_ROWS + 16, CONV_DIM), F32),
        pltpu.VMEM((BLOCK_ROWS, CONV_DIM), F32),
        pltpu.VMEM((2, N_KV, n_keys, n_q_cols), F32),
        pltpu.VMEM((N_KV, n_keys, n_q_cols), BF16),
        pltpu.VMEM((2, N_KV, 1, n_q_cols), F32),
    ]
    kern = functools.partial(_mixer_kernel, seq_len=seq_len, ctx_len=ctx_len,
                             use_rope=use_rope, emit_kv=emit_kv)
    return pl.pallas_call(
        kern,
        grid=(n_blocks,),
        in_specs=in_specs,
        out_specs=out_specs,
        out_shape=out_shape,
        scratch_shapes=scratch,
        compiler_params=pltpu.CompilerParams(
            dimension_semantics=("arbitrary",), vmem_limit_bytes=VMEM_LIMIT),
        name="mixer_ctx" if emit_kv else "mixer_latent",
    )(*args)


def _ffn_kernel(x_ref, mod_ref, nffn_ref, wgu_ref, wd_ref, out_ref):
    mod = mod_ref[...]
    shift = mod[:, 0:D_MODEL]
    scale1 = 1.0 + mod[:, D_MODEL:2 * D_MODEL]
    gate = mod[:, 2 * D_MODEL:3 * D_MODEL]
    for r0 in range(0, FFN_ROWS, FFN_SUB_ROWS):
        x = x_ref[r0:r0 + FFN_SUB_ROWS, :]
        hb = ((_rms_rows(x) * nffn_ref[...]) * scale1 + shift).astype(BF16)
        acc = None
        for c0, cw in FF_CHUNKS:
            gt = jnp.dot(hb, wgu_ref[:, c0:c0 + cw], preferred_element_type=F32)
            up = jnp.dot(hb, wgu_ref[:, D_FF + c0:D_FF + c0 + cw], preferred_element_type=F32)
            act = ((gt * jax.nn.sigmoid(gt)) * up).astype(BF16)
            part = jnp.dot(act, wd_ref[c0:c0 + cw, :], preferred_element_type=F32)
            acc = part if acc is None else acc + part
        out_ref[r0:r0 + FFN_SUB_ROWS, :] = x + gate * acc


def _ffn_call(x_rows, mod3, mod_row_of_tile, norm_ffn, w_gate_up, w_down, name):
    n_rows = x_rows.shape[0]
    return pl.pallas_call(
        _ffn_kernel,
        grid=(n_rows // FFN_ROWS,),
        in_specs=[
            pl.BlockSpec((FFN_ROWS, D_MODEL), lambda i: (i, 0)),
            pl.BlockSpec((None, 1, 3 * D_MODEL), lambda i: (mod_row_of_tile(i), 0, 1)),
            _const_spec(norm_ffn.shape),
            _const_spec(w_gate_up.shape),
            _const_spec(w_down.shape),
        ],
        out_specs=pl.BlockSpec((FFN_ROWS, D_MODEL), lambda i: (i, 0)),
        out_shape=jax.ShapeDtypeStruct((n_rows, D_MODEL), F32),
        compiler_params=pltpu.CompilerParams(
            dimension_semantics=("arbitrary",), vmem_limit_bytes=VMEM_LIMIT),
        name=name,
    )(x_rows, mod3, norm_ffn, w_gate_up, w_down)


def _rope_tables(n_tokens):
    rows = n_tokens // GRID_W
    row = jnp.repeat(jnp.arange(rows, dtype=F32), GRID_W)
    col = jnp.tile(jnp.arange(GRID_W, dtype=F32), rows)
    inv = 1.0 / (ROPE_THETA ** (jnp.arange(ROT_PAIRS, dtype=F32) / ROT_PAIRS))
    ang = jnp.stack([row[:, None] * inv, col[:, None] * inv], axis=1)
    cos, sin = jnp.cos(ang), jnp.sin(ang)
    cos_h = jnp.concatenate([cos, cos], axis=-1).reshape(n_tokens, HEAD_DIM)
    sin_h = jnp.concatenate([-sin, sin], axis=-1).reshape(n_tokens, HEAD_DIM)
    reps = LANES // HEAD_DIM
    return jnp.tile(cos_h, (1, reps)), jnp.tile(sin_h, (1, reps))


def _group_mean_matrix():
    idx = np.arange(MXU_DIM) // HEAD_DIM
    g = (idx[:, None] == idx[None, :]).astype(np.float32) / HEAD_DIM
    return jnp.asarray(np.concatenate([g, g], axis=0), dtype=BF16)


def kernel(x_prompt, x_sample, c, cache_k, cache_v, c_ctx, norm_mix, norm_ffn, w_ada, b_ada,
           w_in, q_norm, k_norm, conv_w, attn_out_norm, conv_out_norm, w_out, w_gate_up, w_down):
    depth = w_in.shape[0]
    assert depth == 1
    n_prompt, seq, _ = x_prompt.shape
    n_sample, dec_seq, _ = x_sample.shape
    past = cache_k.shape[2]
    assert dec_seq == BLOCK_ROWS and BLOCK_ROWS % seq == 0 and n_sample <= CTX_ROW

    cond = jnp.zeros((COND_ROWS, D_MODEL), F32)
    cond = cond.at[0:n_sample].set(c).at[CTX_ROW].set(c_ctx)
    mod = _ada_call(cond, w_ada[0], b_ada[0][None, :])
    mod3 = mod.reshape(COND_ROWS, 1, 6 * D_MODEL)

    consts = (
        norm_mix[0][None, :],
        w_in[0].astype(BF16),
        _group_mean_matrix(),
        jnp.concatenate([jnp.tile(q_norm[0], N_HEADS), jnp.tile(k_norm[0], N_KV)])[None, :],
        conv_w[0],
        attn_out_norm[0][None, :],
        conv_out_norm[0][None, :],
        w_out[0].astype(BF16),
    )
    nffn = norm_ffn[0][None, :]
    wgu = w_gate_up[0].astype(BF16)
    wd = w_down[0].astype(BF16)

    per_block = BLOCK_ROWS // seq
    xp_blocks = x_prompt.reshape(n_prompt // per_block, BLOCK_ROWS, D_MODEL)
    xp1, k_new, v_new = _mixer_call(xp_blocks, mod3, lambda b: CTX_ROW, None, None, None, consts,
                                    seq_len=seq, emit_kv=True)
    yp = _ffn_call(xp1.reshape(-1, D_MODEL), mod3, lambda i: CTX_ROW, nffn, wgu, wd, "ffn_ctx")

    ck = cache_k[:, 0].reshape(n_sample, past, KV_DIM)
    cv = cache_v[:, 0].reshape(n_sample, past, KV_DIM)
    (xs1,) = _mixer_call(x_sample, mod3, lambda b: b, ck, cv, _rope_tables(dec_seq), consts,
                         seq_len=dec_seq, emit_kv=False)
    tiles_per_seq = dec_seq // FFN_ROWS
    ys = _ffn_call(xs1.reshape(-1, D_MODEL), mod3, lambda i: i // tiles_per_seq, nffn, wgu, wd,
                   "ffn_latent")

    return (yp.reshape(n_prompt, seq, D_MODEL),
            ys.reshape(n_sample, dec_seq, D_MODEL),
            k_new.reshape(n_prompt, 1, seq, N_KV, HEAD_DIM),
            v_new.reshape(n_prompt, 1, seq, N_KV, HEAD_DIM))
```

```python
import functools

import numpy as np
import jax
import jax.numpy as jnp
from jax import lax
from jax.experimental import pallas as pl
from jax.experimental.pallas import tpu as pltpu

D_MODEL = 1024
HEAD_DIM = 64
ATTN_DIM = 512
N_HEADS = 8
N_KV = 2
KV_GROUP = N_HEADS // N_KV
KV_DIM = N_KV * HEAD_DIM
CONV_DIM = 512
D_FF = 2816
QK_DIM = ATTN_DIM + KV_DIM
QKV_DIM = ATTN_DIM + 2 * KV_DIM
IN_DIM = QKV_DIM + 3 * CONV_DIM
GRID_W = 64
ROT_PAIRS = HEAD_DIM // 4
ROPE_THETA = 10000.0
RMS_EPS = 1e-6
Q_SCALE = HEAD_DIM ** -0.5 * 1.4426950408889634
V_ROWS = HEAD_DIM + 16

LANES = 128
MXU_DIM = 256
BLOCK_ROWS = 1024
CHUNK_ROWS = 512
LATENT_Q_ROWS = 128
P_ROWS = 16
FFN_ROWS = 1024
FFN_SUB_ROWS = 512
FF_CHUNKS = ((0, 1024), (1024, 1024), (2048, 768))
COND_ROWS = 16
CTX_ROW = 8
ADA_COLS = 1024
VMEM_LIMIT = 56 * 1024 * 1024

F32 = jnp.float32
BF16 = jnp.bfloat16


def _const_spec(shape):
    nd = len(shape)
    return pl.BlockSpec(shape, lambda *_: (0,) * nd, pipeline_mode=pl.Buffered(1))


def _ada_kernel(cond_ref, w_ref, b_ref, out_ref):
    c = cond_ref[...]
    s = (c * jax.nn.sigmoid(c)).astype(BF16)
    out_ref[...] = jnp.dot(s, w_ref[...].astype(BF16), preferred_element_type=F32) + b_ref[...]


def _ada_call(cond, w_ada, b_ada):
    n = w_ada.shape[1]
    return pl.pallas_call(
        _ada_kernel,
        grid=(n // ADA_COLS,),
        in_specs=[
            pl.BlockSpec((COND_ROWS, D_MODEL), lambda j: (0, 0)),
            pl.BlockSpec((D_MODEL, ADA_COLS), lambda j: (0, j)),
            pl.BlockSpec((1, ADA_COLS), lambda j: (0, j)),
        ],
        out_specs=pl.BlockSpec((COND_ROWS, ADA_COLS), lambda j: (0, j)),
        out_shape=jax.ShapeDtypeStruct((COND_ROWS, n), F32),
        compiler_params=pltpu.CompilerParams(
            dimension_semantics=("arbitrary",), vmem_limit_bytes=VMEM_LIMIT),
        name="ada_rows",
    )(cond, w_ada, b_ada)


def _rms_rows(x):
    return x * lax.rsqrt(jnp.mean(x * x, axis=-1, keepdims=True) + RMS_EPS)


def _mixer_kernel(*refs, seq_len, ctx_len, use_rope, emit_kv):
    it = iter(refs)
    x_ref, mod_ref = next(it), next(it)
    if ctx_len:
        ck_ref, cv_ref = next(it), next(it)
    if use_rope:
        cos_ref, sin_ref = next(it), next(it)
    (nmix_ref, win_ref, gsum_ref, qkg_ref, convw_ref, ga_ref, gc_ref, wout_ref) = (
        next(it) for _ in range(8))
    out_ref = next(it)
    if emit_kv:
        ko_ref, vo_ref = next(it), next(it)
    (q_s, k_s, vt_s, attn_s, t_s, gb_s, s_ref, p_ref, m_ref) = (next(it) for _ in range(9))

    n_chunks = BLOCK_ROWS // CHUNK_ROWS
    if ctx_len:
        q_rows, n_keys, key_blk = LATENT_Q_ROWS, ctx_len + seq_len, BLOCK_ROWS
    else:
        q_rows, n_keys, key_blk = seq_len, seq_len, seq_len
    n_qb = BLOCK_ROWS // q_rows

    mod = mod_ref[...]
    shift = mod[:, 0:D_MODEL]
    scale1 = 1.0 + mod[:, D_MODEL:2 * D_MODEL]
    gate = mod[:, 2 * D_MODEL:3 * D_MODEL]

    def put_values_t(blk, off, v_rows):
        vt = v_rows.T.astype(BF16)
        for g in range(N_KV):
            vt_s[blk, g, 0:HEAD_DIM, off:off + v_rows.shape[0]] = (
                vt[g * HEAD_DIM:(g + 1) * HEAD_DIM, :])

    tail = (lax.broadcasted_iota(jnp.int32, (V_ROWS - HEAD_DIM, n_keys), 0) == 0).astype(BF16)
    for blk in range(vt_s.shape[0]):
        for g in range(N_KV):
            vt_s[blk, g, HEAD_DIM:V_ROWS, :] = tail

    if ctx_len:
        for g in range(N_KV):
            k_s[g, 0:ctx_len, :] = ck_ref[:, g, :].astype(BF16)
        put_values_t(0, 0, jnp.concatenate([cv_ref[:, g, :] for g in range(N_KV)], axis=-1))

    if use_rope:
        lane = lax.broadcasted_iota(jnp.int32, (CHUNK_ROWS, LANES), 1)
        first_half = (lane % (2 * ROT_PAIRS)) < ROT_PAIRS

    for c in range(n_chunks):
        r0 = c * CHUNK_ROWS
        x = x_ref[r0:r0 + CHUNK_ROWS, :]
        h = (_rms_rows(x) * nmix_ref[...]) * scale1 + shift
        hb = h.astype(BF16)
        qkv = jnp.dot(hb, win_ref[:, 0:QKV_DIM], preferred_element_type=F32)

        groups = []
        for g0 in range(0, QKV_DIM, MXU_DIM):
            sq = qkv[:, g0:g0 + MXU_DIM]
            sq = sq * sq
            hi = sq.astype(BF16)
            lo = (sq - hi.astype(F32)).astype(BF16)
            groups.append(jnp.dot(jnp.concatenate([hi, lo], axis=-1), gsum_ref[...],
                                  preferred_element_type=F32))
        ms = jnp.concatenate(groups, axis=-1)[:, 0:QK_DIM]
        qk = (qkv[:, 0:QK_DIM] * lax.rsqrt(ms + RMS_EPS)) * qkg_ref[...]
        vv = qkv[:, QK_DIM:QKV_DIM]

        if emit_kv:
            for r1 in range(0, CHUNK_ROWS, seq_len):
                for g in range(N_KV):
                    c0 = ATTN_DIM + g * HEAD_DIM
                    ko_ref[(r0 + r1) // seq_len, :, g, :] = qk[r1:r1 + seq_len, c0:c0 + HEAD_DIM]
                    vo_ref[(r0 + r1) // seq_len, :, g, :] = (
                        vv[r1:r1 + seq_len, g * HEAD_DIM:(g + 1) * HEAD_DIM])

        for cg in range(QK_DIM // LANES):
            xg = qk[:, cg * LANES:(cg + 1) * LANES]
            if use_rope:
                cs = cos_ref[r0:r0 + CHUNK_ROWS, :]
                sn = sin_ref[r0:r0 + CHUNK_ROWS, :]
                partner = jnp.where(first_half,
                                    pltpu.roll(xg, LANES - ROT_PAIRS, axis=1),
                                    pltpu.roll(xg, ROT_PAIRS, axis=1))
                xg = xg * cs + partner * sn
            if cg < ATTN_DIM // LANES:
                xb = (xg * Q_SCALE).astype(BF16)
                q_s[2 * cg, r0:r0 + CHUNK_ROWS, :] = xb[:, 0:HEAD_DIM]
                q_s[2 * cg + 1, r0:r0 + CHUNK_ROWS, :] = xb[:, HEAD_DIM:LANES]
            else:
                xb = xg.astype(BF16)
                for g in range(N_KV):
                    k_s[g, ctx_len + r0:ctx_len + r0 + CHUNK_ROWS, :] = (
                        xb[:, g * HEAD_DIM:(g + 1) * HEAD_DIM])
        w = min(key_blk, CHUNK_ROWS)
        for r1 in range(r0, r0 + CHUNK_ROWS, w):
            blk, off = (0, ctx_len + r1) if ctx_len else (r1 // key_blk, 0)
            put_values_t(blk, off, vv[r1 - r0:r1 - r0 + w, :])

        cvp = jnp.dot(hb, win_ref[:, QKV_DIM:IN_DIM], preferred_element_type=F32)
        gb_s[r0:r0 + CHUNK_ROWS, :] = cvp[:, 0:CONV_DIM]
        t_s[8 + r0:8 + r0 + CHUNK_ROWS, :] = (cvp[:, CONV_DIM:2 * CONV_DIM]
                                              * cvp[:, 2 * CONV_DIM:3 * CONV_DIM])
    t_s[0:8, :] = jnp.zeros((8, CONV_DIM), F32)
    t_s[8 + BLOCK_ROWS:16 + BLOCK_ROWS, :] = jnp.zeros((8, CONV_DIM), F32)

    def stage_a(qb, g, par):
        r0 = pl.multiple_of(qb * q_rows, q_rows)
        k0 = 0 if ctx_len else r0
        qs = jnp.concatenate(
            [q_s[KV_GROUP * g + j, pl.ds(r0, q_rows), :] for j in range(KV_GROUP)], axis=0)
        kk = k_s[g, pl.ds(k0, n_keys), :]
        s = lax.dot_general(kk, qs, (((1,), (1,)), ((), ())), preferred_element_type=F32)
        s_ref[par, g] = s
        m_ref[par, g] = jnp.max(s, axis=0, keepdims=True)

    def stage_b(g, par):
        m = m_ref[par, g]
        for k1 in range(0, n_keys, P_ROWS):
            p_ref[g, k1:k1 + P_ROWS, :] = jnp.exp2(
                s_ref[par, g, k1:k1 + P_ROWS, :] - m).astype(BF16)

    def stage_c(qb, g, par):
        r0 = pl.multiple_of(qb * q_rows, q_rows)
        blk = 0 if ctx_len else qb
        ot = jnp.dot(vt_s[blk, g], p_ref[g], preferred_element_type=F32)
        ot = ot[0:HEAD_DIM, :] / ot[HEAD_DIM:HEAD_DIM + 1, :]
        for jj in range(KV_GROUP // 2):
            pair_t = jnp.concatenate(
                [ot[:, (2 * jj) * q_rows:(2 * jj + 1) * q_rows],
                 ot[:, (2 * jj + 1) * q_rows:(2 * jj + 2) * q_rows]], axis=0)
            col = (KV_GROUP * g + 2 * jj) * HEAD_DIM
            attn_s[pl.ds(r0, q_rows), col:col + LANES] = pair_t.T

    def step(j, par, do_a=True, do_bc=True):
        for g in range(N_KV):
            if do_a:
                stage_a(j + 1, g, 1 - par)
            if do_bc:
                stage_b(g, par)
                stage_c(j, g, par)

    step(-1, 1, do_bc=False)

    def attn_step(j, carry):
        for par in range(2):
            pl.when(j % 2 == par)(functools.partial(step, j, par))
        return carry

    lax.fori_loop(0, n_qb - 1, attn_step, 0)
    step(n_qb - 1, (n_qb - 1) % 2, do_a=False)

    w0 = convw_ref[0:1, :]
    w1 = convw_ref[1:2, :]
    w2 = convw_ref[2:3, :]
    for c in range(n_chunks):
        r0 = c * CHUNK_ROWS
        pos = (lax.broadcasted_iota(jnp.int32, (CHUNK_ROWS, 1), 0) + r0) % seq_len
        t_prev = jnp.where(pos == 0, 0.0, t_s[7 + r0:7 + r0 + CHUNK_ROWS, :])
        t_mid = t_s[8 + r0:8 + r0 + CHUNK_ROWS, :]
        t_next = jnp.where(pos == seq_len - 1, 0.0, t_s[9 + r0:9 + r0 + CHUNK_ROWS, :])
        y = gb_s[r0:r0 + CHUNK_ROWS, :] * (w0 * t_prev + w1 * t_mid + w2 * t_next)
        yn = _rms_rows(y) * gc_ref[...]
        an = _rms_rows(attn_s[r0:r0 + CHUNK_ROWS, :]) * ga_ref[...]
        merged = jnp.concatenate([an, yn], axis=-1).astype(BF16)
        mix = jnp.dot(merged, wout_ref[...], preferred_element_type=F32)
        out_ref[r0:r0 + CHUNK_ROWS, :] = x_ref[r0:r0 + CHUNK_ROWS, :] + gate * mix


def _mixer_call(x_blocks, mod3, mod_row_of_block, ctx_k, ctx_v, rope, consts, *, seq_len,
                emit_kv):
    n_blocks = x_blocks.shape[0]
    ctx_len = 0 if ctx_k is None else ctx_k.shape[2]
    use_rope = rope is not None
    if ctx_len:
        assert seq_len == BLOCK_ROWS
        q_rows, n_keys, n_key_blocks = LATENT_Q_ROWS, ctx_len + BLOCK_ROWS, 1
    else:
        q_rows, n_keys, n_key_blocks = seq_len, seq_len, BLOCK_ROWS // seq_len
    n_q_cols = KV_GROUP * q_rows

    blk = lambda cols: pl.BlockSpec((None, BLOCK_ROWS, cols), lambda b: (b, 0, 0))
    args = [x_blocks, mod3]
    in_specs = [blk(D_MODEL),
                pl.BlockSpec((None, 1, 3 * D_MODEL), lambda b: (mod_row_of_block(b), 0, 0))]
    if ctx_len:
        args += [ctx_k, ctx_v]
        in_specs += [pl.BlockSpec((None, None, ctx_len, N_KV, HEAD_DIM),
                                  lambda b: (b, 0, 0, 0, 0))] * 2
    if use_rope:
        args += list(rope)
        in_specs += [_const_spec((BLOCK_ROWS, LANES))] * 2
    args += list(consts)
    in_specs += [_const_spec(a.shape) for a in consts]

    out_shape = [jax.ShapeDtypeStruct((n_blocks, BLOCK_ROWS, D_MODEL), F32)]
    out_specs = [blk(D_MODEL)]
    if emit_kv:
        per_block = BLOCK_ROWS // seq_len
        out_shape += [jax.ShapeDtypeStruct(
            (n_blocks * per_block, 1, seq_len, N_KV, HEAD_DIM), F32)] * 2
        out_specs += [pl.BlockSpec((per_block, None, seq_len, N_KV, HEAD_DIM),
                                   lambda b: (b, 0, 0, 0, 0))] * 2

    scratch = [
        pltpu.VMEM((N_HEADS, BLOCK_ROWS, HEAD_DIM), BF16),
        pltpu.VMEM((N_KV, ctx_len + BLOCK_ROWS, HEAD_DIM), BF16),
        pltpu.VMEM((n_key_blocks, N_KV, V_ROWS, n_keys), BF16),
        pltpu.VMEM((BLOCK_ROWS, ATTN_DIM), F32),
        pltpu.VMEM((BLOCK_ROWS + 16, CONV_DIM), F32),
        pltpu.VMEM((BLOCK_ROWS, CONV_DIM), F32),
        pltpu.VMEM((2, N_KV, n_keys, n_q_cols), F32),
        pltpu.VMEM((N_KV, n_keys, n_q_cols), BF16),
        pltpu.VMEM((2, N_KV, 1, n_q_cols), F32),
    ]
    kern = functools.partial(_mixer_kernel, seq_len=seq_len, ctx_len=ctx_len,
                             use_rope=use_rope, emit_kv=emit_kv)
    return pl.pallas_call(
        kern,
        grid=(n_blocks,),
        in_specs=in_specs,
        out_specs=out_specs,
        out_shape=out_shape,
        scratch_shapes=scratch,
        compiler_params=pltpu.CompilerParams(
            dimension_semantics=("arbitrary",), vmem_limit_bytes=VMEM_LIMIT),
        name="mixer_ctx" if emit_kv else "mixer_latent",
    )(*args)


def _ffn_kernel(x_ref, mod_ref, nffn_ref, wgu_ref, wd_ref, out_ref):
    mod = mod_ref[...]
    shift = mod[:, 0:D_MODEL]
    scale1 = 1.0 + mod[:, D_MODEL:2 * D_MODEL]
    gate = mod[:, 2 * D_MODEL:3 * D_MODEL]
    for r0 in range(0, FFN_ROWS, FFN_SUB_ROWS):
        x = x_ref[r0:r0 + FFN_SUB_ROWS, :]
        hb = ((_rms_rows(x) * nffn_ref[...]) * scale1 + shift).astype(BF16)
        acc = None
        for c0, cw in FF_CHUNKS:
            gt = jnp.dot(hb, wgu_ref[:, c0:c0 + cw], preferred_element_type=F32)
            up = jnp.dot(hb, wgu_ref[:, D_FF + c0:D_FF + c0 + cw], preferred_element_type=F32)
            act = ((gt * jax.nn.sigmoid(gt)) * up).astype(BF16)
            part = jnp.dot(act, wd_ref[c0:c0 + cw, :], preferred_element_type=F32)
            acc = part if acc is None else acc + part
        out_ref[r0:r0 + FFN_SUB_ROWS, :] = x + gate * acc


def _ffn_call(x_rows, mod3, mod_row_of_tile, norm_ffn, w_gate_up, w_down, name):
    n_rows = x_rows.shape[0]
    return pl.pallas_call(
        _ffn_kernel,
        grid=(n_rows // FFN_ROWS,),
        in_specs=[
            pl.BlockSpec((FFN_ROWS, D_MODEL), lambda i: (i, 0)),
            pl.BlockSpec((None, 1, 3 * D_MODEL), lambda i: (mod_row_of_tile(i), 0, 1)),
            _const_spec(norm_ffn.shape),
            _const_spec(w_gate_up.shape),
            _const_spec(w_down.shape),
        ],
        out_specs=pl.BlockSpec((FFN_ROWS, D_MODEL), lambda i: (i, 0)),
        out_shape=jax.ShapeDtypeStruct((n_rows, D_MODEL), F32),
        compiler_params=pltpu.CompilerParams(
            dimension_semantics=("arbitrary",), vmem_limit_bytes=VMEM_LIMIT),
        name=name,
    )(x_rows, mod3, norm_ffn, w_gate_up, w_down)


def _rope_tables(n_tokens):
    rows = n_tokens // GRID_W
    row = jnp.repeat(jnp.arange(rows, dtype=F32), GRID_W)
    col = jnp.tile(jnp.arange(GRID_W, dtype=F32), rows)
    inv = 1.0 / (ROPE_THETA ** (jnp.arange(ROT_PAIRS, dtype=F32) / ROT_PAIRS))
    ang = jnp.stack([row[:, None] * inv, col[:, None] * inv], axis=1)
    cos, sin = jnp.cos(ang), jnp.sin(ang)
    cos_h = jnp.concatenate([cos, cos], axis=-1).reshape(n_tokens, HEAD_DIM)
    sin_h = jnp.concatenate([-sin, sin], axis=-1).reshape(n_tokens, HEAD_DIM)
    reps = LANES // HEAD_DIM
    return jnp.tile(cos_h, (1, reps)), jnp.tile(sin_h, (1, reps))


def _group_mean_matrix():
    idx = np.arange(MXU_DIM) // HEAD_DIM
    g = (idx[:, None] == idx[None, :]).astype(np.float32) / HEAD_DIM
    return jnp.asarray(np.concatenate([g, g], axis=0), dtype=BF16)


def kernel(x_prompt, x_sample, c, cache_k, cache_v, c_ctx, norm_mix, norm_ffn, w_ada, b_ada,
           w_in, q_norm, k_norm, conv_w, attn_out_norm, conv_out_norm, w_out, w_gate_up, w_down):
    depth = w_in.shape[0]
    assert depth == 1
    n_prompt, seq, _ = x_prompt.shape
    n_sample, dec_seq, _ = x_sample.shape
    assert dec_seq == BLOCK_ROWS and BLOCK_ROWS % seq == 0 and n_sample <= CTX_ROW

    cond = jnp.zeros((COND_ROWS, D_MODEL), F32)
    cond = cond.at[0:n_sample].set(c).at[CTX_ROW].set(c_ctx)
    mod = _ada_call(cond, w_ada[0], b_ada[0][None, :])
    mod3 = mod.reshape(COND_ROWS, 1, 6 * D_MODEL)

    consts = (
        norm_mix[0][None, :],
        w_in[0].astype(BF16),
        _group_mean_matrix(),
        jnp.concatenate([jnp.tile(q_norm[0], N_HEADS), jnp.tile(k_norm[0], N_KV)])[None, :],
        conv_w[0],
        attn_out_norm[0][None, :],
        conv_out_norm[0][None, :],
        w_out[0].astype(BF16),
    )
    nffn = norm_ffn[0][None, :]
    wgu = w_gate_up[0].astype(BF16)
    wd = w_down[0].astype(BF16)

    per_block = BLOCK_ROWS // seq
    xp_blocks = x_prompt.reshape(n_prompt // per_block, BLOCK_ROWS, D_MODEL)
    xp1, k_new, v_new = _mixer_call(xp_blocks, mod3, lambda b: CTX_ROW, None, None, None, consts,
                                    seq_len=seq, emit_kv=True)
    yp = _ffn_call(xp1.reshape(-1, D_MODEL), mod3, lambda i: CTX_ROW, nffn, wgu, wd, "ffn_ctx")

    (xs1,) = _mixer_call(x_sample, mod3, lambda b: b, cache_k, cache_v, _rope_tables(dec_seq),
                         consts, seq_len=dec_seq, emit_kv=False)
    tiles_per_seq = dec_seq // FFN_ROWS
    ys = _ffn_call(xs1.reshape(-1, D_MODEL), mod3, lambda i: i // tiles_per_seq, nffn, wgu, wd,
                   "ffn_latent")

    return (yp.reshape(n_prompt, seq, D_MODEL),
            ys.reshape(n_sample, dec_seq, D_MODEL),
            k_new, v_new)
```

```python
import functools

import numpy as np
import jax
import jax.numpy as jnp
from jax import lax
from jax.experimental import pallas as pl
from jax.experimental.pallas import tpu as pltpu

D_MODEL = 1024
HEAD_DIM = 64
ATTN_DIM = 512
N_HEADS = 8
N_KV = 2
KV_GROUP = N_HEADS // N_KV
KV_DIM = N_KV * HEAD_DIM
CONV_DIM = 512
D_FF = 2816
QK_DIM = ATTN_DIM + KV_DIM
QKV_DIM = ATTN_DIM + 2 * KV_DIM
IN_DIM = QKV_DIM + 3 * CONV_DIM
GRID_W = 64
ROT_PAIRS = HEAD_DIM // 4
ROPE_THETA = 10000.0
RMS_EPS = 1e-6
Q_SCALE = HEAD_DIM ** -0.5 * 1.4426950408889634
V_ROWS = HEAD_DIM + 16

LANES = 128
MXU_DIM = 256
BLOCK_ROWS = 1024
CHUNK_ROWS = 512
LATENT_Q_ROWS = 128
P_ROWS = 16
FFN_ROWS = 1024
FFN_SUB_ROWS = 512
FF_CHUNKS = ((0, 1024), (1024, 1024), (2048, 768))
COND_ROWS = 16
CTX_ROW = 8
ADA_COLS = 1024
VMEM_LIMIT = 56 * 1024 * 1024

F32 = jnp.float32
BF16 = jnp.bfloat16


def _const_spec(shape):
    nd = len(shape)
    return pl.BlockSpec(shape, lambda *_: (0,) * nd, pipeline_mode=pl.Buffered(1))


def _ada_kernel(cond_ref, w_ref, b_ref, out_ref):
    c = cond_ref[...]
    s = (c * jax.nn.sigmoid(c)).astype(BF16)
    out_ref[...] = jnp.dot(s, w_ref[...].astype(BF16), preferred_element_type=F32) + b_ref[...]


def _ada_call(cond, w_ada, b_ada):
    n = w_ada.shape[1]
    return pl.pallas_call(
        _ada_kernel,
        grid=(n // ADA_COLS,),
        in_specs=[
            pl.BlockSpec((COND_ROWS, D_MODEL), lambda j: (0, 0)),
            pl.BlockSpec((D_MODEL, ADA_COLS), lambda j: (0, j)),
            pl.BlockSpec((1, ADA_COLS), lambda j: (0, j)),
        ],
        out_specs=pl.BlockSpec((COND_ROWS, ADA_COLS), lambda j: (0, j)),
        out_shape=jax.ShapeDtypeStruct((COND_ROWS, n), F32),
        compiler_params=pltpu.CompilerParams(
            dimension_semantics=("arbitrary",), vmem_limit_bytes=VMEM_LIMIT),
        name="ada_rows",
    )(cond, w_ada, b_ada)


def _rms_rows(x):
    return x * lax.rsqrt(jnp.mean(x * x, axis=-1, keepdims=True) + RMS_EPS)


def _mixer_kernel(*refs, seq_len, ctx_len, use_rope, emit_kv):
    it = iter(refs)
    x_ref, mod_ref = next(it), next(it)
    if ctx_len:
        ck_ref, cv_ref = next(it), next(it)
    if use_rope:
        cos_ref, sin_ref = next(it), next(it)
    (nmix_ref, win_ref, gsum_ref, qkg_ref, convw_ref, ga_ref, gc_ref, wout_ref) = (
        next(it) for _ in range(8))
    out_ref = next(it)
    if emit_kv:
        ko_ref, vo_ref = next(it), next(it)
    (q_s, k_s, vt_s, attn_s, t_s, gb_s, s_ref, p_ref, m_ref) = (next(it) for _ in range(9))

    n_chunks = BLOCK_ROWS // CHUNK_ROWS
    if ctx_len:
        q_rows, n_keys, key_blk = LATENT_Q_ROWS, ctx_len + seq_len, BLOCK_ROWS
    else:
        q_rows, n_keys, key_blk = seq_len, seq_len, seq_len
    n_qb = BLOCK_ROWS // q_rows

    mod = mod_ref[...]
    shift = mod[:, 0:D_MODEL]
    scale1 = 1.0 + mod[:, D_MODEL:2 * D_MODEL]
    gate = mod[:, 2 * D_MODEL:3 * D_MODEL]

    def put_values_t(blk, off, vt):
        vt = vt.astype(BF16)
        for g in range(N_KV):
            vt_s[blk, g, 0:HEAD_DIM, off:off + vt.shape[1]] = vt[g * HEAD_DIM:(g + 1) * HEAD_DIM, :]

    tail = (lax.broadcasted_iota(jnp.int32, (V_ROWS - HEAD_DIM, n_keys), 0) == 0).astype(BF16)
    for blk in range(vt_s.shape[0]):
        for g in range(N_KV):
            vt_s[blk, g, HEAD_DIM:V_ROWS, :] = tail

    if ctx_len:
        ck = ck_ref[...].reshape(KV_DIM, ctx_len).T.astype(BF16)
        for g in range(N_KV):
            k_s[g, 0:ctx_len, :] = ck[:, g * HEAD_DIM:(g + 1) * HEAD_DIM]
        put_values_t(0, 0, cv_ref[...].reshape(KV_DIM, ctx_len))

    if use_rope:
        lane = lax.broadcasted_iota(jnp.int32, (CHUNK_ROWS, LANES), 1)
        first_half = (lane % (2 * ROT_PAIRS)) < ROT_PAIRS

    for c in range(n_chunks):
        r0 = c * CHUNK_ROWS
        x = x_ref[r0:r0 + CHUNK_ROWS, :]
        h = (_rms_rows(x) * nmix_ref[...]) * scale1 + shift
        hb = h.astype(BF16)
        qkv = jnp.dot(hb, win_ref[:, 0:QKV_DIM], preferred_element_type=F32)

        groups = []
        for g0 in range(0, QKV_DIM, MXU_DIM):
            sq = qkv[:, g0:g0 + MXU_DIM]
            sq = sq * sq
            hi = sq.astype(BF16)
            lo = (sq - hi.astype(F32)).astype(BF16)
            groups.append(jnp.dot(jnp.concatenate([hi, lo], axis=-1), gsum_ref[...],
                                  preferred_element_type=F32))
        ms = jnp.concatenate(groups, axis=-1)[:, 0:QK_DIM]
        qk = (qkv[:, 0:QK_DIM] * lax.rsqrt(ms + RMS_EPS)) * qkg_ref[...]
        vv = qkv[:, QK_DIM:QKV_DIM]

        if emit_kv:
            for r1 in range(0, CHUNK_ROWS, seq_len):
                kt = qk[r1:r1 + seq_len, ATTN_DIM:QK_DIM].T
                ko_ref[(r0 + r1) // seq_len] = kt.reshape(N_KV, HEAD_DIM, seq_len)

        for cg in range(QK_DIM // LANES):
            xg = qk[:, cg * LANES:(cg + 1) * LANES]
            if use_rope:
                cs = cos_ref[r0:r0 + CHUNK_ROWS, :]
                sn = sin_ref[r0:r0 + CHUNK_ROWS, :]
                partner = jnp.where(first_half,
                                    pltpu.roll(xg, LANES - ROT_PAIRS, axis=1),
                                    pltpu.roll(xg, ROT_PAIRS, axis=1))
                xg = xg * cs + partner * sn
            if cg < ATTN_DIM // LANES:
                xb = (xg * Q_SCALE).astype(BF16)
                q_s[2 * cg, r0:r0 + CHUNK_ROWS, :] = xb[:, 0:HEAD_DIM]
                q_s[2 * cg + 1, r0:r0 + CHUNK_ROWS, :] = xb[:, HEAD_DIM:LANES]
            else:
                xb = xg.astype(BF16)
                for g in range(N_KV):
                    k_s[g, ctx_len + r0:ctx_len + r0 + CHUNK_ROWS, :] = (
                        xb[:, g * HEAD_DIM:(g + 1) * HEAD_DIM])
        w = min(key_blk, CHUNK_ROWS)
        for r1 in range(r0, r0 + CHUNK_ROWS, w):
            blk, off = (0, ctx_len + r1) if ctx_len else (r1 // key_blk, 0)
            vt = vv[r1 - r0:r1 - r0 + w, :].T
            put_values_t(blk, off, vt)
            if emit_kv:
                vo_ref[blk] = vt.reshape(N_KV, HEAD_DIM, seq_len)

        cvp = jnp.dot(hb, win_ref[:, QKV_DIM:IN_DIM], preferred_element_type=F32)
        gb_s[r0:r0 + CHUNK_ROWS, :] = cvp[:, 0:CONV_DIM]
        t_s[8 + r0:8 + r0 + CHUNK_ROWS, :] = (cvp[:, CONV_DIM:2 * CONV_DIM]
                                              * cvp[:, 2 * CONV_DIM:3 * CONV_DIM])
    t_s[0:8, :] = jnp.zeros((8, CONV_DIM), F32)
    t_s[8 + BLOCK_ROWS:16 + BLOCK_ROWS, :] = jnp.zeros((8, CONV_DIM), F32)

    def stage_a(qb, g, par):
        r0 = pl.multiple_of(qb * q_rows, q_rows)
        k0 = 0 if ctx_len else r0
        qs = jnp.concatenate(
            [q_s[KV_GROUP * g + j, pl.ds(r0, q_rows), :] for j in range(KV_GROUP)], axis=0)
        kk = k_s[g, pl.ds(k0, n_keys), :]
        s = lax.dot_general(kk, qs, (((1,), (1,)), ((), ())), preferred_element_type=F32)
        s_ref[par, g] = s
        m_ref[par, g] = jnp.max(s, axis=0, keepdims=True)

    def stage_b(g, par):
        m = m_ref[par, g]
        for k1 in range(0, n_keys, P_ROWS):
            p_ref[g, k1:k1 + P_ROWS, :] = jnp.exp2(
                s_ref[par, g, k1:k1 + P_ROWS, :] - m).astype(BF16)

    def stage_c(qb, g, par):
        r0 = pl.multiple_of(qb * q_rows, q_rows)
        blk = 0 if ctx_len else qb
        ot = jnp.dot(vt_s[blk, g], p_ref[g], preferred_element_type=F32)
        ot = ot[0:HEAD_DIM, :] / ot[HEAD_DIM:HEAD_DIM + 1, :]
        for jj in range(KV_GROUP // 2):
            pair_t = jnp.concatenate(
                [ot[:, (2 * jj) * q_rows:(2 * jj + 1) * q_rows],
                 ot[:, (2 * jj + 1) * q_rows:(2 * jj + 2) * q_rows]], axis=0)
            col = (KV_GROUP * g + 2 * jj) * HEAD_DIM
            attn_s[pl.ds(r0, q_rows), col:col + LANES] = pair_t.T

    def step(j, par, do_a=True, do_bc=True):
        for g in range(N_KV):
            if do_a:
                stage_a(j + 1, g, 1 - par)
            if do_bc:
                stage_b(g, par)
                stage_c(j, g, par)

    step(-1, 1, do_bc=False)

    def attn_step(j, carry):
        for par in range(2):
            pl.when(j % 2 == par)(functools.partial(step, j, par))
        return carry

    lax.fori_loop(0, n_qb - 1, attn_step, 0)
    step(n_qb - 1, (n_qb - 1) % 2, do_a=False)

    w0 = convw_ref[0:1, :]
    w1 = convw_ref[1:2, :]
    w2 = convw_ref[2:3, :]
    for c in range(n_chunks):
        r0 = c * CHUNK_ROWS
        pos = (lax.broadcasted_iota(jnp.int32, (CHUNK_ROWS, 1), 0) + r0) % seq_len
        t_prev = jnp.where(pos == 0, 0.0, t_s[7 + r0:7 + r0 + CHUNK_ROWS, :])
        t_mid = t_s[8 + r0:8 + r0 + CHUNK_ROWS, :]
        t_next = jnp.where(pos == seq_len - 1, 0.0, t_s[9 + r0:9 + r0 + CHUNK_ROWS, :])
        y = gb_s[r0:r0 + CHUNK_ROWS, :] * (w0 * t_prev + w1 * t_mid + w2 * t_next)
        yn = _rms_rows(y) * gc_ref[...]
        an = _rms_rows(attn_s[r0:r0 + CHUNK_ROWS, :]) * ga_ref[...]
        merged = jnp.concatenate([an, yn], axis=-1).astype(BF16)
        mix = jnp.dot(merged, wout_ref[...], preferred_element_type=F32)
        out_ref[r0:r0 + CHUNK_ROWS, :] = x_ref[r0:r0 + CHUNK_ROWS, :] + gate * mix


def _mixer_call(x_blocks, mod3, mod_row_of_block, ctx_k, ctx_v, rope, consts, *, seq_len,
                emit_kv):
    n_blocks = x_blocks.shape[0]
    ctx_len = 0 if ctx_k is None else ctx_k.shape[4]
    use_rope = rope is not None
    if ctx_len:
        assert seq_len == BLOCK_ROWS
        q_rows, n_keys, n_key_blocks = LATENT_Q_ROWS, ctx_len + BLOCK_ROWS, 1
    else:
        q_rows, n_keys, n_key_blocks = seq_len, seq_len, BLOCK_ROWS // seq_len
    n_q_cols = KV_GROUP * q_rows

    blk = lambda cols: pl.BlockSpec((None, BLOCK_ROWS, cols), lambda b: (b, 0, 0))
    args = [x_blocks, mod3]
    in_specs = [blk(D_MODEL),
                pl.BlockSpec((None, 1, 3 * D_MODEL), lambda b: (mod_row_of_block(b), 0, 0))]
    if ctx_len:
        args += [ctx_k, ctx_v]
        in_specs += [pl.BlockSpec((None, None, N_KV, HEAD_DIM, ctx_len),
                                  lambda b: (b, 0, 0, 0, 0))] * 2
    if use_rope:
        args += list(rope)
        in_specs += [_const_spec((BLOCK_ROWS, LANES))] * 2
    args += list(consts)
    in_specs += [_const_spec(a.shape) for a in consts]

    out_shape = [jax.ShapeDtypeStruct((n_blocks, BLOCK_ROWS, D_MODEL), F32)]
    out_specs = [blk(D_MODEL)]
    if emit_kv:
        per_block = BLOCK_ROWS // seq_len
        out_shape += [jax.ShapeDtypeStruct(
            (n_blocks * per_block, 1, N_KV, HEAD_DIM, seq_len), F32)] * 2
        out_specs += [pl.BlockSpec((per_block, None, N_KV, HEAD_DIM, seq_len),
                                   lambda b: (b, 0, 0, 0, 0))] * 2

    scratch = [
        pltpu.VMEM((N_HEADS, BLOCK_ROWS, HEAD_DIM), BF16),
        pltpu.VMEM((N_KV, ctx_len + BLOCK_ROWS, HEAD_DIM), BF16),
        pltpu.VMEM((n_key_blocks, N_KV, V_ROWS, n_keys), BF16),
        pltpu.VMEM((BLOCK_ROWS, ATTN_DIM), F32),
        pltpu.VMEM((BLOCK_ROWS + 16, CONV_DIM), F32),
        pltpu.VMEM((BLOCK_ROWS, CONV_DIM), F32),
        pltpu.VMEM((2, N_KV, n_keys, n_q_cols), F32),
        pltpu.VMEM((N_KV, n_keys, n_q_cols), BF16),
        pltpu.VMEM((2, N_KV, 1, n_q_cols), F32),
    ]
    kern = functools.partial(_mixer_kernel, seq_len=seq_len, ctx_len=ctx_len,
                             use_rope=use_rope, emit_kv=emit_kv)
    return pl.pallas_call(
        kern,
        grid=(n_blocks,),
        in_specs=in_specs,
        out_specs=out_specs,
        out_shape=out_shape,
        scratch_shapes=scratch,
        compiler_params=pltpu.CompilerParams(
            dimension_semantics=("arbitrary",), vmem_limit_bytes=VMEM_LIMIT),
        name="mixer_ctx" if emit_kv else "mixer_latent",
    )(*args)


def _ffn_kernel(x_ref, mod_ref, nffn_ref, wgu_ref, wd_ref, out_ref):
    mod = mod_ref[...]
    shift = mod[:, 0:D_MODEL]
    scale1 = 1.0 + mod[:, D_MODEL:2 * D_MODEL]
    gate = mod[:, 2 * D_MODEL:3 * D_MODEL]
    for r0 in range(0, FFN_ROWS, FFN_SUB_ROWS):
        x = x_ref[r0:r0 + FFN_SUB_ROWS, :]
        hb = ((_rms_rows(x) * nffn_ref[...]) * scale1 + shift).astype(BF16)
        acc = None
        for c0, cw in FF_CHUNKS:
            gt = jnp.dot(hb, wgu_ref[:, c0:c0 + cw], preferred_element_type=F32)
            up = jnp.dot(hb, wgu_ref[:, D_FF + c0:D_FF + c0 + cw], preferred_element_type=F32)
            act = ((gt * jax.nn.sigmoid(gt)) * up).astype(BF16)
            part = jnp.dot(act, wd_ref[c0:c0 + cw, :], preferred_element_type=F32)
            acc = part if acc is None else acc + part
        out_ref[r0:r0 + FFN_SUB_ROWS, :] = x + gate * acc


def _ffn_call(x_rows, mod3, mod_row_of_tile, norm_ffn, w_gate_up, w_down, name):
    n_rows = x_rows.shape[0]
    return pl.pallas_call(
        _ffn_kernel,
        grid=(n_rows // FFN_ROWS,),
        in_specs=[
            pl.BlockSpec((FFN_ROWS, D_MODEL), lambda i: (i, 0)),
            pl.BlockSpec((None, 1, 3 * D_MODEL), lambda i: (mod_row_of_tile(i), 0, 1)),
            _const_spec(norm_ffn.shape),
            _const_spec(w_gate_up.shape),
            _const_spec(w_down.shape),
        ],
        out_specs=pl.BlockSpec((FFN_ROWS, D_MODEL), lambda i: (i, 0)),
        out_shape=jax.ShapeDtypeStruct((n_rows, D_MODEL), F32),
        compiler_params=pltpu.CompilerParams(
            dimension_semantics=("arbitrary",), vmem_limit_bytes=VMEM_LIMIT),
        name=name,
    )(x_rows, mod3, norm_ffn, w_gate_up, w_down)


def _rope_tables(n_tokens):
    rows = n_tokens // GRID_W
    row = jnp.repeat(jnp.arange(rows, dtype=F32), GRID_W)
    col = jnp.tile(jnp.arange(GRID_W, dtype=F32), rows)
    inv = 1.0 / (ROPE_THETA ** (jnp.arange(ROT_PAIRS, dtype=F32) / ROT_PAIRS))
    ang = jnp.stack([row[:, None] * inv, col[:, None] * inv], axis=1)
    cos, sin = jnp.cos(ang), jnp.sin(ang)
    cos_h = jnp.concatenate([cos, cos], axis=-1).reshape(n_tokens, HEAD_DIM)
    sin_h = jnp.concatenate([-sin, sin], axis=-1).reshape(n_tokens, HEAD_DIM)
    reps = LANES // HEAD_DIM
    return jnp.tile(cos_h, (1, reps)), jnp.tile(sin_h, (1, reps))


def _keys_minor(kv):
    return jnp.transpose(kv, (0, 1, 3, 4, 2))


def _group_mean_matrix():
    idx = np.arange(MXU_DIM) // HEAD_DIM
    g = (idx[:, None] == idx[None, :]).astype(np.float32) / HEAD_DIM
    return jnp.asarray(np.concatenate([g, g], axis=0), dtype=BF16)


def kernel(x_prompt, x_sample, c, cache_k, cache_v, c_ctx, norm_mix, norm_ffn, w_ada, b_ada,
           w_in, q_norm, k_norm, conv_w, attn_out_norm, conv_out_norm, w_out, w_gate_up, w_down):
    depth = w_in.shape[0]
    assert depth == 1
    n_prompt, seq, _ = x_prompt.shape
    n_sample, dec_seq, _ = x_sample.shape
    assert dec_seq == BLOCK_ROWS and BLOCK_ROWS % seq == 0 and n_sample <= CTX_ROW

    cond = jnp.zeros((COND_ROWS, D_MODEL), F32)
    cond = cond.at[0:n_sample].set(c).at[CTX_ROW].set(c_ctx)
    mod = _ada_call(cond, w_ada[0], b_ada[0][None, :])
    mod3 = mod.reshape(COND_ROWS, 1, 6 * D_MODEL)

    consts = (
        norm_mix[0][None, :],
        w_in[0].astype(BF16),
        _group_mean_matrix(),
        jnp.concatenate([jnp.tile(q_norm[0], N_HEADS), jnp.tile(k_norm[0], N_KV)])[None, :],
        conv_w[0],
        attn_out_norm[0][None, :],
        conv_out_norm[0][None, :],
        w_out[0].astype(BF16),
    )
    nffn = norm_ffn[0][None, :]
    wgu = w_gate_up[0].astype(BF16)
    wd = w_down[0].astype(BF16)

    per_block = BLOCK_ROWS // seq
    xp_blocks = x_prompt.reshape(n_prompt // per_block, BLOCK_ROWS, D_MODEL)
    xp1, k_new, v_new = _mixer_call(xp_blocks, mod3, lambda b: CTX_ROW, None, None, None, consts,
                                    seq_len=seq, emit_kv=True)
    yp = _ffn_call(xp1.reshape(-1, D_MODEL), mod3, lambda i: CTX_ROW, nffn, wgu, wd, "ffn_ctx")

    (xs1,) = _mixer_call(x_sample, mod3, lambda b: b, _keys_minor(cache_k), _keys_minor(cache_v),
                         _rope_tables(dec_seq), consts, seq_len=dec_seq, emit_kv=False)
    tiles_per_seq = dec_seq // FFN_ROWS
    ys = _ffn_call(xs1.reshape(-1, D_MODEL), mod3, lambda i: i // tiles_per_seq, nffn, wgu, wd,
                   "ffn_latent")

    return (yp.reshape(n_prompt, seq, D_MODEL),
            ys.reshape(n_sample, dec_seq, D_MODEL),
            jnp.transpose(k_new, (0, 1, 4, 2, 3)),
            jnp.transpose(v_new, (0, 1, 4, 2, 3)))
```

```python
import functools

import numpy as np
import jax
import jax.numpy as jnp
from jax import lax
from jax.experimental import pallas as pl
from jax.experimental.pallas import tpu as pltpu

D_MODEL = 1024
HEAD_DIM = 64
ATTN_DIM = 512
N_HEADS = 8
N_KV = 2
KV_GROUP = N_HEADS // N_KV
KV_DIM = N_KV * HEAD_DIM
CONV_DIM = 512
D_FF = 2816
QK_DIM = ATTN_DIM + KV_DIM
QKV_DIM = ATTN_DIM + 2 * KV_DIM
IN_DIM = QKV_DIM + 3 * CONV_DIM
GRID_W = 64
ROT_PAIRS = HEAD_DIM // 4
ROPE_THETA = 10000.0
RMS_EPS = 1e-6
Q_SCALE = HEAD_DIM ** -0.5 * 1.4426950408889634
V_ROWS = HEAD_DIM + 16
MAX_UNSHIFTED_SCORE = 64.0

LANES = 128
MXU_DIM = 256
BLOCK_ROWS = 1024
CHUNK_ROWS = 512
LATENT_Q_ROWS = 128
P_ROWS = 16
FFN_ROWS = 1024
FFN_SUB_ROWS = 512
FF_CHUNKS = ((0, 1024), (1024, 1024), (2048, 768))
COND_ROWS = 16
CTX_ROW = 8
ADA_COLS = 1024
VMEM_LIMIT = 56 * 1024 * 1024

F32 = jnp.float32
BF16 = jnp.bfloat16


def _const_spec(shape):
    nd = len(shape)
    return pl.BlockSpec(shape, lambda *_: (0,) * nd, pipeline_mode=pl.Buffered(1))


def _ada_kernel(cond_ref, w_ref, b_ref, out_ref):
    c = cond_ref[...]
    s = (c * jax.nn.sigmoid(c)).astype(BF16)
    out_ref[...] = jnp.dot(s, w_ref[...].astype(BF16), preferred_element_type=F32) + b_ref[...]


def _ada_call(cond, w_ada, b_ada):
    n = w_ada.shape[1]
    return pl.pallas_call(
        _ada_kernel,
        grid=(n // ADA_COLS,),
        in_specs=[
            pl.BlockSpec((COND_ROWS, D_MODEL), lambda j: (0, 0)),
            pl.BlockSpec((D_MODEL, ADA_COLS), lambda j: (0, j)),
            pl.BlockSpec((1, ADA_COLS), lambda j: (0, j)),
        ],
        out_specs=pl.BlockSpec((COND_ROWS, ADA_COLS), lambda j: (0, j)),
        out_shape=jax.ShapeDtypeStruct((COND_ROWS, n), F32),
        compiler_params=pltpu.CompilerParams(
            dimension_semantics=("arbitrary",), vmem_limit_bytes=VMEM_LIMIT),
        name="ada_rows",
    )(cond, w_ada, b_ada)


def _rms_rows(x):
    return x * lax.rsqrt(jnp.mean(x * x, axis=-1, keepdims=True) + RMS_EPS)


def _mixer_kernel(*refs, seq_len, ctx_len, use_rope, emit_kv):
    it = iter(refs)
    x_ref, mod_ref = next(it), next(it)
    if ctx_len:
        ck_ref, cv_ref = next(it), next(it)
    if use_rope:
        cos_ref, sin_ref = next(it), next(it)
    (nmix_ref, win_ref, gsum_ref, qkg_ref, convw_ref, ga_ref, gc_ref, wout_ref) = (
        next(it) for _ in range(8))
    out_ref = next(it)
    if emit_kv:
        ko_ref, vo_ref = next(it), next(it)
    (q_s, k_s, vt_s, attn_s, t_s, gb_s, s_ref, p_ref, m_ref) = (next(it) for _ in range(9))

    n_chunks = BLOCK_ROWS // CHUNK_ROWS
    if ctx_len:
        q_rows, n_keys, key_blk = LATENT_Q_ROWS, ctx_len + seq_len, BLOCK_ROWS
    else:
        q_rows, n_keys, key_blk = seq_len, seq_len, seq_len
    n_qb = BLOCK_ROWS // q_rows

    mod = mod_ref[...]
    shift = mod[:, 0:D_MODEL]
    scale1 = 1.0 + mod[:, D_MODEL:2 * D_MODEL]
    gate = mod[:, 2 * D_MODEL:3 * D_MODEL]

    def put_values_t(blk, off, vt):
        vt = vt.astype(BF16)
        for g in range(N_KV):
            vt_s[blk, g, 0:HEAD_DIM, off:off + vt.shape[1]] = vt[g * HEAD_DIM:(g + 1) * HEAD_DIM, :]

    tail = (lax.broadcasted_iota(jnp.int32, (V_ROWS - HEAD_DIM, n_keys), 0) == 0).astype(BF16)
    for blk in range(vt_s.shape[0]):
        for g in range(N_KV):
            vt_s[blk, g, HEAD_DIM:V_ROWS, :] = tail

    if ctx_len:
        ck = ck_ref[...].reshape(KV_DIM, ctx_len).T.astype(BF16)
        for g in range(N_KV):
            k_s[g, 0:ctx_len, :] = ck[:, g * HEAD_DIM:(g + 1) * HEAD_DIM]
        put_values_t(0, 0, cv_ref[...].reshape(KV_DIM, ctx_len))

    if use_rope:
        lane = lax.broadcasted_iota(jnp.int32, (CHUNK_ROWS, LANES), 1)
        first_half = (lane % (2 * ROT_PAIRS)) < ROT_PAIRS

    for c in range(n_chunks):
        r0 = c * CHUNK_ROWS
        x = x_ref[r0:r0 + CHUNK_ROWS, :]
        h = (_rms_rows(x) * nmix_ref[...]) * scale1 + shift
        hb = h.astype(BF16)
        qkv = jnp.dot(hb, win_ref[:, 0:QKV_DIM], preferred_element_type=F32)

        groups = []
        for g0 in range(0, QKV_DIM, MXU_DIM):
            sq = qkv[:, g0:g0 + MXU_DIM]
            sq = sq * sq
            hi = sq.astype(BF16)
            lo = (sq - hi.astype(F32)).astype(BF16)
            groups.append(jnp.dot(jnp.concatenate([hi, lo], axis=-1), gsum_ref[...],
                                  preferred_element_type=F32))
        ms = jnp.concatenate(groups, axis=-1)[:, 0:QK_DIM]
        qk = (qkv[:, 0:QK_DIM] * lax.rsqrt(ms + RMS_EPS)) * qkg_ref[...]
        vv = qkv[:, QK_DIM:QKV_DIM]

        if emit_kv:
            for r1 in range(0, CHUNK_ROWS, seq_len):
                kt = qk[r1:r1 + seq_len, ATTN_DIM:QK_DIM].T
                ko_ref[(r0 + r1) // seq_len] = kt.reshape(N_KV, HEAD_DIM, seq_len)

        for cg in range(QK_DIM // LANES):
            xg = qk[:, cg * LANES:(cg + 1) * LANES]
            if use_rope:
                cs = cos_ref[r0:r0 + CHUNK_ROWS, :]
                sn = sin_ref[r0:r0 + CHUNK_ROWS, :]
                partner = jnp.where(first_half,
                                    pltpu.roll(xg, LANES - ROT_PAIRS, axis=1),
                                    pltpu.roll(xg, ROT_PAIRS, axis=1))
                xg = xg * cs + partner * sn
            if cg < ATTN_DIM // LANES:
                xb = (xg * Q_SCALE).astype(BF16)
                q_s[2 * cg, r0:r0 + CHUNK_ROWS, :] = xb[:, 0:HEAD_DIM]
                q_s[2 * cg + 1, r0:r0 + CHUNK_ROWS, :] = xb[:, HEAD_DIM:LANES]
            else:
                xb = xg.astype(BF16)
                for g in range(N_KV):
                    k_s[g, ctx_len + r0:ctx_len + r0 + CHUNK_ROWS, :] = (
                        xb[:, g * HEAD_DIM:(g + 1) * HEAD_DIM])
        w = min(key_blk, CHUNK_ROWS)
        for r1 in range(r0, r0 + CHUNK_ROWS, w):
            blk, off = (0, ctx_len + r1) if ctx_len else (r1 // key_blk, 0)
            vt = vv[r1 - r0:r1 - r0 + w, :].T
            put_values_t(blk, off, vt)
            if emit_kv:
                vo_ref[blk] = vt.reshape(N_KV, HEAD_DIM, seq_len)

        cvp = jnp.dot(hb, win_ref[:, QKV_DIM:IN_DIM], preferred_element_type=F32)
        gb_s[r0:r0 + CHUNK_ROWS, :] = cvp[:, 0:CONV_DIM]
        t_s[8 + r0:8 + r0 + CHUNK_ROWS, :] = (cvp[:, CONV_DIM:2 * CONV_DIM]
                                              * cvp[:, 2 * CONV_DIM:3 * CONV_DIM])
    t_s[0:8, :] = jnp.zeros((8, CONV_DIM), F32)
    t_s[8 + BLOCK_ROWS:16 + BLOCK_ROWS, :] = jnp.zeros((8, CONV_DIM), F32)

    def scores_t(qb, g):
        r0 = pl.multiple_of(qb * q_rows, q_rows)
        k0 = 0 if ctx_len else r0
        qs = jnp.concatenate(
            [q_s[KV_GROUP * g + j, pl.ds(r0, q_rows), :] for j in range(KV_GROUP)], axis=0)
        kk = k_s[g, pl.ds(k0, n_keys), :]
        return lax.dot_general(kk, qs, (((1,), (1,)), ((), ())),
                               preferred_element_type=F32)

    def values_out(qb, g, p):
        r0 = pl.multiple_of(qb * q_rows, q_rows)
        blk = 0 if ctx_len else qb
        ot = jnp.dot(vt_s[blk, g], p, preferred_element_type=F32)
        ot = ot[0:HEAD_DIM, :] / ot[HEAD_DIM:HEAD_DIM + 1, :]
        for jj in range(KV_GROUP // 2):
            pair_t = jnp.concatenate(
                [ot[:, (2 * jj) * q_rows:(2 * jj + 1) * q_rows],
                 ot[:, (2 * jj + 1) * q_rows:(2 * jj + 2) * q_rows]], axis=0)
            col = (KV_GROUP * g + 2 * jj) * HEAD_DIM
            attn_s[pl.ds(r0, q_rows), col:col + LANES] = pair_t.T

    def pipeline(step):
        step(-1, 1, first=True)

        def body(j, carry):
            for par in range(2):
                pl.when(j % 2 == par)(functools.partial(step, j, par))
            return carry

        lax.fori_loop(0, n_qb - 1, body, 0)
        step(n_qb - 1, (n_qb - 1) % 2, last=True)

    def pipeline_pairs(step):
        assert n_qb % 2 == 0
        step(-1, 1, first=True)
        step(0, 0)

        def body(t, carry):
            step(2 * t + 1, 1)
            step(2 * t + 2, 0)
            return carry

        lax.fori_loop(0, (n_qb - 2) // 2, body, 0)
        step(n_qb - 1, 1, last=True)

    def shifted_step(j, par, first=False, last=False):
        for g in range(N_KV):
            if not last:
                s = scores_t(j + 1, g)
                s_ref[1 - par, g] = s
                m_ref[1 - par, g] = jnp.max(s, axis=0, keepdims=True)
            if not first:
                m = m_ref[par, g]
                for k1 in range(0, n_keys, P_ROWS):
                    p_ref[0, g, k1:k1 + P_ROWS, :] = jnp.exp2(
                        s_ref[par, g, k1:k1 + P_ROWS, :] - m).astype(BF16)
                values_out(j, g, p_ref[0, g])

    def unshifted_step(j, par, first=False, last=False):
        for g in range(N_KV):
            if not last:
                p_ref[1 - par, g] = jnp.exp2(scores_t(j + 1, g)).astype(BF16)
            if not first:
                values_out(j, g, p_ref[par, g])

    gains = jnp.abs(qkg_ref[...])
    q_bound = (Q_SCALE * Q_SCALE * HEAD_DIM) * jnp.max(gains[:, 0:ATTN_DIM]) ** 2
    k_bound = HEAD_DIM * jnp.max(gains[:, ATTN_DIM:QK_DIM]) ** 2
    if ctx_len:
        ck2 = ck_ref[...]
        k_bound = jnp.maximum(k_bound, jnp.max(jnp.sum(ck2 * ck2, axis=1)))
    small_scores = q_bound * k_bound <= MAX_UNSHIFTED_SCORE ** 2
    pl.when(small_scores)(lambda: pipeline_pairs(unshifted_step))
    pl.when(jnp.logical_not(small_scores))(lambda: pipeline(shifted_step))

    w0 = convw_ref[0:1, :]
    w1 = convw_ref[1:2, :]
    w2 = convw_ref[2:3, :]
    for c in range(n_chunks):
        r0 = c * CHUNK_ROWS
        pos = (lax.broadcasted_iota(jnp.int32, (CHUNK_ROWS, 1), 0) + r0) % seq_len
        t_prev = jnp.where(pos == 0, 0.0, t_s[7 + r0:7 + r0 + CHUNK_ROWS, :])
        t_mid = t_s[8 + r0:8 + r0 + CHUNK_ROWS, :]
        t_next = jnp.where(pos == seq_len - 1, 0.0, t_s[9 + r0:9 + r0 + CHUNK_ROWS, :])
        y = gb_s[r0:r0 + CHUNK_ROWS, :] * (w0 * t_prev + w1 * t_mid + w2 * t_next)
        yn = _rms_rows(y) * gc_ref[...]
        an = _rms_rows(attn_s[r0:r0 + CHUNK_ROWS, :]) * ga_ref[...]
        merged = jnp.concatenate([an, yn], axis=-1).astype(BF16)
        mix = jnp.dot(merged, wout_ref[...], preferred_element_type=F32)
        out_ref[r0:r0 + CHUNK_ROWS, :] = x_ref[r0:r0 + CHUNK_ROWS, :] + gate * mix


def _mixer_call(x_blocks, mod3, mod_row_of_block, ctx_k, ctx_v, rope, consts, *, seq_len,
                emit_kv):
    n_blocks = x_blocks.shape[0]
    ctx_len = 0 if ctx_k is None else ctx_k.shape[4]
    use_rope = rope is not None
    if ctx_len:
        assert seq_len == BLOCK_ROWS
        q_rows, n_keys, n_key_blocks = LATENT_Q_ROWS, ctx_len + BLOCK_ROWS, 1
    else:
        q_rows, n_keys, n_key_blocks = seq_len, seq_len, BLOCK_ROWS // seq_len
    n_q_cols = KV_GROUP * q_rows

    blk = lambda cols: pl.BlockSpec((None, BLOCK_ROWS, cols), lambda b: (b, 0, 0))
    args = [x_blocks, mod3]
    in_specs = [blk(D_MODEL),
                pl.BlockSpec((None, 1, 3 * D_MODEL), lambda b: (mod_row_of_block(b), 0, 0))]
    if ctx_len:
        args += [ctx_k, ctx_v]
        in_specs += [pl.BlockSpec((None, None, N_KV, HEAD_DIM, ctx_len),
                                  lambda b: (b, 0, 0, 0, 0))] * 2
    if use_rope:
        args += list(rope)
        in_specs += [_const_spec((BLOCK_ROWS, LANES))] * 2
    args += list(consts)
    in_specs += [_const_spec(a.shape) for a in consts]

    out_shape = [jax.ShapeDtypeStruct((n_blocks, BLOCK_ROWS, D_MODEL), F32)]
    out_specs = [blk(D_MODEL)]
    if emit_kv:
        per_block = BLOCK_ROWS // seq_len
        out_shape += [jax.ShapeDtypeStruct(
            (n_blocks * per_block, 1, N_KV, HEAD_DIM, seq_len), F32)] * 2
        out_specs += [pl.BlockSpec((per_block, None, N_KV, HEAD_DIM, seq_len),
                                   lambda b: (b, 0, 0, 0, 0))] * 2

    scratch = [
        pltpu.VMEM((N_HEADS, BLOCK_ROWS, HEAD_DIM), BF16),
        pltpu.VMEM((N_KV, ctx_len + BLOCK_ROWS, HEAD_DIM), BF16),
        pltpu.VMEM((n_key_blocks, N_KV, V_ROWS, n_keys), BF16),
        pltpu.VMEM((BLOCK_ROWS, ATTN_DIM), F32),
        pltpu.VMEM((BLOCK_ROWS + 16, CONV_DIM), F32),
        pltpu.VMEM((BLOCK_ROWS, CONV_DIM), F32),
        pltpu.VMEM((2, N_KV, n_keys, n_q_cols), F32),
        pltpu.VMEM((2, N_KV, n_keys, n_q_cols), BF16),
        pltpu.VMEM((2, N_KV, 1, n_q_cols), F32),
    ]
    kern = functools.partial(_mixer_kernel, seq_len=seq_len, ctx_len=ctx_len,
                             use_rope=use_rope, emit_kv=emit_kv)
    return pl.pallas_call(
        kern,
        grid=(n_blocks,),
        in_specs=in_specs,
        out_specs=out_specs,
        out_shape=out_shape,
        scratch_shapes=scratch,
        compiler_params=pltpu.CompilerParams(
            dimension_semantics=("arbitrary",), vmem_limit_bytes=VMEM_LIMIT),
        name="mixer_ctx" if emit_kv else "mixer_latent",
    )(*args)


def _ffn_kernel(x_ref, mod_ref, nffn_ref, wgu_ref, wd_ref, out_ref):
    mod = mod_ref[...]
    shift = mod[:, 0:D_MODEL]
    scale1 = 1.0 + mod[:, D_MODEL:2 * D_MODEL]
    gate = mod[:, 2 * D_MODEL:3 * D_MODEL]
    for r0 in range(0, FFN_ROWS, FFN_SUB_ROWS):
        x = x_ref[r0:r0 + FFN_SUB_ROWS, :]
        hb = ((_rms_rows(x) * nffn_ref[...]) * scale1 + shift).astype(BF16)
        acc = None
        for c0, cw in FF_CHUNKS:
            gt = jnp.dot(hb, wgu_ref[:, c0:c0 + cw], preferred_element_type=F32)
            up = jnp.dot(hb, wgu_ref[:, D_FF + c0:D_FF + c0 + cw], preferred_element_type=F32)
            act = ((gt * jax.nn.sigmoid(gt)) * up).astype(BF16)
            part = jnp.dot(act, wd_ref[c0:c0 + cw, :], preferred_element_type=F32)
            acc = part if acc is None else acc + part
        out_ref[r0:r0 + FFN_SUB_ROWS, :] = x + gate * acc


def _ffn_call(x_rows, mod3, mod_row_of_tile, norm_ffn, w_gate_up, w_down, name):
    n_rows = x_rows.shape[0]
    return pl.pallas_call(
        _ffn_kernel,
        grid=(n_rows // FFN_ROWS,),
        in_specs=[
            pl.BlockSpec((FFN_ROWS, D_MODEL), lambda i: (i, 0)),
            pl.BlockSpec((None, 1, 3 * D_MODEL), lambda i: (mod_row_of_tile(i), 0, 1)),
            _const_spec(norm_ffn.shape),
            _const_spec(w_gate_up.shape),
            _const_spec(w_down.shape),
        ],
        out_specs=pl.BlockSpec((FFN_ROWS, D_MODEL), lambda i: (i, 0)),
        out_shape=jax.ShapeDtypeStruct((n_rows, D_MODEL), F32),
        compiler_params=pltpu.CompilerParams(
            dimension_semantics=("arbitrary",), vmem_limit_bytes=VMEM_LIMIT),
        name=name,
    )(x_rows, mod3, norm_ffn, w_gate_up, w_down)


def _rope_tables(n_tokens):
    rows = n_tokens // GRID_W
    row = jnp.repeat(jnp.arange(rows, dtype=F32), GRID_W)
    col = jnp.tile(jnp.arange(GRID_W, dtype=F32), rows)
    inv = 1.0 / (ROPE_THETA ** (jnp.arange(ROT_PAIRS, dtype=F32) / ROT_PAIRS))
    ang = jnp.stack([row[:, None] * inv, col[:, None] * inv], axis=1)
    cos, sin = jnp.cos(ang), jnp.sin(ang)
    cos_h = jnp.concatenate([cos, cos], axis=-1).reshape(n_tokens, HEAD_DIM)
    sin_h = jnp.concatenate([-sin, sin], axis=-1).reshape(n_tokens, HEAD_DIM)
    reps = LANES // HEAD_DIM
    return jnp.tile(cos_h, (1, reps)), jnp.tile(sin_h, (1, reps))


def _keys_minor(kv):
    return jnp.transpose(kv, (0, 1, 3, 4, 2))


def _group_mean_matrix():
    idx = np.arange(MXU_DIM) // HEAD_DIM
    g = (idx[:, None] == idx[None, :]).astype(np.float32) / HEAD_DIM
    return jnp.asarray(np.concatenate([g, g], axis=0), dtype=BF16)


def kernel(x_prompt, x_sample, c, cache_k, cache_v, c_ctx, norm_mix, norm_ffn, w_ada, b_ada,
           w_in, q_norm, k_norm, conv_w, attn_out_norm, conv_out_norm, w_out, w_gate_up, w_down):
    depth = w_in.shape[0]
    assert depth == 1
    n_prompt, seq, _ = x_prompt.shape
    n_sample, dec_seq, _ = x_sample.shape
    assert dec_seq == BLOCK_ROWS and BLOCK_ROWS % seq == 0 and n_sample <= CTX_ROW

    cond = jnp.zeros((COND_ROWS, D_MODEL), F32)
    cond = cond.at[0:n_sample].set(c).at[CTX_ROW].set(c_ctx)
    mod = _ada_call(cond, w_ada[0], b_ada[0][None, :])
    mod3 = mod.reshape(COND_ROWS, 1, 6 * D_MODEL)

    consts = (
        norm_mix[0][None, :],
        w_in[0].astype(BF16),
        _group_mean_matrix(),
        jnp.concatenate([jnp.tile(q_norm[0], N_HEADS), jnp.tile(k_norm[0], N_KV)])[None, :],
        conv_w[0],
        attn_out_norm[0][None, :],
        conv_out_norm[0][None, :],
        w_out[0].astype(BF16),
    )
    nffn = norm_ffn[0][None, :]
    wgu = w_gate_up[0].astype(BF16)
    wd = w_down[0].astype(BF16)

    per_block = BLOCK_ROWS // seq
    xp_blocks = x_prompt.reshape(n_prompt // per_block, BLOCK_ROWS, D_MODEL)
    xp1, k_new, v_new = _mixer_call(xp_blocks, mod3, lambda b: CTX_ROW, None, None, None, consts,
                                    seq_len=seq, emit_kv=True)
    yp = _ffn_call(xp1.reshape(-1, D_MODEL), mod3, lambda i: CTX_ROW, nffn, wgu, wd, "ffn_ctx")

    (xs1,) = _mixer_call(x_sample, mod3, lambda b: b, _keys_minor(cache_k), _keys_minor(cache_v),
                         _rope_tables(dec_seq), consts, seq_len=dec_seq, emit_kv=False)
    tiles_per_seq = dec_seq // FFN_ROWS
    ys = _ffn_call(xs1.reshape(-1, D_MODEL), mod3, lambda i: i // tiles_per_seq, nffn, wgu, wd,
                   "ffn_latent")

    return (yp.reshape(n_prompt, seq, D_MODEL),
            ys.reshape(n_sample, dec_seq, D_MODEL),
            jnp.transpose(k_new, (0, 1, 4, 2, 3)),
            jnp.transpose(v_new, (0, 1, 4, 2, 3)))
```

```python
import functools

import numpy as np
import jax
import jax.numpy as jnp
from jax import lax
from jax.experimental import pallas as pl
from jax.experimental.pallas import tpu as pltpu

D_MODEL = 1024
HEAD_DIM = 64
ATTN_DIM = 512
N_HEADS = 8
N_KV = 2
KV_GROUP = N_HEADS // N_KV
KV_DIM = N_KV * HEAD_DIM
CONV_DIM = 512
D_FF = 2816
QK_DIM = ATTN_DIM + KV_DIM
QKV_DIM = ATTN_DIM + 2 * KV_DIM
IN_DIM = QKV_DIM + 3 * CONV_DIM
GRID_W = 64
ROT_PAIRS = HEAD_DIM // 4
ROPE_THETA = 10000.0
RMS_EPS = 1e-6
Q_SCALE = HEAD_DIM ** -0.5 * 1.4426950408889634
V_ROWS = HEAD_DIM + 16
MAX_UNSHIFTED_SCORE = 64.0

LANES = 128
MXU_DIM = 256
BLOCK_ROWS = 1024
CHUNK_ROWS = 512
LATENT_Q_ROWS = 128
P_ROWS = 16
FFN_ROWS = 1024
FFN_SUB_ROWS = 512
GU_CAST_ROWS = 128
DOWN_CAST_ROWS = 352
FF_CHUNKS = ((0, 1024), (1024, 1024), (2048, 768))
COND_ROWS = 16
CTX_ROW = 8
ADA_COLS = 1024
VMEM_LIMIT = 56 * 1024 * 1024

F32 = jnp.float32
BF16 = jnp.bfloat16


def _const_spec(shape):
    nd = len(shape)
    return pl.BlockSpec(shape, lambda *_: (0,) * nd, pipeline_mode=pl.Buffered(1))


def _ada_kernel(cond_ref, w_ref, b_ref, out_ref):
    c = cond_ref[...]
    s = (c * jax.nn.sigmoid(c)).astype(BF16)
    out_ref[...] = jnp.dot(s, w_ref[...].astype(BF16), preferred_element_type=F32) + b_ref[...]


def _ada_call(cond, w_ada, b_ada):
    n = w_ada.shape[1]
    return pl.pallas_call(
        _ada_kernel,
        grid=(n // ADA_COLS,),
        in_specs=[
            pl.BlockSpec((COND_ROWS, D_MODEL), lambda j: (0, 0)),
            pl.BlockSpec((D_MODEL, ADA_COLS), lambda j: (0, j)),
            pl.BlockSpec((1, ADA_COLS), lambda j: (0, j)),
        ],
        out_specs=pl.BlockSpec((COND_ROWS, ADA_COLS), lambda j: (0, j)),
        out_shape=jax.ShapeDtypeStruct((COND_ROWS, n), F32),
        compiler_params=pltpu.CompilerParams(
            dimension_semantics=("arbitrary",), vmem_limit_bytes=VMEM_LIMIT),
        name="ada_rows",
    )(cond, w_ada, b_ada)


def _rms_rows(x):
    return x * lax.rsqrt(jnp.mean(x * x, axis=-1, keepdims=True) + RMS_EPS)


def _mixer_kernel(*refs, seq_len, ctx_len, use_rope, emit_kv):
    it = iter(refs)
    x_ref, mod_ref = next(it), next(it)
    if ctx_len:
        ck_ref, cv_ref = next(it), next(it)
    if use_rope:
        cos_ref, sin_ref = next(it), next(it)
    (nmix_ref, win_ref, gsum_ref, qkg_ref, convw_ref, ga_ref, gc_ref, wout_ref) = (
        next(it) for _ in range(8))
    out_ref = next(it)
    if emit_kv:
        ko_ref, vo_ref = next(it), next(it)
    (q_s, k_s, vt_s, attn_s, t_s, gb_s, s_ref, p_ref, m_ref) = (next(it) for _ in range(9))

    n_chunks = BLOCK_ROWS // CHUNK_ROWS
    if ctx_len:
        q_rows, n_keys, key_blk = LATENT_Q_ROWS, ctx_len + seq_len, BLOCK_ROWS
    else:
        q_rows, n_keys, key_blk = seq_len, seq_len, seq_len
    n_qb = BLOCK_ROWS // q_rows

    mod = mod_ref[...]
    shift = mod[:, 0:D_MODEL]
    scale1 = 1.0 + mod[:, D_MODEL:2 * D_MODEL]
    gate = mod[:, 2 * D_MODEL:3 * D_MODEL]

    def put_values_t(blk, off, vt):
        vt = vt.astype(BF16)
        for g in range(N_KV):
            vt_s[blk, g, 0:HEAD_DIM, off:off + vt.shape[1]] = vt[g * HEAD_DIM:(g + 1) * HEAD_DIM, :]

    tail = (lax.broadcasted_iota(jnp.int32, (V_ROWS - HEAD_DIM, n_keys), 0) == 0).astype(BF16)
    for blk in range(vt_s.shape[0]):
        for g in range(N_KV):
            vt_s[blk, g, HEAD_DIM:V_ROWS, :] = tail

    if ctx_len:
        ck = ck_ref[...].reshape(KV_DIM, ctx_len).T.astype(BF16)
        for g in range(N_KV):
            k_s[g, 0:ctx_len, :] = ck[:, g * HEAD_DIM:(g + 1) * HEAD_DIM]
        put_values_t(0, 0, cv_ref[...].reshape(KV_DIM, ctx_len))

    if use_rope:
        lane = lax.broadcasted_iota(jnp.int32, (CHUNK_ROWS, LANES), 1)
        first_half = (lane % (2 * ROT_PAIRS)) < ROT_PAIRS

    for c in range(n_chunks):
        r0 = c * CHUNK_ROWS
        x = x_ref[r0:r0 + CHUNK_ROWS, :]
        h = (_rms_rows(x) * nmix_ref[...]) * scale1 + shift
        hb = h.astype(BF16)
        qkv = jnp.dot(hb, win_ref[:, 0:QKV_DIM], preferred_element_type=F32)

        groups = []
        for g0 in range(0, QKV_DIM, MXU_DIM):
            sq = qkv[:, g0:g0 + MXU_DIM]
            sq = sq * sq
            hi = sq.astype(BF16)
            lo = (sq - hi.astype(F32)).astype(BF16)
            groups.append(jnp.dot(jnp.concatenate([hi, lo], axis=-1), gsum_ref[...],
                                  preferred_element_type=F32))
        ms = jnp.concatenate(groups, axis=-1)[:, 0:QK_DIM]
        qk = (qkv[:, 0:QK_DIM] * lax.rsqrt(ms + RMS_EPS)) * qkg_ref[...]
        vv = qkv[:, QK_DIM:QKV_DIM]

        if emit_kv:
            for r1 in range(0, CHUNK_ROWS, seq_len):
                kt = qk[r1:r1 + seq_len, ATTN_DIM:QK_DIM].T
                ko_ref[(r0 + r1) // seq_len] = kt.reshape(N_KV, HEAD_DIM, seq_len)

        for cg in range(QK_DIM // LANES):
            xg = qk[:, cg * LANES:(cg + 1) * LANES]
            if use_rope:
                cs = cos_ref[r0:r0 + CHUNK_ROWS, :]
                sn = sin_ref[r0:r0 + CHUNK_ROWS, :]
                partner = jnp.where(first_half,
                                    pltpu.roll(xg, LANES - ROT_PAIRS, axis=1),
                                    pltpu.roll(xg, ROT_PAIRS, axis=1))
                xg = xg * cs + partner * sn
            if cg < ATTN_DIM // LANES:
                xb = (xg * Q_SCALE).astype(BF16)
                q_s[2 * cg, r0:r0 + CHUNK_ROWS, :] = xb[:, 0:HEAD_DIM]
                q_s[2 * cg + 1, r0:r0 + CHUNK_ROWS, :] = xb[:, HEAD_DIM:LANES]
            else:
                xb = xg.astype(BF16)
                for g in range(N_KV):
                    k_s[g, ctx_len + r0:ctx_len + r0 + CHUNK_ROWS, :] = (
                        xb[:, g * HEAD_DIM:(g + 1) * HEAD_DIM])
        w = min(key_blk, CHUNK_ROWS)
        for r1 in range(r0, r0 + CHUNK_ROWS, w):
            blk, off = (0, ctx_len + r1) if ctx_len else (r1 // key_blk, 0)
            vt = vv[r1 - r0:r1 - r0 + w, :].T
            put_values_t(blk, off, vt)
            if emit_kv:
                vo_ref[blk] = vt.reshape(N_KV, HEAD_DIM, seq_len)

        cvp = jnp.dot(hb, win_ref[:, QKV_DIM:IN_DIM], preferred_element_type=F32)
        gb_s[r0:r0 + CHUNK_ROWS, :] = cvp[:, 0:CONV_DIM]
        t_s[8 + r0:8 + r0 + CHUNK_ROWS, :] = (cvp[:, CONV_DIM:2 * CONV_DIM]
                                              * cvp[:, 2 * CONV_DIM:3 * CONV_DIM])
    t_s[0:8, :] = jnp.zeros((8, CONV_DIM), F32)
    t_s[8 + BLOCK_ROWS:16 + BLOCK_ROWS, :] = jnp.zeros((8, CONV_DIM), F32)

    def scores_t(qb, g):
        r0 = pl.multiple_of(qb * q_rows, q_rows)
        k0 = 0 if ctx_len else r0
        qs = jnp.concatenate(
            [q_s[KV_GROUP * g + j, pl.ds(r0, q_rows), :] for j in range(KV_GROUP)], axis=0)
        kk = k_s[g, pl.ds(k0, n_keys), :]
        return lax.dot_general(kk, qs, (((1,), (1,)), ((), ())),
                               preferred_element_type=F32)

    def values_out(qb, g, p):
        r0 = pl.multiple_of(qb * q_rows, q_rows)
        blk = 0 if ctx_len else qb
        ot = jnp.dot(vt_s[blk, g], p, preferred_element_type=F32)
        ot = ot[0:HEAD_DIM, :] / ot[HEAD_DIM:HEAD_DIM + 1, :]
        for jj in range(KV_GROUP // 2):
            pair_t = jnp.concatenate(
                [ot[:, (2 * jj) * q_rows:(2 * jj + 1) * q_rows],
                 ot[:, (2 * jj + 1) * q_rows:(2 * jj + 2) * q_rows]], axis=0)
            col = (KV_GROUP * g + 2 * jj) * HEAD_DIM
            attn_s[pl.ds(r0, q_rows), col:col + LANES] = pair_t.T

    def pipeline(step):
        step(-1, 1, first=True)

        def body(j, carry):
            for par in range(2):
                pl.when(j % 2 == par)(functools.partial(step, j, par))
            return carry

        lax.fori_loop(0, n_qb - 1, body, 0)
        step(n_qb - 1, (n_qb - 1) % 2, last=True)

    def pipeline_pairs(step):
        assert n_qb % 2 == 0
        step(-1, 1, first=True)
        step(0, 0)

        def body(t, carry):
            step(2 * t + 1, 1)
            step(2 * t + 2, 0)
            return carry

        lax.fori_loop(0, (n_qb - 2) // 2, body, 0)
        step(n_qb - 1, 1, last=True)

    def shifted_step(j, par, first=False, last=False):
        for g in range(N_KV):
            if not last:
                s = scores_t(j + 1, g)
                s_ref[1 - par, g] = s
                m_ref[1 - par, g] = jnp.max(s, axis=0, keepdims=True)
            if not first:
                m = m_ref[par, g]
                for k1 in range(0, n_keys, P_ROWS):
                    p_ref[0, g, k1:k1 + P_ROWS, :] = jnp.exp2(
                        s_ref[par, g, k1:k1 + P_ROWS, :] - m).astype(BF16)
                values_out(j, g, p_ref[0, g])

    def unshifted_step(j, par, first=False, last=False):
        for g in range(N_KV):
            if not last:
                p_ref[1 - par, g] = jnp.exp2(scores_t(j + 1, g)).astype(BF16)
            if not first:
                values_out(j, g, p_ref[par, g])

    gains = jnp.abs(qkg_ref[...])
    q_bound = (Q_SCALE * Q_SCALE * HEAD_DIM) * jnp.max(gains[:, 0:ATTN_DIM]) ** 2
    k_bound = HEAD_DIM * jnp.max(gains[:, ATTN_DIM:QK_DIM]) ** 2
    if ctx_len:
        ck2 = ck_ref[...]
        k_bound = jnp.maximum(k_bound, jnp.max(jnp.sum(ck2 * ck2, axis=1)))
    small_scores = q_bound * k_bound <= MAX_UNSHIFTED_SCORE ** 2
    pl.when(small_scores)(lambda: pipeline_pairs(unshifted_step))
    pl.when(jnp.logical_not(small_scores))(lambda: pipeline(shifted_step))

    w0 = convw_ref[0:1, :]
    w1 = convw_ref[1:2, :]
    w2 = convw_ref[2:3, :]
    for c in range(n_chunks):
        r0 = c * CHUNK_ROWS
        pos = (lax.broadcasted_iota(jnp.int32, (CHUNK_ROWS, 1), 0) + r0) % seq_len
        t_prev = jnp.where(pos == 0, 0.0, t_s[7 + r0:7 + r0 + CHUNK_ROWS, :])
        t_mid = t_s[8 + r0:8 + r0 + CHUNK_ROWS, :]
        t_next = jnp.where(pos == seq_len - 1, 0.0, t_s[9 + r0:9 + r0 + CHUNK_ROWS, :])
        y = gb_s[r0:r0 + CHUNK_ROWS, :] * (w0 * t_prev + w1 * t_mid + w2 * t_next)
        yn = _rms_rows(y) * gc_ref[...]
        an = _rms_rows(attn_s[r0:r0 + CHUNK_ROWS, :]) * ga_ref[...]
        merged = jnp.concatenate([an, yn], axis=-1).astype(BF16)
        mix = jnp.dot(merged, wout_ref[...], preferred_element_type=F32)
        out_ref[r0:r0 + CHUNK_ROWS, :] = x_ref[r0:r0 + CHUNK_ROWS, :] + gate * mix


def _mixer_call(x_blocks, mod3, mod_row_of_block, ctx_k, ctx_v, rope, consts, *, seq_len,
                emit_kv):
    n_blocks = x_blocks.shape[0]
    ctx_len = 0 if ctx_k is None else ctx_k.shape[4]
    use_rope = rope is not None
    if ctx_len:
        assert seq_len == BLOCK_ROWS
        q_rows, n_keys, n_key_blocks = LATENT_Q_ROWS, ctx_len + BLOCK_ROWS, 1
    else:
        q_rows, n_keys, n_key_blocks = seq_len, seq_len, BLOCK_ROWS // seq_len
    n_q_cols = KV_GROUP * q_rows

    blk = lambda cols: pl.BlockSpec((None, BLOCK_ROWS, cols), lambda b: (b, 0, 0))
    args = [x_blocks, mod3]
    in_specs = [blk(D_MODEL),
                pl.BlockSpec((None, 1, 3 * D_MODEL), lambda b: (mod_row_of_block(b), 0, 0))]
    if ctx_len:
        args += [ctx_k, ctx_v]
        in_specs += [pl.BlockSpec((None, None, N_KV, HEAD_DIM, ctx_len),
                                  lambda b: (b, 0, 0, 0, 0))] * 2
    if use_rope:
        args += list(rope)
        in_specs += [_const_spec((BLOCK_ROWS, LANES))] * 2
    args += list(consts)
    in_specs += [_const_spec(a.shape) for a in consts]

    out_shape = [jax.ShapeDtypeStruct((n_blocks, BLOCK_ROWS, D_MODEL), F32)]
    out_specs = [blk(D_MODEL)]
    if emit_kv:
        per_block = BLOCK_ROWS // seq_len
        out_shape += [jax.ShapeDtypeStruct(
            (n_blocks * per_block, 1, N_KV, HEAD_DIM, seq_len), F32)] * 2
        out_specs += [pl.BlockSpec((per_block, None, N_KV, HEAD_DIM, seq_len),
                                   lambda b: (b, 0, 0, 0, 0))] * 2

    scratch = [
        pltpu.VMEM((N_HEADS, BLOCK_ROWS, HEAD_DIM), BF16),
        pltpu.VMEM((N_KV, ctx_len + BLOCK_ROWS, HEAD_DIM), BF16),
        pltpu.VMEM((n_key_blocks, N_KV, V_ROWS, n_keys), BF16),
        pltpu.VMEM((BLOCK_ROWS, ATTN_DIM), F32),
        pltpu.VMEM((BLOCK_ROWS + 16, CONV_DIM), F32),
        pltpu.VMEM((BLOCK_ROWS, CONV_DIM), F32),
        pltpu.VMEM((2, N_KV, n_keys, n_q_cols), F32),
        pltpu.VMEM((2, N_KV, n_keys, n_q_cols), BF16),
        pltpu.VMEM((2, N_KV, 1, n_q_cols), F32),
    ]
    kern = functools.partial(_mixer_kernel, seq_len=seq_len, ctx_len=ctx_len,
                             use_rope=use_rope, emit_kv=emit_kv)
    return pl.pallas_call(
        kern,
        grid=(n_blocks,),
        in_specs=in_specs,
        out_specs=out_specs,
        out_shape=out_shape,
        scratch_shapes=scratch,
        compiler_params=pltpu.CompilerParams(
            dimension_semantics=("arbitrary",), vmem_limit_bytes=VMEM_LIMIT),
        name="mixer_ctx" if emit_kv else "mixer_latent",
    )(*args)


def _ffn_rows(x_ref, mod_ref, nffn_ref, wgu_ref, wd_ref, out_ref):
    mod = mod_ref[...]
    shift = mod[:, 0:D_MODEL]
    scale1 = 1.0 + mod[:, D_MODEL:2 * D_MODEL]
    gate = mod[:, 2 * D_MODEL:3 * D_MODEL]
    for r0 in range(0, FFN_ROWS, FFN_SUB_ROWS):
        x = x_ref[r0:r0 + FFN_SUB_ROWS, :]
        hb = ((_rms_rows(x) * nffn_ref[...]) * scale1 + shift).astype(BF16)
        acc = None
        for c0, cw in FF_CHUNKS:
            gt = jnp.dot(hb, wgu_ref[:, c0:c0 + cw], preferred_element_type=F32)
            up = jnp.dot(hb, wgu_ref[:, D_FF + c0:D_FF + c0 + cw], preferred_element_type=F32)
            act = ((gt * jax.nn.sigmoid(gt)) * up).astype(BF16)
            part = jnp.dot(act, wd_ref[c0:c0 + cw, :], preferred_element_type=F32)
            acc = part if acc is None else acc + part
        out_ref[r0:r0 + FFN_SUB_ROWS, :] = x + gate * acc


def _ffn_kernel(x_ref, mod_ref, nffn_ref, wgu_ref, wd_ref, out_ref):
    _ffn_rows(x_ref, mod_ref, nffn_ref, wgu_ref, wd_ref, out_ref)


def _cast_rows(src_hbm, dst, stage, sem, chunk_rows):
    n = src_hbm.shape[0] // chunk_rows

    def fetch(c):
        return pltpu.make_async_copy(src_hbm.at[pl.ds(c * chunk_rows, chunk_rows), :],
                                     stage.at[c % 2], sem.at[c % 2])

    fetch(0).start()
    for c in range(n):
        if c + 1 < n:
            fetch(c + 1).start()
        fetch(c).wait()
        dst[c * chunk_rows:(c + 1) * chunk_rows, :] = stage[c % 2].astype(BF16)


def _ffn_cast_kernel(x_ref, mod_ref, nffn_ref, wgu_hbm, wd_hbm, out_ref, wgu_out, wd_out,
                     wgu_s, wd_s, stage_gu, stage_d, sem_in, sem_out):
    first = pl.program_id(0) == 0
    save = (pltpu.make_async_copy(wgu_s, wgu_out, sem_out.at[0]),
            pltpu.make_async_copy(wd_s, wd_out, sem_out.at[1]))

    @pl.when(first)
    def _():
        _cast_rows(wgu_hbm, wgu_s, stage_gu, sem_in, GU_CAST_ROWS)
        _cast_rows(wd_hbm, wd_s, stage_d, sem_in, DOWN_CAST_ROWS)
        for cp in save:
            cp.start()

    _ffn_rows(x_ref, mod_ref, nffn_ref, wgu_s, wd_s, out_ref)

    @pl.when(first)
    def _():
        for cp in save:
            cp.wait()


def _ffn_call(x_rows, mod3, mod_row_of_tile, norm_ffn, w_gate_up, w_down, name):
    n_rows = x_rows.shape[0]
    cast = w_gate_up.dtype == F32
    rows_spec = pl.BlockSpec((FFN_ROWS, D_MODEL), lambda i: (i, 0))
    in_specs = [rows_spec,
                pl.BlockSpec((None, 1, 3 * D_MODEL), lambda i: (mod_row_of_tile(i), 0, 1)),
                _const_spec(norm_ffn.shape)]
    out_shape = [jax.ShapeDtypeStruct((n_rows, D_MODEL), F32)]
    out_specs = [rows_spec]
    scratch = []
    if cast:
        hbm = pl.BlockSpec(memory_space=pl.ANY)
        in_specs += [hbm, hbm]
        out_shape += [jax.ShapeDtypeStruct(w_gate_up.shape, BF16),
                      jax.ShapeDtypeStruct(w_down.shape, BF16)]
        out_specs += [hbm, hbm]
        scratch = [pltpu.VMEM(w_gate_up.shape, BF16), pltpu.VMEM(w_down.shape, BF16),
                   pltpu.VMEM((2, GU_CAST_ROWS, w_gate_up.shape[1]), F32),
                   pltpu.VMEM((2, DOWN_CAST_ROWS, w_down.shape[1]), F32),
                   pltpu.SemaphoreType.DMA((2,)), pltpu.SemaphoreType.DMA((2,))]
    else:
        in_specs += [_const_spec(w_gate_up.shape), _const_spec(w_down.shape)]
    res = pl.pallas_call(
        _ffn_cast_kernel if cast else _ffn_kernel,
        grid=(n_rows // FFN_ROWS,),
        in_specs=in_specs,
        out_specs=out_specs,
        out_shape=out_shape,
        scratch_shapes=scratch,
        compiler_params=pltpu.CompilerParams(
            dimension_semantics=("arbitrary",), vmem_limit_bytes=VMEM_LIMIT),
        name=name,
    )(x_rows, mod3, norm_ffn, w_gate_up, w_down)
    return res if cast else res[0]


def _rope_tables(n_tokens):
    rows = n_tokens // GRID_W
    row = jnp.repeat(jnp.arange(rows, dtype=F32), GRID_W)
    col = jnp.tile(jnp.arange(GRID_W, dtype=F32), rows)
    inv = 1.0 / (ROPE_THETA ** (jnp.arange(ROT_PAIRS, dtype=F32) / ROT_PAIRS))
    ang = jnp.stack([row[:, None] * inv, col[:, None] * inv], axis=1)
    cos, sin = jnp.cos(ang), jnp.sin(ang)
    cos_h = jnp.concatenate([cos, cos], axis=-1).reshape(n_tokens, HEAD_DIM)
    sin_h = jnp.concatenate([-sin, sin], axis=-1).reshape(n_tokens, HEAD_DIM)
    reps = LANES // HEAD_DIM
    return jnp.tile(cos_h, (1, reps)), jnp.tile(sin_h, (1, reps))


def _keys_minor(kv):
    return jnp.transpose(kv, (0, 1, 3, 4, 2))


def _group_mean_matrix():
    idx = np.arange(MXU_DIM) // HEAD_DIM
    g = (idx[:, None] == idx[None, :]).astype(np.float32) / HEAD_DIM
    return jnp.asarray(np.concatenate([g, g], axis=0), dtype=BF16)


def kernel(x_prompt, x_sample, c, cache_k, cache_v, c_ctx, norm_mix, norm_ffn, w_ada, b_ada,
           w_in, q_norm, k_norm, conv_w, attn_out_norm, conv_out_norm, w_out, w_gate_up, w_down):
    depth = w_in.shape[0]
    assert depth == 1
    n_prompt, seq, _ = x_prompt.shape
    n_sample, dec_seq, _ = x_sample.shape
    assert dec_seq == BLOCK_ROWS and BLOCK_ROWS % seq == 0 and n_sample <= CTX_ROW

    cond = jnp.zeros((COND_ROWS, D_MODEL), F32)
    cond = cond.at[0:n_sample].set(c).at[CTX_ROW].set(c_ctx)
    mod = _ada_call(cond, w_ada[0], b_ada[0][None, :])
    mod3 = mod.reshape(COND_ROWS, 1, 6 * D_MODEL)

    consts = (
        norm_mix[0][None, :],
        w_in[0].astype(BF16),
        _group_mean_matrix(),
        jnp.concatenate([jnp.tile(q_norm[0], N_HEADS), jnp.tile(k_norm[0], N_KV)])[None, :],
        conv_w[0],
        attn_out_norm[0][None, :],
        conv_out_norm[0][None, :],
        w_out[0].astype(BF16),
    )
    nffn = norm_ffn[0][None, :]

    per_block = BLOCK_ROWS // seq
    xp_blocks = x_prompt.reshape(n_prompt // per_block, BLOCK_ROWS, D_MODEL)
    xp1, k_new, v_new = _mixer_call(xp_blocks, mod3, lambda b: CTX_ROW, None, None, None, consts,
                                    seq_len=seq, emit_kv=True)
    yp, wgu, wd = _ffn_call(xp1.reshape(-1, D_MODEL), mod3, lambda i: CTX_ROW, nffn,
                            w_gate_up[0], w_down[0], "ffn_ctx")

    (xs1,) = _mixer_call(x_sample, mod3, lambda b: b, _keys_minor(cache_k), _keys_minor(cache_v),
                         _rope_tables(dec_seq), consts, seq_len=dec_seq, emit_kv=False)
    tiles_per_seq = dec_seq // FFN_ROWS
    ys = _ffn_call(xs1.reshape(-1, D_MODEL), mod3, lambda i: i // tiles_per_seq, nffn, wgu, wd,
                   "ffn_latent")

    return (yp.reshape(n_prompt, seq, D_MODEL),
            ys.reshape(n_sample, dec_seq, D_MODEL),
            jnp.transpose(k_new, (0, 1, 4, 2, 3)),
            jnp.transpose(v_new, (0, 1, 4, 2, 3)))
```

```python
import functools

import numpy as np
import jax
import jax.numpy as jnp
from jax import lax
from jax.experimental import pallas as pl
from jax.experimental.pallas import tpu as pltpu

D_MODEL = 1024
HEAD_DIM = 64
ATTN_DIM = 512
N_HEADS = 8
N_KV = 2
KV_GROUP = N_HEADS // N_KV
KV_DIM = N_KV * HEAD_DIM
CONV_DIM = 512
D_FF = 2816
QK_DIM = ATTN_DIM + KV_DIM
QKV_DIM = ATTN_DIM + 2 * KV_DIM
IN_DIM = QKV_DIM + 3 * CONV_DIM
GRID_W = 64
ROT_PAIRS = HEAD_DIM // 4
ROPE_THETA = 10000.0
RMS_EPS = 1e-6
Q_SCALE = HEAD_DIM ** -0.5 * 1.4426950408889634
V_ROWS = HEAD_DIM + 16
MAX_UNSHIFTED_SCORE = 64.0

LANES = 128
MXU_DIM = 256
BLOCK_ROWS = 1024
CHUNK_ROWS = 512
LATENT_Q_ROWS = 128
P_ROWS = 16
FFN_ROWS = 1024
FFN_SUB_ROWS = 512
MIX_CAST_ROWS = 128
GU_CAST_ROWS = 128
DOWN_CAST_ROWS = 352
FF_CHUNKS = ((0, 1024), (1024, 1024), (2048, 768))
COND_ROWS = 16
CTX_ROW = 8
ADA_COLS = 1024
VMEM_LIMIT = 56 * 1024 * 1024

F32 = jnp.float32
BF16 = jnp.bfloat16


def _const_spec(shape):
    nd = len(shape)
    return pl.BlockSpec(shape, lambda *_: (0,) * nd, pipeline_mode=pl.Buffered(1))


def _ada_kernel(cond_ref, w_ref, b_ref, out_ref):
    c = cond_ref[...]
    s = (c * jax.nn.sigmoid(c)).astype(BF16)
    out_ref[...] = jnp.dot(s, w_ref[...].astype(BF16), preferred_element_type=F32) + b_ref[...]


def _ada_call(cond, w_ada, b_ada):
    n = w_ada.shape[1]
    return pl.pallas_call(
        _ada_kernel,
        grid=(n // ADA_COLS,),
        in_specs=[
            pl.BlockSpec((COND_ROWS, D_MODEL), lambda j: (0, 0)),
            pl.BlockSpec((D_MODEL, ADA_COLS), lambda j: (0, j)),
            pl.BlockSpec((1, ADA_COLS), lambda j: (0, j)),
        ],
        out_specs=pl.BlockSpec((COND_ROWS, ADA_COLS), lambda j: (0, j)),
        out_shape=jax.ShapeDtypeStruct((COND_ROWS, n), F32),
        compiler_params=pltpu.CompilerParams(
            dimension_semantics=("arbitrary",), vmem_limit_bytes=VMEM_LIMIT),
        name="ada_rows",
    )(cond, w_ada, b_ada)


def _rms_rows(x):
    return x * lax.rsqrt(jnp.mean(x * x, axis=-1, keepdims=True) + RMS_EPS)


def _mixer_kernel(*refs, seq_len, ctx_len, use_rope, emit_kv, cast_weights):
    it = iter(refs)
    x_ref, mod_ref = next(it), next(it)
    if ctx_len:
        ck_ref, cv_ref = next(it), next(it)
    if use_rope:
        cos_ref, sin_ref = next(it), next(it)
    (nmix_ref, win_ref, gsum_ref, qkg_ref, convw_ref, ga_ref, gc_ref, wout_ref) = (
        next(it) for _ in range(8))
    out_ref = next(it)
    if emit_kv:
        ko_ref, vo_ref = next(it), next(it)
    if cast_weights:
        win_out, wout_out = next(it), next(it)
    (q_s, k_s, vt_s, attn_s, t_s, gb_s, s_ref, p_ref, m_ref) = (next(it) for _ in range(9))

    if cast_weights:
        win_hbm, wout_hbm = win_ref, wout_ref
        win_ref, wout_ref, stage_in, stage_out, sem_in, sem_out = (next(it) for _ in range(6))
        first = pl.program_id(0) == 0
        save = (pltpu.make_async_copy(win_ref, win_out, sem_out.at[0]),
                pltpu.make_async_copy(wout_ref, wout_out, sem_out.at[1]))

        @pl.when(first)
        def _():
            _cast_rows(win_hbm, win_ref, stage_in, sem_in, MIX_CAST_ROWS)
            _cast_rows(wout_hbm, wout_ref, stage_out, sem_in, MIX_CAST_ROWS)
            for cp in save:
                cp.start()

    n_chunks = BLOCK_ROWS // CHUNK_ROWS
    if ctx_len:
        q_rows, n_keys, key_blk = LATENT_Q_ROWS, ctx_len + seq_len, BLOCK_ROWS
    else:
        q_rows, n_keys, key_blk = seq_len, seq_len, seq_len
    n_qb = BLOCK_ROWS // q_rows

    mod = mod_ref[...]
    shift = mod[:, 0:D_MODEL]
    scale1 = 1.0 + mod[:, D_MODEL:2 * D_MODEL]
    gate = mod[:, 2 * D_MODEL:3 * D_MODEL]

    def put_values_t(blk, off, vt):
        vt = vt.astype(BF16)
        for g in range(N_KV):
            vt_s[blk, g, 0:HEAD_DIM, off:off + vt.shape[1]] = vt[g * HEAD_DIM:(g + 1) * HEAD_DIM, :]

    tail = (lax.broadcasted_iota(jnp.int32, (V_ROWS - HEAD_DIM, n_keys), 0) == 0).astype(BF16)
    for blk in range(vt_s.shape[0]):
        for g in range(N_KV):
            vt_s[blk, g, HEAD_DIM:V_ROWS, :] = tail

    if ctx_len:
        ck = ck_ref[...].reshape(KV_DIM, ctx_len).T.astype(BF16)
        for g in range(N_KV):
            k_s[g, 0:ctx_len, :] = ck[:, g * HEAD_DIM:(g + 1) * HEAD_DIM]
        put_values_t(0, 0, cv_ref[...].reshape(KV_DIM, ctx_len))

    if use_rope:
        lane = lax.broadcasted_iota(jnp.int32, (CHUNK_ROWS, LANES), 1)
        first_half = (lane % (2 * ROT_PAIRS)) < ROT_PAIRS

    for c in range(n_chunks):
        r0 = c * CHUNK_ROWS
        x = x_ref[r0:r0 + CHUNK_ROWS, :]
        h = (_rms_rows(x) * nmix_ref[...]) * scale1 + shift
        hb = h.astype(BF16)
        qkv = jnp.dot(hb, win_ref[:, 0:QKV_DIM], preferred_element_type=F32)

        groups = []
        for g0 in range(0, QKV_DIM, MXU_DIM):
            sq = qkv[:, g0:g0 + MXU_DIM]
            sq = sq * sq
            hi = sq.astype(BF16)
            lo = (sq - hi.astype(F32)).astype(BF16)
            groups.append(jnp.dot(jnp.concatenate([hi, lo], axis=-1), gsum_ref[...],
                                  preferred_element_type=F32))
        ms = jnp.concatenate(groups, axis=-1)[:, 0:QK_DIM]
        qk = (qkv[:, 0:QK_DIM] * lax.rsqrt(ms + RMS_EPS)) * qkg_ref[...]
        vv = qkv[:, QK_DIM:QKV_DIM]

        if emit_kv:
            for r1 in range(0, CHUNK_ROWS, seq_len):
                kt = qk[r1:r1 + seq_len, ATTN_DIM:QK_DIM].T
                ko_ref[(r0 + r1) // seq_len] = kt.reshape(N_KV, HEAD_DIM, seq_len)

        for cg in range(QK_DIM // LANES):
            xg = qk[:, cg * LANES:(cg + 1) * LANES]
            if use_rope:
                cs = cos_ref[r0:r0 + CHUNK_ROWS, :]
                sn = sin_ref[r0:r0 + CHUNK_ROWS, :]
                partner = jnp.where(first_half,
                                    pltpu.roll(xg, LANES - ROT_PAIRS, axis=1),
                                    pltpu.roll(xg, ROT_PAIRS, axis=1))
                xg = xg * cs + partner * sn
            if cg < ATTN_DIM // LANES:
                xb = (xg * Q_SCALE).astype(BF16)
                q_s[2 * cg, r0:r0 + CHUNK_ROWS, :] = xb[:, 0:HEAD_DIM]
                q_s[2 * cg + 1, r0:r0 + CHUNK_ROWS, :] = xb[:, HEAD_DIM:LANES]
            else:
                xb = xg.astype(BF16)
                for g in range(N_KV):
                    k_s[g, ctx_len + r0:ctx_len + r0 + CHUNK_ROWS, :] = (
                        xb[:, g * HEAD_DIM:(g + 1) * HEAD_DIM])
        w = min(key_blk, CHUNK_ROWS)
        for r1 in range(r0, r0 + CHUNK_ROWS, w):
            blk, off = (0, ctx_len + r1) if ctx_len else (r1 // key_blk, 0)
            vt = vv[r1 - r0:r1 - r0 + w, :].T
            put_values_t(blk, off, vt)
            if emit_kv:
                vo_ref[blk] = vt.reshape(N_KV, HEAD_DIM, seq_len)

        cvp = jnp.dot(hb, win_ref[:, QKV_DIM:IN_DIM], preferred_element_type=F32)
        gb_s[r0:r0 + CHUNK_ROWS, :] = cvp[:, 0:CONV_DIM]
        t_s[8 + r0:8 + r0 + CHUNK_ROWS, :] = (cvp[:, CONV_DIM:2 * CONV_DIM]
                                              * cvp[:, 2 * CONV_DIM:3 * CONV_DIM])
    t_s[0:8, :] = jnp.zeros((8, CONV_DIM), F32)
    t_s[8 + BLOCK_ROWS:16 + BLOCK_ROWS, :] = jnp.zeros((8, CONV_DIM), F32)

    def scores_t(qb, g):
        r0 = pl.multiple_of(qb * q_rows, q_rows)
        k0 = 0 if ctx_len else r0
        qs = jnp.concatenate(
            [q_s[KV_GROUP * g + j, pl.ds(r0, q_rows), :] for j in range(KV_GROUP)], axis=0)
        kk = k_s[g, pl.ds(k0, n_keys), :]
        return lax.dot_general(kk, qs, (((1,), (1,)), ((), ())),
                               preferred_element_type=F32)

    def values_out(qb, g, p):
        r0 = pl.multiple_of(qb * q_rows, q_rows)
        blk = 0 if ctx_len else qb
        ot = jnp.dot(vt_s[blk, g], p, preferred_element_type=F32)
        ot = ot[0:HEAD_DIM, :] / ot[HEAD_DIM:HEAD_DIM + 1, :]
        for jj in range(KV_GROUP // 2):
            pair_t = jnp.concatenate(
                [ot[:, (2 * jj) * q_rows:(2 * jj + 1) * q_rows],
                 ot[:, (2 * jj + 1) * q_rows:(2 * jj + 2) * q_rows]], axis=0)
            col = (KV_GROUP * g + 2 * jj) * HEAD_DIM
            attn_s[pl.ds(r0, q_rows), col:col + LANES] = pair_t.T

    def pipeline(step):
        step(-1, 1, first=True)

        def body(j, carry):
            for par in range(2):
                pl.when(j % 2 == par)(functools.partial(step, j, par))
            return carry

        lax.fori_loop(0, n_qb - 1, body, 0)
        step(n_qb - 1, (n_qb - 1) % 2, last=True)

    def pipeline_pairs(step):
        assert n_qb % 2 == 0
        step(-1, 1, first=True)
        step(0, 0)

        def body(t, carry):
            step(2 * t + 1, 1)
            step(2 * t + 2, 0)
            return carry

        lax.fori_loop(0, (n_qb - 2) // 2, body, 0)
        step(n_qb - 1, 1, last=True)

    def shifted_step(j, par, first=False, last=False):
        for g in range(N_KV):
            if not last:
                s = scores_t(j + 1, g)
                s_ref[1 - par, g] = s
                m_ref[1 - par, g] = jnp.max(s, axis=0, keepdims=True)
            if not first:
                m = m_ref[par, g]
                for k1 in range(0, n_keys, P_ROWS):
                    p_ref[0, g, k1:k1 + P_ROWS, :] = jnp.exp2(
                        s_ref[par, g, k1:k1 + P_ROWS, :] - m).astype(BF16)
                values_out(j, g, p_ref[0, g])

    def unshifted_step(j, par, first=False, last=False):
        for g in range(N_KV):
            if not last:
                p_ref[1 - par, g] = jnp.exp2(scores_t(j + 1, g)).astype(BF16)
            if not first:
                values_out(j, g, p_ref[par, g])

    gains = jnp.abs(qkg_ref[...])
    q_bound = (Q_SCALE * Q_SCALE * HEAD_DIM) * jnp.max(gains[:, 0:ATTN_DIM]) ** 2
    k_bound = HEAD_DIM * jnp.max(gains[:, ATTN_DIM:QK_DIM]) ** 2
    if ctx_len:
        ck2 = ck_ref[...]
        k_bound = jnp.maximum(k_bound, jnp.max(jnp.sum(ck2 * ck2, axis=1)))
    small_scores = q_bound * k_bound <= MAX_UNSHIFTED_SCORE ** 2
    pl.when(small_scores)(lambda: pipeline_pairs(unshifted_step))
    pl.when(jnp.logical_not(small_scores))(lambda: pipeline(shifted_step))

    w0 = convw_ref[0:1, :]
    w1 = convw_ref[1:2, :]
    w2 = convw_ref[2:3, :]
    for c in range(n_chunks):
        r0 = c * CHUNK_ROWS
        pos = (lax.broadcasted_iota(jnp.int32, (CHUNK_ROWS, 1), 0) + r0) % seq_len
        t_prev = jnp.where(pos == 0, 0.0, t_s[7 + r0:7 + r0 + CHUNK_ROWS, :])
        t_mid = t_s[8 + r0:8 + r0 + CHUNK_ROWS, :]
        t_next = jnp.where(pos == seq_len - 1, 0.0, t_s[9 + r0:9 + r0 + CHUNK_ROWS, :])
        y = gb_s[r0:r0 + CHUNK_ROWS, :] * (w0 * t_prev + w1 * t_mid + w2 * t_next)
        yn = _rms_rows(y) * gc_ref[...]
        an = _rms_rows(attn_s[r0:r0 + CHUNK_ROWS, :]) * ga_ref[...]
        merged = jnp.concatenate([an, yn], axis=-1).astype(BF16)
        mix = jnp.dot(merged, wout_ref[...], preferred_element_type=F32)
        out_ref[r0:r0 + CHUNK_ROWS, :] = x_ref[r0:r0 + CHUNK_ROWS, :] + gate * mix

    if cast_weights:
        @pl.when(first)
        def _():
            for cp in save:
                cp.wait()


def _mixer_call(x_blocks, mod3, mod_row_of_block, ctx_k, ctx_v, rope, consts, *, seq_len,
                emit_kv):
    n_blocks = x_blocks.shape[0]
    ctx_len = 0 if ctx_k is None else ctx_k.shape[4]
    use_rope = rope is not None
    if ctx_len:
        assert seq_len == BLOCK_ROWS
        q_rows, n_keys, n_key_blocks = LATENT_Q_ROWS, ctx_len + BLOCK_ROWS, 1
    else:
        q_rows, n_keys, n_key_blocks = seq_len, seq_len, BLOCK_ROWS // seq_len
    n_q_cols = KV_GROUP * q_rows

    blk = lambda cols: pl.BlockSpec((None, BLOCK_ROWS, cols), lambda b: (b, 0, 0))
    args = [x_blocks, mod3]
    in_specs = [blk(D_MODEL),
                pl.BlockSpec((None, 1, 3 * D_MODEL), lambda b: (mod_row_of_block(b), 0, 0))]
    if ctx_len:
        args += [ctx_k, ctx_v]
        in_specs += [pl.BlockSpec((None, None, N_KV, HEAD_DIM, ctx_len),
                                  lambda b: (b, 0, 0, 0, 0))] * 2
    if use_rope:
        args += list(rope)
        in_specs += [_const_spec((BLOCK_ROWS, LANES))] * 2
    args += list(consts)
    w_in, w_out = consts[1], consts[7]
    cast_weights = w_in.dtype == F32
    in_hbm = (1, 7) if cast_weights else ()
    in_specs += [pl.BlockSpec(memory_space=pl.ANY) if k in in_hbm else _const_spec(a.shape)
                 for k, a in enumerate(consts)]

    out_shape = [jax.ShapeDtypeStruct((n_blocks, BLOCK_ROWS, D_MODEL), F32)]
    out_specs = [blk(D_MODEL)]
    if emit_kv:
        per_block = BLOCK_ROWS // seq_len
        out_shape += [jax.ShapeDtypeStruct(
            (n_blocks * per_block, 1, N_KV, HEAD_DIM, seq_len), F32)] * 2
        out_specs += [pl.BlockSpec((per_block, None, N_KV, HEAD_DIM, seq_len),
                                   lambda b: (b, 0, 0, 0, 0))] * 2

    scratch = [
        pltpu.VMEM((N_HEADS, BLOCK_ROWS, HEAD_DIM), BF16),
        pltpu.VMEM((N_KV, ctx_len + BLOCK_ROWS, HEAD_DIM), BF16),
        pltpu.VMEM((n_key_blocks, N_KV, V_ROWS, n_keys), BF16),
        pltpu.VMEM((BLOCK_ROWS, ATTN_DIM), F32),
        pltpu.VMEM((BLOCK_ROWS + 16, CONV_DIM), F32),
        pltpu.VMEM((BLOCK_ROWS, CONV_DIM), F32),
        pltpu.VMEM((2, N_KV, n_keys, n_q_cols), F32),
        pltpu.VMEM((2, N_KV, n_keys, n_q_cols), BF16),
        pltpu.VMEM((2, N_KV, 1, n_q_cols), F32),
    ]
    if cast_weights:
        hbm = pl.BlockSpec(memory_space=pl.ANY)
        out_shape += [jax.ShapeDtypeStruct(w_in.shape, BF16),
                      jax.ShapeDtypeStruct(w_out.shape, BF16)]
        out_specs += [hbm, hbm]
        scratch += [pltpu.VMEM(w_in.shape, BF16), pltpu.VMEM(w_out.shape, BF16),
                    pltpu.VMEM((2, MIX_CAST_ROWS, w_in.shape[1]), F32),
                    pltpu.VMEM((2, MIX_CAST_ROWS, w_out.shape[1]), F32),
                    pltpu.SemaphoreType.DMA((2,)), pltpu.SemaphoreType.DMA((2,))]
    kern = functools.partial(_mixer_kernel, seq_len=seq_len, ctx_len=ctx_len,
                             use_rope=use_rope, emit_kv=emit_kv, cast_weights=cast_weights)
    return pl.pallas_call(
        kern,
        grid=(n_blocks,),
        in_specs=in_specs,
        out_specs=out_specs,
        out_shape=out_shape,
        scratch_shapes=scratch,
        compiler_params=pltpu.CompilerParams(
            dimension_semantics=("arbitrary",), vmem_limit_bytes=VMEM_LIMIT),
        name="mixer_ctx" if emit_kv else "mixer_latent",
    )(*args)


def _ffn_rows(x_ref, mod_ref, nffn_ref, wgu_ref, wd_ref, out_ref):
    mod = mod_ref[...]
    shift = mod[:, 0:D_MODEL]
    scale1 = 1.0 + mod[:, D_MODEL:2 * D_MODEL]
    gate = mod[:, 2 * D_MODEL:3 * D_MODEL]
    for r0 in range(0, FFN_ROWS, FFN_SUB_ROWS):
        x = x_ref[r0:r0 + FFN_SUB_ROWS, :]
        hb = ((_rms_rows(x) * nffn_ref[...]) * scale1 + shift).astype(BF16)
        acc = None
        for c0, cw in FF_CHUNKS:
            gt = jnp.dot(hb, wgu_ref[:, c0:c0 + cw], preferred_element_type=F32)
            up = jnp.dot(hb, wgu_ref[:, D_FF + c0:D_FF + c0 + cw], preferred_element_type=F32)
            act = ((gt * jax.nn.sigmoid(gt)) * up).astype(BF16)
            part = jnp.dot(act, wd_ref[c0:c0 + cw, :], preferred_element_type=F32)
            acc = part if acc is None else acc + part
        out_ref[r0:r0 + FFN_SUB_ROWS, :] = x + gate * acc


def _ffn_kernel(x_ref, mod_ref, nffn_ref, wgu_ref, wd_ref, out_ref):
    _ffn_rows(x_ref, mod_ref, nffn_ref, wgu_ref, wd_ref, out_ref)


def _cast_rows(src_hbm, dst, stage, sem, chunk_rows):
    n = src_hbm.shape[0] // chunk_rows

    def fetch(c):
        return pltpu.make_async_copy(src_hbm.at[pl.ds(c * chunk_rows, chunk_rows), :],
                                     stage.at[c % 2], sem.at[c % 2])

    fetch(0).start()
    for c in range(n):
        if c + 1 < n:
            fetch(c + 1).start()
        fetch(c).wait()
        dst[c * chunk_rows:(c + 1) * chunk_rows, :] = stage[c % 2].astype(BF16)


def _ffn_cast_kernel(x_ref, mod_ref, nffn_ref, wgu_hbm, wd_hbm, out_ref, wgu_out, wd_out,
                     wgu_s, wd_s, stage_gu, stage_d, sem_in, sem_out):
    first = pl.program_id(0) == 0
    save = (pltpu.make_async_copy(wgu_s, wgu_out, sem_out.at[0]),
            pltpu.make_async_copy(wd_s, wd_out, sem_out.at[1]))

    @pl.when(first)
    def _():
        _cast_rows(wgu_hbm, wgu_s, stage_gu, sem_in, GU_CAST_ROWS)
        _cast_rows(wd_hbm, wd_s, stage_d, sem_in, DOWN_CAST_ROWS)
        for cp in save:
            cp.start()

    _ffn_rows(x_ref, mod_ref, nffn_ref, wgu_s, wd_s, out_ref)

    @pl.when(first)
    def _():
        for cp in save:
            cp.wait()


def _ffn_call(x_rows, mod3, mod_row_of_tile, norm_ffn, w_gate_up, w_down, name):
    n_rows = x_rows.shape[0]
    cast = w_gate_up.dtype == F32
    rows_spec = pl.BlockSpec((FFN_ROWS, D_MODEL), lambda i: (i, 0))
    in_specs = [rows_spec,
                pl.BlockSpec((None, 1, 3 * D_MODEL), lambda i: (mod_row_of_tile(i), 0, 1)),
                _const_spec(norm_ffn.shape)]
    out_shape = [jax.ShapeDtypeStruct((n_rows, D_MODEL), F32)]
    out_specs = [rows_spec]
    scratch = []
    if cast:
        hbm = pl.BlockSpec(memory_space=pl.ANY)
        in_specs += [hbm, hbm]
        out_shape += [jax.ShapeDtypeStruct(w_gate_up.shape, BF16),
                      jax.ShapeDtypeStruct(w_down.shape, BF16)]
        out_specs += [hbm, hbm]
        scratch = [pltpu.VMEM(w_gate_up.shape, BF16), pltpu.VMEM(w_down.shape, BF16),
                   pltpu.VMEM((2, GU_CAST_ROWS, w_gate_up.shape[1]), F32),
                   pltpu.VMEM((2, DOWN_CAST_ROWS, w_down.shape[1]), F32),
                   pltpu.SemaphoreType.DMA((2,)), pltpu.SemaphoreType.DMA((2,))]
    else:
        in_specs += [_const_spec(w_gate_up.shape), _const_spec(w_down.shape)]
    res = pl.pallas_call(
        _ffn_cast_kernel if cast else _ffn_kernel,
        grid=(n_rows // FFN_ROWS,),
        in_specs=in_specs,
        out_specs=out_specs,
        out_shape=out_shape,
        scratch_shapes=scratch,
        compiler_params=pltpu.CompilerParams(
            dimension_semantics=("arbitrary",), vmem_limit_bytes=VMEM_LIMIT),
        name=name,
    )(x_rows, mod3, norm_ffn, w_gate_up, w_down)
    return res if cast else res[0]


def _rope_tables(n_tokens):
    rows = n_tokens // GRID_W
    row = jnp.repeat(jnp.arange(rows, dtype=F32), GRID_W)
    col = jnp.tile(jnp.arange(GRID_W, dtype=F32), rows)
    inv = 1.0 / (ROPE_THETA ** (jnp.arange(ROT_PAIRS, dtype=F32) / ROT_PAIRS))
    ang = jnp.stack([row[:, None] * inv, col[:, None] * inv], axis=1)
    cos, sin = jnp.cos(ang), jnp.sin(ang)
    cos_h = jnp.concatenate([cos, cos], axis=-1).reshape(n_tokens, HEAD_DIM)
    sin_h = jnp.concatenate([-sin, sin], axis=-1).reshape(n_tokens, HEAD_DIM)
    reps = LANES // HEAD_DIM
    return jnp.tile(cos_h, (1, reps)), jnp.tile(sin_h, (1, reps))


def _keys_minor(kv):
    return jnp.transpose(kv, (0, 1, 3, 4, 2))


def _group_mean_matrix():
    idx = np.arange(MXU_DIM) // HEAD_DIM
    g = (idx[:, None] == idx[None, :]).astype(np.float32) / HEAD_DIM
    return jnp.asarray(np.concatenate([g, g], axis=0), dtype=BF16)


def kernel(x_prompt, x_sample, c, cache_k, cache_v, c_ctx, norm_mix, norm_ffn, w_ada, b_ada,
           w_in, q_norm, k_norm, conv_w, attn_out_norm, conv_out_norm, w_out, w_gate_up, w_down):
    depth = w_in.shape[0]
    assert depth == 1
    n_prompt, seq, _ = x_prompt.shape
    n_sample, dec_seq, _ = x_sample.shape
    assert dec_seq == BLOCK_ROWS and BLOCK_ROWS % seq == 0 and n_sample <= CTX_ROW

    cond = jnp.zeros((COND_ROWS, D_MODEL), F32)
    cond = cond.at[0:n_sample].set(c).at[CTX_ROW].set(c_ctx)
    mod = _ada_call(cond, w_ada[0], b_ada[0][None, :])
    mod3 = mod.reshape(COND_ROWS, 1, 6 * D_MODEL)

    def mixer_consts(w_in_l, w_out_l):
        return (
            norm_mix[0][None, :],
            w_in_l,
            _group_mean_matrix(),
            jnp.concatenate([jnp.tile(q_norm[0], N_HEADS), jnp.tile(k_norm[0], N_KV)])[None, :],
            conv_w[0],
            attn_out_norm[0][None, :],
            conv_out_norm[0][None, :],
            w_out_l,
        )
    nffn = norm_ffn[0][None, :]

    per_block = BLOCK_ROWS // seq
    xp_blocks = x_prompt.reshape(n_prompt // per_block, BLOCK_ROWS, D_MODEL)
    xp1, k_new, v_new, w_in_b, w_out_b = _mixer_call(
        xp_blocks, mod3, lambda b: CTX_ROW, None, None, None, mixer_consts(w_in[0], w_out[0]),
        seq_len=seq, emit_kv=True)
    consts = mixer_consts(w_in_b, w_out_b)
    yp, wgu, wd = _ffn_call(xp1.reshape(-1, D_MODEL), mod3, lambda i: CTX_ROW, nffn,
                            w_gate_up[0], w_down[0], "ffn_ctx")

    (xs1,) = _mixer_call(x_sample, mod3, lambda b: b, _keys_minor(cache_k), _keys_minor(cache_v),
                         _rope_tables(dec_seq), consts, seq_len=dec_seq, emit_kv=False)
    tiles_per_seq = dec_seq // FFN_ROWS
    ys = _ffn_call(xs1.reshape(-1, D_MODEL), mod3, lambda i: i // tiles_per_seq, nffn, wgu, wd,
                   "ffn_latent")

    return (yp.reshape(n_prompt, seq, D_MODEL),
            ys.reshape(n_sample, dec_seq, D_MODEL),
            jnp.transpose(k_new, (0, 1, 4, 2, 3)),
            jnp.transpose(v_new, (0, 1, 4, 2, 3)))
```

```python
import functools

import numpy as np
import jax
import jax.numpy as jnp
from jax import lax
from jax.experimental import pallas as pl
from jax.experimental.pallas import tpu as pltpu

D_MODEL = 1024
HEAD_DIM = 64
ATTN_DIM = 512
N_HEADS = 8
N_KV = 2
KV_GROUP = N_HEADS // N_KV
KV_DIM = N_KV * HEAD_DIM
CONV_DIM = 512
D_FF = 2816
QK_DIM = ATTN_DIM + KV_DIM
QKV_DIM = ATTN_DIM + 2 * KV_DIM
IN_DIM = QKV_DIM + 3 * CONV_DIM
GRID_W = 64
ROT_PAIRS = HEAD_DIM // 4
ROPE_THETA = 10000.0
RMS_EPS = 1e-6
Q_SCALE = HEAD_DIM ** -0.5 * 1.4426950408889634
V_ROWS = HEAD_DIM + 16
MAX_UNSHIFTED_SCORE = 64.0

LANES = 128
MXU_DIM = 256
BLOCK_ROWS = 1024
CHUNK_ROWS = 512
LATENT_Q_ROWS = 128
P_ROWS = 16
FFN_ROWS = 1024
FFN_SUB_ROWS = 512
GU_CAST_ROWS = 128
DOWN_CAST_ROWS = 352
FF_CHUNKS = ((0, 1024), (1024, 1024), (2048, 768))
COND_ROWS = 16
MOD_ROWS = 8
CTX_ROW = 8
ADA_COLS = 2048
VMEM_LIMIT = 56 * 1024 * 1024

F32 = jnp.float32
BF16 = jnp.bfloat16


def _mod_spec(row, half):
    blk = 0 if row is None else row // MOD_ROWS
    return pl.BlockSpec((MOD_ROWS, 3 * D_MODEL), lambda i: (blk, half))


def _mod_row(mod_ref, row):
    r = pl.program_id(0) if row is None else row % MOD_ROWS
    return mod_ref[pl.ds(r, 1), :]


def _const_spec(shape):
    nd = len(shape)
    return pl.BlockSpec(shape, lambda *_: (0,) * nd, pipeline_mode=pl.Buffered(1))


def _ada_kernel(cond_ref, w_ref, b_ref, out_ref):
    c = cond_ref[...]
    s = (c * jax.nn.sigmoid(c)).astype(BF16)
    out_ref[...] = jnp.dot(s, w_ref[...].astype(BF16), preferred_element_type=F32) + b_ref[...]


def _ada_call(cond, w_ada, b_ada):
    n = w_ada.shape[1]
    return pl.pallas_call(
        _ada_kernel,
        grid=(n // ADA_COLS,),
        in_specs=[
            pl.BlockSpec((COND_ROWS, D_MODEL), lambda j: (0, 0)),
            pl.BlockSpec((D_MODEL, ADA_COLS), lambda j: (0, j)),
            pl.BlockSpec((1, ADA_COLS), lambda j: (0, j)),
        ],
        out_specs=pl.BlockSpec((COND_ROWS, ADA_COLS), lambda j: (0, j)),
        out_shape=jax.ShapeDtypeStruct((COND_ROWS, n), F32),
        compiler_params=pltpu.CompilerParams(
            dimension_semantics=("arbitrary",), vmem_limit_bytes=VMEM_LIMIT),
        name="ada_rows",
    )(cond, w_ada, b_ada)


def _rms_rows(x):
    return x * lax.rsqrt(jnp.mean(x * x, axis=-1, keepdims=True) + RMS_EPS)


def _mixer_kernel(*refs, seq_len, ctx_len, use_rope, emit_kv, mod_row):
    it = iter(refs)
    x_ref, mod_ref = next(it), next(it)
    if ctx_len:
        ck_ref, cv_ref = next(it), next(it)
    if use_rope:
        cos_ref, sin_ref = next(it), next(it)
    (nmix_ref, win_ref, gsum_ref, qkg_ref, convw_ref, ga_ref, gc_ref, wout_ref) = (
        next(it) for _ in range(8))
    out_ref = next(it)
    if emit_kv:
        ko_ref, vo_ref = next(it), next(it)
    (q_s, k_s, vt_s, attn_s, t_s, gb_s, s_ref, p_ref, m_ref) = (next(it) for _ in range(9))

    n_chunks = BLOCK_ROWS // CHUNK_ROWS
    if ctx_len:
        q_rows, n_keys, key_blk = LATENT_Q_ROWS, ctx_len + seq_len, BLOCK_ROWS
    else:
        q_rows, n_keys, key_blk = seq_len, seq_len, seq_len
    n_qb = BLOCK_ROWS // q_rows

    mod = _mod_row(mod_ref, mod_row)
    shift = mod[:, 0:D_MODEL]
    scale1 = 1.0 + mod[:, D_MODEL:2 * D_MODEL]
    gate = mod[:, 2 * D_MODEL:3 * D_MODEL]

    def put_values_t(blk, off, vt):
        vt = vt.astype(BF16)
        for g in range(N_KV):
            vt_s[blk, g, 0:HEAD_DIM, off:off + vt.shape[1]] = vt[g * HEAD_DIM:(g + 1) * HEAD_DIM, :]

    tail = (lax.broadcasted_iota(jnp.int32, (V_ROWS - HEAD_DIM, n_keys), 0) == 0).astype(BF16)
    for blk in range(vt_s.shape[0]):
        for g in range(N_KV):
            vt_s[blk, g, HEAD_DIM:V_ROWS, :] = tail

    if ctx_len:
        ck = ck_ref[...].reshape(KV_DIM, ctx_len).T.astype(BF16)
        for g in range(N_KV):
            k_s[g, 0:ctx_len, :] = ck[:, g * HEAD_DIM:(g + 1) * HEAD_DIM]
        put_values_t(0, 0, cv_ref[...].reshape(KV_DIM, ctx_len))

    if use_rope:
        lane = lax.broadcasted_iota(jnp.int32, (CHUNK_ROWS, LANES), 1)
        first_half = (lane % (2 * ROT_PAIRS)) < ROT_PAIRS

    for c in range(n_chunks):
        r0 = c * CHUNK_ROWS
        x = x_ref[r0:r0 + CHUNK_ROWS, :]
        h = (_rms_rows(x) * nmix_ref[...]) * scale1 + shift
        hb = h.astype(BF16)
        qkv = jnp.dot(hb, win_ref[:, 0:QKV_DIM], preferred_element_type=F32)

        groups = []
        for g0 in range(0, QKV_DIM, MXU_DIM):
            sq = qkv[:, g0:g0 + MXU_DIM]
            groups.append(jnp.dot((sq * sq).astype(BF16), gsum_ref[...],
                                  preferred_element_type=F32))
        ms = jnp.concatenate(groups, axis=-1)[:, 0:QK_DIM]
        qk = (qkv[:, 0:QK_DIM] * lax.rsqrt(ms + RMS_EPS)) * qkg_ref[...]
        vv = qkv[:, QK_DIM:QKV_DIM]

        if emit_kv:
            for r1 in range(0, CHUNK_ROWS, seq_len):
                kt = qk[r1:r1 + seq_len, ATTN_DIM:QK_DIM].T
                ko_ref[(r0 + r1) // seq_len] = kt.reshape(N_KV, HEAD_DIM, seq_len)

        for cg in range(QK_DIM // LANES):
            xg = qk[:, cg * LANES:(cg + 1) * LANES]
            if use_rope:
                cs = cos_ref[r0:r0 + CHUNK_ROWS, :]
                sn = sin_ref[r0:r0 + CHUNK_ROWS, :]
                partner = jnp.where(first_half,
                                    pltpu.roll(xg, LANES - ROT_PAIRS, axis=1),
                                    pltpu.roll(xg, ROT_PAIRS, axis=1))
                xg = xg * cs + partner * sn
            if cg < ATTN_DIM // LANES:
                xb = (xg * Q_SCALE).astype(BF16)
                q_s[2 * cg, r0:r0 + CHUNK_ROWS, :] = xb[:, 0:HEAD_DIM]
                q_s[2 * cg + 1, r0:r0 + CHUNK_ROWS, :] = xb[:, HEAD_DIM:LANES]
            else:
                xb = xg.astype(BF16)
                for g in range(N_KV):
                    k_s[g, ctx_len + r0:ctx_len + r0 + CHUNK_ROWS, :] = (
                        xb[:, g * HEAD_DIM:(g + 1) * HEAD_DIM])
        w = min(key_blk, CHUNK_ROWS)
        for r1 in range(r0, r0 + CHUNK_ROWS, w):
            blk, off = (0, ctx_len + r1) if ctx_len else (r1 // key_blk, 0)
            vt = vv[r1 - r0:r1 - r0 + w, :].T
            put_values_t(blk, off, vt)
            if emit_kv:
                vo_ref[blk] = vt.reshape(N_KV, HEAD_DIM, seq_len)

        cvp = jnp.dot(hb, win_ref[:, QKV_DIM:IN_DIM], preferred_element_type=F32)
        gb_s[r0:r0 + CHUNK_ROWS, :] = cvp[:, 0:CONV_DIM]
        t_s[8 + r0:8 + r0 + CHUNK_ROWS, :] = (cvp[:, CONV_DIM:2 * CONV_DIM]
                                              * cvp[:, 2 * CONV_DIM:3 * CONV_DIM])
    t_s[0:8, :] = jnp.zeros((8, CONV_DIM), F32)
    t_s[8 + BLOCK_ROWS:16 + BLOCK_ROWS, :] = jnp.zeros((8, CONV_DIM), F32)

    def scores_t(qb, g):
        r0 = pl.multiple_of(qb * q_rows, q_rows)
        k0 = 0 if ctx_len else r0
        qs = jnp.concatenate(
            [q_s[KV_GROUP * g + j, pl.ds(r0, q_rows), :] for j in range(KV_GROUP)], axis=0)
        kk = k_s[g, pl.ds(k0, n_keys), :]
        return lax.dot_general(kk, qs, (((1,), (1,)), ((), ())),
                               preferred_element_type=F32)

    def values_out(qb, g, p):
        r0 = pl.multiple_of(qb * q_rows, q_rows)
        blk = 0 if ctx_len else qb
        ot = jnp.dot(vt_s[blk, g], p, preferred_element_type=F32)
        ot = ot[0:HEAD_DIM, :] / ot[HEAD_DIM:HEAD_DIM + 1, :]
        for jj in range(KV_GROUP // 2):
            pair_t = jnp.concatenate(
                [ot[:, (2 * jj) * q_rows:(2 * jj + 1) * q_rows],
                 ot[:, (2 * jj + 1) * q_rows:(2 * jj + 2) * q_rows]], axis=0)
            col = (KV_GROUP * g + 2 * jj) * HEAD_DIM
            attn_s[pl.ds(r0, q_rows), col:col + LANES] = pair_t.T

    def pipeline(step):
        step(-1, 1, first=True)

        def body(j, carry):
            for par in range(2):
                pl.when(j % 2 == par)(functools.partial(step, j, par))
            return carry

        lax.fori_loop(0, n_qb - 1, body, 0)
        step(n_qb - 1, (n_qb - 1) % 2, last=True)

    def pipeline_pairs(step):
        assert n_qb % 2 == 0
        step(-1, 1, first=True)
        step(0, 0)

        def body(t, carry):
            step(2 * t + 1, 1)
            step(2 * t + 2, 0)
            return carry

        lax.fori_loop(0, (n_qb - 2) // 2, body, 0)
        step(n_qb - 1, 1, last=True)

    def shifted_step(j, par, first=False, last=False):
        for g in range(N_KV):
            if not last:
                s = scores_t(j + 1, g)
                s_ref[1 - par, g] = s
                m_ref[1 - par, g] = jnp.max(s, axis=0, keepdims=True)
            if not first:
                m = m_ref[par, g]
                for k1 in range(0, n_keys, P_ROWS):
                    p_ref[0, g, k1:k1 + P_ROWS, :] = jnp.exp2(
                        s_ref[par, g, k1:k1 + P_ROWS, :] - m).astype(BF16)
                values_out(j, g, p_ref[0, g])

    def unshifted_step(j, par, first=False, last=False):
        for g in range(N_KV):
            if not last:
                p_ref[1 - par, g] = jnp.exp2(scores_t(j + 1, g)).astype(BF16)
            if not first:
                values_out(j, g, p_ref[par, g])

    gains = jnp.abs(qkg_ref[...])
    q_bound = (Q_SCALE * Q_SCALE * HEAD_DIM) * jnp.max(gains[:, 0:ATTN_DIM]) ** 2
    k_bound = HEAD_DIM * jnp.max(gains[:, ATTN_DIM:QK_DIM]) ** 2
    if ctx_len:
        ck2 = ck_ref[...]
        k_bound = jnp.maximum(k_bound, jnp.max(jnp.sum(ck2 * ck2, axis=1)))
    small_scores = q_bound * k_bound <= MAX_UNSHIFTED_SCORE ** 2
    pl.when(small_scores)(lambda: pipeline_pairs(unshifted_step))
    pl.when(jnp.logical_not(small_scores))(lambda: pipeline(shifted_step))

    w0 = convw_ref[0:1, :]
    w1 = convw_ref[1:2, :]
    w2 = convw_ref[2:3, :]
    for r0 in range(0, BLOCK_ROWS, CHUNK_ROWS):
        rows = slice(r0, r0 + CHUNK_ROWS)
        t_prev = t_s[7 + r0:7 + r0 + CHUNK_ROWS, :]
        t_mid = t_s[8 + r0:8 + r0 + CHUNK_ROWS, :]
        t_next = t_s[9 + r0:9 + r0 + CHUNK_ROWS, :]
        if seq_len < BLOCK_ROWS:
            pos = (lax.broadcasted_iota(jnp.int32, (CHUNK_ROWS, 1), 0) + r0) % seq_len
            t_prev = jnp.where(pos == 0, 0.0, t_prev)
            t_next = jnp.where(pos == seq_len - 1, 0.0, t_next)
        y = gb_s[rows, :] * (w0 * t_prev + w1 * t_mid + w2 * t_next)
        yn = _rms_rows(y) * gc_ref[...]
        an = _rms_rows(attn_s[rows, :]) * ga_ref[...]
        merged = jnp.concatenate([an, yn], axis=-1).astype(BF16)
        mix = jnp.dot(merged, wout_ref[...], preferred_element_type=F32)
        out_ref[rows, :] = x_ref[rows, :] + gate * mix


def _mixer_call(x_blocks, mod, mod_row, ctx_k, ctx_v, rope, consts, *, seq_len, emit_kv):
    n_blocks = x_blocks.shape[0]
    ctx_len = 0 if ctx_k is None else ctx_k.shape[4]
    use_rope = rope is not None
    if ctx_len:
        assert seq_len == BLOCK_ROWS
        q_rows, n_keys, n_key_blocks = LATENT_Q_ROWS, ctx_len + BLOCK_ROWS, 1
    else:
        q_rows, n_keys, n_key_blocks = seq_len, seq_len, BLOCK_ROWS // seq_len
    n_q_cols = KV_GROUP * q_rows

    blk = lambda cols: pl.BlockSpec((None, BLOCK_ROWS, cols), lambda b: (b, 0, 0))
    assert mod_row is not None or n_blocks <= MOD_ROWS
    args = [x_blocks, mod]
    in_specs = [blk(D_MODEL), _mod_spec(mod_row, 0)]
    if ctx_len:
        args += [ctx_k, ctx_v]
        in_specs += [pl.BlockSpec((None, None, N_KV, HEAD_DIM, ctx_len),
                                  lambda b: (b, 0, 0, 0, 0))] * 2
    if use_rope:
        args += list(rope)
        in_specs += [_const_spec((BLOCK_ROWS, LANES))] * 2
    args += list(consts)
    in_specs += [_const_spec(a.shape) for a in consts]

    out_shape = [jax.ShapeDtypeStruct((n_blocks, BLOCK_ROWS, D_MODEL), F32)]
    out_specs = [blk(D_MODEL)]
    if emit_kv:
        per_block = BLOCK_ROWS // seq_len
        out_shape += [jax.ShapeDtypeStruct(
            (n_blocks * per_block, 1, N_KV, HEAD_DIM, seq_len), F32)] * 2
        out_specs += [pl.BlockSpec((per_block, None, N_KV, HEAD_DIM, seq_len),
                                   lambda b: (b, 0, 0, 0, 0))] * 2

    scratch = [
        pltpu.VMEM((N_HEADS, BLOCK_ROWS, HEAD_DIM), BF16),
        pltpu.VMEM((N_KV, ctx_len + BLOCK_ROWS, HEAD_DIM), BF16),
        pltpu.VMEM((n_key_blocks, N_KV, V_ROWS, n_keys), BF16),
        pltpu.VMEM((BLOCK_ROWS, ATTN_DIM), F32),
        pltpu.VMEM((BLOCK_ROWS + 16, CONV_DIM), F32),
        pltpu.VMEM((BLOCK_ROWS, CONV_DIM), F32),
        pltpu.VMEM((2, N_KV, n_keys, n_q_cols), F32),
        pltpu.VMEM((2, N_KV, n_keys, n_q_cols), BF16),
        pltpu.VMEM((2, N_KV, 1, n_q_cols), F32),
    ]
    kern = functools.partial(_mixer_kernel, seq_len=seq_len, ctx_len=ctx_len,
                             use_rope=use_rope, emit_kv=emit_kv, mod_row=mod_row)
    return pl.pallas_call(
        kern,
        grid=(n_blocks,),
        in_specs=in_specs,
        out_specs=out_specs,
        out_shape=out_shape,
        scratch_shapes=scratch,
        compiler_params=pltpu.CompilerParams(
            dimension_semantics=("arbitrary",), vmem_limit_bytes=VMEM_LIMIT),
        name="mixer_ctx" if emit_kv else "mixer_latent",
    )(*args)


def _ffn_rows(x_ref, mod_ref, nffn_ref, wgu_ref, wd_ref, out_ref, mod_row):
    mod = _mod_row(mod_ref, mod_row)
    shift = mod[:, 0:D_MODEL]
    scale1 = 1.0 + mod[:, D_MODEL:2 * D_MODEL]
    gate = mod[:, 2 * D_MODEL:3 * D_MODEL]
    for r0 in range(0, FFN_ROWS, FFN_SUB_ROWS):
        x = x_ref[r0:r0 + FFN_SUB_ROWS, :]
        hb = ((_rms_rows(x) * nffn_ref[...]) * scale1 + shift).astype(BF16)
        acc = None
        for c0, cw in FF_CHUNKS:
            gt = jnp.dot(hb, wgu_ref[:, c0:c0 + cw], preferred_element_type=F32)
            up = jnp.dot(hb, wgu_ref[:, D_FF + c0:D_FF + c0 + cw], preferred_element_type=F32)
            act = ((gt * jax.nn.sigmoid(gt)) * up).astype(BF16)
            part = jnp.dot(act, wd_ref[c0:c0 + cw, :], preferred_element_type=F32)
            acc = part if acc is None else acc + part
        out_ref[r0:r0 + FFN_SUB_ROWS, :] = x + gate * acc


def _ffn_kernel(x_ref, mod_ref, nffn_ref, wgu_ref, wd_ref, out_ref, *, mod_row):
    _ffn_rows(x_ref, mod_ref, nffn_ref, wgu_ref, wd_ref, out_ref, mod_row)


def _cast_rows(src_hbm, dst, stage, sem, chunk_rows):
    n = src_hbm.shape[0] // chunk_rows

    def fetch(c):
        return pltpu.make_async_copy(src_hbm.at[pl.ds(c * chunk_rows, chunk_rows), :],
                                     stage.at[c % 2], sem.at[c % 2])

    fetch(0).start()
    for c in range(n):
        if c + 1 < n:
            fetch(c + 1).start()
        fetch(c).wait()
        dst[c * chunk_rows:(c + 1) * chunk_rows, :] = stage[c % 2].astype(BF16)


def _ffn_cast_kernel(x_ref, mod_ref, nffn_ref, wgu_hbm, wd_hbm, out_ref, wgu_out, wd_out,
                     wgu_s, wd_s, stage_gu, stage_d, sem_in, sem_out, *, mod_row):
    first = pl.program_id(0) == 0
    save = (pltpu.make_async_copy(wgu_s, wgu_out, sem_out.at[0]),
            pltpu.make_async_copy(wd_s, wd_out, sem_out.at[1]))

    @pl.when(first)
    def _():
        _cast_rows(wgu_hbm, wgu_s, stage_gu, sem_in, GU_CAST_ROWS)
        _cast_rows(wd_hbm, wd_s, stage_d, sem_in, DOWN_CAST_ROWS)
        for cp in save:
            cp.start()

    _ffn_rows(x_ref, mod_ref, nffn_ref, wgu_s, wd_s, out_ref, mod_row)

    @pl.when(first)
    def _():
        for cp in save:
            cp.wait()


def _ffn_call(x_rows, mod, mod_row, norm_ffn, w_gate_up, w_down, name):
    n_rows = x_rows.shape[0]
    assert mod_row is not None or n_rows // FFN_ROWS <= MOD_ROWS
    cast = w_gate_up.dtype == F32
    rows_spec = pl.BlockSpec((FFN_ROWS, D_MODEL), lambda i: (i, 0))
    in_specs = [rows_spec, _mod_spec(mod_row, 1), _const_spec(norm_ffn.shape)]
    out_shape = [jax.ShapeDtypeStruct((n_rows, D_MODEL), F32)]
    out_specs = [rows_spec]
    scratch = []
    if cast:
        hbm = pl.BlockSpec(memory_space=pl.ANY)
        in_specs += [hbm, hbm]
        out_shape += [jax.ShapeDtypeStruct(w_gate_up.shape, BF16),
                      jax.ShapeDtypeStruct(w_down.shape, BF16)]
        out_specs += [hbm, hbm]
        scratch = [pltpu.VMEM(w_gate_up.shape, BF16), pltpu.VMEM(w_down.shape, BF16),
                   pltpu.VMEM((2, GU_CAST_ROWS, w_gate_up.shape[1]), F32),
                   pltpu.VMEM((2, DOWN_CAST_ROWS, w_down.shape[1]), F32),
                   pltpu.SemaphoreType.DMA((2,)), pltpu.SemaphoreType.DMA((2,))]
    else:
        in_specs += [_const_spec(w_gate_up.shape), _const_spec(w_down.shape)]
    res = pl.pallas_call(
        functools.partial(_ffn_cast_kernel if cast else _ffn_kernel, mod_row=mod_row),
        grid=(n_rows // FFN_ROWS,),
        in_specs=in_specs,
        out_specs=out_specs,
        out_shape=out_shape,
        scratch_shapes=scratch,
        compiler_params=pltpu.CompilerParams(
            dimension_semantics=("arbitrary",), vmem_limit_bytes=VMEM_LIMIT),
        name=name,
    )(x_rows, mod, norm_ffn, w_gate_up, w_down)
    return res if cast else res[0]


def _rope_tables(n_tokens):
    rows = n_tokens // GRID_W
    row = jnp.repeat(jnp.arange(rows, dtype=F32), GRID_W)
    col = jnp.tile(jnp.arange(GRID_W, dtype=F32), rows)
    inv = 1.0 / (ROPE_THETA ** (jnp.arange(ROT_PAIRS, dtype=F32) / ROT_PAIRS))
    ang = jnp.stack([row[:, None] * inv, col[:, None] * inv], axis=1)
    cos, sin = jnp.cos(ang), jnp.sin(ang)
    cos_h = jnp.concatenate([cos, cos], axis=-1).reshape(n_tokens, HEAD_DIM)
    sin_h = jnp.concatenate([-sin, sin], axis=-1).reshape(n_tokens, HEAD_DIM)
    reps = LANES // HEAD_DIM
    return jnp.tile(cos_h, (1, reps)), jnp.tile(sin_h, (1, reps))


def _keys_minor(kv):
    return jnp.transpose(kv, (0, 1, 3, 4, 2))


def _group_mean_matrix():
    idx = np.arange(MXU_DIM) // HEAD_DIM
    g = (idx[:, None] == idx[None, :]).astype(np.float32) / HEAD_DIM
    return jnp.asarray(g, dtype=BF16)


def kernel(x_prompt, x_sample, c, cache_k, cache_v, c_ctx, norm_mix, norm_ffn, w_ada, b_ada,
           w_in, q_norm, k_norm, conv_w, attn_out_norm, conv_out_norm, w_out, w_gate_up, w_down):
    depth = w_in.shape[0]
    assert depth == 1
    n_prompt, seq, _ = x_prompt.shape
    n_sample, dec_seq, _ = x_sample.shape
    assert dec_seq == BLOCK_ROWS and BLOCK_ROWS % seq == 0 and n_sample <= CTX_ROW

    cond = jnp.zeros((COND_ROWS, D_MODEL), F32)
    cond = cond.at[0:n_sample].set(c).at[CTX_ROW].set(c_ctx)
    mod = _ada_call(cond, w_ada[0], b_ada[0][None, :])

    consts = (
        norm_mix[0][None, :],
        w_in[0].astype(BF16),
        _group_mean_matrix(),
        jnp.concatenate([jnp.tile(q_norm[0], N_HEADS), jnp.tile(k_norm[0], N_KV)])[None, :],
        conv_w[0],
        attn_out_norm[0][None, :],
        conv_out_norm[0][None, :],
        w_out[0].astype(BF16),
    )
    nffn = norm_ffn[0][None, :]

    per_block = BLOCK_ROWS // seq
    xp_blocks = x_prompt.reshape(n_prompt // per_block, BLOCK_ROWS, D_MODEL)
    xp1, k_new, v_new = _mixer_call(xp_blocks, mod, CTX_ROW, None, None, None, consts,
                                    seq_len=seq, emit_kv=True)
    yp, wgu, wd = _ffn_call(xp1.reshape(-1, D_MODEL), mod, CTX_ROW, nffn,
                            w_gate_up[0], w_down[0], "ffn_ctx")

    assert dec_seq == FFN_ROWS
    (xs1,) = _mixer_call(x_sample, mod, None, _keys_minor(cache_k), _keys_minor(cache_v),
                         _rope_tables(dec_seq), consts, seq_len=dec_seq, emit_kv=False)
    ys = _ffn_call(xs1.reshape(-1, D_MODEL), mod, None, nffn, wgu, wd, "ffn_latent")

    return (yp.reshape(n_prompt, seq, D_MODEL),
            ys.reshape(n_sample, dec_seq, D_MODEL),
            jnp.transpose(k_new, (0, 1, 4, 2, 3)),
            jnp.transpose(v_new, (0, 1, 4, 2, 3)))
```

```python
import functools

import numpy as np
import jax
import jax.numpy as jnp
from jax import lax
from jax.experimental import pallas as pl
from jax.experimental.pallas import tpu as pltpu

D_MODEL = 1024
HEAD_DIM = 64
ATTN_DIM = 512
N_HEADS = 8
N_KV = 2
KV_GROUP = N_HEADS // N_KV
KV_DIM = N_KV * HEAD_DIM
CONV_DIM = 512
D_FF = 2816
QK_DIM = ATTN_DIM + KV_DIM
QKV_DIM = ATTN_DIM + 2 * KV_DIM
IN_DIM = QKV_DIM + 3 * CONV_DIM
GRID_W = 64
ROT_PAIRS = HEAD_DIM // 4
ROPE_THETA = 10000.0
RMS_EPS = 1e-6
Q_SCALE = HEAD_DIM ** -0.5 * 1.4426950408889634
V_ROWS = HEAD_DIM + 16
MAX_UNSHIFTED_SCORE = 64.0

LANES = 128
MXU_DIM = 256
BLOCK_ROWS = 1024
CHUNK_ROWS = 512
LATENT_Q_ROWS = 256
P_ROWS = 16
FFN_ROWS = 1024
FFN_SUB_ROWS = 512
GU_CAST_ROWS = 128
DOWN_CAST_ROWS = 352
FF_CHUNKS = ((0, 1024), (1024, 1024), (2048, 768))
COND_ROWS = 16
MOD_ROWS = 8
CTX_ROW = 8
ADA_COLS = 2048
VMEM_LIMIT = 56 * 1024 * 1024

F32 = jnp.float32
BF16 = jnp.bfloat16


def _mod_spec(row, half):
    blk = 0 if row is None else row // MOD_ROWS
    return pl.BlockSpec((MOD_ROWS, 3 * D_MODEL), lambda i: (blk, half))


def _mod_row(mod_ref, row):
    r = pl.program_id(0) if row is None else row % MOD_ROWS
    return mod_ref[pl.ds(r, 1), :]


def _const_spec(shape):
    nd = len(shape)
    return pl.BlockSpec(shape, lambda *_: (0,) * nd, pipeline_mode=pl.Buffered(1))


def _ada_kernel(cond_ref, w_ref, b_ref, out_ref):
    c = cond_ref[...]
    s = (c * jax.nn.sigmoid(c)).astype(BF16)
    out_ref[...] = jnp.dot(s, w_ref[...].astype(BF16), preferred_element_type=F32) + b_ref[...]


def _ada_call(cond, w_ada, b_ada):
    n = w_ada.shape[1]
    return pl.pallas_call(
        _ada_kernel,
        grid=(n // ADA_COLS,),
        in_specs=[
            pl.BlockSpec((COND_ROWS, D_MODEL), lambda j: (0, 0)),
            pl.BlockSpec((D_MODEL, ADA_COLS), lambda j: (0, j)),
            pl.BlockSpec((1, ADA_COLS), lambda j: (0, j)),
        ],
        out_specs=pl.BlockSpec((COND_ROWS, ADA_COLS), lambda j: (0, j)),
        out_shape=jax.ShapeDtypeStruct((COND_ROWS, n), F32),
        compiler_params=pltpu.CompilerParams(
            dimension_semantics=("arbitrary",), vmem_limit_bytes=VMEM_LIMIT),
        name="ada_rows",
    )(cond, w_ada, b_ada)


def _rms_rows(x):
    return x * lax.rsqrt(jnp.mean(x * x, axis=-1, keepdims=True) + RMS_EPS)


def _mixer_kernel(*refs, seq_len, ctx_len, use_rope, emit_kv, mod_row):
    it = iter(refs)
    x_ref, mod_ref = next(it), next(it)
    if ctx_len:
        ck_ref, cv_ref = next(it), next(it)
    if use_rope:
        cos_ref, sin_ref = next(it), next(it)
    (nmix_ref, win_ref, gsum_ref, qkg_ref, convw_ref, ga_ref, gc_ref, wout_ref) = (
        next(it) for _ in range(8))
    out_ref = next(it)
    if emit_kv:
        ko_ref, vo_ref = next(it), next(it)
    (q_s, k_s, vt_s, attn_s, t_s, gb_s, s_ref, p_ref) = (next(it) for _ in range(8))

    n_chunks = BLOCK_ROWS // CHUNK_ROWS
    if ctx_len:
        q_rows, n_keys, key_blk = LATENT_Q_ROWS, ctx_len + seq_len, BLOCK_ROWS
    else:
        q_rows, n_keys, key_blk = seq_len, seq_len, seq_len
    n_qb = BLOCK_ROWS // q_rows

    mod = _mod_row(mod_ref, mod_row)
    shift = mod[:, 0:D_MODEL]
    scale1 = 1.0 + mod[:, D_MODEL:2 * D_MODEL]
    gate = mod[:, 2 * D_MODEL:3 * D_MODEL]

    def put_values_t(blk, off, vt):
        vt = vt.astype(BF16)
        for g in range(N_KV):
            vt_s[blk, g, 0:HEAD_DIM, off:off + vt.shape[1]] = vt[g * HEAD_DIM:(g + 1) * HEAD_DIM, :]

    tail = (lax.broadcasted_iota(jnp.int32, (V_ROWS - HEAD_DIM, n_keys), 0) == 0).astype(BF16)
    for blk in range(vt_s.shape[0]):
        for g in range(N_KV):
            vt_s[blk, g, HEAD_DIM:V_ROWS, :] = tail

    if ctx_len:
        ck = ck_ref[...].reshape(KV_DIM, ctx_len).T.astype(BF16)
        for g in range(N_KV):
            k_s[g, 0:ctx_len, :] = ck[:, g * HEAD_DIM:(g + 1) * HEAD_DIM]
        put_values_t(0, 0, cv_ref[...].reshape(KV_DIM, ctx_len))

    if use_rope:
        lane = lax.broadcasted_iota(jnp.int32, (CHUNK_ROWS, LANES), 1)
        first_half = (lane % (2 * ROT_PAIRS)) < ROT_PAIRS

    for c in range(n_chunks):
        r0 = c * CHUNK_ROWS
        x = x_ref[r0:r0 + CHUNK_ROWS, :]
        h = (_rms_rows(x) * nmix_ref[...]) * scale1 + shift
        hb = h.astype(BF16)
        qkv = jnp.dot(hb, win_ref[:, 0:QKV_DIM], preferred_element_type=F32)

        groups = []
        for g0 in range(0, QKV_DIM, MXU_DIM):
            sq = qkv[:, g0:g0 + MXU_DIM]
            groups.append(jnp.dot((sq * sq).astype(BF16), gsum_ref[...],
                                  preferred_element_type=F32))
        ms = jnp.concatenate(groups, axis=-1)[:, 0:QK_DIM]
        qk = (qkv[:, 0:QK_DIM] * lax.rsqrt(ms + RMS_EPS)) * qkg_ref[...]
        vv = qkv[:, QK_DIM:QKV_DIM]

        if emit_kv:
            for r1 in range(0, CHUNK_ROWS, seq_len):
                kt = qk[r1:r1 + seq_len, ATTN_DIM:QK_DIM].T
                ko_ref[(r0 + r1) // seq_len] = kt.reshape(N_KV, HEAD_DIM, seq_len)

        for cg in range(QK_DIM // LANES):
            xg = qk[:, cg * LANES:(cg + 1) * LANES]
            if use_rope:
                cs = cos_ref[r0:r0 + CHUNK_ROWS, :]
                sn = sin_ref[r0:r0 + CHUNK_ROWS, :]
                partner = jnp.where(first_half,
                                    pltpu.roll(xg, LANES - ROT_PAIRS, axis=1),
                                    pltpu.roll(xg, ROT_PAIRS, axis=1))
                xg = xg * cs + partner * sn
            if cg < ATTN_DIM // LANES:
                xb = (xg * Q_SCALE).astype(BF16)
                q_s[2 * cg, r0:r0 + CHUNK_ROWS, :] = xb[:, 0:HEAD_DIM]
                q_s[2 * cg + 1, r0:r0 + CHUNK_ROWS, :] = xb[:, HEAD_DIM:LANES]
            else:
                xb = xg.astype(BF16)
                for g in range(N_KV):
                    k_s[g, ctx_len + r0:ctx_len + r0 + CHUNK_ROWS, :] = (
                        xb[:, g * HEAD_DIM:(g + 1) * HEAD_DIM])
        w = min(key_blk, CHUNK_ROWS)
        for r1 in range(r0, r0 + CHUNK_ROWS, w):
            blk, off = (0, ctx_len + r1) if ctx_len else (r1 // key_blk, 0)
            vt = vv[r1 - r0:r1 - r0 + w, :].T
            put_values_t(blk, off, vt)
            if emit_kv:
                vo_ref[blk] = vt.reshape(N_KV, HEAD_DIM, seq_len)

        cvp = jnp.dot(hb, win_ref[:, QKV_DIM:IN_DIM], preferred_element_type=F32)
        gb_s[r0:r0 + CHUNK_ROWS, :] = cvp[:, 0:CONV_DIM]
        t_s[8 + r0:8 + r0 + CHUNK_ROWS, :] = (cvp[:, CONV_DIM:2 * CONV_DIM]
                                              * cvp[:, 2 * CONV_DIM:3 * CONV_DIM])
    t_s[0:8, :] = jnp.zeros((8, CONV_DIM), F32)
    t_s[8 + BLOCK_ROWS:16 + BLOCK_ROWS, :] = jnp.zeros((8, CONV_DIM), F32)

    def scores_t(qb, g):
        r0 = pl.multiple_of(qb * q_rows, q_rows)
        k0 = 0 if ctx_len else r0
        qs = jnp.concatenate(
            [q_s[KV_GROUP * g + j, pl.ds(r0, q_rows), :] for j in range(KV_GROUP)], axis=0)
        kk = k_s[g, pl.ds(k0, n_keys), :]
        return lax.dot_general(kk, qs, (((1,), (1,)), ((), ())),
                               preferred_element_type=F32)

    def values_out(qb, g, p):
        r0 = pl.multiple_of(qb * q_rows, q_rows)
        blk = 0 if ctx_len else qb
        ot = jnp.dot(vt_s[blk, g], p, preferred_element_type=F32)
        ot = ot[0:HEAD_DIM, :] / ot[HEAD_DIM:HEAD_DIM + 1, :]
        for jj in range(KV_GROUP // 2):
            pair_t = jnp.concatenate(
                [ot[:, (2 * jj) * q_rows:(2 * jj + 1) * q_rows],
                 ot[:, (2 * jj + 1) * q_rows:(2 * jj + 2) * q_rows]], axis=0)
            col = (KV_GROUP * g + 2 * jj) * HEAD_DIM
            attn_s[pl.ds(r0, q_rows), col:col + LANES] = pair_t.T

    def pipeline_pairs(step):
        assert n_qb % 2 == 0
        step(-1, 1, first=True)
        step(0, 0)

        def body(t, carry):
            step(2 * t + 1, 1)
            step(2 * t + 2, 0)
            return carry

        lax.fori_loop(0, (n_qb - 2) // 2, body, 0)
        step(n_qb - 1, 1, last=True)

    def shifted_block(qb, carry):
        for g in range(N_KV):
            s = scores_t(qb, g)
            s_ref[...] = s
            m = jnp.max(s, axis=0, keepdims=True)
            for k1 in range(0, n_keys, P_ROWS):
                p_ref[0, g, k1:k1 + P_ROWS, :] = jnp.exp2(
                    s_ref[k1:k1 + P_ROWS, :] - m).astype(BF16)
            values_out(qb, g, p_ref[0, g])
        return carry

    def unshifted_step(j, par, first=False, last=False):
        for g in range(N_KV):
            if not last:
                p_ref[1 - par, g] = jnp.exp2(scores_t(j + 1, g)).astype(BF16)
            if not first:
                values_out(j, g, p_ref[par, g])

    gains = jnp.abs(qkg_ref[...])
    q_bound = (Q_SCALE * Q_SCALE * HEAD_DIM) * jnp.max(gains[:, 0:ATTN_DIM]) ** 2
    k_bound = HEAD_DIM * jnp.max(gains[:, ATTN_DIM:QK_DIM]) ** 2
    if ctx_len:
        ck2 = ck_ref[...]
        k_bound = jnp.maximum(k_bound, jnp.max(jnp.sum(ck2 * ck2, axis=1)))
    small_scores = q_bound * k_bound <= MAX_UNSHIFTED_SCORE ** 2
    pl.when(small_scores)(lambda: pipeline_pairs(unshifted_step))

    @pl.when(jnp.logical_not(small_scores))
    def _():
        lax.fori_loop(0, n_qb, shifted_block, 0)

    w0 = convw_ref[0:1, :]
    w1 = convw_ref[1:2, :]
    w2 = convw_ref[2:3, :]
    for r0 in range(0, BLOCK_ROWS, CHUNK_ROWS):
        rows = slice(r0, r0 + CHUNK_ROWS)
        t_prev = t_s[7 + r0:7 + r0 + CHUNK_ROWS, :]
        t_mid = t_s[8 + r0:8 + r0 + CHUNK_ROWS, :]
        t_next = t_s[9 + r0:9 + r0 + CHUNK_ROWS, :]
        if seq_len < BLOCK_ROWS:
            pos = (lax.broadcasted_iota(jnp.int32, (CHUNK_ROWS, 1), 0) + r0) % seq_len
            t_prev = jnp.where(pos == 0, 0.0, t_prev)
            t_next = jnp.where(pos == seq_len - 1, 0.0, t_next)
        y = gb_s[rows, :] * (w0 * t_prev + w1 * t_mid + w2 * t_next)
        yn = _rms_rows(y) * gc_ref[...]
        an = _rms_rows(attn_s[rows, :]) * ga_ref[...]
        merged = jnp.concatenate([an, yn], axis=-1).astype(BF16)
        mix = jnp.dot(merged, wout_ref[...], preferred_element_type=F32)
        out_ref[rows, :] = x_ref[rows, :] + gate * mix


def _mixer_call(x_blocks, mod, mod_row, ctx_k, ctx_v, rope, consts, *, seq_len, emit_kv):
    n_blocks = x_blocks.shape[0]
    ctx_len = 0 if ctx_k is None else ctx_k.shape[4]
    use_rope = rope is not None
    if ctx_len:
        assert seq_len == BLOCK_ROWS
        q_rows, n_keys, n_key_blocks = LATENT_Q_ROWS, ctx_len + BLOCK_ROWS, 1
    else:
        q_rows, n_keys, n_key_blocks = seq_len, seq_len, BLOCK_ROWS // seq_len
    n_q_cols = KV_GROUP * q_rows

    blk = lambda cols: pl.BlockSpec((None, BLOCK_ROWS, cols), lambda b: (b, 0, 0))
    assert mod_row is not None or n_blocks <= MOD_ROWS
    args = [x_blocks, mod]
    in_specs = [blk(D_MODEL), _mod_spec(mod_row, 0)]
    if ctx_len:
        args += [ctx_k, ctx_v]
        in_specs += [pl.BlockSpec((None, None, N_KV, HEAD_DIM, ctx_len),
                                  lambda b: (b, 0, 0, 0, 0))] * 2
    if use_rope:
        args += list(rope)
        in_specs += [_const_spec((BLOCK_ROWS, LANES))] * 2
    args += list(consts)
    in_specs += [_const_spec(a.shape) for a in consts]

    out_shape = [jax.ShapeDtypeStruct((n_blocks, BLOCK_ROWS, D_MODEL), F32)]
    out_specs = [blk(D_MODEL)]
    if emit_kv:
        per_block = BLOCK_ROWS // seq_len
        out_shape += [jax.ShapeDtypeStruct(
            (n_blocks * per_block, 1, N_KV, HEAD_DIM, seq_len), F32)] * 2
        out_specs += [pl.BlockSpec((per_block, None, N_KV, HEAD_DIM, seq_len),
                                   lambda b: (b, 0, 0, 0, 0))] * 2

    scratch = [
        pltpu.VMEM((N_HEADS, BLOCK_ROWS, HEAD_DIM), BF16),
        pltpu.VMEM((N_KV, ctx_len + BLOCK_ROWS, HEAD_DIM), BF16),
        pltpu.VMEM((n_key_blocks, N_KV, V_ROWS, n_keys), BF16),
        pltpu.VMEM((BLOCK_ROWS, ATTN_DIM), F32),
        pltpu.VMEM((BLOCK_ROWS + 16, CONV_DIM), F32),
        pltpu.VMEM((BLOCK_ROWS, CONV_DIM), F32),
        pltpu.VMEM((n_keys, n_q_cols), F32),
        pltpu.VMEM((2, N_KV, n_keys, n_q_cols), BF16),
    ]
    kern = functools.partial(_mixer_kernel, seq_len=seq_len, ctx_len=ctx_len,
                             use_rope=use_rope, emit_kv=emit_kv, mod_row=mod_row)
    return pl.pallas_call(
        kern,
        grid=(n_blocks,),
        in_specs=in_specs,
        out_specs=out_specs,
        out_shape=out_shape,
        scratch_shapes=scratch,
        compiler_params=pltpu.CompilerParams(
            dimension_semantics=("arbitrary",), vmem_limit_bytes=VMEM_LIMIT),
        name="mixer_ctx" if emit_kv else "mixer_latent",
    )(*args)


def _ffn_rows(x_ref, mod_ref, nffn_ref, wgu_ref, wd_ref, out_ref, mod_row):
    mod = _mod_row(mod_ref, mod_row)
    shift = mod[:, 0:D_MODEL]
    scale1 = 1.0 + mod[:, D_MODEL:2 * D_MODEL]
    gate = mod[:, 2 * D_MODEL:3 * D_MODEL]
    for r0 in range(0, FFN_ROWS, FFN_SUB_ROWS):
        x = x_ref[r0:r0 + FFN_SUB_ROWS, :]
        hb = ((_rms_rows(x) * nffn_ref[...]) * scale1 + shift).astype(BF16)
        acc = None
        for c0, cw in FF_CHUNKS:
            gt = jnp.dot(hb, wgu_ref[:, c0:c0 + cw], preferred_element_type=F32)
            up = jnp.dot(hb, wgu_ref[:, D_FF + c0:D_FF + c0 + cw], preferred_element_type=F32)
            act = ((gt * jax.nn.sigmoid(gt)) * up).astype(BF16)
            part = jnp.dot(act, wd_ref[c0:c0 + cw, :], preferred_element_type=F32)
            acc = part if acc is None else acc + part
        out_ref[r0:r0 + FFN_SUB_ROWS, :] = x + gate * acc


def _ffn_kernel(x_ref, mod_ref, nffn_ref, wgu_ref, wd_ref, out_ref, *, mod_row):
    _ffn_rows(x_ref, mod_ref, nffn_ref, wgu_ref, wd_ref, out_ref, mod_row)


def _cast_rows(src_hbm, dst, stage, sem, chunk_rows):
    n = src_hbm.shape[0] // chunk_rows

    def fetch(c):
        return pltpu.make_async_copy(src_hbm.at[pl.ds(c * chunk_rows, chunk_rows), :],
                                     stage.at[c % 2], sem.at[c % 2])

    fetch(0).start()
    for c in range(n):
        if c + 1 < n:
            fetch(c + 1).start()
        fetch(c).wait()
        dst[c * chunk_rows:(c + 1) * chunk_rows, :] = stage[c % 2].astype(BF16)


def _ffn_cast_kernel(x_ref, mod_ref, nffn_ref, wgu_hbm, wd_hbm, out_ref, wgu_out, wd_out,
                     wgu_s, wd_s, stage_gu, stage_d, sem_in, sem_out, *, mod_row):
    first = pl.program_id(0) == 0
    save = (pltpu.make_async_copy(wgu_s, wgu_out, sem_out.at[0]),
            pltpu.make_async_copy(wd_s, wd_out, sem_out.at[1]))

    @pl.when(first)
    def _():
        _cast_rows(wgu_hbm, wgu_s, stage_gu, sem_in, GU_CAST_ROWS)
        _cast_rows(wd_hbm, wd_s, stage_d, sem_in, DOWN_CAST_ROWS)
        for cp in save:
            cp.start()

    _ffn_rows(x_ref, mod_ref, nffn_ref, wgu_s, wd_s, out_ref, mod_row)

    @pl.when(first)
    def _():
        for cp in save:
            cp.wait()


def _ffn_call(x_rows, mod, mod_row, norm_ffn, w_gate_up, w_down, name):
    n_rows = x_rows.shape[0]
    assert mod_row is not None or n_rows // FFN_ROWS <= MOD_ROWS
    cast = w_gate_up.dtype == F32
    rows_spec = pl.BlockSpec((FFN_ROWS, D_MODEL), lambda i: (i, 0))
    in_specs = [rows_spec, _mod_spec(mod_row, 1), _const_spec(norm_ffn.shape)]
    out_shape = [jax.ShapeDtypeStruct((n_rows, D_MODEL), F32)]
    out_specs = [rows_spec]
    scratch = []
    if cast:
        hbm = pl.BlockSpec(memory_space=pl.ANY)
        in_specs += [hbm, hbm]
        out_shape += [jax.ShapeDtypeStruct(w_gate_up.shape, BF16),
                      jax.ShapeDtypeStruct(w_down.shape, BF16)]
        out_specs += [hbm, hbm]
        scratch = [pltpu.VMEM(w_gate_up.shape, BF16), pltpu.VMEM(w_down.shape, BF16),
                   pltpu.VMEM((2, GU_CAST_ROWS, w_gate_up.shape[1]), F32),
                   pltpu.VMEM((2, DOWN_CAST_ROWS, w_down.shape[1]), F32),
                   pltpu.SemaphoreType.DMA((2,)), pltpu.SemaphoreType.DMA((2,))]
    else:
        in_specs += [_const_spec(w_gate_up.shape), _const_spec(w_down.shape)]
    res = pl.pallas_call(
        functools.partial(_ffn_cast_kernel if cast else _ffn_kernel, mod_row=mod_row),
        grid=(n_rows // FFN_ROWS,),
        in_specs=in_specs,
        out_specs=out_specs,
        out_shape=out_shape,
        scratch_shapes=scratch,
        compiler_params=pltpu.CompilerParams(
            dimension_semantics=("arbitrary",), vmem_limit_bytes=VMEM_LIMIT),
        name=name,
    )(x_rows, mod, norm_ffn, w_gate_up, w_down)
    return res if cast else res[0]


def _rope_tables(n_tokens):
    rows = n_tokens // GRID_W
    row = jnp.repeat(jnp.arange(rows, dtype=F32), GRID_W)
    col = jnp.tile(jnp.arange(GRID_W, dtype=F32), rows)
    inv = 1.0 / (ROPE_THETA ** (jnp.arange(ROT_PAIRS, dtype=F32) / ROT_PAIRS))
    ang = jnp.stack([row[:, None] * inv, col[:, None] * inv], axis=1)
    cos, sin = jnp.cos(ang), jnp.sin(ang)
    cos_h = jnp.concatenate([cos, cos], axis=-1).reshape(n_tokens, HEAD_DIM)
    sin_h = jnp.concatenate([-sin, sin], axis=-1).reshape(n_tokens, HEAD_DIM)
    reps = LANES // HEAD_DIM
    return jnp.tile(cos_h, (1, reps)), jnp.tile(sin_h, (1, reps))


def _keys_minor(kv):
    return jnp.transpose(kv, (0, 1, 3, 4, 2))


def _group_mean_matrix():
    idx = np.arange(MXU_DIM) // HEAD_DIM
    g = (idx[:, None] == idx[None, :]).astype(np.float32) / HEAD_DIM
    return jnp.asarray(g, dtype=BF16)


def kernel(x_prompt, x_sample, c, cache_k, cache_v, c_ctx, norm_mix, norm_ffn, w_ada, b_ada,
           w_in, q_norm, k_norm, conv_w, attn_out_norm, conv_out_norm, w_out, w_gate_up, w_down):
    depth = w_in.shape[0]
    assert depth == 1
    n_prompt, seq, _ = x_prompt.shape
    n_sample, dec_seq, _ = x_sample.shape
    assert dec_seq == BLOCK_ROWS and BLOCK_ROWS % seq == 0 and n_sample <= CTX_ROW

    cond = jnp.zeros((COND_ROWS, D_MODEL), F32)
    cond = cond.at[0:n_sample].set(c).at[CTX_ROW].set(c_ctx)
    mod = _ada_call(cond, w_ada[0], b_ada[0][None, :])

    consts = (
        norm_mix[0][None, :],
        w_in[0].astype(BF16),
        _group_mean_matrix(),
        jnp.concatenate([jnp.tile(q_norm[0], N_HEADS), jnp.tile(k_norm[0], N_KV)])[None, :],
        conv_w[0],
        attn_out_norm[0][None, :],
        conv_out_norm[0][None, :],
        w_out[0].astype(BF16),
    )
    nffn = norm_ffn[0][None, :]

    per_block = BLOCK_ROWS // seq
    xp_blocks = x_prompt.reshape(n_prompt // per_block, BLOCK_ROWS, D_MODEL)
    xp1, k_new, v_new = _mixer_call(xp_blocks, mod, CTX_ROW, None, None, None, consts,
                                    seq_len=seq, emit_kv=True)
    yp, wgu, wd = _ffn_call(xp1.reshape(-1, D_MODEL), mod, CTX_ROW, nffn,
                            w_gate_up[0], w_down[0], "ffn_ctx")

    assert dec_seq == FFN_ROWS
    (xs1,) = _mixer_call(x_sample, mod, None, _keys_minor(cache_k), _keys_minor(cache_v),
                         _rope_tables(dec_seq), consts, seq_len=dec_seq, emit_kv=False)
    ys = _ffn_call(xs1.reshape(-1, D_MODEL), mod, None, nffn, wgu, wd, "ffn_latent")

    return (yp.reshape(n_prompt, seq, D_MODEL),
            ys.reshape(n_sample, dec_seq, D_MODEL),
            jnp.transpose(k_new, (0, 1, 4, 2, 3)),
            jnp.transpose(v_new, (0, 1, 4, 2, 3)))
```

```python
import functools

import numpy as np
import jax
import jax.numpy as jnp
from jax import lax
from jax.experimental import pallas as pl
from jax.experimental.pallas import tpu as pltpu

D_MODEL = 1024
HEAD_DIM = 64
ATTN_DIM = 512
N_HEADS = 8
N_KV = 2
KV_GROUP = N_HEADS // N_KV
KV_DIM = N_KV * HEAD_DIM
CONV_DIM = 512
D_FF = 2816
QK_DIM = ATTN_DIM + KV_DIM
QKV_DIM = ATTN_DIM + 2 * KV_DIM
IN_DIM = QKV_DIM + 3 * CONV_DIM
GRID_W = 64
ROT_PAIRS = HEAD_DIM // 4
ROPE_THETA = 10000.0
RMS_EPS = 1e-6
Q_SCALE = HEAD_DIM ** -0.5 * 1.4426950408889634
V_ROWS = HEAD_DIM + 16
MAX_UNSHIFTED_SCORE = 64.0

LANES = 128
MXU_DIM = 256
BLOCK_ROWS = 1024
CHUNK_ROWS = 512
LATENT_Q_ROWS = 256
P_ROWS = 16
FFN_ROWS = 1024
FFN_SUB_ROWS = 512
GU_CAST_ROWS = 128
DOWN_CAST_ROWS = 352
FF_CHUNKS = ((0, 1024), (1024, 1024), (2048, 768))
COND_ROWS = 16
MOD_ROWS = 8
CTX_ROW = 8
ADA_COLS = 2048
VMEM_LIMIT = 56 * 1024 * 1024

F32 = jnp.float32
BF16 = jnp.bfloat16


def _mod_spec(row, half):
    blk = 0 if row is None else row // MOD_ROWS
    return pl.BlockSpec((MOD_ROWS, 3 * D_MODEL), lambda i: (blk, half))


def _mod_row(mod_ref, row):
    r = pl.program_id(0) if row is None else row % MOD_ROWS
    return mod_ref[pl.ds(r, 1), :]


def _const_spec(shape):
    nd = len(shape)
    return pl.BlockSpec(shape, lambda *_: (0,) * nd, pipeline_mode=pl.Buffered(1))


def _ada_kernel(cond_ref, w_ref, b_ref, out_ref):
    c = cond_ref[...]
    s = (c * jax.nn.sigmoid(c)).astype(BF16)
    out_ref[...] = jnp.dot(s, w_ref[...].astype(BF16), preferred_element_type=F32) + b_ref[...]


def _ada_call(cond, w_ada, b_ada):
    n = w_ada.shape[1]
    return pl.pallas_call(
        _ada_kernel,
        grid=(n // ADA_COLS,),
        in_specs=[
            pl.BlockSpec((COND_ROWS, D_MODEL), lambda j: (0, 0)),
            pl.BlockSpec((D_MODEL, ADA_COLS), lambda j: (0, j)),
            pl.BlockSpec((1, ADA_COLS), lambda j: (0, j)),
        ],
        out_specs=pl.BlockSpec((COND_ROWS, ADA_COLS), lambda j: (0, j)),
        out_shape=jax.ShapeDtypeStruct((COND_ROWS, n), F32),
        compiler_params=pltpu.CompilerParams(
            dimension_semantics=("arbitrary",), vmem_limit_bytes=VMEM_LIMIT),
        name="ada_rows",
    )(cond, w_ada, b_ada)


def _rms_rows(x):
    return x * lax.rsqrt(jnp.mean(x * x, axis=-1, keepdims=True) + RMS_EPS)


def _mixer_kernel(*refs, seq_len, ctx_len, use_rope, emit_kv, mod_row):
    it = iter(refs)
    x_ref, mod_ref = next(it), next(it)
    if ctx_len:
        ck_ref, cv_ref = next(it), next(it)
    if use_rope:
        cos_ref, sin_ref = next(it), next(it)
    (nmix_ref, win_ref, gsum_ref, qkg_ref, convw_ref, ga_ref, gc_ref, wout_ref) = (
        next(it) for _ in range(8))
    out_ref = next(it)
    if emit_kv:
        ko_ref, vo_ref = next(it), next(it)
    (q_s, k_s, vt_s, attn_s, t_s, gb_s, s_ref, p_ref) = (next(it) for _ in range(8))

    n_chunks = BLOCK_ROWS // CHUNK_ROWS
    if ctx_len:
        q_rows, n_keys, key_blk = LATENT_Q_ROWS, ctx_len + seq_len, BLOCK_ROWS
    else:
        q_rows, n_keys, key_blk = seq_len, seq_len, seq_len
    n_qb = BLOCK_ROWS // q_rows

    mod = _mod_row(mod_ref, mod_row)
    shift = mod[:, 0:D_MODEL]
    scale1 = 1.0 + mod[:, D_MODEL:2 * D_MODEL]
    gate = mod[:, 2 * D_MODEL:3 * D_MODEL]

    def put_values_t(blk, off, vt):
        vt = vt.astype(BF16)
        for g in range(N_KV):
            vt_s[blk, g, 0:HEAD_DIM, off:off + vt.shape[1]] = vt[g * HEAD_DIM:(g + 1) * HEAD_DIM, :]

    tail = (lax.broadcasted_iota(jnp.int32, (V_ROWS - HEAD_DIM, n_keys), 0) == 0).astype(BF16)
    for blk in range(vt_s.shape[0]):
        for g in range(N_KV):
            vt_s[blk, g, HEAD_DIM:V_ROWS, :] = tail

    if ctx_len:
        ck = ck_ref[...].reshape(KV_DIM, ctx_len).T.astype(BF16)
        for g in range(N_KV):
            k_s[g, 0:ctx_len, :] = ck[:, g * HEAD_DIM:(g + 1) * HEAD_DIM]
        put_values_t(0, 0, cv_ref[...].reshape(KV_DIM, ctx_len))

    if use_rope:
        lane = lax.broadcasted_iota(jnp.int32, (CHUNK_ROWS, LANES), 1)
        first_half = (lane % (2 * ROT_PAIRS)) < ROT_PAIRS

    for c in range(n_chunks):
        r0 = c * CHUNK_ROWS
        x = x_ref[r0:r0 + CHUNK_ROWS, :]
        h = (_rms_rows(x) * nmix_ref[...]) * scale1 + shift
        hb = h.astype(BF16)
        qkv = jnp.dot(hb, win_ref[:, 0:QKV_DIM], preferred_element_type=F32)

        groups = []
        for g0 in range(0, QKV_DIM, MXU_DIM):
            sq = qkv[:, g0:g0 + MXU_DIM]
            groups.append(jnp.dot((sq * sq).astype(BF16), gsum_ref[...],
                                  preferred_element_type=F32))
        ms = jnp.concatenate(groups, axis=-1)[:, 0:QK_DIM]
        qk = (qkv[:, 0:QK_DIM] * lax.rsqrt(ms + RMS_EPS)) * qkg_ref[...]
        vv = qkv[:, QK_DIM:QKV_DIM]

        if emit_kv:
            for r1 in range(0, CHUNK_ROWS, seq_len):
                kt = qk[r1:r1 + seq_len, ATTN_DIM:QK_DIM].T
                ko_ref[(r0 + r1) // seq_len] = kt.reshape(N_KV, HEAD_DIM, seq_len)

        for cg in range(QK_DIM // LANES):
            xg = qk[:, cg * LANES:(cg + 1) * LANES]
            if use_rope:
                cs = cos_ref[r0:r0 + CHUNK_ROWS, :]
                sn = sin_ref[r0:r0 + CHUNK_ROWS, :]
                partner = jnp.where(first_half,
                                    pltpu.roll(xg, LANES - ROT_PAIRS, axis=1),
                                    pltpu.roll(xg, ROT_PAIRS, axis=1))
                xg = xg * cs + partner * sn
            if cg < ATTN_DIM // LANES:
                xb = (xg * Q_SCALE).astype(BF16)
                q_s[2 * cg, r0:r0 + CHUNK_ROWS, :] = xb[:, 0:HEAD_DIM]
                q_s[2 * cg + 1, r0:r0 + CHUNK_ROWS, :] = xb[:, HEAD_DIM:LANES]
            else:
                xb = xg.astype(BF16)
                for g in range(N_KV):
                    k_s[g, ctx_len + r0:ctx_len + r0 + CHUNK_ROWS, :] = (
                        xb[:, g * HEAD_DIM:(g + 1) * HEAD_DIM])
        w = min(key_blk, CHUNK_ROWS)
        for r1 in range(r0, r0 + CHUNK_ROWS, w):
            blk, off = (0, ctx_len + r1) if ctx_len else (r1 // key_blk, 0)
            vt = vv[r1 - r0:r1 - r0 + w, :].T
            put_values_t(blk, off, vt)
            if emit_kv:
                vo_ref[blk] = vt.reshape(N_KV, HEAD_DIM, seq_len)

        cvp = jnp.dot(hb, win_ref[:, QKV_DIM:IN_DIM], preferred_element_type=F32)
        gb_s[r0:r0 + CHUNK_ROWS, :] = cvp[:, 0:CONV_DIM]
        t_s[8 + r0:8 + r0 + CHUNK_ROWS, :] = (cvp[:, CONV_DIM:2 * CONV_DIM]
                                              * cvp[:, 2 * CONV_DIM:3 * CONV_DIM])
    t_s[0:8, :] = jnp.zeros((8, CONV_DIM), F32)
    t_s[8 + BLOCK_ROWS:16 + BLOCK_ROWS, :] = jnp.zeros((8, CONV_DIM), F32)

    def scores_t(qb, g):
        r0 = pl.multiple_of(qb * q_rows, q_rows)
        k0 = 0 if ctx_len else r0
        qs = jnp.concatenate(
            [q_s[KV_GROUP * g + j, pl.ds(r0, q_rows), :] for j in range(KV_GROUP)], axis=0)
        kk = k_s[g, pl.ds(k0, n_keys), :]
        return lax.dot_general(kk, qs, (((1,), (1,)), ((), ())),
                               preferred_element_type=F32)

    def values_out(qb, g, p):
        r0 = pl.multiple_of(qb * q_rows, q_rows)
        blk = 0 if ctx_len else qb
        ot = jnp.dot(vt_s[blk, g], p, preferred_element_type=F32)
        ot = ot[0:HEAD_DIM, :] / ot[HEAD_DIM:HEAD_DIM + 1, :]
        for jj in range(KV_GROUP // 2):
            pair_t = jnp.concatenate(
                [ot[:, (2 * jj) * q_rows:(2 * jj + 1) * q_rows],
                 ot[:, (2 * jj + 1) * q_rows:(2 * jj + 2) * q_rows]], axis=0)
            col = (KV_GROUP * g + 2 * jj) * HEAD_DIM
            attn_s[pl.ds(r0, q_rows), col:col + LANES] = pair_t.T

    def pipeline_pairs(step):
        assert n_qb % 2 == 0
        step(-1, 1, first=True)
        step(0, 0)

        def body(t, carry):
            step(2 * t + 1, 1)
            step(2 * t + 2, 0)
            return carry

        lax.fori_loop(0, (n_qb - 2) // 2, body, 0)
        step(n_qb - 1, 1, last=True)

    def shifted_block(qb, carry):
        for g in range(N_KV):
            s = scores_t(qb, g)
            s_ref[...] = s
            m = jnp.max(s, axis=0, keepdims=True)
            for k1 in range(0, n_keys, P_ROWS):
                p_ref[0, g, k1:k1 + P_ROWS, :] = jnp.exp2(
                    s_ref[k1:k1 + P_ROWS, :] - m).astype(BF16)
            values_out(qb, g, p_ref[0, g])
        return carry

    def unshifted_step(j, par, first=False, last=False):
        for g in range(N_KV):
            if not last:
                p_ref[1 - par, g] = jnp.exp2(scores_t(j + 1, g)).astype(BF16)
            if not first:
                values_out(j, g, p_ref[par, g])

    gains = jnp.abs(qkg_ref[...])
    q_bound = (Q_SCALE * Q_SCALE * HEAD_DIM) * jnp.max(gains[:, 0:ATTN_DIM]) ** 2
    k_bound = HEAD_DIM * jnp.max(gains[:, ATTN_DIM:QK_DIM]) ** 2
    if ctx_len:
        ck2 = ck_ref[...]
        k_bound = jnp.maximum(k_bound, jnp.max(jnp.sum(ck2 * ck2, axis=1)))
    small_scores = q_bound * k_bound <= MAX_UNSHIFTED_SCORE ** 2
    pl.when(small_scores)(lambda: pipeline_pairs(unshifted_step))

    @pl.when(jnp.logical_not(small_scores))
    def _():
        lax.fori_loop(0, n_qb, shifted_block, 0)

    w0 = convw_ref[0:1, :]
    w1 = convw_ref[1:2, :]
    w2 = convw_ref[2:3, :]
    for r0 in range(0, BLOCK_ROWS, CHUNK_ROWS):
        rows = slice(r0, r0 + CHUNK_ROWS)
        t_prev = t_s[7 + r0:7 + r0 + CHUNK_ROWS, :]
        t_mid = t_s[8 + r0:8 + r0 + CHUNK_ROWS, :]
        t_next = t_s[9 + r0:9 + r0 + CHUNK_ROWS, :]
        if seq_len < BLOCK_ROWS:
            pos = (lax.broadcasted_iota(jnp.int32, (CHUNK_ROWS, 1), 0) + r0) % seq_len
            t_prev = jnp.where(pos == 0, 0.0, t_prev)
            t_next = jnp.where(pos == seq_len - 1, 0.0, t_next)
        y = gb_s[rows, :] * (w0 * t_prev + w1 * t_mid + w2 * t_next)
        yn = _rms_rows(y) * gc_ref[...]
        an = _rms_rows(attn_s[rows, :]) * ga_ref[...]
        merged = jnp.concatenate([an, yn], axis=-1).astype(BF16)
        mix = jnp.dot(merged, wout_ref[...], preferred_element_type=F32)
        out_ref[rows, :] = x_ref[rows, :] + gate * mix


def _mixer_call(x_blocks, mod, mod_row, ctx_k, ctx_v, rope, consts, *, seq_len, emit_kv):
    n_blocks = x_blocks.shape[0]
    ctx_len = 0 if ctx_k is None else ctx_k.shape[4]
    use_rope = rope is not None
    if ctx_len:
        assert seq_len == BLOCK_ROWS
        q_rows, n_keys, n_key_blocks = LATENT_Q_ROWS, ctx_len + BLOCK_ROWS, 1
    else:
        q_rows, n_keys, n_key_blocks = seq_len, seq_len, BLOCK_ROWS // seq_len
    n_q_cols = KV_GROUP * q_rows

    blk = lambda cols: pl.BlockSpec((None, BLOCK_ROWS, cols), lambda b: (b, 0, 0))
    assert mod_row is not None or n_blocks <= MOD_ROWS
    args = [x_blocks, mod]
    in_specs = [blk(D_MODEL), _mod_spec(mod_row, 0)]
    if ctx_len:
        args += [ctx_k, ctx_v]
        in_specs += [pl.BlockSpec((None, None, N_KV, HEAD_DIM, ctx_len),
                                  lambda b: (b, 0, 0, 0, 0))] * 2
    if use_rope:
        args += list(rope)
        in_specs += [_const_spec((BLOCK_ROWS, LANES))] * 2
    args += list(consts)
    in_specs += [_const_spec(a.shape) for a in consts]

    out_shape = [jax.ShapeDtypeStruct((n_blocks, BLOCK_ROWS, D_MODEL), F32)]
    out_specs = [blk(D_MODEL)]
    if emit_kv:
        per_block = BLOCK_ROWS // seq_len
        out_shape += [jax.ShapeDtypeStruct(
            (n_blocks * per_block, 1, N_KV, HEAD_DIM, seq_len), F32)] * 2
        out_specs += [pl.BlockSpec((per_block, None, N_KV, HEAD_DIM, seq_len),
                                   lambda b: (b, 0, 0, 0, 0))] * 2

    scratch = [
        pltpu.VMEM((N_HEADS, BLOCK_ROWS, HEAD_DIM), BF16),
        pltpu.VMEM((N_KV, ctx_len + BLOCK_ROWS, HEAD_DIM), BF16),
        pltpu.VMEM((n_key_blocks, N_KV, V_ROWS, n_keys), BF16),
        pltpu.VMEM((BLOCK_ROWS, ATTN_DIM), F32),
        pltpu.VMEM((BLOCK_ROWS + 16, CONV_DIM), F32),
        pltpu.VMEM((BLOCK_ROWS, CONV_DIM), F32),
        pltpu.VMEM((n_keys, n_q_cols), F32),
        pltpu.VMEM((2, N_KV, n_keys, n_q_cols), BF16),
    ]
    kern = functools.partial(_mixer_kernel, seq_len=seq_len, ctx_len=ctx_len,
                             use_rope=use_rope, emit_kv=emit_kv, mod_row=mod_row)
    return pl.pallas_call(
        kern,
        grid=(n_blocks,),
        in_specs=in_specs,
        out_specs=out_specs,
        out_shape=out_shape,
        scratch_shapes=scratch,
        compiler_params=pltpu.CompilerParams(
            dimension_semantics=("arbitrary",), vmem_limit_bytes=VMEM_LIMIT),
        name="mixer_ctx" if emit_kv else "mixer_latent",
    )(*args)


def _zero_row_after(dep, width):
    bits = pltpu.bitcast(dep, jnp.uint32)
    acc = bits[0:8, :]
    for r in range(8, bits.shape[0], 8):
        acc = acc | bits[r:r + 8, :]
    word = acc[:, 0:LANES]
    for c in range(LANES, acc.shape[1], LANES):
        word = word | acc[:, c:c + LANES]
    zero = pltpu.bitcast((word >> 16) >> 16, F32)[0:1, :]
    return jnp.concatenate([zero] * (width // LANES), axis=-1)


def _ffn_rows(x_ref, mod_ref, nffn_ref, wgu_ref, wd_ref, out_ref, mod_row):
    mod = _mod_row(mod_ref, mod_row)
    shift = mod[:, 0:D_MODEL]
    scale1 = 1.0 + mod[:, D_MODEL:2 * D_MODEL]
    gate = mod[:, 2 * D_MODEL:3 * D_MODEL]
    starts = list(range(0, FFN_ROWS, FFN_SUB_ROWS))
    hbs = [((_rms_rows(x_ref[r0:r0 + FFN_SUB_ROWS, :]) * nffn_ref[...]) * scale1
            + shift).astype(BF16) for r0 in starts]
    for k, r0 in enumerate(starts):
        x = x_ref[r0:r0 + FFN_SUB_ROWS, :]
        hb = hbs[k]
        acc = None
        for ci, (c0, cw) in enumerate(FF_CHUNKS):
            gt = jnp.dot(hb, wgu_ref[:, c0:c0 + cw], preferred_element_type=F32)
            if ci == 1 and k + 1 < len(starts):
                gt = gt + _zero_row_after(hbs[k + 1], cw)
            up = jnp.dot(hb, wgu_ref[:, D_FF + c0:D_FF + c0 + cw], preferred_element_type=F32)
            act = ((gt * jax.nn.sigmoid(gt)) * up).astype(BF16)
            part = jnp.dot(act, wd_ref[c0:c0 + cw, :], preferred_element_type=F32)
            acc = part if acc is None else acc + part
        out_ref[r0:r0 + FFN_SUB_ROWS, :] = x + gate * acc


def _ffn_kernel(x_ref, mod_ref, nffn_ref, wgu_ref, wd_ref, out_ref, *, mod_row):
    _ffn_rows(x_ref, mod_ref, nffn_ref, wgu_ref, wd_ref, out_ref, mod_row)


def _cast_rows(src_hbm, dst, stage, sem, chunk_rows):
    n = src_hbm.shape[0] // chunk_rows

    def fetch(c):
        return pltpu.make_async_copy(src_hbm.at[pl.ds(c * chunk_rows, chunk_rows), :],
                                     stage.at[c % 2], sem.at[c % 2])

    fetch(0).start()
    for c in range(n):
        if c + 1 < n:
            fetch(c + 1).start()
        fetch(c).wait()
        dst[c * chunk_rows:(c + 1) * chunk_rows, :] = stage[c % 2].astype(BF16)


def _ffn_cast_kernel(x_ref, mod_ref, nffn_ref, wgu_hbm, wd_hbm, out_ref, wgu_out, wd_out,
                     wgu_s, wd_s, stage_gu, stage_d, sem_in, sem_out, *, mod_row):
    first = pl.program_id(0) == 0
    save = (pltpu.make_async_copy(wgu_s, wgu_out, sem_out.at[0]),
            pltpu.make_async_copy(wd_s, wd_out, sem_out.at[1]))

    @pl.when(first)
    def _():
        _cast_rows(wgu_hbm, wgu_s, stage_gu, sem_in, GU_CAST_ROWS)
        _cast_rows(wd_hbm, wd_s, stage_d, sem_in, DOWN_CAST_ROWS)
        for cp in save:
            cp.start()

    _ffn_rows(x_ref, mod_ref, nffn_ref, wgu_s, wd_s, out_ref, mod_row)

    @pl.when(first)
    def _():
        for cp in save:
            cp.wait()


def _ffn_call(x_rows, mod, mod_row, norm_ffn, w_gate_up, w_down, name):
    n_rows = x_rows.shape[0]
    assert mod_row is not None or n_rows // FFN_ROWS <= MOD_ROWS
    cast = w_gate_up.dtype == F32
    rows_spec = pl.BlockSpec((FFN_ROWS, D_MODEL), lambda i: (i, 0))
    in_specs = [rows_spec, _mod_spec(mod_row, 1), _const_spec(norm_ffn.shape)]
    out_shape = [jax.ShapeDtypeStruct((n_rows, D_MODEL), F32)]
    out_specs = [rows_spec]
    scratch = []
    if cast:
        hbm = pl.BlockSpec(memory_space=pl.ANY)
        in_specs += [hbm, hbm]
        out_shape += [jax.ShapeDtypeStruct(w_gate_up.shape, BF16),
                      jax.ShapeDtypeStruct(w_down.shape, BF16)]
        out_specs += [hbm, hbm]
        scratch = [pltpu.VMEM(w_gate_up.shape, BF16), pltpu.VMEM(w_down.shape, BF16),
                   pltpu.VMEM((2, GU_CAST_ROWS, w_gate_up.shape[1]), F32),
                   pltpu.VMEM((2, DOWN_CAST_ROWS, w_down.shape[1]), F32),
                   pltpu.SemaphoreType.DMA((2,)), pltpu.SemaphoreType.DMA((2,))]
    else:
        in_specs += [_const_spec(w_gate_up.shape), _const_spec(w_down.shape)]
    res = pl.pallas_call(
        functools.partial(_ffn_cast_kernel if cast else _ffn_kernel, mod_row=mod_row),
        grid=(n_rows // FFN_ROWS,),
        in_specs=in_specs,
        out_specs=out_specs,
        out_shape=out_shape,
        scratch_shapes=scratch,
        compiler_params=pltpu.CompilerParams(
            dimension_semantics=("arbitrary",), vmem_limit_bytes=VMEM_LIMIT),
        name=name,
    )(x_rows, mod, norm_ffn, w_gate_up, w_down)
    return res if cast else res[0]


def _rope_tables(n_tokens):
    rows = n_tokens // GRID_W
    row = jnp.repeat(jnp.arange(rows, dtype=F32), GRID_W)
    col = jnp.tile(jnp.arange(GRID_W, dtype=F32), rows)
    inv = 1.0 / (ROPE_THETA ** (jnp.arange(ROT_PAIRS, dtype=F32) / ROT_PAIRS))
    ang = jnp.stack([row[:, None] * inv, col[:, None] * inv], axis=1)
    cos, sin = jnp.cos(ang), jnp.sin(ang)
    cos_h = jnp.concatenate([cos, cos], axis=-1).reshape(n_tokens, HEAD_DIM)
    sin_h = jnp.concatenate([-sin, sin], axis=-1).reshape(n_tokens, HEAD_DIM)
    reps = LANES // HEAD_DIM
    return jnp.tile(cos_h, (1, reps)), jnp.tile(sin_h, (1, reps))


def _keys_minor(kv):
    return jnp.transpose(kv, (0, 1, 3, 4, 2))


def _group_mean_matrix():
    idx = np.arange(MXU_DIM) // HEAD_DIM
    g = (idx[:, None] == idx[None, :]).astype(np.float32) / HEAD_DIM
    return jnp.asarray(g, dtype=BF16)


def kernel(x_prompt, x_sample, c, cache_k, cache_v, c_ctx, norm_mix, norm_ffn, w_ada, b_ada,
           w_in, q_norm, k_norm, conv_w, attn_out_norm, conv_out_norm, w_out, w_gate_up, w_down):
    depth = w_in.shape[0]
    assert depth == 1
    n_prompt, seq, _ = x_prompt.shape
    n_sample, dec_seq, _ = x_sample.shape
    assert dec_seq == BLOCK_ROWS and BLOCK_ROWS % seq == 0 and n_sample <= CTX_ROW

    cond = jnp.zeros((COND_ROWS, D_MODEL), F32)
    cond = cond.at[0:n_sample].set(c).at[CTX_ROW].set(c_ctx)
    mod = _ada_call(cond, w_ada[0], b_ada[0][None, :])

    consts = (
        norm_mix[0][None, :],
        w_in[0].astype(BF16),
        _group_mean_matrix(),
        jnp.concatenate([jnp.tile(q_norm[0], N_HEADS), jnp.tile(k_norm[0], N_KV)])[None, :],
        conv_w[0],
        attn_out_norm[0][None, :],
        conv_out_norm[0][None, :],
        w_out[0].astype(BF16),
    )
    nffn = norm_ffn[0][None, :]

    per_block = BLOCK_ROWS // seq
    xp_blocks = x_prompt.reshape(n_prompt // per_block, BLOCK_ROWS, D_MODEL)
    xp1, k_new, v_new = _mixer_call(xp_blocks, mod, CTX_ROW, None, None, None, consts,
                                    seq_len=seq, emit_kv=True)
    yp, wgu, wd = _ffn_call(xp1.reshape(-1, D_MODEL), mod, CTX_ROW, nffn,
                            w_gate_up[0], w_down[0], "ffn_ctx")

    assert dec_seq == FFN_ROWS
    (xs1,) = _mixer_call(x_sample, mod, None, _keys_minor(cache_k), _keys_minor(cache_v),
                         _rope_tables(dec_seq), consts, seq_len=dec_seq, emit_kv=False)
    ys = _ffn_call(xs1.reshape(-1, D_MODEL), mod, None, nffn, wgu, wd, "ffn_latent")

    return (yp.reshape(n_prompt, seq, D_MODEL),
            ys.reshape(n_sample, dec_seq, D_MODEL),
            jnp.transpose(k_new, (0, 1, 4, 2, 3)),
            jnp.transpose(v_new, (0, 1, 4, 2, 3)))
```

```python
import functools

import numpy as np
import jax
import jax.numpy as jnp
from jax import lax
from jax.experimental import pallas as pl
from jax.experimental.pallas import tpu as pltpu

D_MODEL = 1024
HEAD_DIM = 64
ATTN_DIM = 512
N_HEADS = 8
N_KV = 2
KV_GROUP = N_HEADS // N_KV
KV_DIM = N_KV * HEAD_DIM
CONV_DIM = 512
D_FF = 2816
QK_DIM = ATTN_DIM + KV_DIM
QKV_DIM = ATTN_DIM + 2 * KV_DIM
IN_DIM = QKV_DIM + 3 * CONV_DIM
GRID_W = 64
ROT_PAIRS = HEAD_DIM // 4
ROPE_THETA = 10000.0
RMS_EPS = 1e-6
Q_SCALE = HEAD_DIM ** -0.5 * 1.4426950408889634
V_ROWS = HEAD_DIM + 16
MAX_UNSHIFTED_SCORE = 64.0

LANES = 128
MXU_DIM = 256
BLOCK_ROWS = 1024
CHUNK_ROWS = 512
LATENT_Q_ROWS = 256
P_ROWS = 16
FFN_ROWS = 1024
FFN_SUB_ROWS = 512
GU_CAST_ROWS = 128
DOWN_CAST_ROWS = 352
FF_CHUNKS = ((0, 1024), (1024, 1024), (2048, 768))
COND_ROWS = 16
MOD_ROWS = 8
CTX_ROW = 8
ADA_COLS = 2048
VMEM_LIMIT = 56 * 1024 * 1024

F32 = jnp.float32
BF16 = jnp.bfloat16


def _mod_spec(row, half):
    blk = 0 if row is None else row // MOD_ROWS
    return pl.BlockSpec((MOD_ROWS, 3 * D_MODEL), lambda i: (blk, half))


def _mod_row(mod_ref, row):
    r = pl.program_id(0) if row is None else row % MOD_ROWS
    return mod_ref[pl.ds(r, 1), :]


def _const_spec(shape):
    nd = len(shape)
    return pl.BlockSpec(shape, lambda *_: (0,) * nd, pipeline_mode=pl.Buffered(1))


def _ada_kernel(cond_ref, w_ref, b_ref, out_ref):
    c = cond_ref[...]
    s = (c * jax.nn.sigmoid(c)).astype(BF16)
    out_ref[...] = jnp.dot(s, w_ref[...].astype(BF16), preferred_element_type=F32) + b_ref[...]


def _ada_call(cond, w_ada, b_ada):
    n = w_ada.shape[1]
    return pl.pallas_call(
        _ada_kernel,
        grid=(n // ADA_COLS,),
        in_specs=[
            pl.BlockSpec((COND_ROWS, D_MODEL), lambda j: (0, 0)),
            pl.BlockSpec((D_MODEL, ADA_COLS), lambda j: (0, j)),
            pl.BlockSpec((1, ADA_COLS), lambda j: (0, j)),
        ],
        out_specs=pl.BlockSpec((COND_ROWS, ADA_COLS), lambda j: (0, j)),
        out_shape=jax.ShapeDtypeStruct((COND_ROWS, n), F32),
        compiler_params=pltpu.CompilerParams(
            dimension_semantics=("arbitrary",), vmem_limit_bytes=VMEM_LIMIT),
        name="ada_rows",
    )(cond, w_ada, b_ada)


def _rms_rows(x):
    return x * lax.rsqrt(jnp.mean(x * x, axis=-1, keepdims=True) + RMS_EPS)


def _mixer_kernel(*refs, seq_len, ctx_len, use_rope, emit_kv, mod_row):
    it = iter(refs)
    x_ref, mod_ref = next(it), next(it)
    if ctx_len:
        ck_ref, cv_ref = next(it), next(it)
    if use_rope:
        cos_ref, sin_ref = next(it), next(it)
    (nmix_ref, win_ref, gsum_ref, qkg_ref, convw_ref, ga_ref, gc_ref, wout_ref) = (
        next(it) for _ in range(8))
    out_ref = next(it)
    if emit_kv:
        ko_ref, vo_ref = next(it), next(it)
    (q_s, k_s, vt_s, attn_s, t_s, gb_s, s_ref, p_ref) = (next(it) for _ in range(8))

    n_chunks = BLOCK_ROWS // CHUNK_ROWS
    if ctx_len:
        q_rows, n_keys, key_blk = LATENT_Q_ROWS, ctx_len + seq_len, BLOCK_ROWS
    else:
        q_rows, n_keys, key_blk = seq_len, seq_len, seq_len
    n_qb = BLOCK_ROWS // q_rows

    mod = _mod_row(mod_ref, mod_row)
    shift = mod[:, 0:D_MODEL]
    scale1 = 1.0 + mod[:, D_MODEL:2 * D_MODEL]
    gate = mod[:, 2 * D_MODEL:3 * D_MODEL]

    def put_values_t(blk, off, vt):
        vt = vt.astype(BF16)
        for g in range(N_KV):
            vt_s[blk, g, 0:HEAD_DIM, off:off + vt.shape[1]] = vt[g * HEAD_DIM:(g + 1) * HEAD_DIM, :]

    tail = (lax.broadcasted_iota(jnp.int32, (V_ROWS - HEAD_DIM, n_keys), 0) == 0).astype(BF16)
    for blk in range(vt_s.shape[0]):
        for g in range(N_KV):
            vt_s[blk, g, HEAD_DIM:V_ROWS, :] = tail

    if ctx_len:
        ck = ck_ref[...].reshape(KV_DIM, ctx_len).T.astype(BF16)
        for g in range(N_KV):
            k_s[g, 0:ctx_len, :] = ck[:, g * HEAD_DIM:(g + 1) * HEAD_DIM]
        put_values_t(0, 0, cv_ref[...].reshape(KV_DIM, ctx_len))

    if use_rope:
        lane = lax.broadcasted_iota(jnp.int32, (CHUNK_ROWS, LANES), 1)
        first_half = (lane % (2 * ROT_PAIRS)) < ROT_PAIRS

    t_s[0:8, :] = jnp.zeros((8, CONV_DIM), F32)
    t_s[8 + BLOCK_ROWS:16 + BLOCK_ROWS, :] = jnp.zeros((8, CONV_DIM), F32)
    w0 = convw_ref[0:1, :]
    w1 = convw_ref[1:2, :]
    w2 = convw_ref[2:3, :]

    def conv_rows(a, n):
        slab = t_s[a:a + n + 16, :]
        t_prev = pltpu.roll(slab, 1, axis=0)[8:8 + n, :]
        t_mid = slab[8:8 + n, :]
        t_next = pltpu.roll(slab, n + 15, axis=0)[8:8 + n, :]
        if seq_len < BLOCK_ROWS:
            pos = (lax.broadcasted_iota(jnp.int32, (n, 1), 0) + a) % seq_len
            t_prev = jnp.where(pos == 0, 0.0, t_prev)
            t_next = jnp.where(pos == seq_len - 1, 0.0, t_next)
        y = gb_s[a:a + n, :] * (w0 * t_prev + w1 * t_mid + w2 * t_next)
        return (_rms_rows(y) * gc_ref[...]).astype(BF16)

    def project_conv(r0, hb):
        cvp = jnp.dot(hb, win_ref[:, QKV_DIM:IN_DIM], preferred_element_type=F32)
        gb_s[r0:r0 + CHUNK_ROWS, :] = cvp[:, 0:CONV_DIM]
        t_s[8 + r0:8 + r0 + CHUNK_ROWS, :] = (cvp[:, CONV_DIM:2 * CONV_DIM]
                                              * cvp[:, 2 * CONV_DIM:3 * CONV_DIM])

    for c in range(n_chunks):
        r0 = c * CHUNK_ROWS
        x = x_ref[r0:r0 + CHUNK_ROWS, :]
        h = (_rms_rows(x) * nmix_ref[...]) * scale1 + shift
        hb = h.astype(BF16)
        qkv = jnp.dot(hb, win_ref[:, 0:QKV_DIM], preferred_element_type=F32)

        groups = []
        for g0 in range(0, QKV_DIM, MXU_DIM):
            sq = qkv[:, g0:g0 + MXU_DIM]
            groups.append(jnp.dot((sq * sq).astype(BF16), gsum_ref[...],
                                  preferred_element_type=F32))
        ms = jnp.concatenate(groups, axis=-1)[:, 0:QK_DIM]
        qk = (qkv[:, 0:QK_DIM] * lax.rsqrt(ms + RMS_EPS)) * qkg_ref[...]
        vv = qkv[:, QK_DIM:QKV_DIM]

        if emit_kv:
            for r1 in range(0, CHUNK_ROWS, seq_len):
                kt = qk[r1:r1 + seq_len, ATTN_DIM:QK_DIM].T
                ko_ref[(r0 + r1) // seq_len] = kt.reshape(N_KV, HEAD_DIM, seq_len)

        for cg in range(QK_DIM // LANES):
            xg = qk[:, cg * LANES:(cg + 1) * LANES]
            if use_rope:
                cs = cos_ref[r0:r0 + CHUNK_ROWS, :]
                sn = sin_ref[r0:r0 + CHUNK_ROWS, :]
                partner = jnp.where(first_half,
                                    pltpu.roll(xg, LANES - ROT_PAIRS, axis=1),
                                    pltpu.roll(xg, ROT_PAIRS, axis=1))
                xg = xg * cs + partner * sn
            if cg < ATTN_DIM // LANES:
                xb = (xg * Q_SCALE).astype(BF16)
                q_s[2 * cg, r0:r0 + CHUNK_ROWS, :] = xb[:, 0:HEAD_DIM]
                q_s[2 * cg + 1, r0:r0 + CHUNK_ROWS, :] = xb[:, HEAD_DIM:LANES]
            else:
                xb = xg.astype(BF16)
                for g in range(N_KV):
                    k_s[g, ctx_len + r0:ctx_len + r0 + CHUNK_ROWS, :] = (
                        xb[:, g * HEAD_DIM:(g + 1) * HEAD_DIM])
        w = min(key_blk, CHUNK_ROWS)
        for r1 in range(r0, r0 + CHUNK_ROWS, w):
            blk, off = (0, ctx_len + r1) if ctx_len else (r1 // key_blk, 0)
            vt = vv[r1 - r0:r1 - r0 + w, :].T
            put_values_t(blk, off, vt)
            if emit_kv:
                vo_ref[blk] = vt.reshape(N_KV, HEAD_DIM, seq_len)

        project_conv(r0, hb)

    def scores_t(qb, g):
        r0 = pl.multiple_of(qb * q_rows, q_rows)
        k0 = 0 if ctx_len else r0
        qs = jnp.concatenate(
            [q_s[KV_GROUP * g + j, pl.ds(r0, q_rows), :] for j in range(KV_GROUP)], axis=0)
        kk = k_s[g, pl.ds(k0, n_keys), :]
        return lax.dot_general(kk, qs, (((1,), (1,)), ((), ())),
                               preferred_element_type=F32)

    def values_out(qb, g, p):
        r0 = pl.multiple_of(qb * q_rows, q_rows)
        blk = 0 if ctx_len else qb
        ot = jnp.dot(vt_s[blk, g], p, preferred_element_type=F32)
        ot = ot[0:HEAD_DIM, :] / ot[HEAD_DIM:HEAD_DIM + 1, :]
        for jj in range(KV_GROUP // 2):
            pair_t = jnp.concatenate(
                [ot[:, (2 * jj) * q_rows:(2 * jj + 1) * q_rows],
                 ot[:, (2 * jj + 1) * q_rows:(2 * jj + 2) * q_rows]], axis=0)
            col = (KV_GROUP * g + 2 * jj) * HEAD_DIM
            attn_s[pl.ds(r0, q_rows), col:col + LANES] = pair_t.T

    def pipeline_pairs(step):
        assert n_qb % 2 == 0
        step(-1, 1, first=True)
        step(0, 0)

        def body(t, carry):
            step(2 * t + 1, 1)
            step(2 * t + 2, 0)
            return carry

        lax.fori_loop(0, (n_qb - 2) // 2, body, 0)
        step(n_qb - 1, 1, last=True)

    def shifted_block(qb, carry):
        for g in range(N_KV):
            s = scores_t(qb, g)
            s_ref[...] = s
            m = jnp.max(s, axis=0, keepdims=True)
            for k1 in range(0, n_keys, P_ROWS):
                p_ref[0, g, k1:k1 + P_ROWS, :] = jnp.exp2(
                    s_ref[k1:k1 + P_ROWS, :] - m).astype(BF16)
            values_out(qb, g, p_ref[0, g])
        return carry

    def unshifted_step(j, par, first=False, last=False):
        for g in range(N_KV):
            if not last:
                p_ref[1 - par, g] = jnp.exp2(scores_t(j + 1, g)).astype(BF16)
            if not first:
                values_out(j, g, p_ref[par, g])

    gains = jnp.abs(qkg_ref[...])
    q_bound = (Q_SCALE * Q_SCALE * HEAD_DIM) * jnp.max(gains[:, 0:ATTN_DIM]) ** 2
    k_bound = HEAD_DIM * jnp.max(gains[:, ATTN_DIM:QK_DIM]) ** 2
    if ctx_len:
        ck2 = ck_ref[...]
        k_bound = jnp.maximum(k_bound, jnp.max(jnp.sum(ck2 * ck2, axis=1)))
    small_scores = q_bound * k_bound <= MAX_UNSHIFTED_SCORE ** 2
    pl.when(small_scores)(lambda: pipeline_pairs(unshifted_step))

    @pl.when(jnp.logical_not(small_scores))
    def _():
        lax.fori_loop(0, n_qb, shifted_block, 0)

    for r0 in range(0, BLOCK_ROWS, CHUNK_ROWS):
        rows = slice(r0, r0 + CHUNK_ROWS)
        an = (_rms_rows(attn_s[rows, :]) * ga_ref[...]).astype(BF16)
        merged = jnp.concatenate([an, conv_rows(r0, CHUNK_ROWS)], axis=-1)
        mix = jnp.dot(merged, wout_ref[...], preferred_element_type=F32)
        out_ref[rows, :] = x_ref[rows, :] + gate * mix


def _mixer_call(x_blocks, mod, mod_row, ctx_k, ctx_v, rope, consts, *, seq_len, emit_kv):
    n_blocks = x_blocks.shape[0]
    ctx_len = 0 if ctx_k is None else ctx_k.shape[4]
    use_rope = rope is not None
    if ctx_len:
        assert seq_len == BLOCK_ROWS
        q_rows, n_keys, n_key_blocks = LATENT_Q_ROWS, ctx_len + BLOCK_ROWS, 1
    else:
        q_rows, n_keys, n_key_blocks = seq_len, seq_len, BLOCK_ROWS // seq_len
    n_q_cols = KV_GROUP * q_rows

    blk = lambda cols: pl.BlockSpec((None, BLOCK_ROWS, cols), lambda b: (b, 0, 0))
    assert mod_row is not None or n_blocks <= MOD_ROWS
    args = [x_blocks, mod]
    in_specs = [blk(D_MODEL), _mod_spec(mod_row, 0)]
    if ctx_len:
        args += [ctx_k, ctx_v]
        in_specs += [pl.BlockSpec((None, None, N_KV, HEAD_DIM, ctx_len),
                                  lambda b: (b, 0, 0, 0, 0))] * 2
    if use_rope:
        args += list(rope)
        in_specs += [_const_spec((BLOCK_ROWS, LANES))] * 2
    args += list(consts)
    in_specs += [_const_spec(a.shape) for a in consts]

    out_shape = [jax.ShapeDtypeStruct((n_blocks, BLOCK_ROWS, D_MODEL), F32)]
    out_specs = [blk(D_MODEL)]
    if emit_kv:
        per_block = BLOCK_ROWS // seq_len
        out_shape += [jax.ShapeDtypeStruct(
            (n_blocks * per_block, 1, N_KV, HEAD_DIM, seq_len), F32)] * 2
        out_specs += [pl.BlockSpec((per_block, None, N_KV, HEAD_DIM, seq_len),
                                   lambda b: (b, 0, 0, 0, 0))] * 2

    scratch = [
        pltpu.VMEM((N_HEADS, BLOCK_ROWS, HEAD_DIM), BF16),
        pltpu.VMEM((N_KV, ctx_len + BLOCK_ROWS, HEAD_DIM), BF16),
        pltpu.VMEM((n_key_blocks, N_KV, V_ROWS, n_keys), BF16),
        pltpu.VMEM((BLOCK_ROWS, ATTN_DIM), F32),
        pltpu.VMEM((BLOCK_ROWS + 16, CONV_DIM), F32),
        pltpu.VMEM((BLOCK_ROWS, CONV_DIM), F32),
        pltpu.VMEM((n_keys, n_q_cols), F32),
        pltpu.VMEM((2, N_KV, n_keys, n_q_cols), BF16),
    ]
    kern = functools.partial(_mixer_kernel, seq_len=seq_len, ctx_len=ctx_len,
                             use_rope=use_rope, emit_kv=emit_kv, mod_row=mod_row)
    return pl.pallas_call(
        kern,
        grid=(n_blocks,),
        in_specs=in_specs,
        out_specs=out_specs,
        out_shape=out_shape,
        scratch_shapes=scratch,
        compiler_params=pltpu.CompilerParams(
            dimension_semantics=("arbitrary",), vmem_limit_bytes=VMEM_LIMIT),
        name="mixer_ctx" if emit_kv else "mixer_latent",
    )(*args)


def _zero_row_after(dep, width):
    bits = pltpu.bitcast(dep, jnp.uint32)
    acc = bits[0:8, :]
    for r in range(8, bits.shape[0], 8):
        acc = acc | bits[r:r + 8, :]
    word = acc[:, 0:LANES]
    for c in range(LANES, acc.shape[1], LANES):
        word = word | acc[:, c:c + LANES]
    zero = pltpu.bitcast((word >> 16) >> 16, F32)[0:1, :]
    return jnp.concatenate([zero] * (width // LANES), axis=-1)


def _ffn_rows(x_ref, mod_ref, nffn_ref, wgu_ref, wd_ref, out_ref, mod_row):
    mod = _mod_row(mod_ref, mod_row)
    shift = mod[:, 0:D_MODEL]
    scale1 = 1.0 + mod[:, D_MODEL:2 * D_MODEL]
    gate = mod[:, 2 * D_MODEL:3 * D_MODEL]
    starts = list(range(0, FFN_ROWS, FFN_SUB_ROWS))
    hbs = [((_rms_rows(x_ref[r0:r0 + FFN_SUB_ROWS, :]) * nffn_ref[...]) * scale1
            + shift).astype(BF16) for r0 in starts]
    for k, r0 in enumerate(starts):
        x = x_ref[r0:r0 + FFN_SUB_ROWS, :]
        hb = hbs[k]
        acc = None
        for ci, (c0, cw) in enumerate(FF_CHUNKS):
            gt = jnp.dot(hb, wgu_ref[:, c0:c0 + cw], preferred_element_type=F32)
            if ci == 1 and k + 1 < len(starts):
                gt = gt + _zero_row_after(hbs[k + 1], cw)
            up = jnp.dot(hb, wgu_ref[:, D_FF + c0:D_FF + c0 + cw], preferred_element_type=F32)
            act = ((gt * jax.nn.sigmoid(gt)) * up).astype(BF16)
            part = jnp.dot(act, wd_ref[c0:c0 + cw, :], preferred_element_type=F32)
            acc = part if acc is None else acc + part
        out_ref[r0:r0 + FFN_SUB_ROWS, :] = x + gate * acc


def _ffn_kernel(x_ref, mod_ref, nffn_ref, wgu_ref, wd_ref, out_ref, *, mod_row):
    _ffn_rows(x_ref, mod_ref, nffn_ref, wgu_ref, wd_ref, out_ref, mod_row)


def _cast_rows(src_hbm, dst, stage, sem, chunk_rows):
    n = src_hbm.shape[0] // chunk_rows

    def fetch(c):
        return pltpu.make_async_copy(src_hbm.at[pl.ds(c * chunk_rows, chunk_rows), :],
                                     stage.at[c % 2], sem.at[c % 2])

    fetch(0).start()
    for c in range(n):
        if c + 1 < n:
            fetch(c + 1).start()
        fetch(c).wait()
        dst[c * chunk_rows:(c + 1) * chunk_rows, :] = stage[c % 2].astype(BF16)


def _ffn_cast_kernel(x_ref, mod_ref, nffn_ref, wgu_hbm, wd_hbm, out_ref, wgu_out, wd_out,
                     wgu_s, wd_s, stage_gu, stage_d, sem_in, sem_out, *, mod_row):
    first = pl.program_id(0) == 0
    save = (pltpu.make_async_copy(wgu_s, wgu_out, sem_out.at[0]),
            pltpu.make_async_copy(wd_s, wd_out, sem_out.at[1]))

    @pl.when(first)
    def _():
        _cast_rows(wgu_hbm, wgu_s, stage_gu, sem_in, GU_CAST_ROWS)
        _cast_rows(wd_hbm, wd_s, stage_d, sem_in, DOWN_CAST_ROWS)
        for cp in save:
            cp.start()

    _ffn_rows(x_ref, mod_ref, nffn_ref, wgu_s, wd_s, out_ref, mod_row)

    @pl.when(first)
    def _():
        for cp in save:
            cp.wait()


def _ffn_call(x_rows, mod, mod_row, norm_ffn, w_gate_up, w_down, name):
    n_rows = x_rows.shape[0]
    assert mod_row is not None or n_rows // FFN_ROWS <= MOD_ROWS
    cast = w_gate_up.dtype == F32
    rows_spec = pl.BlockSpec((FFN_ROWS, D_MODEL), lambda i: (i, 0))
    in_specs = [rows_spec, _mod_spec(mod_row, 1), _const_spec(norm_ffn.shape)]
    out_shape = [jax.ShapeDtypeStruct((n_rows, D_MODEL), F32)]
    out_specs = [rows_spec]
    scratch = []
    if cast:
        hbm = pl.BlockSpec(memory_space=pl.ANY)
        in_specs += [hbm, hbm]
        out_shape += [jax.ShapeDtypeStruct(w_gate_up.shape, BF16),
                      jax.ShapeDtypeStruct(w_down.shape, BF16)]
        out_specs += [hbm, hbm]
        scratch = [pltpu.VMEM(w_gate_up.shape, BF16), pltpu.VMEM(w_down.shape, BF16),
                   pltpu.VMEM((2, GU_CAST_ROWS, w_gate_up.shape[1]), F32),
                   pltpu.VMEM((2, DOWN_CAST_ROWS, w_down.shape[1]), F32),
                   pltpu.SemaphoreType.DMA((2,)), pltpu.SemaphoreType.DMA((2,))]
    else:
        in_specs += [_const_spec(w_gate_up.shape), _const_spec(w_down.shape)]
    res = pl.pallas_call(
        functools.partial(_ffn_cast_kernel if cast else _ffn_kernel, mod_row=mod_row),
        grid=(n_rows // FFN_ROWS,),
        in_specs=in_specs,
        out_specs=out_specs,
        out_shape=out_shape,
        scratch_shapes=scratch,
        compiler_params=pltpu.CompilerParams(
            dimension_semantics=("arbitrary",), vmem_limit_bytes=VMEM_LIMIT),
        name=name,
    )(x_rows, mod, norm_ffn, w_gate_up, w_down)
    return res if cast else res[0]


def _rope_tables(n_tokens):
    rows = n_tokens // GRID_W
    row = jnp.repeat(jnp.arange(rows, dtype=F32), GRID_W)
    col = jnp.tile(jnp.arange(GRID_W, dtype=F32), rows)
    inv = 1.0 / (ROPE_THETA ** (jnp.arange(ROT_PAIRS, dtype=F32) / ROT_PAIRS))
    ang = jnp.stack([row[:, None] * inv, col[:, None] * inv], axis=1)
    cos, sin = jnp.cos(ang), jnp.sin(ang)
    cos_h = jnp.concatenate([cos, cos], axis=-1).reshape(n_tokens, HEAD_DIM)
    sin_h = jnp.concatenate([-sin, sin], axis=-1).reshape(n_tokens, HEAD_DIM)
    reps = LANES // HEAD_DIM
    return jnp.tile(cos_h, (1, reps)), jnp.tile(sin_h, (1, reps))


def _keys_minor(kv):
    return jnp.transpose(kv, (0, 1, 3, 4, 2))


def _group_mean_matrix():
    idx = np.arange(MXU_DIM) // HEAD_DIM
    g = (idx[:, None] == idx[None, :]).astype(np.float32) / HEAD_DIM
    return jnp.asarray(g, dtype=BF16)


def kernel(x_prompt, x_sample, c, cache_k, cache_v, c_ctx, norm_mix, norm_ffn, w_ada, b_ada,
           w_in, q_norm, k_norm, conv_w, attn_out_norm, conv_out_norm, w_out, w_gate_up, w_down):
    depth = w_in.shape[0]
    assert depth == 1
    n_prompt, seq, _ = x_prompt.shape
    n_sample, dec_seq, _ = x_sample.shape
    assert dec_seq == BLOCK_ROWS and BLOCK_ROWS % seq == 0 and n_sample <= CTX_ROW

    cond = jnp.zeros((COND_ROWS, D_MODEL), F32)
    cond = cond.at[0:n_sample].set(c).at[CTX_ROW].set(c_ctx)
    mod = _ada_call(cond, w_ada[0], b_ada[0][None, :])

    consts = (
        norm_mix[0][None, :],
        w_in[0].astype(BF16),
        _group_mean_matrix(),
        jnp.concatenate([jnp.tile(q_norm[0], N_HEADS), jnp.tile(k_norm[0], N_KV)])[None, :],
        conv_w[0],
        attn_out_norm[0][None, :],
        conv_out_norm[0][None, :],
        w_out[0].astype(BF16),
    )
    nffn = norm_ffn[0][None, :]

    per_block = BLOCK_ROWS // seq
    xp_blocks = x_prompt.reshape(n_prompt // per_block, BLOCK_ROWS, D_MODEL)
    xp1, k_new, v_new = _mixer_call(xp_blocks, mod, CTX_ROW, None, None, None, consts,
                                    seq_len=seq, emit_kv=True)
    yp, wgu, wd = _ffn_call(xp1.reshape(-1, D_MODEL), mod, CTX_ROW, nffn,
                            w_gate_up[0], w_down[0], "ffn_ctx")

    assert dec_seq == FFN_ROWS
    (xs1,) = _mixer_call(x_sample, mod, None, _keys_minor(cache_k), _keys_minor(cache_v),
                         _rope_tables(dec_seq), consts, seq_len=dec_seq, emit_kv=False)
    ys = _ffn_call(xs1.reshape(-1, D_MODEL), mod, None, nffn, wgu, wd, "ffn_latent")

    return (yp.reshape(n_prompt, seq, D_MODEL),
            ys.reshape(n_sample, dec_seq, D_MODEL),
            jnp.transpose(k_new, (0, 1, 4, 2, 3)),
            jnp.transpose(v_new, (0, 1, 4, 2, 3)))
```

```python
import functools

import numpy as np
import jax
import jax.numpy as jnp
from jax import lax
from jax.experimental import pallas as pl
from jax.experimental.pallas import tpu as pltpu

D_MODEL = 1024
HEAD_DIM = 64
ATTN_DIM = 512
N_HEADS = 8
N_KV = 2
KV_GROUP = N_HEADS // N_KV
KV_DIM = N_KV * HEAD_DIM
CONV_DIM = 512
D_FF = 2816
QK_DIM = ATTN_DIM + KV_DIM
QKV_DIM = ATTN_DIM + 2 * KV_DIM
IN_DIM = QKV_DIM + 3 * CONV_DIM
GRID_W = 64
ROT_PAIRS = HEAD_DIM // 4
ROPE_THETA = 10000.0
RMS_EPS = 1e-6
Q_SCALE = HEAD_DIM ** -0.5 * 1.4426950408889634
V_ROWS = HEAD_DIM + 16
MAX_UNSHIFTED_SCORE = 64.0

LANES = 128
MXU_DIM = 256
BLOCK_ROWS = 1024
CHUNK_ROWS = 512
LATENT_Q_ROWS = 256
P_ROWS = 16
FFN_ROWS = 512
GU_CAST_ROWS = 128
DOWN_CAST_ROWS = 352
FF_CHUNKS = ((0, 1024), (1024, 1024), (2048, 768))
COND_ROWS = 16
MOD_ROWS = 8
CTX_ROW = 8
ADA_COLS = 2048
VMEM_LIMIT = 56 * 1024 * 1024

F32 = jnp.float32
BF16 = jnp.bfloat16


def _mod_spec(row, half):
    blk = 0 if row is None else row // MOD_ROWS
    return pl.BlockSpec((MOD_ROWS, 3 * D_MODEL), lambda i: (blk, half))


def _mod_row(mod_ref, row):
    r = pl.program_id(0) if row is None else row % MOD_ROWS
    return mod_ref[pl.ds(r, 1), :]


def _const_spec(shape):
    nd = len(shape)
    return pl.BlockSpec(shape, lambda *_: (0,) * nd, pipeline_mode=pl.Buffered(1))


def _ada_kernel(cond_ref, w_ref, b_ref, out_ref):
    c = cond_ref[...]
    s = (c * jax.nn.sigmoid(c)).astype(BF16)
    out_ref[...] = jnp.dot(s, w_ref[...].astype(BF16), preferred_element_type=F32) + b_ref[...]


def _ada_call(cond, w_ada, b_ada):
    n = w_ada.shape[1]
    return pl.pallas_call(
        _ada_kernel,
        grid=(n // ADA_COLS,),
        in_specs=[
            pl.BlockSpec((COND_ROWS, D_MODEL), lambda j: (0, 0)),
            pl.BlockSpec((D_MODEL, ADA_COLS), lambda j: (0, j)),
            pl.BlockSpec((1, ADA_COLS), lambda j: (0, j)),
        ],
        out_specs=pl.BlockSpec((COND_ROWS, ADA_COLS), lambda j: (0, j)),
        out_shape=jax.ShapeDtypeStruct((COND_ROWS, n), F32),
        compiler_params=pltpu.CompilerParams(
            dimension_semantics=("arbitrary",), vmem_limit_bytes=VMEM_LIMIT),
        name="ada_rows",
    )(cond, w_ada, b_ada)


def _rms_rows(x):
    return x * lax.rsqrt(jnp.mean(x * x, axis=-1, keepdims=True) + RMS_EPS)


def _mixer_kernel(*refs, seq_len, ctx_len, use_rope, emit_kv, mod_row):
    it = iter(refs)
    x_ref, mod_ref = next(it), next(it)
    if ctx_len:
        ck_ref, cv_ref = next(it), next(it)
    if use_rope:
        cos_ref, sin_ref = next(it), next(it)
    (nmix_ref, win_ref, gsum_ref, qkg_ref, convw_ref, ga_ref, gc_ref, wout_ref) = (
        next(it) for _ in range(8))
    out_ref = next(it)
    if emit_kv:
        ko_ref, vo_ref = next(it), next(it)
    (q_s, k_s, vt_s, attn_s, t_s, gb_s, s_ref, p_ref) = (next(it) for _ in range(8))

    n_chunks = BLOCK_ROWS // CHUNK_ROWS
    if ctx_len:
        q_rows, n_keys, key_blk = LATENT_Q_ROWS, ctx_len + seq_len, BLOCK_ROWS
    else:
        q_rows, n_keys, key_blk = seq_len, seq_len, seq_len
    n_qb = BLOCK_ROWS // q_rows

    mod = _mod_row(mod_ref, mod_row)
    shift = mod[:, 0:D_MODEL]
    scale1 = 1.0 + mod[:, D_MODEL:2 * D_MODEL]
    gate = mod[:, 2 * D_MODEL:3 * D_MODEL]

    def put_values_t(blk, off, vt):
        vt = vt.astype(BF16)
        for g in range(N_KV):
            vt_s[blk, g, 0:HEAD_DIM, off:off + vt.shape[1]] = vt[g * HEAD_DIM:(g + 1) * HEAD_DIM, :]

    tail = (lax.broadcasted_iota(jnp.int32, (V_ROWS - HEAD_DIM, n_keys), 0) == 0).astype(BF16)
    for blk in range(vt_s.shape[0]):
        for g in range(N_KV):
            vt_s[blk, g, HEAD_DIM:V_ROWS, :] = tail

    if ctx_len:
        ck = ck_ref[...].reshape(KV_DIM, ctx_len).T.astype(BF16)
        for g in range(N_KV):
            k_s[g, 0:ctx_len, :] = ck[:, g * HEAD_DIM:(g + 1) * HEAD_DIM]
        put_values_t(0, 0, cv_ref[...].reshape(KV_DIM, ctx_len))

    if use_rope:
        lane = lax.broadcasted_iota(jnp.int32, (CHUNK_ROWS, LANES), 1)
        first_half = (lane % (2 * ROT_PAIRS)) < ROT_PAIRS

    t_s[0:8, :] = jnp.zeros((8, CONV_DIM), F32)
    t_s[8 + BLOCK_ROWS:16 + BLOCK_ROWS, :] = jnp.zeros((8, CONV_DIM), F32)
    w0 = convw_ref[0:1, :]
    w1 = convw_ref[1:2, :]
    w2 = convw_ref[2:3, :]

    def conv_rows(a, n):
        slab = t_s[a:a + n + 16, :]
        t_prev = pltpu.roll(slab, 1, axis=0)[8:8 + n, :]
        t_mid = slab[8:8 + n, :]
        t_next = pltpu.roll(slab, n + 15, axis=0)[8:8 + n, :]
        if seq_len < BLOCK_ROWS:
            pos = (lax.broadcasted_iota(jnp.int32, (n, 1), 0) + a) % seq_len
            t_prev = jnp.where(pos == 0, 0.0, t_prev)
            t_next = jnp.where(pos == seq_len - 1, 0.0, t_next)
        y = gb_s[a:a + n, :] * (w0 * t_prev + w1 * t_mid + w2 * t_next)
        return (_rms_rows(y) * gc_ref[...]).astype(BF16)

    def project_conv(r0, hb):
        cvp = jnp.dot(hb, win_ref[:, QKV_DIM:IN_DIM], preferred_element_type=F32)
        gb_s[r0:r0 + CHUNK_ROWS, :] = cvp[:, 0:CONV_DIM]
        t_s[8 + r0:8 + r0 + CHUNK_ROWS, :] = (cvp[:, CONV_DIM:2 * CONV_DIM]
                                              * cvp[:, 2 * CONV_DIM:3 * CONV_DIM])

    for c in range(n_chunks):
        r0 = c * CHUNK_ROWS
        x = x_ref[r0:r0 + CHUNK_ROWS, :]
        h = (_rms_rows(x) * nmix_ref[...]) * scale1 + shift
        hb = h.astype(BF16)
        qkv = jnp.dot(hb, win_ref[:, 0:QKV_DIM], preferred_element_type=F32)

        groups = []
        for g0 in range(0, QKV_DIM, MXU_DIM):
            sq = qkv[:, g0:g0 + MXU_DIM]
            groups.append(jnp.dot((sq * sq).astype(BF16), gsum_ref[...],
                                  preferred_element_type=F32))
        ms = jnp.concatenate(groups, axis=-1)[:, 0:QK_DIM]
        qk = (qkv[:, 0:QK_DIM] * lax.rsqrt(ms + RMS_EPS)) * qkg_ref[...]
        vv = qkv[:, QK_DIM:QKV_DIM]

        if emit_kv:
            for r1 in range(0, CHUNK_ROWS, seq_len):
                kt = qk[r1:r1 + seq_len, ATTN_DIM:QK_DIM].T
                ko_ref[(r0 + r1) // seq_len] = kt.reshape(N_KV, HEAD_DIM, seq_len)

        for cg in range(QK_DIM // LANES):
            xg = qk[:, cg * LANES:(cg + 1) * LANES]
            if use_rope:
                cs = cos_ref[r0:r0 + CHUNK_ROWS, :]
                sn = sin_ref[r0:r0 + CHUNK_ROWS, :]
                partner = jnp.where(first_half,
                                    pltpu.roll(xg, LANES - ROT_PAIRS, axis=1),
                                    pltpu.roll(xg, ROT_PAIRS, axis=1))
                xg = xg * cs + partner * sn
            if cg < ATTN_DIM // LANES:
                xb = (xg * Q_SCALE).astype(BF16)
                q_s[2 * cg, r0:r0 + CHUNK_ROWS, :] = xb[:, 0:HEAD_DIM]
                q_s[2 * cg + 1, r0:r0 + CHUNK_ROWS, :] = xb[:, HEAD_DIM:LANES]
            else:
                xb = xg.astype(BF16)
                for g in range(N_KV):
                    k_s[g, ctx_len + r0:ctx_len + r0 + CHUNK_ROWS, :] = (
                        xb[:, g * HEAD_DIM:(g + 1) * HEAD_DIM])
        w = min(key_blk, CHUNK_ROWS)
        for r1 in range(r0, r0 + CHUNK_ROWS, w):
            blk, off = (0, ctx_len + r1) if ctx_len else (r1 // key_blk, 0)
            vt = vv[r1 - r0:r1 - r0 + w, :].T
            put_values_t(blk, off, vt)
            if emit_kv:
                vo_ref[blk] = vt.reshape(N_KV, HEAD_DIM, seq_len)

        project_conv(r0, hb)

    def scores_t(qb, g):
        r0 = pl.multiple_of(qb * q_rows, q_rows)
        k0 = 0 if ctx_len else r0
        qs = jnp.concatenate(
            [q_s[KV_GROUP * g + j, pl.ds(r0, q_rows), :] for j in range(KV_GROUP)], axis=0)
        kk = k_s[g, pl.ds(k0, n_keys), :]
        return lax.dot_general(kk, qs, (((1,), (1,)), ((), ())),
                               preferred_element_type=F32)

    def values_out(qb, g, p):
        r0 = pl.multiple_of(qb * q_rows, q_rows)
        blk = 0 if ctx_len else qb
        ot = jnp.dot(vt_s[blk, g], p, preferred_element_type=F32)
        ot = ot[0:HEAD_DIM, :] / ot[HEAD_DIM:HEAD_DIM + 1, :]
        for jj in range(KV_GROUP // 2):
            pair_t = jnp.concatenate(
                [ot[:, (2 * jj) * q_rows:(2 * jj + 1) * q_rows],
                 ot[:, (2 * jj + 1) * q_rows:(2 * jj + 2) * q_rows]], axis=0)
            col = (KV_GROUP * g + 2 * jj) * HEAD_DIM
            attn_s[pl.ds(r0, q_rows), col:col + LANES] = pair_t.T

    def pipeline_pairs(step):
        assert n_qb % 2 == 0
        step(-1, 1, first=True)
        step(0, 0)

        def body(t, carry):
            step(2 * t + 1, 1)
            step(2 * t + 2, 0)
            return carry

        lax.fori_loop(0, (n_qb - 2) // 2, body, 0)
        step(n_qb - 1, 1, last=True)

    def shifted_block(qb, carry):
        for g in range(N_KV):
            s = scores_t(qb, g)
            s_ref[...] = s
            m = jnp.max(s, axis=0, keepdims=True)
            for k1 in range(0, n_keys, P_ROWS):
                p_ref[0, g, k1:k1 + P_ROWS, :] = jnp.exp2(
                    s_ref[k1:k1 + P_ROWS, :] - m).astype(BF16)
            values_out(qb, g, p_ref[0, g])
        return carry

    def unshifted_step(j, par, first=False, last=False):
        for g in range(N_KV):
            if not last:
                p_ref[1 - par, g] = jnp.exp2(scores_t(j + 1, g)).astype(BF16)
            if not first:
                values_out(j, g, p_ref[par, g])

    gains = jnp.abs(qkg_ref[...])
    q_bound = (Q_SCALE * Q_SCALE * HEAD_DIM) * jnp.max(gains[:, 0:ATTN_DIM]) ** 2
    k_bound = HEAD_DIM * jnp.max(gains[:, ATTN_DIM:QK_DIM]) ** 2
    if ctx_len:
        ck2 = ck_ref[...]
        k_bound = jnp.maximum(k_bound, jnp.max(jnp.sum(ck2 * ck2, axis=1)))
    small_scores = q_bound * k_bound <= MAX_UNSHIFTED_SCORE ** 2
    pl.when(small_scores)(lambda: pipeline_pairs(unshifted_step))

    @pl.when(jnp.logical_not(small_scores))
    def _():
        lax.fori_loop(0, n_qb, shifted_block, 0)

    for r0 in range(0, BLOCK_ROWS, CHUNK_ROWS):
        rows = slice(r0, r0 + CHUNK_ROWS)
        an = (_rms_rows(attn_s[rows, :]) * ga_ref[...]).astype(BF16)
        merged = jnp.concatenate([an, conv_rows(r0, CHUNK_ROWS)], axis=-1)
        mix = jnp.dot(merged, wout_ref[...], preferred_element_type=F32)
        out_ref[rows, :] = x_ref[rows, :] + gate * mix


def _mixer_call(x_blocks, mod, mod_row, ctx_k, ctx_v, rope, consts, *, seq_len, emit_kv):
    n_blocks = x_blocks.shape[0]
    ctx_len = 0 if ctx_k is None else ctx_k.shape[4]
    use_rope = rope is not None
    if ctx_len:
        assert seq_len == BLOCK_ROWS
        q_rows, n_keys, n_key_blocks = LATENT_Q_ROWS, ctx_len + BLOCK_ROWS, 1
    else:
        q_rows, n_keys, n_key_blocks = seq_len, seq_len, BLOCK_ROWS // seq_len
    n_q_cols = KV_GROUP * q_rows

    blk = lambda cols: pl.BlockSpec((None, BLOCK_ROWS, cols), lambda b: (b, 0, 0))
    assert mod_row is not None or n_blocks <= MOD_ROWS
    args = [x_blocks, mod]
    in_specs = [blk(D_MODEL), _mod_spec(mod_row, 0)]
    if ctx_len:
        args += [ctx_k, ctx_v]
        in_specs += [pl.BlockSpec((None, None, N_KV, HEAD_DIM, ctx_len),
                                  lambda b: (b, 0, 0, 0, 0))] * 2
    if use_rope:
        args += list(rope)
        in_specs += [_const_spec((BLOCK_ROWS, LANES))] * 2
    args += list(consts)
    in_specs += [_const_spec(a.shape) for a in consts]

    out_shape = [jax.ShapeDtypeStruct((n_blocks, BLOCK_ROWS, D_MODEL), F32)]
    out_specs = [blk(D_MODEL)]
    if emit_kv:
        per_block = BLOCK_ROWS // seq_len
        out_shape += [jax.ShapeDtypeStruct(
            (n_blocks * per_block, 1, N_KV, HEAD_DIM, seq_len), F32)] * 2
        out_specs += [pl.BlockSpec((per_block, None, N_KV, HEAD_DIM, seq_len),
                                   lambda b: (b, 0, 0, 0, 0))] * 2

    scratch = [
        pltpu.VMEM((N_HEADS, BLOCK_ROWS, HEAD_DIM), BF16),
        pltpu.VMEM((N_KV, ctx_len + BLOCK_ROWS, HEAD_DIM), BF16),
        pltpu.VMEM((n_key_blocks, N_KV, V_ROWS, n_keys), BF16),
        pltpu.VMEM((BLOCK_ROWS, ATTN_DIM), F32),
        pltpu.VMEM((BLOCK_ROWS + 16, CONV_DIM), F32),
        pltpu.VMEM((BLOCK_ROWS, CONV_DIM), F32),
        pltpu.VMEM((n_keys, n_q_cols), F32),
        pltpu.VMEM((2, N_KV, n_keys, n_q_cols), BF16),
    ]
    kern = functools.partial(_mixer_kernel, seq_len=seq_len, ctx_len=ctx_len,
                             use_rope=use_rope, emit_kv=emit_kv, mod_row=mod_row)
    return pl.pallas_call(
        kern,
        grid=(n_blocks,),
        in_specs=in_specs,
        out_specs=out_specs,
        out_shape=out_shape,
        scratch_shapes=scratch,
        compiler_params=pltpu.CompilerParams(
            dimension_semantics=("arbitrary",), vmem_limit_bytes=VMEM_LIMIT),
        name="mixer_ctx" if emit_kv else "mixer_latent",
    )(*args)


def _ffn_rows(x_ref, mod, nffn_ref, wgu_ref, wd_ref, out_ref):
    shift = mod[:, 0:D_MODEL]
    scale1 = 1.0 + mod[:, D_MODEL:2 * D_MODEL]
    gate = mod[:, 2 * D_MODEL:3 * D_MODEL]
    x = x_ref[...]
    hb = ((_rms_rows(x) * nffn_ref[...]) * scale1 + shift).astype(BF16)
    acc = None
    for c0, cw in FF_CHUNKS:
        gt = jnp.dot(hb, wgu_ref[:, c0:c0 + cw], preferred_element_type=F32)
        up = jnp.dot(hb, wgu_ref[:, D_FF + c0:D_FF + c0 + cw], preferred_element_type=F32)
        act = ((gt * jax.nn.sigmoid(gt)) * up).astype(BF16)
        part = jnp.dot(act, wd_ref[c0:c0 + cw, :], preferred_element_type=F32)
        acc = part if acc is None else acc + part
    out_ref[...] = x + gate * acc


def _cast_rows(src_hbm, dst, stage, sem, chunk_rows):
    n = src_hbm.shape[0] // chunk_rows

    def fetch(c):
        return pltpu.make_async_copy(src_hbm.at[pl.ds(c * chunk_rows, chunk_rows), :],
                                     stage.at[c % 2], sem.at[c % 2])

    fetch(0).start()
    for c in range(n):
        if c + 1 < n:
            fetch(c + 1).start()
        fetch(c).wait()
        dst[c * chunk_rows:(c + 1) * chunk_rows, :] = stage[c % 2].astype(BF16)


def _ffn_kernel(xa_ref, xb_ref, mod_ref, nffn_ref, wgu_hbm, wd_hbm, ya_ref, yb_ref,
                wgu_s, wd_s, stage_gu, stage_d, sem, *, n_a, mod_row_a, tiles_per_mod_row_b):
    i = pl.program_id(0)

    @pl.when(i == 0)
    def _():
        _cast_rows(wgu_hbm, wgu_s, stage_gu, sem, GU_CAST_ROWS)
        _cast_rows(wd_hbm, wd_s, stage_d, sem, DOWN_CAST_ROWS)

    @pl.when(i < n_a)
    def _():
        mod = mod_ref[mod_row_a % MOD_ROWS:mod_row_a % MOD_ROWS + 1, :]
        _ffn_rows(xa_ref, mod, nffn_ref, wgu_s, wd_s, ya_ref)

    @pl.when(i >= n_a)
    def _():
        mod = mod_ref[pl.ds((i - n_a) // tiles_per_mod_row_b, 1), :]
        _ffn_rows(xb_ref, mod, nffn_ref, wgu_s, wd_s, yb_ref)


def _ffn_call(xa_rows, xb_rows, mod, mod_row_a, rows_per_mod_row_b, norm_ffn, w_gate_up, w_down):
    n_a, n_b = xa_rows.shape[0] // FFN_ROWS, xb_rows.shape[0] // FFN_ROWS
    tiles_per_row_b = rows_per_mod_row_b // FFN_ROWS
    assert n_b // tiles_per_row_b <= MOD_ROWS
    spec_a = pl.BlockSpec((FFN_ROWS, D_MODEL), lambda i: (jnp.minimum(i, n_a - 1), 0))
    spec_b = pl.BlockSpec((FFN_ROWS, D_MODEL), lambda i: (jnp.maximum(i - n_a, 0), 0))
    mod_spec = pl.BlockSpec((MOD_ROWS, 3 * D_MODEL),
                            lambda i: (jnp.where(i < n_a, mod_row_a // MOD_ROWS, 0), 1))
    hbm = pl.BlockSpec(memory_space=pl.ANY)
    return pl.pallas_call(
        functools.partial(_ffn_kernel, n_a=n_a, mod_row_a=mod_row_a,
                          tiles_per_mod_row_b=tiles_per_row_b),
        grid=(n_a + n_b,),
        in_specs=[spec_a, spec_b, mod_spec, _const_spec(norm_ffn.shape), hbm, hbm],
        out_specs=[spec_a, spec_b],
        out_shape=[jax.ShapeDtypeStruct(xa_rows.shape, F32),
                   jax.ShapeDtypeStruct(xb_rows.shape, F32)],
        scratch_shapes=[pltpu.VMEM(w_gate_up.shape, BF16), pltpu.VMEM(w_down.shape, BF16),
                        pltpu.VMEM((2, GU_CAST_ROWS, w_gate_up.shape[1]), F32),
                        pltpu.VMEM((2, DOWN_CAST_ROWS, w_down.shape[1]), F32),
                        pltpu.SemaphoreType.DMA((2,))],
        compiler_params=pltpu.CompilerParams(
            dimension_semantics=("arbitrary",), vmem_limit_bytes=VMEM_LIMIT),
        name="ffn",
    )(xa_rows, xb_rows, mod, norm_ffn, w_gate_up, w_down)


def _rope_tables(n_tokens):
    rows = n_tokens // GRID_W
    row = jnp.repeat(jnp.arange(rows, dtype=F32), GRID_W)
    col = jnp.tile(jnp.arange(GRID_W, dtype=F32), rows)
    inv = 1.0 / (ROPE_THETA ** (jnp.arange(ROT_PAIRS, dtype=F32) / ROT_PAIRS))
    ang = jnp.stack([row[:, None] * inv, col[:, None] * inv], axis=1)
    cos, sin = jnp.cos(ang), jnp.sin(ang)
    cos_h = jnp.concatenate([cos, cos], axis=-1).reshape(n_tokens, HEAD_DIM)
    sin_h = jnp.concatenate([-sin, sin], axis=-1).reshape(n_tokens, HEAD_DIM)
    reps = LANES // HEAD_DIM
    return jnp.tile(cos_h, (1, reps)), jnp.tile(sin_h, (1, reps))


def _keys_minor(kv):
    return jnp.transpose(kv, (0, 1, 3, 4, 2))


def _group_mean_matrix():
    idx = np.arange(MXU_DIM) // HEAD_DIM
    g = (idx[:, None] == idx[None, :]).astype(np.float32) / HEAD_DIM
    return jnp.asarray(g, dtype=BF16)


def kernel(x_prompt, x_sample, c, cache_k, cache_v, c_ctx, norm_mix, norm_ffn, w_ada, b_ada,
           w_in, q_norm, k_norm, conv_w, attn_out_norm, conv_out_norm, w_out, w_gate_up, w_down):
    depth = w_in.shape[0]
    assert depth == 1
    n_prompt, seq, _ = x_prompt.shape
    n_sample, dec_seq, _ = x_sample.shape
    assert dec_seq == BLOCK_ROWS and BLOCK_ROWS % seq == 0 and n_sample <= CTX_ROW

    cond = jnp.zeros((COND_ROWS, D_MODEL), F32)
    cond = cond.at[0:n_sample].set(c).at[CTX_ROW].set(c_ctx)
    mod = _ada_call(cond, w_ada[0], b_ada[0][None, :])

    consts = (
        norm_mix[0][None, :],
        w_in[0].astype(BF16),
        _group_mean_matrix(),
        jnp.concatenate([jnp.tile(q_norm[0], N_HEADS), jnp.tile(k_norm[0], N_KV)])[None, :],
        conv_w[0],
        attn_out_norm[0][None, :],
        conv_out_norm[0][None, :],
        w_out[0].astype(BF16),
    )
    nffn = norm_ffn[0][None, :]

    per_block = BLOCK_ROWS // seq
    xp_blocks = x_prompt.reshape(n_prompt // per_block, BLOCK_ROWS, D_MODEL)
    xp1, k_new, v_new = _mixer_call(xp_blocks, mod, CTX_ROW, None, None, None, consts,
                                    seq_len=seq, emit_kv=True)

    (xs1,) = _mixer_call(x_sample, mod, None, _keys_minor(cache_k), _keys_minor(cache_v),
                         _rope_tables(dec_seq), consts, seq_len=dec_seq, emit_kv=False)

    yp, ys = _ffn_call(xp1.reshape(-1, D_MODEL), xs1.reshape(-1, D_MODEL), mod, CTX_ROW, dec_seq,
                       nffn, w_gate_up[0], w_down[0])

    return (yp.reshape(n_prompt, seq, D_MODEL),
            ys.reshape(n_sample, dec_seq, D_MODEL),
            jnp.transpose(k_new, (0, 1, 4, 2, 3)),
            jnp.transpose(v_new, (0, 1, 4, 2, 3)))
```

```python
import functools

import numpy as np
import jax
import jax.numpy as jnp
from jax import lax
from jax.experimental import pallas as pl
from jax.experimental.pallas import tpu as pltpu

D_MODEL = 1024
HEAD_DIM = 64
ATTN_DIM = 512
N_HEADS = 8
N_KV = 2
KV_GROUP = N_HEADS // N_KV
KV_DIM = N_KV * HEAD_DIM
CONV_DIM = 512
D_FF = 2816
QK_DIM = ATTN_DIM + KV_DIM
QKV_DIM = ATTN_DIM + 2 * KV_DIM
IN_DIM = QKV_DIM + 3 * CONV_DIM
GRID_W = 64
ROT_PAIRS = HEAD_DIM // 4
ROPE_THETA = 10000.0
RMS_EPS = 1e-6
Q_SCALE = HEAD_DIM ** -0.5 * 1.4426950408889634
V_ROWS = HEAD_DIM + 16
MAX_UNSHIFTED_SCORE = 64.0

LANES = 128
MXU_DIM = 256
BLOCK_ROWS = 1024
CHUNK_ROWS = 512
LATENT_Q_ROWS = 256
P_ROWS = 16
FFN_ROWS = 512
GU_CAST_ROWS = 128
DOWN_CAST_ROWS = 352
FF_CHUNKS = ((0, 1024), (1024, 1024), (2048, 768))
COND_ROWS = 16
MOD_ROWS = 8
CTX_ROW = 8
ADA_COLS = 2048
VMEM_LIMIT = 56 * 1024 * 1024

F32 = jnp.float32
BF16 = jnp.bfloat16


def _mod_spec(row, half):
    blk = 0 if row is None else row // MOD_ROWS
    return pl.BlockSpec((MOD_ROWS, 3 * D_MODEL), lambda i: (blk, half))


def _mod_row(mod_ref, row):
    r = pl.program_id(0) if row is None else row % MOD_ROWS
    return mod_ref[pl.ds(r, 1), :]


def _const_spec(shape):
    nd = len(shape)
    return pl.BlockSpec(shape, lambda *_: (0,) * nd, pipeline_mode=pl.Buffered(1))


def _ada_kernel(cond_ref, w_ref, b_ref, out_ref):
    c = cond_ref[...]
    s = (c * jax.nn.sigmoid(c)).astype(BF16)
    out_ref[...] = jnp.dot(s, w_ref[...].astype(BF16), preferred_element_type=F32) + b_ref[...]


def _ada_call(cond, w_ada, b_ada):
    n = w_ada.shape[1]
    return pl.pallas_call(
        _ada_kernel,
        grid=(n // ADA_COLS,),
        in_specs=[
            pl.BlockSpec((COND_ROWS, D_MODEL), lambda j: (0, 0)),
            pl.BlockSpec((D_MODEL, ADA_COLS), lambda j: (0, j)),
            pl.BlockSpec((1, ADA_COLS), lambda j: (0, j)),
        ],
        out_specs=pl.BlockSpec((COND_ROWS, ADA_COLS), lambda j: (0, j)),
        out_shape=jax.ShapeDtypeStruct((COND_ROWS, n), F32),
        compiler_params=pltpu.CompilerParams(
            dimension_semantics=("arbitrary",), vmem_limit_bytes=VMEM_LIMIT),
        name="ada_rows",
    )(cond, w_ada, b_ada)


def _rms_rows(x):
    return x * lax.rsqrt(jnp.mean(x * x, axis=-1, keepdims=True) + RMS_EPS)


def _mixer_kernel(*refs, seq_len, ctx_len, use_rope, emit_kv, mod_row):
    it = iter(refs)
    x_ref, mod_ref = next(it), next(it)
    if ctx_len:
        ck_ref, cv_ref = next(it), next(it)
    if use_rope:
        cos_ref, sin_ref = next(it), next(it)
    (nmix_ref, win_ref, gsum_ref, qkg_ref, convw_ref, ga_ref, gc_ref, wout_ref) = (
        next(it) for _ in range(8))
    out_ref = next(it)
    if emit_kv:
        ko_ref, vo_ref = next(it), next(it)
    (q_s, k_s, vt_s, attn_s, t_s, gb_s, s_ref, p_ref) = (next(it) for _ in range(8))

    n_chunks = BLOCK_ROWS // CHUNK_ROWS
    if ctx_len:
        q_rows, n_keys, key_blk = LATENT_Q_ROWS, ctx_len + seq_len, BLOCK_ROWS
    else:
        q_rows, n_keys, key_blk = seq_len, seq_len, seq_len
    n_qb = BLOCK_ROWS // q_rows

    mod = _mod_row(mod_ref, mod_row)
    shift = mod[:, 0:D_MODEL]
    scale1 = 1.0 + mod[:, D_MODEL:2 * D_MODEL]
    gate = mod[:, 2 * D_MODEL:3 * D_MODEL]

    def put_values_t(blk, off, vt):
        vt = vt.astype(BF16)
        for g in range(N_KV):
            vt_s[blk, g, 0:HEAD_DIM, off:off + vt.shape[1]] = vt[g * HEAD_DIM:(g + 1) * HEAD_DIM, :]

    tail = (lax.broadcasted_iota(jnp.int32, (V_ROWS - HEAD_DIM, n_keys), 0) == 0).astype(BF16)
    for blk in range(vt_s.shape[0]):
        for g in range(N_KV):
            vt_s[blk, g, HEAD_DIM:V_ROWS, :] = tail

    if ctx_len:
        ck = ck_ref[...].reshape(KV_DIM, ctx_len).T.astype(BF16)
        for g in range(N_KV):
            k_s[g, 0:ctx_len, :] = ck[:, g * HEAD_DIM:(g + 1) * HEAD_DIM]
        put_values_t(0, 0, cv_ref[...].reshape(KV_DIM, ctx_len))

    if use_rope:
        lane = lax.broadcasted_iota(jnp.int32, (CHUNK_ROWS, LANES), 1)
        first_half = (lane % (2 * ROT_PAIRS)) < ROT_PAIRS

    t_s[0:8, :] = jnp.zeros((8, CONV_DIM), F32)
    t_s[8 + BLOCK_ROWS:16 + BLOCK_ROWS, :] = jnp.zeros((8, CONV_DIM), F32)
    w0 = convw_ref[0:1, :]
    w1 = convw_ref[1:2, :]
    w2 = convw_ref[2:3, :]

    def conv_rows(a, n):
        slab = t_s[a:a + n + 16, :]
        t_prev = pltpu.roll(slab, 1, axis=0)[8:8 + n, :]
        t_mid = slab[8:8 + n, :]
        t_next = pltpu.roll(slab, n + 15, axis=0)[8:8 + n, :]
        if seq_len < BLOCK_ROWS:
            pos = (lax.broadcasted_iota(jnp.int32, (n, 1), 0) + a) % seq_len
            t_prev = jnp.where(pos == 0, 0.0, t_prev)
            t_next = jnp.where(pos == seq_len - 1, 0.0, t_next)
        y = gb_s[a:a + n, :] * (w0 * t_prev + w1 * t_mid + w2 * t_next)
        return (_rms_rows(y) * gc_ref[...]).astype(BF16)

    def project_conv(r0, hb):
        cvp = jnp.dot(hb, win_ref[:, QKV_DIM:IN_DIM], preferred_element_type=F32)
        gb_s[r0:r0 + CHUNK_ROWS, :] = cvp[:, 0:CONV_DIM]
        t_s[8 + r0:8 + r0 + CHUNK_ROWS, :] = (cvp[:, CONV_DIM:2 * CONV_DIM]
                                              * cvp[:, 2 * CONV_DIM:3 * CONV_DIM])

    for c in range(n_chunks):
        r0 = c * CHUNK_ROWS
        x = x_ref[r0:r0 + CHUNK_ROWS, :]
        h = (_rms_rows(x) * nmix_ref[...]) * scale1 + shift
        hb = h.astype(BF16)
        qkv = jnp.dot(hb, win_ref[:, 0:QKV_DIM], preferred_element_type=F32)

        groups = []
        for g0 in range(0, QKV_DIM, MXU_DIM):
            sq = qkv[:, g0:g0 + MXU_DIM]
            groups.append(jnp.dot((sq * sq).astype(BF16), gsum_ref[...],
                                  preferred_element_type=F32))
        ms = jnp.concatenate(groups, axis=-1)[:, 0:QK_DIM]
        qk = (qkv[:, 0:QK_DIM] * lax.rsqrt(ms + RMS_EPS)) * qkg_ref[...]
        vv = qkv[:, QK_DIM:QKV_DIM]

        if emit_kv:
            for r1 in range(0, CHUNK_ROWS, seq_len):
                kt = qk[r1:r1 + seq_len, ATTN_DIM:QK_DIM].T
                ko_ref[(r0 + r1) // seq_len] = kt.reshape(N_KV, HEAD_DIM, seq_len)

        for cg in range(QK_DIM // LANES):
            xg = qk[:, cg * LANES:(cg + 1) * LANES]
            if use_rope:
                cs = cos_ref[r0:r0 + CHUNK_ROWS, :]
                sn = sin_ref[r0:r0 + CHUNK_ROWS, :]
                partner = jnp.where(first_half,
                                    pltpu.roll(xg, LANES - ROT_PAIRS, axis=1),
                                    pltpu.roll(xg, ROT_PAIRS, axis=1))
                xg = xg * cs + partner * sn
            if cg < ATTN_DIM // LANES:
                xb = (xg * Q_SCALE).astype(BF16)
                q_s[2 * cg, r0:r0 + CHUNK_ROWS, :] = xb[:, 0:HEAD_DIM]
                q_s[2 * cg + 1, r0:r0 + CHUNK_ROWS, :] = xb[:, HEAD_DIM:LANES]
            else:
                xb = xg.astype(BF16)
                for g in range(N_KV):
                    k_s[g, ctx_len + r0:ctx_len + r0 + CHUNK_ROWS, :] = (
                        xb[:, g * HEAD_DIM:(g + 1) * HEAD_DIM])
        w = min(key_blk, CHUNK_ROWS)
        for r1 in range(r0, r0 + CHUNK_ROWS, w):
            blk, off = (0, ctx_len + r1) if ctx_len else (r1 // key_blk, 0)
            vt = vv[r1 - r0:r1 - r0 + w, :].T
            put_values_t(blk, off, vt)
            if emit_kv:
                vo_ref[blk] = vt.reshape(N_KV, HEAD_DIM, seq_len)

        project_conv(r0, hb)

    def scores_t(qb, g):
        r0 = pl.multiple_of(qb * q_rows, q_rows)
        k0 = 0 if ctx_len else r0
        qs = jnp.concatenate(
            [q_s[KV_GROUP * g + j, pl.ds(r0, q_rows), :] for j in range(KV_GROUP)], axis=0)
        kk = k_s[g, pl.ds(k0, n_keys), :]
        return lax.dot_general(kk, qs, (((1,), (1,)), ((), ())),
                               preferred_element_type=F32)

    def values_out(qb, g, p):
        r0 = pl.multiple_of(qb * q_rows, q_rows)
        blk = 0 if ctx_len else qb
        ot = jnp.dot(vt_s[blk, g], p, preferred_element_type=F32)
        ot = ot[0:HEAD_DIM, :] / ot[HEAD_DIM:HEAD_DIM + 1, :]
        for jj in range(KV_GROUP // 2):
            pair_t = jnp.concatenate(
                [ot[:, (2 * jj) * q_rows:(2 * jj + 1) * q_rows],
                 ot[:, (2 * jj + 1) * q_rows:(2 * jj + 2) * q_rows]], axis=0)
            col = (KV_GROUP * g + 2 * jj) * HEAD_DIM
            attn_s[pl.ds(r0, q_rows), col:col + LANES] = pair_t.T

    def pipeline_pairs(step):
        assert n_qb % 2 == 0
        step(-1, 1, first=True)
        step(0, 0)

        def body(t, carry):
            step(2 * t + 1, 1)
            step(2 * t + 2, 0)
            return carry

        lax.fori_loop(0, (n_qb - 2) // 2, body, 0)
        step(n_qb - 1, 1, last=True)

    def shifted_block(qb, carry):
        for g in range(N_KV):
            s = scores_t(qb, g)
            s_ref[...] = s
            m = jnp.max(s, axis=0, keepdims=True)
            for k1 in range(0, n_keys, P_ROWS):
                p_ref[0, g, k1:k1 + P_ROWS, :] = jnp.exp2(
                    s_ref[k1:k1 + P_ROWS, :] - m).astype(BF16)
            values_out(qb, g, p_ref[0, g])
        return carry

    def unshifted_step(j, par, first=False, last=False):
        for g in range(N_KV):
            if not last:
                p_ref[1 - par, g] = jnp.exp2(scores_t(j + 1, g)).astype(BF16)
            if not first:
                values_out(j, g, p_ref[par, g])

    gains = jnp.abs(qkg_ref[...])
    q_bound = (Q_SCALE * Q_SCALE * HEAD_DIM) * jnp.max(gains[:, 0:ATTN_DIM]) ** 2
    k_bound = HEAD_DIM * jnp.max(gains[:, ATTN_DIM:QK_DIM]) ** 2
    if ctx_len:
        ck2 = ck_ref[...]
        k_bound = jnp.maximum(k_bound, jnp.max(jnp.sum(ck2 * ck2, axis=1)))
    small_scores = q_bound * k_bound <= MAX_UNSHIFTED_SCORE ** 2
    pl.when(small_scores)(lambda: pipeline_pairs(unshifted_step))

    @pl.when(jnp.logical_not(small_scores))
    def _():
        lax.fori_loop(0, n_qb, shifted_block, 0)

    for r0 in range(0, BLOCK_ROWS, CHUNK_ROWS):
        rows = slice(r0, r0 + CHUNK_ROWS)
        an = (_rms_rows(attn_s[rows, :]) * ga_ref[...]).astype(BF16)
        merged = jnp.concatenate([an, conv_rows(r0, CHUNK_ROWS)], axis=-1)
        mix = jnp.dot(merged, wout_ref[...], preferred_element_type=F32)
        out_ref[rows, :] = x_ref[rows, :] + gate * mix


def _mixer_call(x_blocks, mod, mod_row, ctx_k, ctx_v, rope, consts, *, seq_len, emit_kv):
    n_blocks = x_blocks.shape[0]
    ctx_len = 0 if ctx_k is None else ctx_k.shape[4]
    use_rope = rope is not None
    if ctx_len:
        assert seq_len == BLOCK_ROWS
        q_rows, n_keys, n_key_blocks = LATENT_Q_ROWS, ctx_len + BLOCK_ROWS, 1
    else:
        q_rows, n_keys, n_key_blocks = seq_len, seq_len, BLOCK_ROWS // seq_len
    n_q_cols = KV_GROUP * q_rows

    blk = lambda cols: pl.BlockSpec((None, BLOCK_ROWS, cols), lambda b: (b, 0, 0))
    assert mod_row is not None or n_blocks <= MOD_ROWS
    args = [x_blocks, mod]
    in_specs = [blk(D_MODEL), _mod_spec(mod_row, 0)]
    if ctx_len:
        args += [ctx_k, ctx_v]
        in_specs += [pl.BlockSpec((None, None, N_KV, HEAD_DIM, ctx_len),
                                  lambda b: (b, 0, 0, 0, 0))] * 2
    if use_rope:
        args += list(rope)
        in_specs += [_const_spec((BLOCK_ROWS, LANES))] * 2
    args += list(consts)
    in_specs += [_const_spec(a.shape) for a in consts]

    out_shape = [jax.ShapeDtypeStruct((n_blocks, BLOCK_ROWS, D_MODEL), F32)]
    out_specs = [blk(D_MODEL)]
    if emit_kv:
        per_block = BLOCK_ROWS // seq_len
        out_shape += [jax.ShapeDtypeStruct(
            (n_blocks * per_block, 1, N_KV, HEAD_DIM, seq_len), F32)] * 2
        out_specs += [pl.BlockSpec((per_block, None, N_KV, HEAD_DIM, seq_len),
                                   lambda b: (b, 0, 0, 0, 0))] * 2

    scratch = [
        pltpu.VMEM((N_HEADS, BLOCK_ROWS, HEAD_DIM), BF16),
        pltpu.VMEM((N_KV, ctx_len + BLOCK_ROWS, HEAD_DIM), BF16),
        pltpu.VMEM((n_key_blocks, N_KV, V_ROWS, n_keys), BF16),
        pltpu.VMEM((BLOCK_ROWS, ATTN_DIM), F32),
        pltpu.VMEM((BLOCK_ROWS + 16, CONV_DIM), F32),
        pltpu.VMEM((BLOCK_ROWS, CONV_DIM), F32),
        pltpu.VMEM((n_keys, n_q_cols), F32),
        pltpu.VMEM((2, N_KV, n_keys, n_q_cols), BF16),
    ]
    kern = functools.partial(_mixer_kernel, seq_len=seq_len, ctx_len=ctx_len,
                             use_rope=use_rope, emit_kv=emit_kv, mod_row=mod_row)
    return pl.pallas_call(
        kern,
        grid=(n_blocks,),
        in_specs=in_specs,
        out_specs=out_specs,
        out_shape=out_shape,
        scratch_shapes=scratch,
        compiler_params=pltpu.CompilerParams(
            dimension_semantics=("arbitrary",), vmem_limit_bytes=VMEM_LIMIT),
        name="mixer_ctx" if emit_kv else "mixer_latent",
    )(*args)


def _ffn_rows(x_ref, mod, nffn_ref, wgu_ref, wd_ref, out_ref):
    shift = mod[:, 0:D_MODEL]
    scale1 = 1.0 + mod[:, D_MODEL:2 * D_MODEL]
    gate = mod[:, 2 * D_MODEL:3 * D_MODEL]
    x = x_ref[...]
    hb = ((_rms_rows(x) * nffn_ref[...]) * scale1 + shift).astype(BF16)
    acc = None
    for c0, cw in FF_CHUNKS:
        gt = jnp.dot(hb, wgu_ref[:, c0:c0 + cw], preferred_element_type=F32)
        up = jnp.dot(hb, wgu_ref[:, D_FF + c0:D_FF + c0 + cw], preferred_element_type=F32)
        act = ((gt * jax.nn.sigmoid(gt)) * up).astype(BF16)
        part = jnp.dot(act, wd_ref[c0:c0 + cw, :], preferred_element_type=F32)
        acc = part if acc is None else acc + part
    out_ref[...] = x + gate * acc


def _cast_rows(src_hbm, dst, stage, sem, chunk_rows):
    n = src_hbm.shape[0] // chunk_rows

    def fetch(c):
        return pltpu.make_async_copy(src_hbm.at[pl.ds(c * chunk_rows, chunk_rows), :],
                                     stage.at[c % 2], sem.at[c % 2])

    fetch(0).start()
    for c in range(n):
        if c + 1 < n:
            fetch(c + 1).start()
        fetch(c).wait()
        dst[c * chunk_rows:(c + 1) * chunk_rows, :] = stage[c % 2].astype(BF16)


def _ffn_kernel(xa_ref, xb_ref, mod_ref, nffn_ref, wgu_hbm, wd_hbm, ya_ref, yb_ref,
                wgu_s, wd_s, stage_gu, stage_d, sem, *, n_a, mod_row_a, tiles_per_mod_row_b):
    i = pl.program_id(0)

    @pl.when(i == 0)
    def _():
        _cast_rows(wgu_hbm, wgu_s, stage_gu, sem, GU_CAST_ROWS)
        _cast_rows(wd_hbm, wd_s, stage_d, sem, DOWN_CAST_ROWS)

    @pl.when(i < n_a)
    def _():
        mod = mod_ref[mod_row_a % MOD_ROWS:mod_row_a % MOD_ROWS + 1, :]
        _ffn_rows(xa_ref, mod, nffn_ref, wgu_s, wd_s, ya_ref)

    @pl.when(i >= n_a)
    def _():
        mod = mod_ref[pl.ds((i - n_a) // tiles_per_mod_row_b, 1), :]
        _ffn_rows(xb_ref, mod, nffn_ref, wgu_s, wd_s, yb_ref)


def _ffn_call(xa_rows, xb_rows, mod, mod_row_a, rows_per_mod_row_b, norm_ffn, w_gate_up, w_down):
    n_a, n_b = xa_rows.shape[0] // FFN_ROWS, xb_rows.shape[0] // FFN_ROWS
    tiles_per_row_b = rows_per_mod_row_b // FFN_ROWS
    assert n_b // tiles_per_row_b <= MOD_ROWS
    spec_a = pl.BlockSpec((FFN_ROWS, D_MODEL), lambda i: (jnp.minimum(i, n_a - 1), 0))
    spec_b = pl.BlockSpec((FFN_ROWS, D_MODEL), lambda i: (jnp.maximum(i - n_a, 0), 0))
    mod_spec = pl.BlockSpec((MOD_ROWS, 3 * D_MODEL),
                            lambda i: (jnp.where(i < n_a, mod_row_a // MOD_ROWS, 0), 1))
    hbm = pl.BlockSpec(memory_space=pl.ANY)
    return pl.pallas_call(
        functools.partial(_ffn_kernel, n_a=n_a, mod_row_a=mod_row_a,
                          tiles_per_mod_row_b=tiles_per_row_b),
        grid=(n_a + n_b,),
        in_specs=[spec_a, spec_b, mod_spec, _const_spec(norm_ffn.shape), hbm, hbm],
        out_specs=[spec_a, spec_b],
        out_shape=[jax.ShapeDtypeStruct(xa_rows.shape, F32),
                   jax.ShapeDtypeStruct(xb_rows.shape, F32)],
        scratch_shapes=[pltpu.VMEM(w_gate_up.shape, BF16), pltpu.VMEM(w_down.shape, BF16),
                        pltpu.VMEM((2, GU_CAST_ROWS, w_gate_up.shape[1]), F32),
                        pltpu.VMEM((2, DOWN_CAST_ROWS, w_down.shape[1]), F32),
                        pltpu.SemaphoreType.DMA((2,))],
        compiler_params=pltpu.CompilerParams(
            dimension_semantics=("arbitrary",), vmem_limit_bytes=VMEM_LIMIT),
        name="ffn",
    )(xa_rows, xb_rows, mod, norm_ffn, w_gate_up, w_down)


def _rope_tables(n_tokens):
    rows = n_tokens // GRID_W
    row = jnp.repeat(jnp.arange(rows, dtype=F32), GRID_W)
    col = jnp.tile(jnp.arange(GRID_W, dtype=F32), rows)
    inv = 1.0 / (ROPE_THETA ** (jnp.arange(ROT_PAIRS, dtype=F32) / ROT_PAIRS))
    ang = jnp.stack([row[:, None] * inv, col[:, None] * inv], axis=1)
    cos, sin = jnp.cos(ang), jnp.sin(ang)
    cos_h = jnp.concatenate([cos, cos], axis=-1).reshape(n_tokens, HEAD_DIM)
    sin_h = jnp.concatenate([-sin, sin], axis=-1).reshape(n_tokens, HEAD_DIM)
    reps = LANES // HEAD_DIM
    return jnp.tile(cos_h, (1, reps)), jnp.tile(sin_h, (1, reps))


def _keys_minor(kv):
    return jnp.transpose(kv, (0, 1, 3, 4, 2))


def _group_mean_matrix():
    idx = np.arange(MXU_DIM) // HEAD_DIM
    g = (idx[:, None] == idx[None, :]).astype(np.float32) / HEAD_DIM
    return jnp.asarray(g, dtype=BF16)


def kernel(x_prompt, x_sample, c, cache_k, cache_v, c_ctx, norm_mix, norm_ffn, w_ada, b_ada,
           w_in, q_norm, k_norm, conv_w, attn_out_norm, conv_out_norm, w_out, w_gate_up, w_down):
    depth = w_in.shape[0]
    assert depth == 1
    n_prompt, seq, _ = x_prompt.shape
    n_sample, dec_seq, _ = x_sample.shape
    assert dec_seq == BLOCK_ROWS and BLOCK_ROWS % seq == 0 and n_sample <= CTX_ROW

    cond = jnp.zeros((COND_ROWS, D_MODEL), F32)
    cond = cond.at[0:n_sample].set(c).at[CTX_ROW].set(c_ctx)
    mod = _ada_call(cond, w_ada[0], b_ada[0][None, :])

    consts = (
        norm_mix[0][None, :],
        w_in[0].astype(BF16),
        _group_mean_matrix(),
        jnp.concatenate([jnp.tile(q_norm[0], N_HEADS), jnp.tile(k_norm[0], N_KV)])[None, :],
        conv_w[0],
        attn_out_norm[0][None, :],
        conv_out_norm[0][None, :],
        w_out[0].astype(BF16),
    )
    nffn = norm_ffn[0][None, :]

    (xs1,) = _mixer_call(x_sample, mod, None, _keys_minor(cache_k), _keys_minor(cache_v),
                         _rope_tables(dec_seq), consts, seq_len=dec_seq, emit_kv=False)

    per_block = BLOCK_ROWS // seq
    xp_blocks = x_prompt.reshape(n_prompt // per_block, BLOCK_ROWS, D_MODEL)
    xp1, k_new, v_new = _mixer_call(xp_blocks, mod, CTX_ROW, None, None, None, consts,
                                    seq_len=seq, emit_kv=True)

    yp, ys = _ffn_call(xp1.reshape(-1, D_MODEL), xs1.reshape(-1, D_MODEL), mod, CTX_ROW, dec_seq,
                       nffn, w_gate_up[0], w_down[0])

    return (yp.reshape(n_prompt, seq, D_MODEL),
            ys.reshape(n_sample, dec_seq, D_MODEL),
            jnp.transpose(k_new, (0, 1, 4, 2, 3)),
            jnp.transpose(v_new, (0, 1, 4, 2, 3)))
```

```python
import functools

import numpy as np
import jax
import jax.numpy as jnp
from jax import lax
from jax.experimental import pallas as pl
from jax.experimental.pallas import tpu as pltpu

D_MODEL = 1024
HEAD_DIM = 64
ATTN_DIM = 512
N_HEADS = 8
N_KV = 2
KV_GROUP = N_HEADS // N_KV
KV_DIM = N_KV * HEAD_DIM
CONV_DIM = 512
D_FF = 2816
QK_DIM = ATTN_DIM + KV_DIM
QKV_DIM = ATTN_DIM + 2 * KV_DIM
IN_DIM = QKV_DIM + 3 * CONV_DIM
GRID_W = 64
ROT_PAIRS = HEAD_DIM // 4
ROPE_THETA = 10000.0
RMS_EPS = 1e-6
Q_SCALE = HEAD_DIM ** -0.5 * 1.4426950408889634
V_ROWS = HEAD_DIM + 16
MAX_UNSHIFTED_SCORE = 64.0

LANES = 128
MXU_DIM = 256
BLOCK_ROWS = 1024
CHUNK_ROWS = 512
LATENT_Q_ROWS = 256
P_ROWS = 16
FFN_ROWS = 512
GU_CAST_ROWS = 128
DOWN_CAST_ROWS = 352
FF_CHUNKS = ((0, 1024), (1024, 1024), (2048, 768))
COND_ROWS = 16
MOD_ROWS = 8
CTX_ROW = 8
ADA_COLS = 2048
VMEM_TEMP_BYTES = 8 * 1024 * 1024
VMEM_MAX_BYTES = 56 * 1024 * 1024

F32 = jnp.float32
BF16 = jnp.bfloat16


def _vmem_limit(windows, scratch=()):
    total = VMEM_TEMP_BYTES
    for spec, dtype in windows:
        if spec.block_shape is None:
            continue
        n = int(np.prod([1 if d is None else d for d in spec.block_shape]))
        total += n * jnp.dtype(dtype).itemsize * (1 if spec.pipeline_mode is not None else 2)
    for s in scratch:
        if s.memory_space == pltpu.VMEM:
            total += int(np.prod(s.shape)) * jnp.dtype(s.dtype).itemsize
    return min(total, VMEM_MAX_BYTES)


def _mod_spec(row, half):
    blk = 0 if row is None else row // MOD_ROWS
    return pl.BlockSpec((MOD_ROWS, 3 * D_MODEL), lambda i: (blk, half))


def _mod_row(mod_ref, row):
    r = pl.program_id(0) if row is None else row % MOD_ROWS
    return mod_ref[pl.ds(r, 1), :]


def _const_spec(shape):
    nd = len(shape)
    return pl.BlockSpec(shape, lambda *_: (0,) * nd, pipeline_mode=pl.Buffered(1))


def _ada_kernel(cond_ref, w_ref, b_ref, out_ref):
    c = cond_ref[...]
    s = (c * jax.nn.sigmoid(c)).astype(BF16)
    out_ref[...] = jnp.dot(s, w_ref[...].astype(BF16), preferred_element_type=F32) + b_ref[...]


def _ada_call(cond, w_ada, b_ada):
    n = w_ada.shape[1]
    in_specs = [
        pl.BlockSpec((COND_ROWS, D_MODEL), lambda j: (0, 0)),
        pl.BlockSpec((D_MODEL, ADA_COLS), lambda j: (0, j)),
        pl.BlockSpec((1, ADA_COLS), lambda j: (0, j)),
    ]
    out_spec = pl.BlockSpec((COND_ROWS, ADA_COLS), lambda j: (0, j))
    return pl.pallas_call(
        _ada_kernel,
        grid=(n // ADA_COLS,),
        in_specs=in_specs,
        out_specs=out_spec,
        out_shape=jax.ShapeDtypeStruct((COND_ROWS, n), F32),
        compiler_params=pltpu.CompilerParams(
            dimension_semantics=("arbitrary",),
            vmem_limit_bytes=_vmem_limit([(s, F32) for s in in_specs + [out_spec]])),
        name="ada_rows",
    )(cond, w_ada, b_ada)


def _rms_rows(x):
    return x * lax.rsqrt(jnp.mean(x * x, axis=-1, keepdims=True) + RMS_EPS)


def _mixer_kernel(*refs, seq_len, ctx_len, use_rope, emit_kv, mod_row):
    it = iter(refs)
    x_ref, mod_ref = next(it), next(it)
    if ctx_len:
        ck_ref, cv_ref = next(it), next(it)
    if use_rope:
        cos_ref, sin_ref = next(it), next(it)
    (nmix_ref, win_ref, gsum_ref, qkg_ref, convw_ref, ga_ref, gc_ref, wout_ref) = (
        next(it) for _ in range(8))
    out_ref = next(it)
    if emit_kv:
        ko_ref, vo_ref = next(it), next(it)
    (q_s, k_s, vt_s, attn_s, t_s, gb_s, s_ref, p_ref) = (next(it) for _ in range(8))

    n_chunks = BLOCK_ROWS // CHUNK_ROWS
    if ctx_len:
        q_rows, n_keys, key_blk = LATENT_Q_ROWS, ctx_len + seq_len, BLOCK_ROWS
    else:
        q_rows, n_keys, key_blk = seq_len, seq_len, seq_len
    n_qb = BLOCK_ROWS // q_rows

    mod = _mod_row(mod_ref, mod_row)
    shift = mod[:, 0:D_MODEL]
    scale1 = 1.0 + mod[:, D_MODEL:2 * D_MODEL]
    gate = mod[:, 2 * D_MODEL:3 * D_MODEL]

    def put_values_t(blk, off, vt):
        vt = vt.astype(BF16)
        for g in range(N_KV):
            vt_s[blk, g, 0:HEAD_DIM, off:off + vt.shape[1]] = vt[g * HEAD_DIM:(g + 1) * HEAD_DIM, :]

    tail = (lax.broadcasted_iota(jnp.int32, (V_ROWS - HEAD_DIM, n_keys), 0) == 0).astype(BF16)
    for blk in range(vt_s.shape[0]):
        for g in range(N_KV):
            vt_s[blk, g, HEAD_DIM:V_ROWS, :] = tail

    if ctx_len:
        ck = ck_ref[...].reshape(KV_DIM, ctx_len).T.astype(BF16)
        for g in range(N_KV):
            k_s[g, 0:ctx_len, :] = ck[:, g * HEAD_DIM:(g + 1) * HEAD_DIM]
        put_values_t(0, 0, cv_ref[...].reshape(KV_DIM, ctx_len))

    if use_rope:
        lane = lax.broadcasted_iota(jnp.int32, (CHUNK_ROWS, LANES), 1)
        first_half = (lane % (2 * ROT_PAIRS)) < ROT_PAIRS

    t_s[0:8, :] = jnp.zeros((8, CONV_DIM), F32)
    t_s[8 + BLOCK_ROWS:16 + BLOCK_ROWS, :] = jnp.zeros((8, CONV_DIM), F32)
    w0 = convw_ref[0:1, :]
    w1 = convw_ref[1:2, :]
    w2 = convw_ref[2:3, :]

    def conv_rows(a, n):
        slab = t_s[a:a + n + 16, :]
        t_prev = pltpu.roll(slab, 1, axis=0)[8:8 + n, :]
        t_mid = slab[8:8 + n, :]
        t_next = pltpu.roll(slab, n + 15, axis=0)[8:8 + n, :]
        if seq_len < BLOCK_ROWS:
            pos = (lax.broadcasted_iota(jnp.int32, (n, 1), 0) + a) % seq_len
            t_prev = jnp.where(pos == 0, 0.0, t_prev)
            t_next = jnp.where(pos == seq_len - 1, 0.0, t_next)
        y = gb_s[a:a + n, :] * (w0 * t_prev + w1 * t_mid + w2 * t_next)
        return (_rms_rows(y) * gc_ref[...]).astype(BF16)

    def project_conv(r0, hb):
        cvp = jnp.dot(hb, win_ref[:, QKV_DIM:IN_DIM], preferred_element_type=F32)
        gb_s[r0:r0 + CHUNK_ROWS, :] = cvp[:, 0:CONV_DIM]
        t_s[8 + r0:8 + r0 + CHUNK_ROWS, :] = (cvp[:, CONV_DIM:2 * CONV_DIM]
                                              * cvp[:, 2 * CONV_DIM:3 * CONV_DIM])

    for c in range(n_chunks):
        r0 = c * CHUNK_ROWS
        x = x_ref[r0:r0 + CHUNK_ROWS, :]
        h = (_rms_rows(x) * nmix_ref[...]) * scale1 + shift
        hb = h.astype(BF16)
        qkv = jnp.dot(hb, win_ref[:, 0:QKV_DIM], preferred_element_type=F32)

        groups = []
        for g0 in range(0, QKV_DIM, MXU_DIM):
            sq = qkv[:, g0:g0 + MXU_DIM]
            groups.append(jnp.dot((sq * sq).astype(BF16), gsum_ref[...],
                                  preferred_element_type=F32))
        ms = jnp.concatenate(groups, axis=-1)[:, 0:QK_DIM]
        qk = (qkv[:, 0:QK_DIM] * lax.rsqrt(ms + RMS_EPS)) * qkg_ref[...]
        vv = qkv[:, QK_DIM:QKV_DIM]

        if emit_kv:
            for r1 in range(0, CHUNK_ROWS, seq_len):
                kt = qk[r1:r1 + seq_len, ATTN_DIM:QK_DIM].T
                ko_ref[(r0 + r1) // seq_len] = kt.reshape(N_KV, HEAD_DIM, seq_len)

        for cg in range(QK_DIM // LANES):
            xg = qk[:, cg * LANES:(cg + 1) * LANES]
            if use_rope:
                cs = cos_ref[r0:r0 + CHUNK_ROWS, :]
                sn = sin_ref[r0:r0 + CHUNK_ROWS, :]
                partner = jnp.where(first_half,
                                    pltpu.roll(xg, LANES - ROT_PAIRS, axis=1),
                                    pltpu.roll(xg, ROT_PAIRS, axis=1))
                xg = xg * cs + partner * sn
            if cg < ATTN_DIM // LANES:
                xb = (xg * Q_SCALE).astype(BF16)
                q_s[2 * cg, r0:r0 + CHUNK_ROWS, :] = xb[:, 0:HEAD_DIM]
                q_s[2 * cg + 1, r0:r0 + CHUNK_ROWS, :] = xb[:, HEAD_DIM:LANES]
            else:
                xb = xg.astype(BF16)
                for g in range(N_KV):
                    k_s[g, ctx_len + r0:ctx_len + r0 + CHUNK_ROWS, :] = (
                        xb[:, g * HEAD_DIM:(g + 1) * HEAD_DIM])
        w = min(key_blk, CHUNK_ROWS)
        for r1 in range(r0, r0 + CHUNK_ROWS, w):
            blk, off = (0, ctx_len + r1) if ctx_len else (r1 // key_blk, 0)
            vt = vv[r1 - r0:r1 - r0 + w, :].T
            put_values_t(blk, off, vt)
            if emit_kv:
                vo_ref[blk] = vt.reshape(N_KV, HEAD_DIM, seq_len)

        project_conv(r0, hb)

    def scores_t(qb, g):
        r0 = pl.multiple_of(qb * q_rows, q_rows)
        k0 = 0 if ctx_len else r0
        qs = jnp.concatenate(
            [q_s[KV_GROUP * g + j, pl.ds(r0, q_rows), :] for j in range(KV_GROUP)], axis=0)
        kk = k_s[g, pl.ds(k0, n_keys), :]
        return lax.dot_general(kk, qs, (((1,), (1,)), ((), ())),
                               preferred_element_type=F32)

    def values_out(qb, g, p):
        r0 = pl.multiple_of(qb * q_rows, q_rows)
        blk = 0 if ctx_len else qb
        ot = jnp.dot(vt_s[blk, g], p, preferred_element_type=F32)
        ot = ot[0:HEAD_DIM, :] / ot[HEAD_DIM:HEAD_DIM + 1, :]
        for jj in range(KV_GROUP // 2):
            pair_t = jnp.concatenate(
                [ot[:, (2 * jj) * q_rows:(2 * jj + 1) * q_rows],
                 ot[:, (2 * jj + 1) * q_rows:(2 * jj + 2) * q_rows]], axis=0)
            col = (KV_GROUP * g + 2 * jj) * HEAD_DIM
            attn_s[pl.ds(r0, q_rows), col:col + LANES] = pair_t.T

    def pipeline_pairs(step):
        assert n_qb % 2 == 0
        step(-1, 1, first=True)
        step(0, 0)

        def body(t, carry):
            step(2 * t + 1, 1)
            step(2 * t + 2, 0)
            return carry

        lax.fori_loop(0, (n_qb - 2) // 2, body, 0)
        step(n_qb - 1, 1, last=True)

    def shifted_block(qb, carry):
        for g in range(N_KV):
            s = scores_t(qb, g)
            s_ref[...] = s
            m = jnp.max(s, axis=0, keepdims=True)
            for k1 in range(0, n_keys, P_ROWS):
                p_ref[0, g, k1:k1 + P_ROWS, :] = jnp.exp2(
                    s_ref[k1:k1 + P_ROWS, :] - m).astype(BF16)
            values_out(qb, g, p_ref[0, g])
        return carry

    def unshifted_step(j, par, first=False, last=False):
        for g in range(N_KV):
            if not last:
                p_ref[1 - par, g] = jnp.exp2(scores_t(j + 1, g)).astype(BF16)
            if not first:
                values_out(j, g, p_ref[par, g])

    gains = jnp.abs(qkg_ref[...])
    q_bound = (Q_SCALE * Q_SCALE * HEAD_DIM) * jnp.max(gains[:, 0:ATTN_DIM]) ** 2
    k_bound = HEAD_DIM * jnp.max(gains[:, ATTN_DIM:QK_DIM]) ** 2
    if ctx_len:
        ck2 = ck_ref[...]
        k_bound = jnp.maximum(k_bound, jnp.max(jnp.sum(ck2 * ck2, axis=1)))
    small_scores = q_bound * k_bound <= MAX_UNSHIFTED_SCORE ** 2
    pl.when(small_scores)(lambda: pipeline_pairs(unshifted_step))

    @pl.when(jnp.logical_not(small_scores))
    def _():
        lax.fori_loop(0, n_qb, shifted_block, 0)

    for r0 in range(0, BLOCK_ROWS, CHUNK_ROWS):
        rows = slice(r0, r0 + CHUNK_ROWS)
        an = (_rms_rows(attn_s[rows, :]) * ga_ref[...]).astype(BF16)
        merged = jnp.concatenate([an, conv_rows(r0, CHUNK_ROWS)], axis=-1)
        mix = jnp.dot(merged, wout_ref[...], preferred_element_type=F32)
        out_ref[rows, :] = x_ref[rows, :] + gate * mix


def _mixer_call(x_blocks, mod, mod_row, ctx_k, ctx_v, rope, consts, *, seq_len, emit_kv):
    n_blocks = x_blocks.shape[0]
    ctx_len = 0 if ctx_k is None else ctx_k.shape[4]
    use_rope = rope is not None
    if ctx_len:
        assert seq_len == BLOCK_ROWS
        q_rows, n_keys, n_key_blocks = LATENT_Q_ROWS, ctx_len + BLOCK_ROWS, 1
    else:
        q_rows, n_keys, n_key_blocks = seq_len, seq_len, BLOCK_ROWS // seq_len
    n_q_cols = KV_GROUP * q_rows

    blk = lambda cols: pl.BlockSpec((None, BLOCK_ROWS, cols), lambda b: (b, 0, 0))
    assert mod_row is not None or n_blocks <= MOD_ROWS
    args = [x_blocks, mod]
    in_specs = [blk(D_MODEL), _mod_spec(mod_row, 0)]
    if ctx_len:
        args += [ctx_k, ctx_v]
        in_specs += [pl.BlockSpec((None, None, N_KV, HEAD_DIM, ctx_len),
                                  lambda b: (b, 0, 0, 0, 0))] * 2
    if use_rope:
        args += list(rope)
        in_specs += [_const_spec((BLOCK_ROWS, LANES))] * 2
    args += list(consts)
    in_specs += [_const_spec(a.shape) for a in consts]

    out_shape = [jax.ShapeDtypeStruct((n_blocks, BLOCK_ROWS, D_MODEL), F32)]
    out_specs = [blk(D_MODEL)]
    if emit_kv:
        per_block = BLOCK_ROWS // seq_len
        out_shape += [jax.ShapeDtypeStruct(
            (n_blocks * per_block, 1, N_KV, HEAD_DIM, seq_len), F32)] * 2
        out_specs += [pl.BlockSpec((per_block, None, N_KV, HEAD_DIM, seq_len),
                                   lambda b: (b, 0, 0, 0, 0))] * 2

    scratch = [
        pltpu.VMEM((N_HEADS, BLOCK_ROWS, HEAD_DIM), BF16),
        pltpu.VMEM((N_KV, ctx_len + BLOCK_ROWS, HEAD_DIM), BF16),
        pltpu.VMEM((n_key_blocks, N_KV, V_ROWS, n_keys), BF16),
        pltpu.VMEM((BLOCK_ROWS, ATTN_DIM), F32),
        pltpu.VMEM((BLOCK_ROWS + 16, CONV_DIM), F32),
        pltpu.VMEM((BLOCK_ROWS, CONV_DIM), F32),
        pltpu.VMEM((n_keys, n_q_cols), F32),
        pltpu.VMEM((2, N_KV, n_keys, n_q_cols), BF16),
    ]
    kern = functools.partial(_mixer_kernel, seq_len=seq_len, ctx_len=ctx_len,
                             use_rope=use_rope, emit_kv=emit_kv, mod_row=mod_row)
    return pl.pallas_call(
        kern,
        grid=(n_blocks,),
        in_specs=in_specs,
        out_specs=out_specs,
        out_shape=out_shape,
        scratch_shapes=scratch,
        compiler_params=pltpu.CompilerParams(
            dimension_semantics=("arbitrary",),
            vmem_limit_bytes=_vmem_limit(
                [(s, a.dtype) for s, a in zip(in_specs, args)] + [(s, F32) for s in out_specs],
                scratch)),
        name="mixer_ctx" if emit_kv else "mixer_latent",
    )(*args)


def _ffn_rows(x_ref, mod, nffn_ref, wgu_ref, wd_ref, out_ref):
    shift = mod[:, 0:D_MODEL]
    scale1 = 1.0 + mod[:, D_MODEL:2 * D_MODEL]
    gate = mod[:, 2 * D_MODEL:3 * D_MODEL]
    x = x_ref[...]
    hb = ((_rms_rows(x) * nffn_ref[...]) * scale1 + shift).astype(BF16)
    acc = None
    for c0, cw in FF_CHUNKS:
        gt = jnp.dot(hb, wgu_ref[:, c0:c0 + cw], preferred_element_type=F32)
        up = jnp.dot(hb, wgu_ref[:, D_FF + c0:D_FF + c0 + cw], preferred_element_type=F32)
        act = ((gt * jax.nn.sigmoid(gt)) * up).astype(BF16)
        part = jnp.dot(act, wd_ref[c0:c0 + cw, :], preferred_element_type=F32)
        acc = part if acc is None else acc + part
    out_ref[...] = x + gate * acc


def _cast_rows(src_hbm, dst, stage, sem, chunk_rows):
    n = src_hbm.shape[0] // chunk_rows

    def fetch(c):
        return pltpu.make_async_copy(src_hbm.at[pl.ds(c * chunk_rows, chunk_rows), :],
                                     stage.at[c % 2], sem.at[c % 2])

    fetch(0).start()
    for c in range(n):
        if c + 1 < n:
            fetch(c + 1).start()
        fetch(c).wait()
        dst[c * chunk_rows:(c + 1) * chunk_rows, :] = stage[c % 2].astype(BF16)


def _ffn_kernel(xa_ref, xb_ref, mod_ref, nffn_ref, wgu_hbm, wd_hbm, ya_ref, yb_ref,
                wgu_s, wd_s, stage_gu, stage_d, sem, *, n_a, mod_row_a, tiles_per_mod_row_b):
    i = pl.program_id(0)

    @pl.when(i == 0)
    def _():
        _cast_rows(wgu_hbm, wgu_s, stage_gu, sem, GU_CAST_ROWS)
        _cast_rows(wd_hbm, wd_s, stage_d, sem, DOWN_CAST_ROWS)

    @pl.when(i < n_a)
    def _():
        mod = mod_ref[mod_row_a % MOD_ROWS:mod_row_a % MOD_ROWS + 1, :]
        _ffn_rows(xa_ref, mod, nffn_ref, wgu_s, wd_s, ya_ref)

    @pl.when(i >= n_a)
    def _():
        mod = mod_ref[pl.ds((i - n_a) // tiles_per_mod_row_b, 1), :]
        _ffn_rows(xb_ref, mod, nffn_ref, wgu_s, wd_s, yb_ref)


def _ffn_call(xa_rows, xb_rows, mod, mod_row_a, rows_per_mod_row_b, norm_ffn, w_gate_up, w_down):
    n_a, n_b = xa_rows.shape[0] // FFN_ROWS, xb_rows.shape[0] // FFN_ROWS
    tiles_per_row_b = rows_per_mod_row_b // FFN_ROWS
    assert n_b // tiles_per_row_b <= MOD_ROWS
    spec_a = pl.BlockSpec((FFN_ROWS, D_MODEL), lambda i: (jnp.minimum(i, n_a - 1), 0))
    spec_b = pl.BlockSpec((FFN_ROWS, D_MODEL), lambda i: (jnp.maximum(i - n_a, 0), 0))
    mod_spec = pl.BlockSpec((MOD_ROWS, 3 * D_MODEL),
                            lambda i: (jnp.where(i < n_a, mod_row_a // MOD_ROWS, 0), 1))
    hbm = pl.BlockSpec(memory_space=pl.ANY)
    in_specs = [spec_a, spec_b, mod_spec, _const_spec(norm_ffn.shape), hbm, hbm]
    scratch = [pltpu.VMEM(w_gate_up.shape, BF16), pltpu.VMEM(w_down.shape, BF16),
               pltpu.VMEM((2, GU_CAST_ROWS, w_gate_up.shape[1]), F32),
               pltpu.VMEM((2, DOWN_CAST_ROWS, w_down.shape[1]), F32),
               pltpu.SemaphoreType.DMA((2,))]
    return pl.pallas_call(
        functools.partial(_ffn_kernel, n_a=n_a, mod_row_a=mod_row_a,
                          tiles_per_mod_row_b=tiles_per_row_b),
        grid=(n_a + n_b,),
        in_specs=in_specs,
        out_specs=[spec_a, spec_b],
        out_shape=[jax.ShapeDtypeStruct(xa_rows.shape, F32),
                   jax.ShapeDtypeStruct(xb_rows.shape, F32)],
        scratch_shapes=scratch,
        compiler_params=pltpu.CompilerParams(
            dimension_semantics=("arbitrary",),
            vmem_limit_bytes=_vmem_limit(
                [(s, F32) for s in in_specs + [spec_a, spec_b]], scratch)),
        name="ffn",
    )(xa_rows, xb_rows, mod, norm_ffn, w_gate_up, w_down)


def _rope_tables(n_tokens):
    rows = n_tokens // GRID_W
    row = jnp.repeat(jnp.arange(rows, dtype=F32), GRID_W)
    col = jnp.tile(jnp.arange(GRID_W, dtype=F32), rows)
    inv = 1.0 / (ROPE_THETA ** (jnp.arange(ROT_PAIRS, dtype=F32) / ROT_PAIRS))
    ang = jnp.stack([row[:, None] * inv, col[:, None] * inv], axis=1)
    cos, sin = jnp.cos(ang), jnp.sin(ang)
    cos_h = jnp.concatenate([cos, cos], axis=-1).reshape(n_tokens, HEAD_DIM)
    sin_h = jnp.concatenate([-sin, sin], axis=-1).reshape(n_tokens, HEAD_DIM)
    reps = LANES // HEAD_DIM
    return jnp.tile(cos_h, (1, reps)), jnp.tile(sin_h, (1, reps))


def _keys_minor(kv):
    return jnp.transpose(kv, (0, 1, 3, 4, 2))


def _group_mean_matrix():
    idx = np.arange(MXU_DIM) // HEAD_DIM
    g = (idx[:, None] == idx[None, :]).astype(np.float32) / HEAD_DIM
    return jnp.asarray(g, dtype=BF16)


def kernel(x_prompt, x_sample, c, cache_k, cache_v, c_ctx, norm_mix, norm_ffn, w_ada, b_ada,
           w_in, q_norm, k_norm, conv_w, attn_out_norm, conv_out_norm, w_out, w_gate_up, w_down):
    depth = w_in.shape[0]
    assert depth == 1
    n_prompt, seq, _ = x_prompt.shape
    n_sample, dec_seq, _ = x_sample.shape
    assert dec_seq == BLOCK_ROWS and BLOCK_ROWS % seq == 0 and n_sample <= CTX_ROW

    cond = jnp.zeros((COND_ROWS, D_MODEL), F32)
    cond = cond.at[0:n_sample].set(c).at[CTX_ROW].set(c_ctx)
    mod = _ada_call(cond, w_ada[0], b_ada[0][None, :])

    consts = (
        norm_mix[0][None, :],
        w_in[0].astype(BF16),
        _group_mean_matrix(),
        jnp.concatenate([jnp.tile(q_norm[0], N_HEADS), jnp.tile(k_norm[0], N_KV)])[None, :],
        conv_w[0],
        attn_out_norm[0][None, :],
        conv_out_norm[0][None, :],
        w_out[0].astype(BF16),
    )
    nffn = norm_ffn[0][None, :]

    (xs1,) = _mixer_call(x_sample, mod, None, _keys_minor(cache_k), _keys_minor(cache_v),
                         _rope_tables(dec_seq), consts, seq_len=dec_seq, emit_kv=False)

    per_block = BLOCK_ROWS // seq
    xp_blocks = x_prompt.reshape(n_prompt // per_block, BLOCK_ROWS, D_MODEL)
    xp1, k_new, v_new = _mixer_call(xp_blocks, mod, CTX_ROW, None, None, None, consts,
                                    seq_len=seq, emit_kv=True)

    yp, ys = _ffn_call(xp1.reshape(-1, D_MODEL), xs1.reshape(-1, D_MODEL), mod, CTX_ROW, dec_seq,
                       nffn, w_gate_up[0], w_down[0])

    return (yp.reshape(n_prompt, seq, D_MODEL),
            ys.reshape(n_sample, dec_seq, D_MODEL),
            jnp.transpose(k_new, (0, 1, 4, 2, 3)),
            jnp.transpose(v_new, (0, 1, 4, 2, 3)))
```

```python
import functools

import numpy as np
import jax
import jax.numpy as jnp
from jax import lax
from jax.experimental import pallas as pl
from jax.experimental.pallas import tpu as pltpu

D_MODEL = 1024
HEAD_DIM = 64
ATTN_DIM = 512
N_HEADS = 8
N_KV = 2
KV_GROUP = N_HEADS // N_KV
KV_DIM = N_KV * HEAD_DIM
CONV_DIM = 512
D_FF = 2816
QK_DIM = ATTN_DIM + KV_DIM
QKV_DIM = ATTN_DIM + 2 * KV_DIM
IN_DIM = QKV_DIM + 3 * CONV_DIM
GRID_W = 64
ROT_PAIRS = HEAD_DIM // 4
ROPE_THETA = 10000.0
RMS_EPS = 1e-6
Q_SCALE = HEAD_DIM ** -0.5 * 1.4426950408889634
MAX_UNSHIFTED_SCORE = 64.0

LANES = 128
MXU_DIM = 256
BLOCK_ROWS = 1024
CHUNK_ROWS = 512
LATENT_Q_ROWS = 256
P_ROWS = 16
FFN_ROWS = 512
GU_CAST_ROWS = 128
DOWN_CAST_ROWS = 352
FF_CHUNKS = ((0, 1024), (1024, 1024), (2048, 768))
COND_ROWS = 16
MOD_ROWS = 8
CTX_ROW = 8
ADA_COLS = 2048
VMEM_TEMP_BYTES = 8 * 1024 * 1024
VMEM_MAX_BYTES = 56 * 1024 * 1024

F32 = jnp.float32
BF16 = jnp.bfloat16


def _vmem_limit(windows, scratch=()):
    total = VMEM_TEMP_BYTES
    for spec, dtype in windows:
        if spec.block_shape is None:
            continue
        n = int(np.prod([1 if d is None else d for d in spec.block_shape]))
        total += n * jnp.dtype(dtype).itemsize * (1 if spec.pipeline_mode is not None else 2)
    for s in scratch:
        if s.memory_space == pltpu.VMEM:
            total += int(np.prod(s.shape)) * jnp.dtype(s.dtype).itemsize
    return min(total, VMEM_MAX_BYTES)


def _mod_spec(row, half):
    blk = 0 if row is None else row // MOD_ROWS
    return pl.BlockSpec((MOD_ROWS, 3 * D_MODEL), lambda i: (blk, half))


def _mod_row(mod_ref, row):
    r = pl.program_id(0) if row is None else row % MOD_ROWS
    return mod_ref[pl.ds(r, 1), :]


def _const_spec(shape):
    nd = len(shape)
    return pl.BlockSpec(shape, lambda *_: (0,) * nd, pipeline_mode=pl.Buffered(1))


def _ada_kernel(cond_ref, w_ref, b_ref, out_ref):
    c = cond_ref[...]
    s = (c * jax.nn.sigmoid(c)).astype(BF16)
    out_ref[...] = jnp.dot(s, w_ref[...].astype(BF16), preferred_element_type=F32) + b_ref[...]


def _ada_call(cond, w_ada, b_ada):
    n = w_ada.shape[1]
    in_specs = [
        pl.BlockSpec((COND_ROWS, D_MODEL), lambda j: (0, 0)),
        pl.BlockSpec((D_MODEL, ADA_COLS), lambda j: (0, j)),
        pl.BlockSpec((1, ADA_COLS), lambda j: (0, j)),
    ]
    out_spec = pl.BlockSpec((COND_ROWS, ADA_COLS), lambda j: (0, j))
    return pl.pallas_call(
        _ada_kernel,
        grid=(n // ADA_COLS,),
        in_specs=in_specs,
        out_specs=out_spec,
        out_shape=jax.ShapeDtypeStruct((COND_ROWS, n), F32),
        compiler_params=pltpu.CompilerParams(
            dimension_semantics=("arbitrary",),
            vmem_limit_bytes=_vmem_limit([(s, F32) for s in in_specs + [out_spec]])),
        name="ada_rows",
    )(cond, w_ada, b_ada)


def _rms_rows(x):
    return x * lax.rsqrt(jnp.mean(x * x, axis=-1, keepdims=True) + RMS_EPS)


def _mixer_kernel(*refs, seq_len, ctx_len, use_rope, emit_kv, mod_row):
    it = iter(refs)
    x_ref, mod_ref = next(it), next(it)
    if ctx_len:
        ck_ref, cv_ref = next(it), next(it)
    if use_rope:
        cos_ref, sin_ref = next(it), next(it)
    (nmix_ref, win_ref, gsum_ref, qkg_ref, convw_ref, ga_ref, gc_ref, wout_ref) = (
        next(it) for _ in range(8))
    out_ref = next(it)
    if emit_kv:
        ko_ref, vo_ref = next(it), next(it)
    (q_s, k_s, vt_s, attn_s, t_s, gb_s, s_ref, p_ref, l_ref) = (next(it) for _ in range(9))

    n_chunks = BLOCK_ROWS // CHUNK_ROWS
    if ctx_len:
        q_rows, n_keys, key_blk = LATENT_Q_ROWS, ctx_len + seq_len, BLOCK_ROWS
    else:
        q_rows, n_keys, key_blk = seq_len, seq_len, seq_len
    n_qb = BLOCK_ROWS // q_rows

    mod = _mod_row(mod_ref, mod_row)
    shift = mod[:, 0:D_MODEL]
    scale1 = 1.0 + mod[:, D_MODEL:2 * D_MODEL]
    gate = mod[:, 2 * D_MODEL:3 * D_MODEL]

    def put_values_t(blk, off, vt):
        vt = vt.astype(BF16)
        for g in range(N_KV):
            vt_s[blk, g, :, off:off + vt.shape[1]] = vt[g * HEAD_DIM:(g + 1) * HEAD_DIM, :]

    if ctx_len:
        ck = ck_ref[...].reshape(KV_DIM, ctx_len).T.astype(BF16)
        for g in range(N_KV):
            k_s[g, 0:ctx_len, :] = ck[:, g * HEAD_DIM:(g + 1) * HEAD_DIM]
        put_values_t(0, 0, cv_ref[...].reshape(KV_DIM, ctx_len))

    if use_rope:
        lane = lax.broadcasted_iota(jnp.int32, (CHUNK_ROWS, LANES), 1)
        first_half = (lane % (2 * ROT_PAIRS)) < ROT_PAIRS

    t_s[0:8, :] = jnp.zeros((8, CONV_DIM), F32)
    t_s[8 + BLOCK_ROWS:16 + BLOCK_ROWS, :] = jnp.zeros((8, CONV_DIM), F32)
    w0 = convw_ref[0:1, :]
    w1 = convw_ref[1:2, :]
    w2 = convw_ref[2:3, :]

    def conv_rows(a, n):
        slab = t_s[a:a + n + 16, :]
        t_prev = pltpu.roll(slab, 1, axis=0)[8:8 + n, :]
        t_mid = slab[8:8 + n, :]
        t_next = pltpu.roll(slab, n + 15, axis=0)[8:8 + n, :]
        if seq_len < BLOCK_ROWS:
            pos = (lax.broadcasted_iota(jnp.int32, (n, 1), 0) + a) % seq_len
            t_prev = jnp.where(pos == 0, 0.0, t_prev)
            t_next = jnp.where(pos == seq_len - 1, 0.0, t_next)
        y = gb_s[a:a + n, :] * (w0 * t_prev + w1 * t_mid + w2 * t_next)
        return (_rms_rows(y) * gc_ref[...]).astype(BF16)

    def project_conv(r0, hb):
        cvp = jnp.dot(hb, win_ref[:, QKV_DIM:IN_DIM], preferred_element_type=F32)
        gb_s[r0:r0 + CHUNK_ROWS, :] = cvp[:, 0:CONV_DIM]
        t_s[8 + r0:8 + r0 + CHUNK_ROWS, :] = (cvp[:, CONV_DIM:2 * CONV_DIM]
                                              * cvp[:, 2 * CONV_DIM:3 * CONV_DIM])

    for c in range(n_chunks):
        r0 = c * CHUNK_ROWS
        x = x_ref[r0:r0 + CHUNK_ROWS, :]
        h = (_rms_rows(x) * nmix_ref[...]) * scale1 + shift
        hb = h.astype(BF16)
        qkv = jnp.dot(hb, win_ref[:, 0:QKV_DIM], preferred_element_type=F32)

        groups = []
        for g0 in range(0, QKV_DIM, MXU_DIM):
            sq = qkv[:, g0:g0 + MXU_DIM]
            groups.append(jnp.dot((sq * sq).astype(BF16), gsum_ref[...],
                                  preferred_element_type=F32))
        ms = jnp.concatenate(groups, axis=-1)[:, 0:QK_DIM]
        qk = (qkv[:, 0:QK_DIM] * lax.rsqrt(ms + RMS_EPS)) * qkg_ref[...]
        vv = qkv[:, QK_DIM:QKV_DIM]

        if emit_kv:
            for r1 in range(0, CHUNK_ROWS, seq_len):
                kt = qk[r1:r1 + seq_len, ATTN_DIM:QK_DIM].T
                ko_ref[(r0 + r1) // seq_len] = kt.reshape(N_KV, HEAD_DIM, seq_len)

        for cg in range(QK_DIM // LANES):
            xg = qk[:, cg * LANES:(cg + 1) * LANES]
            if use_rope:
                cs = cos_ref[r0:r0 + CHUNK_ROWS, :]
                sn = sin_ref[r0:r0 + CHUNK_ROWS, :]
                partner = jnp.where(first_half,
                                    pltpu.roll(xg, LANES - ROT_PAIRS, axis=1),
                                    pltpu.roll(xg, ROT_PAIRS, axis=1))
                xg = xg * cs + partner * sn
            if cg < ATTN_DIM // LANES:
                xb = (xg * Q_SCALE).astype(BF16)
                q_s[2 * cg, r0:r0 + CHUNK_ROWS, :] = xb[:, 0:HEAD_DIM]
                q_s[2 * cg + 1, r0:r0 + CHUNK_ROWS, :] = xb[:, HEAD_DIM:LANES]
            else:
                xb = xg.astype(BF16)
                for g in range(N_KV):
                    k_s[g, ctx_len + r0:ctx_len + r0 + CHUNK_ROWS, :] = (
                        xb[:, g * HEAD_DIM:(g + 1) * HEAD_DIM])
        w = min(key_blk, CHUNK_ROWS)
        for r1 in range(r0, r0 + CHUNK_ROWS, w):
            blk, off = (0, ctx_len + r1) if ctx_len else (r1 // key_blk, 0)
            vt = vv[r1 - r0:r1 - r0 + w, :].T
            put_values_t(blk, off, vt)
            if emit_kv:
                vo_ref[blk] = vt.reshape(N_KV, HEAD_DIM, seq_len)

        project_conv(r0, hb)

    def scores_t(qb, g):
        r0 = pl.multiple_of(qb * q_rows, q_rows)
        k0 = 0 if ctx_len else r0
        qs = jnp.concatenate(
            [q_s[KV_GROUP * g + j, pl.ds(r0, q_rows), :] for j in range(KV_GROUP)], axis=0)
        kk = k_s[g, pl.ds(k0, n_keys), :]
        return lax.dot_general(kk, qs, (((1,), (1,)), ((), ())),
                               preferred_element_type=F32)

    def values_out(qb, g, p, l):
        r0 = pl.multiple_of(qb * q_rows, q_rows)
        blk = 0 if ctx_len else qb
        ot = jnp.dot(vt_s[blk, g], p, preferred_element_type=F32) / l
        for jj in range(KV_GROUP // 2):
            pair_t = jnp.concatenate(
                [ot[:, (2 * jj) * q_rows:(2 * jj + 1) * q_rows],
                 ot[:, (2 * jj + 1) * q_rows:(2 * jj + 2) * q_rows]], axis=0)
            col = (KV_GROUP * g + 2 * jj) * HEAD_DIM
            attn_s[pl.ds(r0, q_rows), col:col + LANES] = pair_t.T

    def pipeline_pairs(step):
        assert n_qb % 2 == 0
        step(-1, 1, first=True)
        step(0, 0)

        def body(t, carry):
            step(2 * t + 1, 1)
            step(2 * t + 2, 0)
            return carry

        lax.fori_loop(0, (n_qb - 2) // 2, body, 0)
        step(n_qb - 1, 1, last=True)

    def shifted_block(qb, carry):
        for g in range(N_KV):
            s = scores_t(qb, g)
            s_ref[...] = s
            m = jnp.max(s, axis=0, keepdims=True)
            l8 = jnp.zeros((8, s.shape[1]), F32)
            for k1 in range(0, n_keys, P_ROWS):
                e = jnp.exp2(s_ref[k1:k1 + P_ROWS, :] - m)
                p_ref[0, g, k1:k1 + P_ROWS, :] = e.astype(BF16)
                l8 = l8 + (e[0:8, :] + e[8:16, :])
            values_out(qb, g, p_ref[0, g], jnp.sum(l8, axis=0, keepdims=True))
        return carry

    def unshifted_step(j, par, first=False, last=False):
        for g in range(N_KV):
            if not last:
                e = jnp.exp2(scores_t(j + 1, g))
                p_ref[1 - par, g] = e.astype(BF16)
                l_ref[1 - par, g] = jnp.sum(e, axis=0, keepdims=True)
            if not first:
                values_out(j, g, p_ref[par, g], l_ref[par, g])

    gains = jnp.abs(qkg_ref[...])
    q_bound = (Q_SCALE * Q_SCALE * HEAD_DIM) * jnp.max(gains[:, 0:ATTN_DIM]) ** 2
    k_bound = HEAD_DIM * jnp.max(gains[:, ATTN_DIM:QK_DIM]) ** 2
    if ctx_len:
        ck2 = ck_ref[...]
        k_bound = jnp.maximum(k_bound, jnp.max(jnp.sum(ck2 * ck2, axis=1)))
    small_scores = q_bound * k_bound <= MAX_UNSHIFTED_SCORE ** 2
    pl.when(small_scores)(lambda: pipeline_pairs(unshifted_step))

    @pl.when(jnp.logical_not(small_scores))
    def _():
        lax.fori_loop(0, n_qb, shifted_block, 0)

    for r0 in range(0, BLOCK_ROWS, CHUNK_ROWS):
        rows = slice(r0, r0 + CHUNK_ROWS)
        an = (_rms_rows(attn_s[rows, :]) * ga_ref[...]).astype(BF16)
        merged = jnp.concatenate([an, conv_rows(r0, CHUNK_ROWS)], axis=-1)
        mix = jnp.dot(merged, wout_ref[...], preferred_element_type=F32)
        out_ref[rows, :] = x_ref[rows, :] + gate * mix


def _mixer_call(x_blocks, mod, mod_row, ctx_k, ctx_v, rope, consts, *, seq_len, emit_kv):
    n_blocks = x_blocks.shape[0]
    ctx_len = 0 if ctx_k is None else ctx_k.shape[4]
    use_rope = rope is not None
    if ctx_len:
        assert seq_len == BLOCK_ROWS
        q_rows, n_keys, n_key_blocks = LATENT_Q_ROWS, ctx_len + BLOCK_ROWS, 1
    else:
        q_rows, n_keys, n_key_blocks = seq_len, seq_len, BLOCK_ROWS // seq_len
    n_q_cols = KV_GROUP * q_rows

    blk = lambda cols: pl.BlockSpec((None, BLOCK_ROWS, cols), lambda b: (b, 0, 0))
    assert mod_row is not None or n_blocks <= MOD_ROWS
    args = [x_blocks, mod]
    in_specs = [blk(D_MODEL), _mod_spec(mod_row, 0)]
    if ctx_len:
        args += [ctx_k, ctx_v]
        in_specs += [pl.BlockSpec((None, None, N_KV, HEAD_DIM, ctx_len),
                                  lambda b: (b, 0, 0, 0, 0))] * 2
    if use_rope:
        args += list(rope)
        in_specs += [_const_spec((BLOCK_ROWS, LANES))] * 2
    args += list(consts)
    in_specs += [_const_spec(a.shape) for a in consts]

    out_shape = [jax.ShapeDtypeStruct((n_blocks, BLOCK_ROWS, D_MODEL), F32)]
    out_specs = [blk(D_MODEL)]
    if emit_kv:
        per_block = BLOCK_ROWS // seq_len
        out_shape += [jax.ShapeDtypeStruct(
            (n_blocks * per_block, 1, N_KV, HEAD_DIM, seq_len), F32)] * 2
        out_specs += [pl.BlockSpec((per_block, None, N_KV, HEAD_DIM, seq_len),
                                   lambda b: (b, 0, 0, 0, 0))] * 2

    scratch = [
        pltpu.VMEM((N_HEADS, BLOCK_ROWS, HEAD_DIM), BF16),
        pltpu.VMEM((N_KV, ctx_len + BLOCK_ROWS, HEAD_DIM), BF16),
        pltpu.VMEM((n_key_blocks, N_KV, HEAD_DIM, n_keys), BF16),
        pltpu.VMEM((BLOCK_ROWS, ATTN_DIM), F32),
        pltpu.VMEM((BLOCK_ROWS + 16, CONV_DIM), F32),
        pltpu.VMEM((BLOCK_ROWS, CONV_DIM), F32),
        pltpu.VMEM((n_keys, n_q_cols), F32),
        pltpu.VMEM((2, N_KV, n_keys, n_q_cols), BF16),
        pltpu.VMEM((2, N_KV, 1, n_q_cols), F32),
    ]
    kern = functools.partial(_mixer_kernel, seq_len=seq_len, ctx_len=ctx_len,
                             use_rope=use_rope, emit_kv=emit_kv, mod_row=mod_row)
    return pl.pallas_call(
        kern,
        grid=(n_blocks,),
        in_specs=in_specs,
        out_specs=out_specs,
        out_shape=out_shape,
        scratch_shapes=scratch,
        compiler_params=pltpu.CompilerParams(
            dimension_semantics=("arbitrary",),
            vmem_limit_bytes=_vmem_limit(
                [(s, a.dtype) for s, a in zip(in_specs, args)] + [(s, F32) for s in out_specs],
                scratch)),
        name="mixer_ctx" if emit_kv else "mixer_latent",
    )(*args)


def _ffn_rows(x_ref, mod, nffn_ref, wgu_ref, wd_ref, out_ref):
    shift = mod[:, 0:D_MODEL]
    scale1 = 1.0 + mod[:, D_MODEL:2 * D_MODEL]
    gate = mod[:, 2 * D_MODEL:3 * D_MODEL]
    x = x_ref[...]
    hb = ((_rms_rows(x) * nffn_ref[...]) * scale1 + shift).astype(BF16)
    acc = None
    for c0, cw in FF_CHUNKS:
        gt = jnp.dot(hb, wgu_ref[:, c0:c0 + cw], preferred_element_type=F32)
        up = jnp.dot(hb, wgu_ref[:, D_FF + c0:D_FF + c0 + cw], preferred_element_type=F32)
        act = ((gt * jax.nn.sigmoid(gt)) * up).astype(BF16)
        part = jnp.dot(act, wd_ref[c0:c0 + cw, :], preferred_element_type=F32)
        acc = part if acc is None else acc + part
    out_ref[...] = x + gate * acc


def _cast_rows(src_hbm, dst, stage, sem, chunk_rows):
    n = src_hbm.shape[0] // chunk_rows

    def fetch(c):
        return pltpu.make_async_copy(src_hbm.at[pl.ds(c * chunk_rows, chunk_rows), :],
                                     stage.at[c % 2], sem.at[c % 2])

    fetch(0).start()
    for c in range(n):
        if c + 1 < n:
            fetch(c + 1).start()
        fetch(c).wait()
        dst[c * chunk_rows:(c + 1) * chunk_rows, :] = stage[c % 2].astype(BF16)


def _ffn_kernel(xa_ref, xb_ref, mod_ref, nffn_ref, wgu_hbm, wd_hbm, ya_ref, yb_ref,
                wgu_s, wd_s, stage_gu, stage_d, sem, *, n_a, mod_row_a, tiles_per_mod_row_b):
    i = pl.program_id(0)

    @pl.when(i == 0)
    def _():
        _cast_rows(wgu_hbm, wgu_s, stage_gu, sem, GU_CAST_ROWS)
        _cast_rows(wd_hbm, wd_s, stage_d, sem, DOWN_CAST_ROWS)

    @pl.when(i < n_a)
    def _():
        mod = mod_ref[mod_row_a % MOD_ROWS:mod_row_a % MOD_ROWS + 1, :]
        _ffn_rows(xa_ref, mod, nffn_ref, wgu_s, wd_s, ya_ref)

    @pl.when(i >= n_a)
    def _():
        mod = mod_ref[pl.ds((i - n_a) // tiles_per_mod_row_b, 1), :]
        _ffn_rows(xb_ref, mod, nffn_ref, wgu_s, wd_s, yb_ref)


def _ffn_call(xa_rows, xb_rows, mod, mod_row_a, rows_per_mod_row_b, norm_ffn, w_gate_up, w_down):
    n_a, n_b = xa_rows.shape[0] // FFN_ROWS, xb_rows.shape[0] // FFN_ROWS
    tiles_per_row_b = rows_per_mod_row_b // FFN_ROWS
    assert n_b // tiles_per_row_b <= MOD_ROWS
    spec_a = pl.BlockSpec((FFN_ROWS, D_MODEL), lambda i: (jnp.minimum(i, n_a - 1), 0))
    spec_b = pl.BlockSpec((FFN_ROWS, D_MODEL), lambda i: (jnp.maximum(i - n_a, 0), 0))
    mod_spec = pl.BlockSpec((MOD_ROWS, 3 * D_MODEL),
                            lambda i: (jnp.where(i < n_a, mod_row_a // MOD_ROWS, 0), 1))
    hbm = pl.BlockSpec(memory_space=pl.ANY)
    in_specs = [spec_a, spec_b, mod_spec, _const_spec(norm_ffn.shape), hbm, hbm]
    scratch = [pltpu.VMEM(w_gate_up.shape, BF16), pltpu.VMEM(w_down.shape, BF16),
               pltpu.VMEM((2, GU_CAST_ROWS, w_gate_up.shape[1]), F32),
               pltpu.VMEM((2, DOWN_CAST_ROWS, w_down.shape[1]), F32),
               pltpu.SemaphoreType.DMA((2,))]
    return pl.pallas_call(
        functools.partial(_ffn_kernel, n_a=n_a, mod_row_a=mod_row_a,
                          tiles_per_mod_row_b=tiles_per_row_b),
        grid=(n_a + n_b,),
        in_specs=in_specs,
        out_specs=[spec_a, spec_b],
        out_shape=[jax.ShapeDtypeStruct(xa_rows.shape, F32),
                   jax.ShapeDtypeStruct(xb_rows.shape, F32)],
        scratch_shapes=scratch,
        compiler_params=pltpu.CompilerParams(
            dimension_semantics=("arbitrary",),
            vmem_limit_bytes=_vmem_limit(
                [(s, F32) for s in in_specs + [spec_a, spec_b]], scratch)),
        name="ffn",
    )(xa_rows, xb_rows, mod, norm_ffn, w_gate_up, w_down)


def _rope_tables(n_tokens):
    rows = n_tokens // GRID_W
    row = jnp.repeat(jnp.arange(rows, dtype=F32), GRID_W)
    col = jnp.tile(jnp.arange(GRID_W, dtype=F32), rows)
    inv = 1.0 / (ROPE_THETA ** (jnp.arange(ROT_PAIRS, dtype=F32) / ROT_PAIRS))
    ang = jnp.stack([row[:, None] * inv, col[:, None] * inv], axis=1)
    cos, sin = jnp.cos(ang), jnp.sin(ang)
    cos_h = jnp.concatenate([cos, cos], axis=-1).reshape(n_tokens, HEAD_DIM)
    sin_h = jnp.concatenate([-sin, sin], axis=-1).reshape(n_tokens, HEAD_DIM)
    reps = LANES // HEAD_DIM
    return jnp.tile(cos_h, (1, reps)), jnp.tile(sin_h, (1, reps))


def _keys_minor(kv):
    return jnp.transpose(kv, (0, 1, 3, 4, 2))


def _group_mean_matrix():
    idx = np.arange(MXU_DIM) // HEAD_DIM
    g = (idx[:, None] == idx[None, :]).astype(np.float32) / HEAD_DIM
    return jnp.asarray(g, dtype=BF16)


def kernel(x_prompt, x_sample, c, cache_k, cache_v, c_ctx, norm_mix, norm_ffn, w_ada, b_ada,
           w_in, q_norm, k_norm, conv_w, attn_out_norm, conv_out_norm, w_out, w_gate_up, w_down):
    depth = w_in.shape[0]
    assert depth == 1
    n_prompt, seq, _ = x_prompt.shape
    n_sample, dec_seq, _ = x_sample.shape
    assert dec_seq == BLOCK_ROWS and BLOCK_ROWS % seq == 0 and n_sample <= CTX_ROW

    cond = jnp.zeros((COND_ROWS, D_MODEL), F32)
    cond = cond.at[0:n_sample].set(c).at[CTX_ROW].set(c_ctx)
    mod = _ada_call(cond, w_ada[0], b_ada[0][None, :])

    consts = (
        norm_mix[0][None, :],
        w_in[0].astype(BF16),
        _group_mean_matrix(),
        jnp.concatenate([jnp.tile(q_norm[0], N_HEADS), jnp.tile(k_norm[0], N_KV)])[None, :],
        conv_w[0],
        attn_out_norm[0][None, :],
        conv_out_norm[0][None, :],
        w_out[0].astype(BF16),
    )
    nffn = norm_ffn[0][None, :]

    (xs1,) = _mixer_call(x_sample, mod, None, _keys_minor(cache_k), _keys_minor(cache_v),
                         _rope_tables(dec_seq), consts, seq_len=dec_seq, emit_kv=False)

    per_block = BLOCK_ROWS // seq
    xp_blocks = x_prompt.reshape(n_prompt // per_block, BLOCK_ROWS, D_MODEL)
    xp1, k_new, v_new = _mixer_call(xp_blocks, mod, CTX_ROW, None, None, None, consts,
                                    seq_len=seq, emit_kv=True)

    yp, ys = _ffn_call(xp1.reshape(-1, D_MODEL), xs1.reshape(-1, D_MODEL), mod, CTX_ROW, dec_seq,
                       nffn, w_gate_up[0], w_down[0])

    return (yp.reshape(n_prompt, seq, D_MODEL),
            ys.reshape(n_sample, dec_seq, D_MODEL),
            jnp.transpose(k_new, (0, 1, 4, 2, 3)),
            jnp.transpose(v_new, (0, 1, 4, 2, 3)))
```

```python
import functools

import numpy as np
import jax
import jax.numpy as jnp
from jax import lax
from jax.experimental import pallas as pl
from jax.experimental.pallas import tpu as pltpu

D_MODEL = 1024
HEAD_DIM = 64
ATTN_DIM = 512
N_HEADS = 8
N_KV = 2
KV_GROUP = N_HEADS // N_KV
KV_DIM = N_KV * HEAD_DIM
CONV_DIM = 512
D_FF = 2816
QK_DIM = ATTN_DIM + KV_DIM
QKV_DIM = ATTN_DIM + 2 * KV_DIM
IN_DIM = QKV_DIM + 3 * CONV_DIM
GRID_W = 64
ROT_PAIRS = HEAD_DIM // 4
ROPE_THETA = 10000.0
RMS_EPS = 1e-6
LOG2_E = 1.4426950408889634
Q_SCALE = HEAD_DIM ** -0.5 * LOG2_E
MAX_UNSHIFTED_SCORE = 64.0

LANES = 128
MXU_DIM = 256
BLOCK_ROWS = 1024
CHUNK_ROWS = 512
LATENT_Q_ROWS = 256
P_ROWS = 64
FFN_ROWS = 512
GU_CAST_ROWS = 128
DOWN_CAST_ROWS = 352
FF_CHUNKS = ((0, 1024), (1024, 1024), (2048, 768))
COND_ROWS = 16
MOD_ROWS = 8
CTX_ROW = 8
ADA_COLS = 2048
VMEM_TEMP_BYTES = 8 * 1024 * 1024
VMEM_MAX_BYTES = 56 * 1024 * 1024

F32 = jnp.float32
BF16 = jnp.bfloat16


def _vmem_limit(windows, scratch=()):
    total = VMEM_TEMP_BYTES
    for spec, dtype in windows:
        if spec.block_shape is None:
            continue
        n = int(np.prod([1 if d is None else d for d in spec.block_shape]))
        total += n * jnp.dtype(dtype).itemsize * (1 if spec.pipeline_mode is not None else 2)
    for s in scratch:
        if s.memory_space == pltpu.VMEM:
            total += int(np.prod(s.shape)) * jnp.dtype(s.dtype).itemsize
    return min(total, VMEM_MAX_BYTES)


def _mod_spec(row, half):
    blk = 0 if row is None else row // MOD_ROWS
    return pl.BlockSpec((MOD_ROWS, 3 * D_MODEL), lambda i: (blk, half))


def _mod_row(mod_ref, row):
    r = pl.program_id(0) if row is None else row % MOD_ROWS
    return mod_ref[pl.ds(r, 1), :]


def _const_spec(shape):
    nd = len(shape)
    return pl.BlockSpec(shape, lambda *_: (0,) * nd, pipeline_mode=pl.Buffered(1))


def _ada_kernel(cond_ref, w_ref, b_ref, out_ref):
    c = cond_ref[...]
    s = (c * jax.nn.sigmoid(c)).astype(BF16)
    out_ref[...] = jnp.dot(s, w_ref[...].astype(BF16), preferred_element_type=F32) + b_ref[...]


def _ada_call(cond, w_ada, b_ada):
    n = w_ada.shape[1]
    in_specs = [
        pl.BlockSpec((COND_ROWS, D_MODEL), lambda j: (0, 0)),
        pl.BlockSpec((D_MODEL, ADA_COLS), lambda j: (0, j)),
        pl.BlockSpec((1, ADA_COLS), lambda j: (0, j)),
    ]
    out_spec = pl.BlockSpec((COND_ROWS, ADA_COLS), lambda j: (0, j))
    return pl.pallas_call(
        _ada_kernel,
        grid=(n // ADA_COLS,),
        in_specs=in_specs,
        out_specs=out_spec,
        out_shape=jax.ShapeDtypeStruct((COND_ROWS, n), F32),
        compiler_params=pltpu.CompilerParams(
            dimension_semantics=("arbitrary",),
            vmem_limit_bytes=_vmem_limit([(s, F32) for s in in_specs + [out_spec]])),
        name="ada_rows",
    )(cond, w_ada, b_ada)


def _rms_rows(x):
    return x * lax.rsqrt(jnp.mean(x * x, axis=-1, keepdims=True) + RMS_EPS)


def _mixer_kernel(*refs, seq_len, ctx_len, use_rope, emit_kv, mod_row):
    it = iter(refs)
    x_ref, mod_ref = next(it), next(it)
    if ctx_len:
        ck_ref, cv_ref = next(it), next(it)
    if use_rope:
        cos_ref, sin_ref = next(it), next(it)
    (nmix_ref, win_ref, gsum_ref, qkg_ref, convw_ref, ga_ref, gc_ref, wout_ref) = (
        next(it) for _ in range(8))
    out_ref = next(it)
    if emit_kv:
        ko_ref, vo_ref = next(it), next(it)
    (q_s, k_s, vt_s, attn_s, t_s, gb_s, s_ref, p_ref, l_ref) = (next(it) for _ in range(9))

    n_chunks = BLOCK_ROWS // CHUNK_ROWS
    if ctx_len:
        q_rows, n_keys, key_blk = LATENT_Q_ROWS, ctx_len + seq_len, BLOCK_ROWS
    else:
        q_rows, n_keys, key_blk = seq_len, seq_len, seq_len
    n_qb = BLOCK_ROWS // q_rows

    mod = _mod_row(mod_ref, mod_row)
    shift = mod[:, 0:D_MODEL]
    scale1 = 1.0 + mod[:, D_MODEL:2 * D_MODEL]
    gate = mod[:, 2 * D_MODEL:3 * D_MODEL]

    def put_values_t(blk, off, vt):
        vt = vt.astype(BF16)
        for g in range(N_KV):
            vt_s[blk, g, :, off:off + vt.shape[1]] = vt[g * HEAD_DIM:(g + 1) * HEAD_DIM, :]

    if ctx_len:
        ck = ck_ref[...].reshape(KV_DIM, ctx_len).T.astype(BF16)
        for g in range(N_KV):
            k_s[g, 0:ctx_len, :] = ck[:, g * HEAD_DIM:(g + 1) * HEAD_DIM]
        put_values_t(0, 0, cv_ref[...].reshape(KV_DIM, ctx_len))

    if use_rope:
        lane = lax.broadcasted_iota(jnp.int32, (CHUNK_ROWS, LANES), 1)
        first_half = (lane % (2 * ROT_PAIRS)) < ROT_PAIRS

    t_s[0:8, :] = jnp.zeros((8, CONV_DIM), F32)
    t_s[8 + BLOCK_ROWS:16 + BLOCK_ROWS, :] = jnp.zeros((8, CONV_DIM), F32)
    w0 = convw_ref[0:1, :]
    w1 = convw_ref[1:2, :]
    w2 = convw_ref[2:3, :]

    def conv_rows(a, n):
        slab = t_s[a:a + n + 16, :]
        t_prev = pltpu.roll(slab, 1, axis=0)[8:8 + n, :]
        t_mid = slab[8:8 + n, :]
        t_next = pltpu.roll(slab, n + 15, axis=0)[8:8 + n, :]
        if seq_len < BLOCK_ROWS:
            pos = (lax.broadcasted_iota(jnp.int32, (n, 1), 0) + a) % seq_len
            t_prev = jnp.where(pos == 0, 0.0, t_prev)
            t_next = jnp.where(pos == seq_len - 1, 0.0, t_next)
        y = gb_s[a:a + n, :] * (w0 * t_prev + w1 * t_mid + w2 * t_next)
        return (_rms_rows(y) * gc_ref[...]).astype(BF16)

    def project_conv(r0, hb):
        cvp = jnp.dot(hb, win_ref[:, QKV_DIM:IN_DIM], preferred_element_type=F32)
        gb_s[r0:r0 + CHUNK_ROWS, :] = cvp[:, 0:CONV_DIM]
        t_s[8 + r0:8 + r0 + CHUNK_ROWS, :] = (cvp[:, CONV_DIM:2 * CONV_DIM]
                                              * cvp[:, 2 * CONV_DIM:3 * CONV_DIM])

    for c in range(n_chunks):
        r0 = c * CHUNK_ROWS
        x = x_ref[r0:r0 + CHUNK_ROWS, :]
        h = (_rms_rows(x) * nmix_ref[...]) * scale1 + shift
        hb = h.astype(BF16)
        qkv = jnp.dot(hb, win_ref[:, 0:QKV_DIM], preferred_element_type=F32)

        groups = []
        for g0 in range(0, QKV_DIM, MXU_DIM):
            sq = qkv[:, g0:g0 + MXU_DIM]
            groups.append(jnp.dot((sq * sq).astype(BF16), gsum_ref[...],
                                  preferred_element_type=F32))
        ms = jnp.concatenate(groups, axis=-1)[:, 0:QK_DIM]
        qk = (qkv[:, 0:QK_DIM] * lax.rsqrt(ms + RMS_EPS)) * qkg_ref[...]
        vv = qkv[:, QK_DIM:QKV_DIM]

        if emit_kv:
            for r1 in range(0, CHUNK_ROWS, seq_len):
                kt = qk[r1:r1 + seq_len, ATTN_DIM:QK_DIM].T
                ko_ref[(r0 + r1) // seq_len] = kt.reshape(N_KV, HEAD_DIM, seq_len)

        for cg in range(QK_DIM // LANES):
            xg = qk[:, cg * LANES:(cg + 1) * LANES]
            if use_rope:
                cs = cos_ref[r0:r0 + CHUNK_ROWS, :]
                sn = sin_ref[r0:r0 + CHUNK_ROWS, :]
                partner = jnp.where(first_half,
                                    pltpu.roll(xg, LANES - ROT_PAIRS, axis=1),
                                    pltpu.roll(xg, ROT_PAIRS, axis=1))
                xg = xg * cs + partner * sn
            if cg < ATTN_DIM // LANES:
                xb = (xg * Q_SCALE).astype(BF16)
                q_s[2 * cg, r0:r0 + CHUNK_ROWS, :] = xb[:, 0:HEAD_DIM]
                q_s[2 * cg + 1, r0:r0 + CHUNK_ROWS, :] = xb[:, HEAD_DIM:LANES]
            else:
                xb = xg.astype(BF16)
                for g in range(N_KV):
                    k_s[g, ctx_len + r0:ctx_len + r0 + CHUNK_ROWS, :] = (
                        xb[:, g * HEAD_DIM:(g + 1) * HEAD_DIM])
        w = min(key_blk, CHUNK_ROWS)
        for r1 in range(r0, r0 + CHUNK_ROWS, w):
            blk, off = (0, ctx_len + r1) if ctx_len else (r1 // key_blk, 0)
            vt = vv[r1 - r0:r1 - r0 + w, :].T
            put_values_t(blk, off, vt)
            if emit_kv:
                vo_ref[blk] = vt.reshape(N_KV, HEAD_DIM, seq_len)

        project_conv(r0, hb)

    def scores_t(qb, g):
        r0 = pl.multiple_of(qb * q_rows, q_rows)
        k0 = 0 if ctx_len else r0
        qs = jnp.concatenate(
            [q_s[KV_GROUP * g + j, pl.ds(r0, q_rows), :] for j in range(KV_GROUP)], axis=0)
        kk = k_s[g, pl.ds(k0, n_keys), :]
        return lax.dot_general(kk, qs, (((1,), (1,)), ((), ())),
                               preferred_element_type=F32)

    def values_out(qb, g, p, l):
        r0 = pl.multiple_of(qb * q_rows, q_rows)
        blk = 0 if ctx_len else qb
        ot = jnp.dot(vt_s[blk, g], p, preferred_element_type=F32) / l
        for jj in range(KV_GROUP // 2):
            pair_t = jnp.concatenate(
                [ot[:, (2 * jj) * q_rows:(2 * jj + 1) * q_rows],
                 ot[:, (2 * jj + 1) * q_rows:(2 * jj + 2) * q_rows]], axis=0)
            col = (KV_GROUP * g + 2 * jj) * HEAD_DIM
            attn_s[pl.ds(r0, q_rows), col:col + LANES] = pair_t.T

    def pipeline_pairs(step):
        assert n_qb % 2 == 0
        step(-1, 1, first=True)
        step(0, 0)

        def body(t, carry):
            step(2 * t + 1, 1)
            step(2 * t + 2, 0)
            return carry

        lax.fori_loop(0, (n_qb - 2) // 2, body, 0)
        step(n_qb - 1, 1, last=True)

    def shifted_block(qb, carry):
        for g in range(N_KV):
            s = scores_t(qb, g)
            s_ref[...] = s
            m = jnp.max(s, axis=0, keepdims=True)

            def exp_rows(c, l8, g=g, m=m):
                k1 = pl.multiple_of(c * P_ROWS, P_ROWS)
                e = jnp.exp2(s_ref[pl.ds(k1, P_ROWS), :] - m)
                p_ref[0, g, pl.ds(k1, P_ROWS), :] = e.astype(BF16)
                return l8 + e.reshape(P_ROWS // 8, 8, e.shape[1]).sum(axis=0)

            l8 = lax.fori_loop(0, n_keys // P_ROWS, exp_rows, jnp.zeros((8, s.shape[1]), F32))
            values_out(qb, g, p_ref[0, g], jnp.sum(l8, axis=0, keepdims=True))
        return carry

    def unshifted_step(j, par, first=False, last=False):
        for g in range(N_KV):
            if not last:
                e = jnp.exp2(scores_t(j + 1, g))
                p_ref[1 - par, g] = e.astype(BF16)
                l_ref[1 - par, g] = jnp.sum(e, axis=0, keepdims=True)
            if not first:
                values_out(j, g, p_ref[par, g], l_ref[par, g])

    gains = jnp.abs(qkg_ref[...])
    q_bound = (Q_SCALE * Q_SCALE * HEAD_DIM) * jnp.max(gains[:, 0:ATTN_DIM]) ** 2
    k_bound = HEAD_DIM * jnp.max(gains[:, ATTN_DIM:QK_DIM]) ** 2
    if ctx_len:
        ck2 = ck_ref[...]
        k_bound = jnp.maximum(k_bound, jnp.max(jnp.sum(ck2 * ck2, axis=1)))
    small_scores = q_bound * k_bound <= MAX_UNSHIFTED_SCORE ** 2
    pl.when(small_scores)(lambda: pipeline_pairs(unshifted_step))

    @pl.when(jnp.logical_not(small_scores))
    def _():
        lax.fori_loop(0, n_qb, shifted_block, 0)

    for r0 in range(0, BLOCK_ROWS, CHUNK_ROWS):
        rows = slice(r0, r0 + CHUNK_ROWS)
        an = (_rms_rows(attn_s[rows, :]) * ga_ref[...]).astype(BF16)
        merged = jnp.concatenate([an, conv_rows(r0, CHUNK_ROWS)], axis=-1)
        mix = jnp.dot(merged, wout_ref[...], preferred_element_type=F32)
        out_ref[rows, :] = x_ref[rows, :] + gate * mix


def _mixer_call(x_blocks, mod, mod_row, ctx_k, ctx_v, rope, consts, *, seq_len, emit_kv):
    n_blocks = x_blocks.shape[0]
    ctx_len = 0 if ctx_k is None else ctx_k.shape[4]
    use_rope = rope is not None
    if ctx_len:
        assert seq_len == BLOCK_ROWS
        q_rows, n_keys, n_key_blocks = LATENT_Q_ROWS, ctx_len + BLOCK_ROWS, 1
    else:
        q_rows, n_keys, n_key_blocks = seq_len, seq_len, BLOCK_ROWS // seq_len
    n_q_cols = KV_GROUP * q_rows

    blk = lambda cols: pl.BlockSpec((None, BLOCK_ROWS, cols), lambda b: (b, 0, 0))
    assert mod_row is not None or n_blocks <= MOD_ROWS
    args = [x_blocks, mod]
    in_specs = [blk(D_MODEL), _mod_spec(mod_row, 0)]
    if ctx_len:
        args += [ctx_k, ctx_v]
        in_specs += [pl.BlockSpec((None, None, N_KV, HEAD_DIM, ctx_len),
                                  lambda b: (b, 0, 0, 0, 0))] * 2
    if use_rope:
        args += list(rope)
        in_specs += [_const_spec((BLOCK_ROWS, LANES))] * 2
    args += list(consts)
    in_specs += [_const_spec(a.shape) for a in consts]

    out_shape = [jax.ShapeDtypeStruct((n_blocks, BLOCK_ROWS, D_MODEL), F32)]
    out_specs = [blk(D_MODEL)]
    if emit_kv:
        per_block = BLOCK_ROWS // seq_len
        out_shape += [jax.ShapeDtypeStruct(
            (n_blocks * per_block, 1, N_KV, HEAD_DIM, seq_len), F32)] * 2
        out_specs += [pl.BlockSpec((per_block, None, N_KV, HEAD_DIM, seq_len),
                                   lambda b: (b, 0, 0, 0, 0))] * 2

    scratch = [
        pltpu.VMEM((N_HEADS, BLOCK_ROWS, HEAD_DIM), BF16),
        pltpu.VMEM((N_KV, ctx_len + BLOCK_ROWS, HEAD_DIM), BF16),
        pltpu.VMEM((n_key_blocks, N_KV, HEAD_DIM, n_keys), BF16),
        pltpu.VMEM((BLOCK_ROWS, ATTN_DIM), F32),
        pltpu.VMEM((BLOCK_ROWS + 16, CONV_DIM), F32),
        pltpu.VMEM((BLOCK_ROWS, CONV_DIM), F32),
        pltpu.VMEM((n_keys, n_q_cols), F32),
        pltpu.VMEM((2, N_KV, n_keys, n_q_cols), BF16),
        pltpu.VMEM((2, N_KV, 1, n_q_cols), F32),
    ]
    kern = functools.partial(_mixer_kernel, seq_len=seq_len, ctx_len=ctx_len,
                             use_rope=use_rope, emit_kv=emit_kv, mod_row=mod_row)
    return pl.pallas_call(
        kern,
        grid=(n_blocks,),
        in_specs=in_specs,
        out_specs=out_specs,
        out_shape=out_shape,
        scratch_shapes=scratch,
        compiler_params=pltpu.CompilerParams(
            dimension_semantics=("arbitrary",),
            vmem_limit_bytes=_vmem_limit(
                [(s, a.dtype) for s, a in zip(in_specs, args)] + [(s, F32) for s in out_specs],
                scratch)),
        name="mixer_ctx" if emit_kv else "mixer_latent",
    )(*args)


def _ffn_rows(x_ref, mod, nffn_ref, wgu_ref, wd_ref, out_ref):
    shift = mod[:, 0:D_MODEL]
    scale1 = 1.0 + mod[:, D_MODEL:2 * D_MODEL]
    gate = mod[:, 2 * D_MODEL:3 * D_MODEL]
    x = x_ref[...]
    hb = ((_rms_rows(x) * nffn_ref[...]) * scale1 + shift).astype(BF16)
    acc = None
    for c0, cw in FF_CHUNKS:
        gt = jnp.dot(hb, wgu_ref[:, c0:c0 + cw], preferred_element_type=F32)
        up = jnp.dot(hb, wgu_ref[:, D_FF + c0:D_FF + c0 + cw], preferred_element_type=F32)
        act = ((gt * jax.nn.sigmoid(gt)) * up).astype(BF16)
        part = jnp.dot(act, wd_ref[c0:c0 + cw, :], preferred_element_type=F32)
        acc = part if acc is None else acc + part
    out_ref[...] = x + gate * acc


def _cast_rows(src_hbm, dst, stage, sem, chunk_rows):
    n = src_hbm.shape[0] // chunk_rows

    def fetch(c):
        return pltpu.make_async_copy(src_hbm.at[pl.ds(c * chunk_rows, chunk_rows), :],
                                     stage.at[c % 2], sem.at[c % 2])

    fetch(0).start()
    for c in range(n):
        if c + 1 < n:
            fetch(c + 1).start()
        fetch(c).wait()
        dst[c * chunk_rows:(c + 1) * chunk_rows, :] = stage[c % 2].astype(BF16)


def _ffn_kernel(xa_ref, xb_ref, mod_ref, nffn_ref, wgu_hbm, wd_hbm, ya_ref, yb_ref,
                wgu_s, wd_s, stage_gu, stage_d, sem, *, n_a, mod_row_a, tiles_per_mod_row_b):
    i = pl.program_id(0)

    @pl.when(i == 0)
    def _():
        _cast_rows(wgu_hbm, wgu_s, stage_gu, sem, GU_CAST_ROWS)
        _cast_rows(wd_hbm, wd_s, stage_d, sem, DOWN_CAST_ROWS)

    @pl.when(i < n_a)
    def _():
        mod = mod_ref[mod_row_a % MOD_ROWS:mod_row_a % MOD_ROWS + 1, :]
        _ffn_rows(xa_ref, mod, nffn_ref, wgu_s, wd_s, ya_ref)

    @pl.when(i >= n_a)
    def _():
        mod = mod_ref[pl.ds((i - n_a) // tiles_per_mod_row_b, 1), :]
        _ffn_rows(xb_ref, mod, nffn_ref, wgu_s, wd_s, yb_ref)


def _ffn_call(xa_rows, xb_rows, mod, mod_row_a, rows_per_mod_row_b, norm_ffn, w_gate_up, w_down):
    n_a, n_b = xa_rows.shape[0] // FFN_ROWS, xb_rows.shape[0] // FFN_ROWS
    tiles_per_row_b = rows_per_mod_row_b // FFN_ROWS
    assert n_b // tiles_per_row_b <= MOD_ROWS
    spec_a = pl.BlockSpec((FFN_ROWS, D_MODEL), lambda i: (jnp.minimum(i, n_a - 1), 0))
    spec_b = pl.BlockSpec((FFN_ROWS, D_MODEL), lambda i: (jnp.maximum(i - n_a, 0), 0))
    mod_spec = pl.BlockSpec((MOD_ROWS, 3 * D_MODEL),
                            lambda i: (jnp.where(i < n_a, mod_row_a // MOD_ROWS, 0), 1))
    hbm = pl.BlockSpec(memory_space=pl.ANY)
    in_specs = [spec_a, spec_b, mod_spec, _const_spec(norm_ffn.shape), hbm, hbm]
    scratch = [pltpu.VMEM(w_gate_up.shape, BF16), pltpu.VMEM(w_down.shape, BF16),
               pltpu.VMEM((2, GU_CAST_ROWS, w_gate_up.shape[1]), F32),
               pltpu.VMEM((2, DOWN_CAST_ROWS, w_down.shape[1]), F32),
               pltpu.SemaphoreType.DMA((2,))]
    return pl.pallas_call(
        functools.partial(_ffn_kernel, n_a=n_a, mod_row_a=mod_row_a,
                          tiles_per_mod_row_b=tiles_per_row_b),
        grid=(n_a + n_b,),
        in_specs=in_specs,
        out_specs=[spec_a, spec_b],
        out_shape=[jax.ShapeDtypeStruct(xa_rows.shape, F32),
                   jax.ShapeDtypeStruct(xb_rows.shape, F32)],
        scratch_shapes=scratch,
        compiler_params=pltpu.CompilerParams(
            dimension_semantics=("arbitrary",),
            vmem_limit_bytes=_vmem_limit(
                [(s, F32) for s in in_specs + [spec_a, spec_b]], scratch)),
        name="ffn",
    )(xa_rows, xb_rows, mod, norm_ffn, w_gate_up, w_down)


def _rope_tables(n_tokens):
    rows = n_tokens // GRID_W
    row = jnp.repeat(jnp.arange(rows, dtype=F32), GRID_W)
    col = jnp.tile(jnp.arange(GRID_W, dtype=F32), rows)
    inv = 1.0 / (ROPE_THETA ** (jnp.arange(ROT_PAIRS, dtype=F32) / ROT_PAIRS))
    ang = jnp.stack([row[:, None] * inv, col[:, None] * inv], axis=1)
    cos, sin = jnp.cos(ang), jnp.sin(ang)
    cos_h = jnp.concatenate([cos, cos], axis=-1).reshape(n_tokens, HEAD_DIM)
    sin_h = jnp.concatenate([-sin, sin], axis=-1).reshape(n_tokens, HEAD_DIM)
    reps = LANES // HEAD_DIM
    return jnp.tile(cos_h, (1, reps)), jnp.tile(sin_h, (1, reps))


def _keys_minor(kv):
    return jnp.transpose(kv, (0, 1, 3, 4, 2))


def _group_mean_matrix():
    idx = np.arange(MXU_DIM) // HEAD_DIM
    g = (idx[:, None] == idx[None, :]).astype(np.float32) / HEAD_DIM
    return jnp.asarray(g, dtype=BF16)


def kernel(x_prompt, x_sample, c, cache_k, cache_v, c_ctx, norm_mix, norm_ffn, w_ada, b_ada,
           w_in, q_norm, k_norm, conv_w, attn_out_norm, conv_out_norm, w_out, w_gate_up, w_down):
    depth = w_in.shape[0]
    assert depth == 1
    n_prompt, seq, _ = x_prompt.shape
    n_sample, dec_seq, _ = x_sample.shape
    assert dec_seq == BLOCK_ROWS and BLOCK_ROWS % seq == 0 and n_sample <= CTX_ROW

    cond = jnp.zeros((COND_ROWS, D_MODEL), F32)
    cond = cond.at[0:n_sample].set(c).at[CTX_ROW].set(c_ctx)
    mod = _ada_call(cond, w_ada[0], b_ada[0][None, :])

    consts = (
        norm_mix[0][None, :],
        w_in[0].astype(BF16),
        _group_mean_matrix(),
        jnp.concatenate([jnp.tile(q_norm[0], N_HEADS), jnp.tile(k_norm[0], N_KV)])[None, :],
        conv_w[0],
        attn_out_norm[0][None, :],
        conv_out_norm[0][None, :],
        w_out[0].astype(BF16),
    )
    nffn = norm_ffn[0][None, :]

    (xs1,) = _mixer_call(x_sample, mod, None, _keys_minor(cache_k), _keys_minor(cache_v),
                         _rope_tables(dec_seq), consts, seq_len=dec_seq, emit_kv=False)

    per_block = BLOCK_ROWS // seq
    xp_blocks = x_prompt.reshape(n_prompt // per_block, BLOCK_ROWS, D_MODEL)
    xp1, k_new, v_new = _mixer_call(xp_blocks, mod, CTX_ROW, None, None, None, consts,
                                    seq_len=seq, emit_kv=True)

    yp, ys = _ffn_call(xp1.reshape(-1, D_MODEL), xs1.reshape(-1, D_MODEL), mod, CTX_ROW, dec_seq,
                       nffn, w_gate_up[0], w_down[0])

    return (yp.reshape(n_prompt, seq, D_MODEL),
            ys.reshape(n_sample, dec_seq, D_MODEL),
            jnp.transpose(k_new, (0, 1, 4, 2, 3)),
            jnp.transpose(v_new, (0, 1, 4, 2, 3)))
```

```python
import functools

import numpy as np
import jax
import jax.numpy as jnp
from jax import lax
from jax.experimental import pallas as pl
from jax.experimental.pallas import tpu as pltpu

D_MODEL = 1024
HEAD_DIM = 64
ATTN_DIM = 512
N_HEADS = 8
N_KV = 2
KV_GROUP = N_HEADS // N_KV
KV_DIM = N_KV * HEAD_DIM
CONV_DIM = 512
D_FF = 2816
QK_DIM = ATTN_DIM + KV_DIM
QKV_DIM = ATTN_DIM + 2 * KV_DIM
IN_DIM = QKV_DIM + 3 * CONV_DIM
GRID_W = 64
ROT_PAIRS = HEAD_DIM // 4
ROPE_THETA = 10000.0
RMS_EPS = 1e-6
LOG2_E = 1.4426950408889634
Q_SCALE = HEAD_DIM ** -0.5 * LOG2_E
V_ROWS = HEAD_DIM + 16
MAX_UNSHIFTED_SCORE = 64.0

LANES = 128
MXU_DIM = 256
BLOCK_ROWS = 1024
CHUNK_ROWS = 512
LATENT_Q_ROWS = 256
P_ROWS = 16
FFN_ROWS = 512
GU_CAST_ROWS = 128
DOWN_CAST_ROWS = 352
FF_CHUNKS = ((0, 1024), (1024, 1024), (2048, 768))
COND_ROWS = 16
MOD_ROWS = 8
CTX_ROW = 8
ADA_COLS = 2048
VMEM_TEMP_BYTES = 8 * 1024 * 1024
VMEM_MAX_BYTES = 56 * 1024 * 1024

F32 = jnp.float32
BF16 = jnp.bfloat16


def _vmem_limit(windows, scratch=()):
    total = VMEM_TEMP_BYTES
    for spec, dtype in windows:
        if spec.block_shape is None:
            continue
        n = int(np.prod([1 if d is None else d for d in spec.block_shape]))
        total += n * jnp.dtype(dtype).itemsize * (1 if spec.pipeline_mode is not None else 2)
    for s in scratch:
        if s.memory_space == pltpu.VMEM:
            total += int(np.prod(s.shape)) * jnp.dtype(s.dtype).itemsize
    return min(total, VMEM_MAX_BYTES)


def _mod_spec(row, half):
    blk = 0 if row is None else row // MOD_ROWS
    return pl.BlockSpec((MOD_ROWS, 3 * D_MODEL), lambda i: (blk, half))


def _mod_row(mod_ref, row):
    r = pl.program_id(0) if row is None else row % MOD_ROWS
    return mod_ref[pl.ds(r, 1), :]


def _const_spec(shape):
    nd = len(shape)
    return pl.BlockSpec(shape, lambda *_: (0,) * nd, pipeline_mode=pl.Buffered(1))


def _ada_kernel(cond_ref, w_ref, b_ref, out_ref):
    c = cond_ref[...]
    s = (c * jax.nn.sigmoid(c)).astype(BF16)
    out_ref[...] = jnp.dot(s, w_ref[...].astype(BF16), preferred_element_type=F32) + b_ref[...]


def _ada_call(cond, w_ada, b_ada):
    n = w_ada.shape[1]
    in_specs = [
        pl.BlockSpec((COND_ROWS, D_MODEL), lambda j: (0, 0)),
        pl.BlockSpec((D_MODEL, ADA_COLS), lambda j: (0, j)),
        pl.BlockSpec((1, ADA_COLS), lambda j: (0, j)),
    ]
    out_spec = pl.BlockSpec((COND_ROWS, ADA_COLS), lambda j: (0, j))
    return pl.pallas_call(
        _ada_kernel,
        grid=(n // ADA_COLS,),
        in_specs=in_specs,
        out_specs=out_spec,
        out_shape=jax.ShapeDtypeStruct((COND_ROWS, n), F32),
        compiler_params=pltpu.CompilerParams(
            dimension_semantics=("arbitrary",),
            vmem_limit_bytes=_vmem_limit([(s, F32) for s in in_specs + [out_spec]])),
        name="ada_rows",
    )(cond, w_ada, b_ada)


def _rms_rows(x):
    return x * lax.rsqrt(jnp.mean(x * x, axis=-1, keepdims=True) + RMS_EPS)


def _mixer_kernel(*refs, seq_len, ctx_len, use_rope, emit_kv, mod_row):
    it = iter(refs)
    x_ref, mod_ref = next(it), next(it)
    if ctx_len:
        ck_ref, cv_ref = next(it), next(it)
    if use_rope:
        cos_ref, sin_ref = next(it), next(it)
    (nmix_ref, win_ref, gsum_ref, qkg_ref, convw_ref, ga_ref, gc_ref, wout_ref) = (
        next(it) for _ in range(8))
    out_ref = next(it)
    if emit_kv:
        ko_ref, vo_ref = next(it), next(it)
    (q_s, k_s, vt_s, attn_s, t_s, gb_s, s_ref, p_ref) = (next(it) for _ in range(8))

    n_chunks = BLOCK_ROWS // CHUNK_ROWS
    if ctx_len:
        q_rows, n_keys, key_blk = LATENT_Q_ROWS, ctx_len + seq_len, BLOCK_ROWS
    else:
        q_rows, n_keys, key_blk = seq_len, seq_len, seq_len
    n_qb = BLOCK_ROWS // q_rows

    mod = _mod_row(mod_ref, mod_row)
    shift = mod[:, 0:D_MODEL]
    scale1 = 1.0 + mod[:, D_MODEL:2 * D_MODEL]
    gate = mod[:, 2 * D_MODEL:3 * D_MODEL]

    def put_values_t(blk, off, vt):
        vt = vt.astype(BF16)
        for g in range(N_KV):
            vt_s[blk, g, 0:HEAD_DIM, off:off + vt.shape[1]] = vt[g * HEAD_DIM:(g + 1) * HEAD_DIM, :]

    tail = (lax.broadcasted_iota(jnp.int32, (V_ROWS - HEAD_DIM, n_keys), 0) == 0).astype(BF16)
    for blk in range(vt_s.shape[0]):
        for g in range(N_KV):
            vt_s[blk, g, HEAD_DIM:V_ROWS, :] = tail

    if ctx_len:
        ck = ck_ref[...].reshape(KV_DIM, ctx_len).T.astype(BF16)
        for g in range(N_KV):
            k_s[g, 0:ctx_len, :] = ck[:, g * HEAD_DIM:(g + 1) * HEAD_DIM]
        put_values_t(0, 0, cv_ref[...].reshape(KV_DIM, ctx_len))

    if use_rope:
        lane = lax.broadcasted_iota(jnp.int32, (CHUNK_ROWS, LANES), 1)
        first_half = (lane % (2 * ROT_PAIRS)) < ROT_PAIRS

    t_s[0:8, :] = jnp.zeros((8, CONV_DIM), F32)
    t_s[8 + BLOCK_ROWS:16 + BLOCK_ROWS, :] = jnp.zeros((8, CONV_DIM), F32)
    w0 = convw_ref[0:1, :]
    w1 = convw_ref[1:2, :]
    w2 = convw_ref[2:3, :]

    def conv_rows(a, n):
        slab = t_s[a:a + n + 16, :]
        t_prev = pltpu.roll(slab, 1, axis=0)[8:8 + n, :]
        t_mid = slab[8:8 + n, :]
        t_next = pltpu.roll(slab, n + 15, axis=0)[8:8 + n, :]
        if seq_len < BLOCK_ROWS:
            pos = (lax.broadcasted_iota(jnp.int32, (n, 1), 0) + a) % seq_len
            t_prev = jnp.where(pos == 0, 0.0, t_prev)
            t_next = jnp.where(pos == seq_len - 1, 0.0, t_next)
        y = gb_s[a:a + n, :] * (w0 * t_prev + w1 * t_mid + w2 * t_next)
        return (_rms_rows(y) * gc_ref[...]).astype(BF16)

    def project_conv(r0, hb):
        cvp = jnp.dot(hb, win_ref[:, QKV_DIM:IN_DIM], preferred_element_type=F32)
        gb_s[r0:r0 + CHUNK_ROWS, :] = cvp[:, 0:CONV_DIM]
        t_s[8 + r0:8 + r0 + CHUNK_ROWS, :] = (cvp[:, CONV_DIM:2 * CONV_DIM]
                                              * cvp[:, 2 * CONV_DIM:3 * CONV_DIM])

    for c in range(n_chunks):
        r0 = c * CHUNK_ROWS
        x = x_ref[r0:r0 + CHUNK_ROWS, :]
        h = (_rms_rows(x) * nmix_ref[...]) * scale1 + shift
        hb = h.astype(BF16)
        qkv = jnp.dot(hb, win_ref[:, 0:QKV_DIM], preferred_element_type=F32)

        groups = []
        for g0 in range(0, QKV_DIM, MXU_DIM):
            sq = qkv[:, g0:g0 + MXU_DIM]
            groups.append(jnp.dot((sq * sq).astype(BF16), gsum_ref[...],
                                  preferred_element_type=F32))
        ms = jnp.concatenate(groups, axis=-1)[:, 0:QK_DIM]
        qk = (qkv[:, 0:QK_DIM] * lax.rsqrt(ms + RMS_EPS)) * qkg_ref[...]
        vv = qkv[:, QK_DIM:QKV_DIM]

        if emit_kv:
            for r1 in range(0, CHUNK_ROWS, seq_len):
                kt = qk[r1:r1 + seq_len, ATTN_DIM:QK_DIM].T
                ko_ref[(r0 + r1) // seq_len] = kt.reshape(N_KV, HEAD_DIM, seq_len)

        for cg in range(QK_DIM // LANES):
            xg = qk[:, cg * LANES:(cg + 1) * LANES]
            if use_rope:
                cs = cos_ref[r0:r0 + CHUNK_ROWS, :]
                sn = sin_ref[r0:r0 + CHUNK_ROWS, :]
                partner = jnp.where(first_half,
                                    pltpu.roll(xg, LANES - ROT_PAIRS, axis=1),
                                    pltpu.roll(xg, ROT_PAIRS, axis=1))
                xg = xg * cs + partner * sn
            if cg < ATTN_DIM // LANES:
                xb = (xg * Q_SCALE).astype(BF16)
                q_s[2 * cg, r0:r0 + CHUNK_ROWS, :] = xb[:, 0:HEAD_DIM]
                q_s[2 * cg + 1, r0:r0 + CHUNK_ROWS, :] = xb[:, HEAD_DIM:LANES]
            else:
                xb = xg.astype(BF16)
                for g in range(N_KV):
                    k_s[g, ctx_len + r0:ctx_len + r0 + CHUNK_ROWS, :] = (
                        xb[:, g * HEAD_DIM:(g + 1) * HEAD_DIM])
        w = min(key_blk, CHUNK_ROWS)
        for r1 in range(r0, r0 + CHUNK_ROWS, w):
            blk, off = (0, ctx_len + r1) if ctx_len else (r1 // key_blk, 0)
            vt = vv[r1 - r0:r1 - r0 + w, :].T
            put_values_t(blk, off, vt)
            if emit_kv:
                vo_ref[blk] = vt.reshape(N_KV, HEAD_DIM, seq_len)

        project_conv(r0, hb)

    def scores_t(qb, g):
        r0 = pl.multiple_of(qb * q_rows, q_rows)
        k0 = 0 if ctx_len else r0
        qs = jnp.concatenate(
            [q_s[KV_GROUP * g + j, pl.ds(r0, q_rows), :] for j in range(KV_GROUP)], axis=0)
        kk = k_s[g, pl.ds(k0, n_keys), :]
        return lax.dot_general(kk, qs, (((1,), (1,)), ((), ())),
                               preferred_element_type=F32)

    def values_out(qb, g, p):
        r0 = pl.multiple_of(qb * q_rows, q_rows)
        blk = 0 if ctx_len else qb
        ot = jnp.dot(vt_s[blk, g], p, preferred_element_type=F32)
        ot = ot[0:HEAD_DIM, :] / ot[HEAD_DIM:HEAD_DIM + 1, :]
        for jj in range(KV_GROUP // 2):
            pair_t = jnp.concatenate(
                [ot[:, (2 * jj) * q_rows:(2 * jj + 1) * q_rows],
                 ot[:, (2 * jj + 1) * q_rows:(2 * jj + 2) * q_rows]], axis=0)
            col = (KV_GROUP * g + 2 * jj) * HEAD_DIM
            attn_s[pl.ds(r0, q_rows), col:col + LANES] = pair_t.T

    def pipeline_pairs(step):
        assert n_qb % 2 == 0
        step(-1, 1, first=True)
        step(0, 0)

        def body(t, carry):
            step(2 * t + 1, 1)
            step(2 * t + 2, 0)
            return carry

        lax.fori_loop(0, (n_qb - 2) // 2, body, 0)
        step(n_qb - 1, 1, last=True)

    def shifted_block(qb, carry):
        for g in range(N_KV):
            s = scores_t(qb, g)
            s_ref[...] = s
            m = jnp.max(s, axis=0, keepdims=True)
            for k1 in range(0, n_keys, P_ROWS):
                p_ref[0, g, k1:k1 + P_ROWS, :] = jnp.exp2(
                    s_ref[k1:k1 + P_ROWS, :] - m).astype(BF16)
            values_out(qb, g, p_ref[0, g])
        return carry

    def unshifted_step(j, par, first=False, last=False):
        for g in range(N_KV):
            if not last:
                p_ref[1 - par, g] = jnp.exp2(scores_t(j + 1, g)).astype(BF16)
            if not first:
                values_out(j, g, p_ref[par, g])

    gains = jnp.abs(qkg_ref[...])
    q_bound = (Q_SCALE * Q_SCALE * HEAD_DIM) * jnp.max(gains[:, 0:ATTN_DIM]) ** 2
    k_bound = HEAD_DIM * jnp.max(gains[:, ATTN_DIM:QK_DIM]) ** 2
    if ctx_len:
        ck2 = ck_ref[...]
        k_bound = jnp.maximum(k_bound, jnp.max(jnp.sum(ck2 * ck2, axis=1)))
    small_scores = q_bound * k_bound <= MAX_UNSHIFTED_SCORE ** 2
    pl.when(small_scores)(lambda: pipeline_pairs(unshifted_step))

    @pl.when(jnp.logical_not(small_scores))
    def _():
        lax.fori_loop(0, n_qb, shifted_block, 0)

    for r0 in range(0, BLOCK_ROWS, CHUNK_ROWS):
        rows = slice(r0, r0 + CHUNK_ROWS)
        an = (_rms_rows(attn_s[rows, :]) * ga_ref[...]).astype(BF16)
        merged = jnp.concatenate([an, conv_rows(r0, CHUNK_ROWS)], axis=-1)
        mix = jnp.dot(merged, wout_ref[...], preferred_element_type=F32)
        out_ref[rows, :] = x_ref[rows, :] + gate * mix


def _mixer_call(x_blocks, mod, mod_row, ctx_k, ctx_v, rope, consts, *, seq_len, emit_kv):
    n_blocks = x_blocks.shape[0]
    ctx_len = 0 if ctx_k is None else ctx_k.shape[4]
    use_rope = rope is not None
    if ctx_len:
        assert seq_len == BLOCK_ROWS
        q_rows, n_keys, n_key_blocks = LATENT_Q_ROWS, ctx_len + BLOCK_ROWS, 1
    else:
        q_rows, n_keys, n_key_blocks = seq_len, seq_len, BLOCK_ROWS // seq_len
    n_q_cols = KV_GROUP * q_rows

    blk = lambda cols: pl.BlockSpec((None, BLOCK_ROWS, cols), lambda b: (b, 0, 0))
    assert mod_row is not None or n_blocks <= MOD_ROWS
    args = [x_blocks, mod]
    in_specs = [blk(D_MODEL), _mod_spec(mod_row, 0)]
    if ctx_len:
        args += [ctx_k, ctx_v]
        in_specs += [pl.BlockSpec((None, None, N_KV, HEAD_DIM, ctx_len),
                                  lambda b: (b, 0, 0, 0, 0))] * 2
    if use_rope:
        args += list(rope)
        in_specs += [_const_spec((BLOCK_ROWS, LANES))] * 2
    args += list(consts)
    in_specs += [_const_spec(a.shape) for a in consts]

    out_shape = [jax.ShapeDtypeStruct((n_blocks, BLOCK_ROWS, D_MODEL), F32)]
    out_specs = [blk(D_MODEL)]
    if emit_kv:
        per_block = BLOCK_ROWS // seq_len
        out_shape += [jax.ShapeDtypeStruct(
            (n_blocks * per_block, 1, N_KV, HEAD_DIM, seq_len), F32)] * 2
        out_specs += [pl.BlockSpec((per_block, None, N_KV, HEAD_DIM, seq_len),
                                   lambda b: (b, 0, 0, 0, 0))] * 2

    scratch = [
        pltpu.VMEM((N_HEADS, BLOCK_ROWS, HEAD_DIM), BF16),
        pltpu.VMEM((N_KV, ctx_len + BLOCK_ROWS, HEAD_DIM), BF16),
        pltpu.VMEM((n_key_blocks, N_KV, V_ROWS, n_keys), BF16),
        pltpu.VMEM((BLOCK_ROWS, ATTN_DIM), F32),
        pltpu.VMEM((BLOCK_ROWS + 16, CONV_DIM), F32),
        pltpu.VMEM((BLOCK_ROWS, CONV_DIM), F32),
        pltpu.VMEM((n_keys, n_q_cols), F32),
        pltpu.VMEM((2, N_KV, n_keys, n_q_cols), BF16),
    ]
    kern = functools.partial(_mixer_kernel, seq_len=seq_len, ctx_len=ctx_len,
                             use_rope=use_rope, emit_kv=emit_kv, mod_row=mod_row)
    return pl.pallas_call(
        kern,
        grid=(n_blocks,),
        in_specs=in_specs,
        out_specs=out_specs,
        out_shape=out_shape,
        scratch_shapes=scratch,
        compiler_params=pltpu.CompilerParams(
            dimension_semantics=("arbitrary",),
            vmem_limit_bytes=_vmem_limit(
                [(s, a.dtype) for s, a in zip(in_specs, args)] + [(s, F32) for s in out_specs],
                scratch)),
        name="mixer_ctx" if emit_kv else "mixer_latent",
    )(*args)


def _ffn_rows(x_ref, mod, nffn_ref, wgu_ref, wd_ref, out_ref):
    shift = mod[:, 0:D_MODEL]
    scale1 = 1.0 + mod[:, D_MODEL:2 * D_MODEL]
    gate = mod[:, 2 * D_MODEL:3 * D_MODEL]
    x = x_ref[...]
    hb = ((_rms_rows(x) * nffn_ref[...]) * scale1 + shift).astype(BF16)
    acc = None
    for c0, cw in FF_CHUNKS:
        gt = jnp.dot(hb, wgu_ref[:, c0:c0 + cw], preferred_element_type=F32)
        up = jnp.dot(hb, wgu_ref[:, D_FF + c0:D_FF + c0 + cw], preferred_element_type=F32)
        act = ((gt * jax.nn.sigmoid(gt)) * up).astype(BF16)
        part = jnp.dot(act, wd_ref[c0:c0 + cw, :], preferred_element_type=F32)
        acc = part if acc is None else acc + part
    out_ref[...] = x + gate * acc


def _cast_rows(src_hbm, dst, stage, sem, chunk_rows):
    n = src_hbm.shape[0] // chunk_rows

    def fetch(c):
        return pltpu.make_async_copy(src_hbm.at[pl.ds(c * chunk_rows, chunk_rows), :],
                                     stage.at[c % 2], sem.at[c % 2])

    fetch(0).start()
    for c in range(n):
        if c + 1 < n:
            fetch(c + 1).start()
        fetch(c).wait()
        dst[c * chunk_rows:(c + 1) * chunk_rows, :] = stage[c % 2].astype(BF16)


def _ffn_kernel(xa_ref, xb_ref, mod_ref, nffn_ref, wgu_hbm, wd_hbm, ya_ref, yb_ref,
                wgu_s, wd_s, stage_gu, stage_d, sem, *, n_a, mod_row_a, tiles_per_mod_row_b):
    i = pl.program_id(0)

    @pl.when(i == 0)
    def _():
        _cast_rows(wgu_hbm, wgu_s, stage_gu, sem, GU_CAST_ROWS)
        _cast_rows(wd_hbm, wd_s, stage_d, sem, DOWN_CAST_ROWS)

    @pl.when(i < n_a)
    def _():
        mod = mod_ref[mod_row_a % MOD_ROWS:mod_row_a % MOD_ROWS + 1, :]
        _ffn_rows(xa_ref, mod, nffn_ref, wgu_s, wd_s, ya_ref)

    @pl.when(i >= n_a)
    def _():
        mod = mod_ref[pl.ds((i - n_a) // tiles_per_mod_row_b, 1), :]
        _ffn_rows(xb_ref, mod, nffn_ref, wgu_s, wd_s, yb_ref)


def _ffn_call(xa_rows, xb_rows, mod, mod_row_a, rows_per_mod_row_b, norm_ffn, w_gate_up, w_down):
    n_a, n_b = xa_rows.shape[0] // FFN_ROWS, xb_rows.shape[0] // FFN_ROWS
    tiles_per_row_b = rows_per_mod_row_b // FFN_ROWS
    assert n_b // tiles_per_row_b <= MOD_ROWS
    spec_a = pl.BlockSpec((FFN_ROWS, D_MODEL), lambda i: (jnp.minimum(i, n_a - 1), 0))
    spec_b = pl.BlockSpec((FFN_ROWS, D_MODEL), lambda i: (jnp.maximum(i - n_a, 0), 0))
    mod_spec = pl.BlockSpec((MOD_ROWS, 3 * D_MODEL),
                            lambda i: (jnp.where(i < n_a, mod_row_a // MOD_ROWS, 0), 1))
    hbm = pl.BlockSpec(memory_space=pl.ANY)
    in_specs = [spec_a, spec_b, mod_spec, _const_spec(norm_ffn.shape), hbm, hbm]
    scratch = [pltpu.VMEM(w_gate_up.shape, BF16), pltpu.VMEM(w_down.shape, BF16),
               pltpu.VMEM((2, GU_CAST_ROWS, w_gate_up.shape[1]), F32),
               pltpu.VMEM((2, DOWN_CAST_ROWS, w_down.shape[1]), F32),
               pltpu.SemaphoreType.DMA((2,))]
    return pl.pallas_call(
        functools.partial(_ffn_kernel, n_a=n_a, mod_row_a=mod_row_a,
                          tiles_per_mod_row_b=tiles_per_row_b),
        grid=(n_a + n_b,),
        in_specs=in_specs,
        out_specs=[spec_a, spec_b],
        out_shape=[jax.ShapeDtypeStruct(xa_rows.shape, F32),
                   jax.ShapeDtypeStruct(xb_rows.shape, F32)],
        scratch_shapes=scratch,
        compiler_params=pltpu.CompilerParams(
            dimension_semantics=("arbitrary",),
            vmem_limit_bytes=_vmem_limit(
                [(s, F32) for s in in_specs + [spec_a, spec_b]], scratch)),
        name="ffn",
    )(xa_rows, xb_rows, mod, norm_ffn, w_gate_up, w_down)


def _rope_tables(n_tokens):
    rows = n_tokens // GRID_W
    row = jnp.repeat(jnp.arange(rows, dtype=F32), GRID_W)
    col = jnp.tile(jnp.arange(GRID_W, dtype=F32), rows)
    inv = 1.0 / (ROPE_THETA ** (jnp.arange(ROT_PAIRS, dtype=F32) / ROT_PAIRS))
    ang = jnp.stack([row[:, None] * inv, col[:, None] * inv], axis=1)
    cos, sin = jnp.cos(ang), jnp.sin(ang)
    cos_h = jnp.concatenate([cos, cos], axis=-1).reshape(n_tokens, HEAD_DIM)
    sin_h = jnp.concatenate([-sin, sin], axis=-1).reshape(n_tokens, HEAD_DIM)
    reps = LANES // HEAD_DIM
    return jnp.tile(cos_h, (1, reps)), jnp.tile(sin_h, (1, reps))


def _keys_minor(kv):
    return jnp.transpose(kv, (0, 1, 3, 4, 2))


def _group_mean_matrix():
    idx = np.arange(MXU_DIM) // HEAD_DIM
    g = (idx[:, None] == idx[None, :]).astype(np.float32) / HEAD_DIM
    return jnp.asarray(g, dtype=BF16)


def kernel(x_prompt, x_sample, c, cache_k, cache_v, c_ctx, norm_mix, norm_ffn, w_ada, b_ada,
           w_in, q_norm, k_norm, conv_w, attn_out_norm, conv_out_norm, w_out, w_gate_up, w_down):
    depth = w_in.shape[0]
    assert depth == 1
    n_prompt, seq, _ = x_prompt.shape
    n_sample, dec_seq, _ = x_sample.shape
    assert dec_seq == BLOCK_ROWS and BLOCK_ROWS % seq == 0 and n_sample <= CTX_ROW

    cond = jnp.zeros((COND_ROWS, D_MODEL), F32)
    cond = cond.at[0:n_sample].set(c).at[CTX_ROW].set(c_ctx)
    mod = _ada_call(cond, w_ada[0], b_ada[0][None, :])

    consts = (
        norm_mix[0][None, :],
        w_in[0].astype(BF16),
        _group_mean_matrix(),
        jnp.concatenate([jnp.tile(q_norm[0], N_HEADS), jnp.tile(k_norm[0], N_KV)])[None, :],
        conv_w[0],
        attn_out_norm[0][None, :],
        conv_out_norm[0][None, :],
        w_out[0].astype(BF16),
    )
    nffn = norm_ffn[0][None, :]

    (xs1,) = _mixer_call(x_sample, mod, None, _keys_minor(cache_k), _keys_minor(cache_v),
                         _rope_tables(dec_seq), consts, seq_len=dec_seq, emit_kv=False)

    per_block = BLOCK_ROWS // seq
    xp_blocks = x_prompt.reshape(n_prompt // per_block, BLOCK_ROWS, D_MODEL)
    xp1, k_new, v_new = _mixer_call(xp_blocks, mod, CTX_ROW, None, None, None, consts,
                                    seq_len=seq, emit_kv=True)

    yp, ys = _ffn_call(xp1.reshape(-1, D_MODEL), xs1.reshape(-1, D_MODEL), mod, CTX_ROW, dec_seq,
                       nffn, w_gate_up[0], w_down[0])

    return (yp.reshape(n_prompt, seq, D_MODEL),
            ys.reshape(n_sample, dec_seq, D_MODEL),
            jnp.transpose(k_new, (0, 1, 4, 2, 3)),
            jnp.transpose(v_new, (0, 1, 4, 2, 3)))
```

```python
import functools

import numpy as np
import jax
import jax.numpy as jnp
from jax import lax
from jax.experimental import pallas as pl
from jax.experimental.pallas import tpu as pltpu

D_MODEL = 1024
HEAD_DIM = 64
ATTN_DIM = 512
N_HEADS = 8
N_KV = 2
KV_GROUP = N_HEADS // N_KV
KV_DIM = N_KV * HEAD_DIM
CONV_DIM = 512
D_FF = 2816
QK_DIM = ATTN_DIM + KV_DIM
QKV_DIM = ATTN_DIM + 2 * KV_DIM
IN_DIM = QKV_DIM + 3 * CONV_DIM
GRID_W = 64
ROT_PAIRS = HEAD_DIM // 4
ROPE_THETA = 10000.0
RMS_EPS = 1e-6
LOG2_E = 1.4426950408889634
Q_SCALE = HEAD_DIM ** -0.5 * LOG2_E
V_ROWS = HEAD_DIM + 16
MAX_UNSHIFTED_SCORE = 64.0

LANES = 128
MXU_DIM = 256
BLOCK_ROWS = 1024
CHUNK_ROWS = 512
LATENT_Q_ROWS = 256
P_ROWS = 16
FFN_ROWS = 512
GU_CAST_ROWS = 128
DOWN_CAST_ROWS = 352
FF_CHUNKS = ((0, 1024), (1024, 1024), (2048, 768))
COND_ROWS = 16
MOD_ROWS = 8
CTX_ROW = 8
ADA_COLS = 2048
VMEM_TEMP_BYTES = 8 * 1024 * 1024
VMEM_MAX_BYTES = 56 * 1024 * 1024

F32 = jnp.float32
BF16 = jnp.bfloat16


def _vmem_limit(windows, scratch=()):
    total = VMEM_TEMP_BYTES
    for spec, dtype in windows:
        if spec.block_shape is None:
            continue
        n = int(np.prod([1 if d is None else d for d in spec.block_shape]))
        total += n * jnp.dtype(dtype).itemsize * (1 if spec.pipeline_mode is not None else 2)
    for s in scratch:
        if s.memory_space == pltpu.VMEM:
            total += int(np.prod(s.shape)) * jnp.dtype(s.dtype).itemsize
    return min(total, VMEM_MAX_BYTES)


def _mod_spec(row, half):
    blk = 0 if row is None else row // MOD_ROWS
    return pl.BlockSpec((MOD_ROWS, 3 * D_MODEL), lambda i: (blk, half))


def _mod_row(mod_ref, row):
    r = pl.program_id(0) if row is None else row % MOD_ROWS
    return mod_ref[pl.ds(r, 1), :]


def _const_spec(shape):
    nd = len(shape)
    return pl.BlockSpec(shape, lambda *_: (0,) * nd, pipeline_mode=pl.Buffered(1))


def _ada_kernel(cond_ref, w_ref, b_ref, out_ref):
    c = cond_ref[...]
    s = (c * jax.nn.sigmoid(c)).astype(BF16)
    out_ref[...] = jnp.dot(s, w_ref[...].astype(BF16), preferred_element_type=F32) + b_ref[...]


def _ada_call(cond, w_ada, b_ada):
    n = w_ada.shape[1]
    in_specs = [
        pl.BlockSpec((COND_ROWS, D_MODEL), lambda j: (0, 0)),
        pl.BlockSpec((D_MODEL, ADA_COLS), lambda j: (0, j)),
        pl.BlockSpec((1, ADA_COLS), lambda j: (0, j)),
    ]
    out_spec = pl.BlockSpec((COND_ROWS, ADA_COLS), lambda j: (0, j))
    return pl.pallas_call(
        _ada_kernel,
        grid=(n // ADA_COLS,),
        in_specs=in_specs,
        out_specs=out_spec,
        out_shape=jax.ShapeDtypeStruct((COND_ROWS, n), F32),
        compiler_params=pltpu.CompilerParams(
            dimension_semantics=("arbitrary",),
            vmem_limit_bytes=_vmem_limit([(s, F32) for s in in_specs + [out_spec]])),
        name="ada_rows",
    )(cond, w_ada, b_ada)


def _rms_rows(x):
    return x * lax.rsqrt(jnp.mean(x * x, axis=-1, keepdims=True) + RMS_EPS)


def _mixer_kernel(*refs, seq_len, ctx_len, use_rope, emit_kv, mod_row):
    it = iter(refs)
    x_ref, mod_ref = next(it), next(it)
    if ctx_len:
        ck_ref, cv_ref = next(it), next(it)
    if use_rope:
        cos_ref, sin_ref = next(it), next(it)
    (nmix_ref, win_ref, gsum_ref, qkg_ref, convw_ref, ga_ref, gc_ref, wout_ref) = (
        next(it) for _ in range(8))
    out_ref = next(it)
    if emit_kv:
        ko_ref, vo_ref = next(it), next(it)
    (q_s, k_s, vt_s, attn_s, t_s, gb_s, s_ref, p_ref) = (next(it) for _ in range(8))

    n_chunks = BLOCK_ROWS // CHUNK_ROWS
    if ctx_len:
        q_rows, n_keys, key_blk = LATENT_Q_ROWS, ctx_len + seq_len, BLOCK_ROWS
    else:
        q_rows, n_keys, key_blk = seq_len, seq_len, seq_len
    n_qb = BLOCK_ROWS // q_rows

    mod = _mod_row(mod_ref, mod_row)
    shift = mod[:, 0:D_MODEL]
    scale1 = 1.0 + mod[:, D_MODEL:2 * D_MODEL]
    gate = mod[:, 2 * D_MODEL:3 * D_MODEL]

    def put_values_t(blk, off, vt):
        vt = vt.astype(BF16)
        for g in range(N_KV):
            vt_s[blk, g, 0:HEAD_DIM, off:off + vt.shape[1]] = vt[g * HEAD_DIM:(g + 1) * HEAD_DIM, :]

    tail = (lax.broadcasted_iota(jnp.int32, (V_ROWS - HEAD_DIM, n_keys), 0) == 0).astype(BF16)
    for blk in range(vt_s.shape[0]):
        for g in range(N_KV):
            vt_s[blk, g, HEAD_DIM:V_ROWS, :] = tail

    if ctx_len:
        ck = ck_ref[...].reshape(KV_DIM, ctx_len).T.astype(BF16)
        for g in range(N_KV):
            k_s[g, 0:ctx_len, :] = ck[:, g * HEAD_DIM:(g + 1) * HEAD_DIM]
        put_values_t(0, 0, cv_ref[...].reshape(KV_DIM, ctx_len))

    if use_rope:
        lane = lax.broadcasted_iota(jnp.int32, (CHUNK_ROWS, LANES), 1)
        first_half = (lane % (2 * ROT_PAIRS)) < ROT_PAIRS

    t_s[0:8, :] = jnp.zeros((8, CONV_DIM), F32)
    t_s[8 + BLOCK_ROWS:16 + BLOCK_ROWS, :] = jnp.zeros((8, CONV_DIM), F32)
    w0 = convw_ref[0:1, :]
    w1 = convw_ref[1:2, :]
    w2 = convw_ref[2:3, :]

    def conv_rows(a, n):
        slab = t_s[a:a + n + 16, :]
        t_prev = pltpu.roll(slab, 1, axis=0)[8:8 + n, :]
        t_mid = slab[8:8 + n, :]
        t_next = pltpu.roll(slab, n + 15, axis=0)[8:8 + n, :]
        if seq_len < BLOCK_ROWS:
            pos = (lax.broadcasted_iota(jnp.int32, (n, 1), 0) + a) % seq_len
            t_prev = jnp.where(pos == 0, 0.0, t_prev)
            t_next = jnp.where(pos == seq_len - 1, 0.0, t_next)
        y = gb_s[a:a + n, :] * (w0 * t_prev + w1 * t_mid + w2 * t_next)
        return (_rms_rows(y) * gc_ref[...]).astype(BF16)

    def project_conv(r0, hb):
        cvp = jnp.dot(hb, win_ref[:, QKV_DIM:IN_DIM], preferred_element_type=F32)
        gb_s[r0:r0 + CHUNK_ROWS, :] = cvp[:, 0:CONV_DIM]
        t_s[8 + r0:8 + r0 + CHUNK_ROWS, :] = (cvp[:, CONV_DIM:2 * CONV_DIM]
                                              * cvp[:, 2 * CONV_DIM:3 * CONV_DIM])

    for c in range(n_chunks):
        r0 = c * CHUNK_ROWS
        x = x_ref[r0:r0 + CHUNK_ROWS, :]
        h = (_rms_rows(x) * nmix_ref[...]) * scale1 + shift
        hb = h.astype(BF16)
        qkv = jnp.dot(hb, win_ref[:, 0:QKV_DIM], preferred_element_type=F32)

        groups = []
        for g0 in range(0, QKV_DIM, MXU_DIM):
            sq = qkv[:, g0:g0 + MXU_DIM]
            groups.append(jnp.dot((sq * sq).astype(BF16), gsum_ref[...],
                                  preferred_element_type=F32))
        ms = jnp.concatenate(groups, axis=-1)[:, 0:QK_DIM]
        qk = (qkv[:, 0:QK_DIM] * lax.rsqrt(ms + RMS_EPS)) * qkg_ref[...]
        vv = qkv[:, QK_DIM:QKV_DIM]

        if emit_kv:
            for r1 in range(0, CHUNK_ROWS, seq_len):
                kt = qk[r1:r1 + seq_len, ATTN_DIM:QK_DIM].T
                ko_ref[(r0 + r1) // seq_len] = kt.reshape(N_KV, HEAD_DIM, seq_len)

        for cg in range(QK_DIM // LANES):
            xg = qk[:, cg * LANES:(cg + 1) * LANES]
            if use_rope:
                cs = cos_ref[r0:r0 + CHUNK_ROWS, :]
                sn = sin_ref[r0:r0 + CHUNK_ROWS, :]
                partner = jnp.where(first_half,
                                    pltpu.roll(xg, LANES - ROT_PAIRS, axis=1),
                                    pltpu.roll(xg, ROT_PAIRS, axis=1))
                xg = xg * cs + partner * sn
            if cg < ATTN_DIM // LANES:
                xb = (xg * Q_SCALE).astype(BF16)
                q_s[2 * cg, r0:r0 + CHUNK_ROWS, :] = xb[:, 0:HEAD_DIM]
                q_s[2 * cg + 1, r0:r0 + CHUNK_ROWS, :] = xb[:, HEAD_DIM:LANES]
            else:
                xb = xg.astype(BF16)
                for g in range(N_KV):
                    k_s[g, ctx_len + r0:ctx_len + r0 + CHUNK_ROWS, :] = (
                        xb[:, g * HEAD_DIM:(g + 1) * HEAD_DIM])
        w = min(key_blk, CHUNK_ROWS)
        for r1 in range(r0, r0 + CHUNK_ROWS, w):
            blk, off = (0, ctx_len + r1) if ctx_len else (r1 // key_blk, 0)
            vt = vv[r1 - r0:r1 - r0 + w, :].T
            put_values_t(blk, off, vt)
            if emit_kv:
                vo_ref[blk] = vt.reshape(N_KV, HEAD_DIM, seq_len)

        project_conv(r0, hb)

    def scores_t(qb, g):
        r0 = pl.multiple_of(qb * q_rows, q_rows)
        k0 = 0 if ctx_len else r0
        qs = jnp.concatenate(
            [q_s[KV_GROUP * g + j, pl.ds(r0, q_rows), :] for j in range(KV_GROUP)], axis=0)
        kk = k_s[g, pl.ds(k0, n_keys), :]
        return lax.dot_general(kk, qs, (((1,), (1,)), ((), ())),
                               preferred_element_type=F32)

    def values_out(qb, g, p):
        r0 = pl.multiple_of(qb * q_rows, q_rows)
        blk = 0 if ctx_len else qb
        ot = jnp.dot(vt_s[blk, g], p, preferred_element_type=F32)
        ot = ot[0:HEAD_DIM, :] / ot[HEAD_DIM:HEAD_DIM + 1, :]
        for jj in range(KV_GROUP // 2):
            pair_t = jnp.concatenate(
                [ot[:, (2 * jj) * q_rows:(2 * jj + 1) * q_rows],
                 ot[:, (2 * jj + 1) * q_rows:(2 * jj + 2) * q_rows]], axis=0)
            col = (KV_GROUP * g + 2 * jj) * HEAD_DIM
            attn_s[pl.ds(r0, q_rows), col:col + LANES] = pair_t.T

    def pipeline_pairs(step):
        assert n_qb % 2 == 0
        step(-1, 1, first=True)
        step(0, 0)

        def body(t, carry):
            step(2 * t + 1, 1)
            step(2 * t + 2, 0)
            return carry

        lax.fori_loop(0, (n_qb - 2) // 2, body, 0)
        step(n_qb - 1, 1, last=True)

    def shifted_block(qb, carry):
        for g in range(N_KV):
            s = scores_t(qb, g)
            s_ref[...] = s
            m = jnp.max(s, axis=0, keepdims=True)
            for k1 in range(0, n_keys, P_ROWS):
                p_ref[0, g, k1:k1 + P_ROWS, :] = jnp.exp2(
                    s_ref[k1:k1 + P_ROWS, :] - m).astype(BF16)
            values_out(qb, g, p_ref[0, g])
        return carry

    def unshifted_step(j, par, first=False, last=False):
        for g in range(N_KV):
            if not last:
                p_ref[1 - par, g] = jnp.exp2(scores_t(j + 1, g)).astype(BF16)
            if not first:
                values_out(j, g, p_ref[par, g])

    gains = jnp.abs(qkg_ref[...])
    q_bound = (Q_SCALE * Q_SCALE * HEAD_DIM) * jnp.max(gains[:, 0:ATTN_DIM]) ** 2
    k_bound = HEAD_DIM * jnp.max(gains[:, ATTN_DIM:QK_DIM]) ** 2
    if ctx_len:
        ck2 = ck_ref[...]
        k_bound = jnp.maximum(k_bound, jnp.max(jnp.sum(ck2 * ck2, axis=1)))
    small_scores = q_bound * k_bound <= MAX_UNSHIFTED_SCORE ** 2
    pl.when(small_scores)(lambda: pipeline_pairs(unshifted_step))

    @pl.when(jnp.logical_not(small_scores))
    def _():
        lax.fori_loop(0, n_qb, shifted_block, 0)

    for r0 in range(0, BLOCK_ROWS, CHUNK_ROWS):
        rows = slice(r0, r0 + CHUNK_ROWS)
        an = (_rms_rows(attn_s[rows, :]) * ga_ref[...]).astype(BF16)
        merged = jnp.concatenate([an, conv_rows(r0, CHUNK_ROWS)], axis=-1)
        mix = jnp.dot(merged, wout_ref[...], preferred_element_type=F32)
        out_ref[rows, :] = x_ref[rows, :] + gate * mix


def _mixer_call(x_blocks, mod, mod_row, ctx_k, ctx_v, rope, consts, *, seq_len, emit_kv):
    n_blocks = x_blocks.shape[0]
    ctx_len = 0 if ctx_k is None else ctx_k.shape[4]
    use_rope = rope is not None
    if ctx_len:
        assert seq_len == BLOCK_ROWS
        q_rows, n_keys, n_key_blocks = LATENT_Q_ROWS, ctx_len + BLOCK_ROWS, 1
    else:
        q_rows, n_keys, n_key_blocks = seq_len, seq_len, BLOCK_ROWS // seq_len
    n_q_cols = KV_GROUP * q_rows

    blk = lambda cols: pl.BlockSpec((None, BLOCK_ROWS, cols), lambda b: (b, 0, 0))
    assert mod_row is not None or n_blocks <= MOD_ROWS
    args = [x_blocks, mod]
    in_specs = [blk(D_MODEL), _mod_spec(mod_row, 0)]
    if ctx_len:
        args += [ctx_k, ctx_v]
        in_specs += [pl.BlockSpec((None, None, N_KV, HEAD_DIM, ctx_len),
                                  lambda b: (b, 0, 0, 0, 0))] * 2
    if use_rope:
        args += list(rope)
        in_specs += [_const_spec((BLOCK_ROWS, LANES))] * 2
    args += list(consts)
    in_specs += [_const_spec(a.shape) for a in consts]

    out_shape = [jax.ShapeDtypeStruct((n_blocks, BLOCK_ROWS, D_MODEL), F32)]
    out_specs = [blk(D_MODEL)]
    if emit_kv:
        per_block = BLOCK_ROWS // seq_len
        out_shape += [jax.ShapeDtypeStruct(
            (n_blocks * per_block, 1, N_KV, HEAD_DIM, seq_len), F32)] * 2
        out_specs += [pl.BlockSpec((per_block, None, N_KV, HEAD_DIM, seq_len),
                                   lambda b: (b, 0, 0, 0, 0))] * 2

    scratch = [
        pltpu.VMEM((N_HEADS, BLOCK_ROWS, HEAD_DIM), BF16),
        pltpu.VMEM((N_KV, ctx_len + BLOCK_ROWS, HEAD_DIM), BF16),
        pltpu.VMEM((n_key_blocks, N_KV, V_ROWS, n_keys), BF16),
        pltpu.VMEM((BLOCK_ROWS, ATTN_DIM), F32),
        pltpu.VMEM((BLOCK_ROWS + 16, CONV_DIM), F32),
        pltpu.VMEM((BLOCK_ROWS, CONV_DIM), F32),
        pltpu.VMEM((n_keys, n_q_cols), F32),
        pltpu.VMEM((2, N_KV, n_keys, n_q_cols), BF16),
    ]
    kern = functools.partial(_mixer_kernel, seq_len=seq_len, ctx_len=ctx_len,
                             use_rope=use_rope, emit_kv=emit_kv, mod_row=mod_row)
    return pl.pallas_call(
        kern,
        grid=(n_blocks,),
        in_specs=in_specs,
        out_specs=out_specs,
        out_shape=out_shape,
        scratch_shapes=scratch,
        compiler_params=pltpu.CompilerParams(
            dimension_semantics=("arbitrary",),
            vmem_limit_bytes=_vmem_limit(
                [(s, a.dtype) for s, a in zip(in_specs, args)] + [(s, F32) for s in out_specs],
                scratch)),
        name="mixer_ctx" if emit_kv else "mixer_latent",
    )(*args)


def _ffn_rows(x_ref, mod, nffn_ref, wgu_ref, wd_ref, out_ref):
    shift = mod[:, 0:D_MODEL]
    scale1 = 1.0 + mod[:, D_MODEL:2 * D_MODEL]
    gate = mod[:, 2 * D_MODEL:3 * D_MODEL]
    x = x_ref[...]
    hb = ((_rms_rows(x) * nffn_ref[...]) * scale1 + shift).astype(BF16)
    acc = None
    for c0, cw in FF_CHUNKS:
        gt = jnp.dot(hb, wgu_ref[:, c0:c0 + cw], preferred_element_type=F32)
        up = jnp.dot(hb, wgu_ref[:, D_FF + c0:D_FF + c0 + cw], preferred_element_type=F32)
        act = ((gt * jax.nn.sigmoid(gt)) * up).astype(BF16)
        part = jnp.dot(act, wd_ref[c0:c0 + cw, :], preferred_element_type=F32)
        acc = part if acc is None else acc + part
    out_ref[...] = x + gate * acc


def _cast_rows(src_hbm, dst, stage, sem, chunk_rows):
    n = src_hbm.shape[0] // chunk_rows

    def fetch(c):
        return pltpu.make_async_copy(src_hbm.at[pl.ds(c * chunk_rows, chunk_rows), :],
                                     stage.at[c % 2], sem.at[c % 2])

    fetch(0).start()
    for c in range(n):
        if c + 1 < n:
            fetch(c + 1).start()
        fetch(c).wait()
        dst[c * chunk_rows:(c + 1) * chunk_rows, :] = stage[c % 2].astype(BF16)


def _ffn_kernel(xa_ref, xb_ref, mod_ref, nffn_ref, wgu_hbm, wd_hbm, ya_ref, yb_ref,
                wgu_s, wd_s, stage_gu, stage_d, sem, *, n_a, mod_row_a, tiles_per_mod_row_b):
    i = pl.program_id(0)

    @pl.when(i == 0)
    def _():
        _cast_rows(wgu_hbm, wgu_s, stage_gu, sem, GU_CAST_ROWS)
        _cast_rows(wd_hbm, wd_s, stage_d, sem, DOWN_CAST_ROWS)

    @pl.when(i < n_a)
    def _():
        mod = mod_ref[mod_row_a % MOD_ROWS:mod_row_a % MOD_ROWS + 1, :]
        _ffn_rows(xa_ref, mod, nffn_ref, wgu_s, wd_s, ya_ref)

    @pl.when(i >= n_a)
    def _():
        mod = mod_ref[pl.ds((i - n_a) // tiles_per_mod_row_b, 1), :]
        _ffn_rows(xb_ref, mod, nffn_ref, wgu_s, wd_s, yb_ref)


def _ffn_call(xa_rows, xb_rows, mod, mod_row_a, rows_per_mod_row_b, norm_ffn, w_gate_up, w_down):
    n_a, n_b = xa_rows.shape[0] // FFN_ROWS, xb_rows.shape[0] // FFN_ROWS
    tiles_per_row_b = rows_per_mod_row_b // FFN_ROWS
    assert n_b // tiles_per_row_b <= MOD_ROWS
    spec_a = pl.BlockSpec((FFN_ROWS, D_MODEL), lambda i: (jnp.minimum(i, n_a - 1), 0))
    spec_b = pl.BlockSpec((FFN_ROWS, D_MODEL), lambda i: (jnp.maximum(i - n_a, 0), 0))
    mod_spec = pl.BlockSpec((MOD_ROWS, 3 * D_MODEL),
                            lambda i: (jnp.where(i < n_a, mod_row_a // MOD_ROWS, 0), 1))
    hbm = pl.BlockSpec(memory_space=pl.ANY)
    in_specs = [spec_a, spec_b, mod_spec, _const_spec(norm_ffn.shape), hbm, hbm]
    scratch = [pltpu.VMEM(w_gate_up.shape, BF16), pltpu.VMEM(w_down.shape, BF16),
               pltpu.VMEM((2, GU_CAST_ROWS, w_gate_up.shape[1]), F32),
               pltpu.VMEM((2, DOWN_CAST_ROWS, w_down.shape[1]), F32),
               pltpu.SemaphoreType.DMA((2,))]
    return pl.pallas_call(
        functools.partial(_ffn_kernel, n_a=n_a, mod_row_a=mod_row_a,
                          tiles_per_mod_row_b=tiles_per_row_b),
        grid=(n_a + n_b,),
        in_specs=in_specs,
        out_specs=[spec_a, spec_b],
        out_shape=[jax.ShapeDtypeStruct(xa_rows.shape, F32),
                   jax.ShapeDtypeStruct(xb_rows.shape, F32)],
        scratch_shapes=scratch,
        compiler_params=pltpu.CompilerParams(
            dimension_semantics=("arbitrary",),
            vmem_limit_bytes=_vmem_limit(
                [(s, F32) for s in in_specs + [spec_a, spec_b]], scratch)),
        name="ffn",
    )(xa_rows, xb_rows, mod, norm_ffn, w_gate_up, w_down)


def _rope_tables(n_tokens):
    rows = n_tokens // GRID_W
    row = jnp.repeat(jnp.arange(rows, dtype=F32), GRID_W)
    col = jnp.tile(jnp.arange(GRID_W, dtype=F32), rows)
    inv = 1.0 / (ROPE_THETA ** (jnp.arange(ROT_PAIRS, dtype=F32) / ROT_PAIRS))
    ang = jnp.stack([row[:, None] * inv, col[:, None] * inv], axis=1)
    cos, sin = jnp.cos(ang), jnp.sin(ang)
    cos_h = jnp.concatenate([cos, cos], axis=-1).reshape(n_tokens, HEAD_DIM)
    sin_h = jnp.concatenate([-sin, sin], axis=-1).reshape(n_tokens, HEAD_DIM)
    reps = LANES // HEAD_DIM
    return jnp.tile(cos_h, (1, reps)), jnp.tile(sin_h, (1, reps))


def _keys_minor(kv):
    return jnp.transpose(kv, (0, 1, 3, 4, 2))


def _group_mean_matrix():
    idx = np.arange(MXU_DIM) // HEAD_DIM
    g = (idx[:, None] == idx[None, :]).astype(np.float32) / HEAD_DIM
    return jnp.asarray(g, dtype=BF16)


def kernel(x_prompt, x_sample, c, cache_k, cache_v, c_ctx, norm_mix, norm_ffn, w_ada, b_ada,
           w_in, q_norm, k_norm, conv_w, attn_out_norm, conv_out_norm, w_out, w_gate_up, w_down):
    depth = w_in.shape[0]
    assert depth == 1
    n_prompt, seq, _ = x_prompt.shape
    n_sample, dec_seq, _ = x_sample.shape
    assert dec_seq == BLOCK_ROWS and BLOCK_ROWS % seq == 0 and n_sample <= CTX_ROW

    cond = jnp.zeros((COND_ROWS, D_MODEL), F32)
    cond = cond.at[0:n_sample].set(c).at[CTX_ROW].set(c_ctx)
    mod = _ada_call(cond, w_ada[0], b_ada[0][None, :])

    consts = (
        norm_mix[0][None, :],
        w_in[0].astype(BF16),
        _group_mean_matrix(),
        jnp.concatenate([jnp.tile(q_norm[0], N_HEADS), jnp.tile(k_norm[0], N_KV)])[None, :],
        conv_w[0],
        attn_out_norm[0][None, :],
        conv_out_norm[0][None, :],
        w_out[0].astype(BF16),
    )
    nffn = norm_ffn[0][None, :]

    (xs1,) = _mixer_call(x_sample, mod, None, _keys_minor(cache_k), _keys_minor(cache_v),
                         _rope_tables(dec_seq), consts, seq_len=dec_seq, emit_kv=False)

    xs1, nmix_after = lax.optimization_barrier((xs1, consts[0]))
    per_block = BLOCK_ROWS // seq
    xp_blocks = x_prompt.reshape(n_prompt // per_block, BLOCK_ROWS, D_MODEL)
    xp1, k_new, v_new = _mixer_call(xp_blocks, mod, CTX_ROW, None, None, None,
                                    (nmix_after,) + consts[1:], seq_len=seq, emit_kv=True)

    yp, ys = _ffn_call(xp1.reshape(-1, D_MODEL), xs1.reshape(-1, D_MODEL), mod, CTX_ROW, dec_seq,
                       nffn, w_gate_up[0], w_down[0])

    return (yp.reshape(n_prompt, seq, D_MODEL),
            ys.reshape(n_sample, dec_seq, D_MODEL),
            jnp.transpose(k_new, (0, 1, 4, 2, 3)),
            jnp.transpose(v_new, (0, 1, 4, 2, 3)))
```

```python
import functools

import numpy as np
import jax
import jax.numpy as jnp
from jax import lax
from jax.experimental import pallas as pl
from jax.experimental.pallas import tpu as pltpu

D_MODEL = 1024
HEAD_DIM = 64
ATTN_DIM = 512
N_HEADS = 8
N_KV = 2
KV_GROUP = N_HEADS // N_KV
KV_DIM = N_KV * HEAD_DIM
CONV_DIM = 512
D_FF = 2816
QK_DIM = ATTN_DIM + KV_DIM
QKV_DIM = ATTN_DIM + 2 * KV_DIM
IN_DIM = QKV_DIM + 3 * CONV_DIM
GRID_W = 64
ROT_PAIRS = HEAD_DIM // 4
ROPE_THETA = 10000.0
RMS_EPS = 1e-6
LOG2_E = 1.4426950408889634
Q_SCALE = HEAD_DIM ** -0.5 * LOG2_E
V_ROWS = HEAD_DIM + 16
MAX_UNSHIFTED_SCORE = 64.0

LANES = 128
MXU_DIM = 256
BLOCK_ROWS = 1024
CHUNK_ROWS = 512
LATENT_Q_ROWS = 256
P_ROWS = 16
FFN_ROWS = 512
FF_CHUNKS = ((0, 1024), (1024, 1024), (2048, 768))
COND_ROWS = 16
MOD_ROWS = 8
CTX_ROW = 8
ADA_COLS = 2048
VMEM_TEMP_BYTES = 10 * 1024 * 1024
VMEM_MAX_BYTES = 60 * 1024 * 1024

F32 = jnp.float32
BF16 = jnp.bfloat16


def _vmem_limit(windows, scratch=()):
    total = VMEM_TEMP_BYTES
    for spec, dtype in windows:
        if spec.block_shape is None:
            continue
        n = int(np.prod([1 if d is None else d for d in spec.block_shape]))
        total += n * jnp.dtype(dtype).itemsize * (1 if spec.pipeline_mode is not None else 2)
    for s in scratch:
        if s.memory_space == pltpu.VMEM:
            total += int(np.prod(s.shape)) * jnp.dtype(s.dtype).itemsize
    return min(total, VMEM_MAX_BYTES)


def _mod_spec(row, half):
    blk = 0 if row is None else row // MOD_ROWS
    return pl.BlockSpec((MOD_ROWS, 3 * D_MODEL), lambda i: (blk, half))


def _mod_row(mod_ref, row):
    r = pl.program_id(0) if row is None else row % MOD_ROWS
    return mod_ref[pl.ds(r, 1), :]


def _const_spec(shape):
    nd = len(shape)
    return pl.BlockSpec(shape, lambda *_: (0,) * nd, pipeline_mode=pl.Buffered(1))


def _ada_kernel(cond_ref, w_ref, b_ref, out_ref):
    c = cond_ref[...]
    s = (c * jax.nn.sigmoid(c)).astype(BF16)
    out_ref[...] = jnp.dot(s, w_ref[...].astype(BF16), preferred_element_type=F32) + b_ref[...]


def _ada_call(cond, w_ada, b_ada):
    n = w_ada.shape[1]
    in_specs = [
        pl.BlockSpec((COND_ROWS, D_MODEL), lambda j: (0, 0)),
        pl.BlockSpec((D_MODEL, ADA_COLS), lambda j: (0, j)),
        pl.BlockSpec((1, ADA_COLS), lambda j: (0, j)),
    ]
    out_spec = pl.BlockSpec((COND_ROWS, ADA_COLS), lambda j: (0, j))
    return pl.pallas_call(
        _ada_kernel,
        grid=(n // ADA_COLS,),
        in_specs=in_specs,
        out_specs=out_spec,
        out_shape=jax.ShapeDtypeStruct((COND_ROWS, n), F32),
        compiler_params=pltpu.CompilerParams(
            dimension_semantics=("arbitrary",),
            vmem_limit_bytes=_vmem_limit([(s, F32) for s in in_specs + [out_spec]])),
        name="ada_rows",
    )(cond, w_ada, b_ada)


def _rms_rows(x):
    return x * lax.rsqrt(jnp.mean(x * x, axis=-1, keepdims=True) + RMS_EPS)


def _mixer_kernel(*refs, seq_len, ctx_len, use_rope, emit_kv, mod_row, n_side):
    it = iter(refs)
    x_ref, mod_ref = next(it), next(it)
    if ctx_len:
        ck_ref, cv_ref = next(it), next(it)
    if use_rope:
        cos_ref, sin_ref = next(it), next(it)
    (nmix_ref, win_ref, gsum_ref, qkg_ref, convw_ref, ga_ref, gc_ref, wout_ref) = (
        next(it) for _ in range(8))
    side_in = [next(it) for _ in range(n_side)]
    out_ref = next(it)
    if emit_kv:
        ko_ref, vo_ref = next(it), next(it)
    side_out = [next(it) for _ in range(n_side)]
    (q_s, k_s, vt_s, attn_s, t_s, gb_s, s_ref, p_ref) = (next(it) for _ in range(8))

    for src, dst in zip(side_in, side_out):
        dst[...] = src[...].astype(BF16)

    n_chunks = BLOCK_ROWS // CHUNK_ROWS
    if ctx_len:
        q_rows, n_keys, key_blk = LATENT_Q_ROWS, ctx_len + seq_len, BLOCK_ROWS
    else:
        q_rows, n_keys, key_blk = seq_len, seq_len, seq_len
    n_qb = BLOCK_ROWS // q_rows

    mod = _mod_row(mod_ref, mod_row)
    shift = mod[:, 0:D_MODEL]
    scale1 = 1.0 + mod[:, D_MODEL:2 * D_MODEL]
    gate = mod[:, 2 * D_MODEL:3 * D_MODEL]

    def put_values_t(blk, off, vt):
        vt = vt.astype(BF16)
        for g in range(N_KV):
            vt_s[blk, g, 0:HEAD_DIM, off:off + vt.shape[1]] = vt[g * HEAD_DIM:(g + 1) * HEAD_DIM, :]

    tail = (lax.broadcasted_iota(jnp.int32, (V_ROWS - HEAD_DIM, n_keys), 0) == 0).astype(BF16)
    for blk in range(vt_s.shape[0]):
        for g in range(N_KV):
            vt_s[blk, g, HEAD_DIM:V_ROWS, :] = tail

    if ctx_len:
        ck = ck_ref[...].reshape(KV_DIM, ctx_len).T.astype(BF16)
        for g in range(N_KV):
            k_s[g, 0:ctx_len, :] = ck[:, g * HEAD_DIM:(g + 1) * HEAD_DIM]
        put_values_t(0, 0, cv_ref[...].reshape(KV_DIM, ctx_len))

    if use_rope:
        lane = lax.broadcasted_iota(jnp.int32, (CHUNK_ROWS, LANES), 1)
        first_half = (lane % (2 * ROT_PAIRS)) < ROT_PAIRS

    t_s[0:8, :] = jnp.zeros((8, CONV_DIM), F32)
    t_s[8 + BLOCK_ROWS:16 + BLOCK_ROWS, :] = jnp.zeros((8, CONV_DIM), F32)
    w0 = convw_ref[0:1, :]
    w1 = convw_ref[1:2, :]
    w2 = convw_ref[2:3, :]

    def conv_rows(a, n):
        slab = t_s[a:a + n + 16, :]
        t_prev = pltpu.roll(slab, 1, axis=0)[8:8 + n, :]
        t_mid = slab[8:8 + n, :]
        t_next = pltpu.roll(slab, n + 15, axis=0)[8:8 + n, :]
        if seq_len < BLOCK_ROWS:
            pos = (lax.broadcasted_iota(jnp.int32, (n, 1), 0) + a) % seq_len
            t_prev = jnp.where(pos == 0, 0.0, t_prev)
            t_next = jnp.where(pos == seq_len - 1, 0.0, t_next)
        y = gb_s[a:a + n, :] * (w0 * t_prev + w1 * t_mid + w2 * t_next)
        return (_rms_rows(y) * gc_ref[...]).astype(BF16)

    def project_conv(r0, hb):
        cvp = jnp.dot(hb, win_ref[:, QKV_DIM:IN_DIM], preferred_element_type=F32)
        gb_s[r0:r0 + CHUNK_ROWS, :] = cvp[:, 0:CONV_DIM]
        t_s[8 + r0:8 + r0 + CHUNK_ROWS, :] = (cvp[:, CONV_DIM:2 * CONV_DIM]
                                              * cvp[:, 2 * CONV_DIM:3 * CONV_DIM])

    for c in range(n_chunks):
        r0 = c * CHUNK_ROWS
        x = x_ref[r0:r0 + CHUNK_ROWS, :]
        h = (_rms_rows(x) * nmix_ref[...]) * scale1 + shift
        hb = h.astype(BF16)
        qkv = jnp.dot(hb, win_ref[:, 0:QKV_DIM], preferred_element_type=F32)

        groups = []
        for g0 in range(0, QKV_DIM, MXU_DIM):
            sq = qkv[:, g0:g0 + MXU_DIM]
            groups.append(jnp.dot((sq * sq).astype(BF16), gsum_ref[...],
                                  preferred_element_type=F32))
        ms = jnp.concatenate(groups, axis=-1)[:, 0:QK_DIM]
        qk = (qkv[:, 0:QK_DIM] * lax.rsqrt(ms + RMS_EPS)) * qkg_ref[...]
        vv = qkv[:, QK_DIM:QKV_DIM]

        if emit_kv:
            for r1 in range(0, CHUNK_ROWS, seq_len):
                kt = qk[r1:r1 + seq_len, ATTN_DIM:QK_DIM].T
                ko_ref[(r0 + r1) // seq_len] = kt.reshape(N_KV, HEAD_DIM, seq_len)

        for cg in range(QK_DIM // LANES):
            xg = qk[:, cg * LANES:(cg + 1) * LANES]
            if use_rope:
                cs = cos_ref[r0:r0 + CHUNK_ROWS, :]
                sn = sin_ref[r0:r0 + CHUNK_ROWS, :]
                partner = jnp.where(first_half,
                                    pltpu.roll(xg, LANES - ROT_PAIRS, axis=1),
                                    pltpu.roll(xg, ROT_PAIRS, axis=1))
                xg = xg * cs + partner * sn
            if cg < ATTN_DIM // LANES:
                xb = (xg * Q_SCALE).astype(BF16)
                q_s[2 * cg, r0:r0 + CHUNK_ROWS, :] = xb[:, 0:HEAD_DIM]
                q_s[2 * cg + 1, r0:r0 + CHUNK_ROWS, :] = xb[:, HEAD_DIM:LANES]
            else:
                xb = xg.astype(BF16)
                for g in range(N_KV):
                    k_s[g, ctx_len + r0:ctx_len + r0 + CHUNK_ROWS, :] = (
                        xb[:, g * HEAD_DIM:(g + 1) * HEAD_DIM])
        w = min(key_blk, CHUNK_ROWS)
        for r1 in range(r0, r0 + CHUNK_ROWS, w):
            blk, off = (0, ctx_len + r1) if ctx_len else (r1 // key_blk, 0)
            vt = vv[r1 - r0:r1 - r0 + w, :].T
            put_values_t(blk, off, vt)
            if emit_kv:
                vo_ref[blk] = vt.reshape(N_KV, HEAD_DIM, seq_len)

        project_conv(r0, hb)

    def scores_t(qb, g):
        r0 = pl.multiple_of(qb * q_rows, q_rows)
        k0 = 0 if ctx_len else r0
        qs = jnp.concatenate(
            [q_s[KV_GROUP * g + j, pl.ds(r0, q_rows), :] for j in range(KV_GROUP)], axis=0)
        kk = k_s[g, pl.ds(k0, n_keys), :]
        return lax.dot_general(kk, qs, (((1,), (1,)), ((), ())),
                               preferred_element_type=F32)

    def values_out(qb, g, p):
        r0 = pl.multiple_of(qb * q_rows, q_rows)
        blk = 0 if ctx_len else qb
        ot = jnp.dot(vt_s[blk, g], p, preferred_element_type=F32)
        ot = ot[0:HEAD_DIM, :] / ot[HEAD_DIM:HEAD_DIM + 1, :]
        for jj in range(KV_GROUP // 2):
            pair_t = jnp.concatenate(
                [ot[:, (2 * jj) * q_rows:(2 * jj + 1) * q_rows],
                 ot[:, (2 * jj + 1) * q_rows:(2 * jj + 2) * q_rows]], axis=0)
            col = (KV_GROUP * g + 2 * jj) * HEAD_DIM
            attn_s[pl.ds(r0, q_rows), col:col + LANES] = pair_t.T

    def pipeline_pairs(step):
        assert n_qb % 2 == 0
        step(-1, 1, first=True)
        step(0, 0)

        def body(t, carry):
            step(2 * t + 1, 1)
            step(2 * t + 2, 0)
            return carry

        lax.fori_loop(0, (n_qb - 2) // 2, body, 0)
        step(n_qb - 1, 1, last=True)

    def shifted_block(qb, carry):
        for g in range(N_KV):
            s = scores_t(qb, g)
            s_ref[...] = s
            m = jnp.max(s, axis=0, keepdims=True)
            for k1 in range(0, n_keys, P_ROWS):
                p_ref[0, g, k1:k1 + P_ROWS, :] = jnp.exp2(
                    s_ref[k1:k1 + P_ROWS, :] - m).astype(BF16)
            values_out(qb, g, p_ref[0, g])
        return carry

    def unshifted_step(j, par, first=False, last=False):
        for g in range(N_KV):
            if not last:
                p_ref[1 - par, g] = jnp.exp2(scores_t(j + 1, g)).astype(BF16)
            if not first:
                values_out(j, g, p_ref[par, g])

    gains = jnp.abs(qkg_ref[...])
    q_bound = (Q_SCALE * Q_SCALE * HEAD_DIM) * jnp.max(gains[:, 0:ATTN_DIM]) ** 2
    k_bound = HEAD_DIM * jnp.max(gains[:, ATTN_DIM:QK_DIM]) ** 2
    if ctx_len:
        ck2 = ck_ref[...]
        k_bound = jnp.maximum(k_bound, jnp.max(jnp.sum(ck2 * ck2, axis=1)))
    small_scores = q_bound * k_bound <= MAX_UNSHIFTED_SCORE ** 2
    pl.when(small_scores)(lambda: pipeline_pairs(unshifted_step))

    @pl.when(jnp.logical_not(small_scores))
    def _():
        lax.fori_loop(0, n_qb, shifted_block, 0)

    for r0 in range(0, BLOCK_ROWS, CHUNK_ROWS):
        rows = slice(r0, r0 + CHUNK_ROWS)
        an = (_rms_rows(attn_s[rows, :]) * ga_ref[...]).astype(BF16)
        merged = jnp.concatenate([an, conv_rows(r0, CHUNK_ROWS)], axis=-1)
        mix = jnp.dot(merged, wout_ref[...], preferred_element_type=F32)
        out_ref[rows, :] = x_ref[rows, :] + gate * mix


def _mixer_call(x_blocks, mod, mod_row, ctx_k, ctx_v, rope, consts, *, seq_len, emit_kv, side=()):
    n_blocks = x_blocks.shape[0]
    ctx_len = 0 if ctx_k is None else ctx_k.shape[4]
    use_rope = rope is not None
    if ctx_len:
        assert seq_len == BLOCK_ROWS
        q_rows, n_keys, n_key_blocks = LATENT_Q_ROWS, ctx_len + BLOCK_ROWS, 1
    else:
        q_rows, n_keys, n_key_blocks = seq_len, seq_len, BLOCK_ROWS // seq_len
    n_q_cols = KV_GROUP * q_rows

    blk = lambda cols: pl.BlockSpec((None, BLOCK_ROWS, cols), lambda b: (b, 0, 0))
    assert mod_row is not None or n_blocks <= MOD_ROWS
    args = [x_blocks, mod]
    in_specs = [blk(D_MODEL), _mod_spec(mod_row, 0)]
    if ctx_len:
        args += [ctx_k, ctx_v]
        in_specs += [pl.BlockSpec((None, None, N_KV, HEAD_DIM, ctx_len),
                                  lambda b: (b, 0, 0, 0, 0))] * 2
    if use_rope:
        args += list(rope)
        in_specs += [_const_spec((BLOCK_ROWS, LANES))] * 2
    args += list(consts)
    in_specs += [_const_spec(a.shape) for a in consts]
    side_specs, side_shapes = [], []
    for w, col_blocks, col_block in side:
        rows, cols = w.shape[0] // n_blocks, w.shape[1] // col_blocks
        assert rows * n_blocks == w.shape[0] and cols * col_blocks == w.shape[1]
        side_specs.append(pl.BlockSpec((rows, cols), lambda b, col_block=col_block: (b, col_block)))
        side_shapes.append(jax.ShapeDtypeStruct((w.shape[0], cols), BF16))
    args += [w for w, _, _ in side]
    in_specs += side_specs

    out_shape = [jax.ShapeDtypeStruct((n_blocks, BLOCK_ROWS, D_MODEL), F32)]
    out_specs = [blk(D_MODEL)]
    if emit_kv:
        per_block = BLOCK_ROWS // seq_len
        out_shape += [jax.ShapeDtypeStruct(
            (n_blocks * per_block, 1, N_KV, HEAD_DIM, seq_len), F32)] * 2
        out_specs += [pl.BlockSpec((per_block, None, N_KV, HEAD_DIM, seq_len),
                                   lambda b: (b, 0, 0, 0, 0))] * 2
    out_shape += side_shapes
    out_specs += [pl.BlockSpec(s.block_shape, lambda b: (b, 0)) for s in side_specs]

    scratch = [
        pltpu.VMEM((N_HEADS, BLOCK_ROWS, HEAD_DIM), BF16),
        pltpu.VMEM((N_KV, ctx_len + BLOCK_ROWS, HEAD_DIM), BF16),
        pltpu.VMEM((n_key_blocks, N_KV, V_ROWS, n_keys), BF16),
        pltpu.VMEM((BLOCK_ROWS, ATTN_DIM), F32),
        pltpu.VMEM((BLOCK_ROWS + 16, CONV_DIM), F32),
        pltpu.VMEM((BLOCK_ROWS, CONV_DIM), F32),
        pltpu.VMEM((n_keys, n_q_cols), F32),
        pltpu.VMEM((2, N_KV, n_keys, n_q_cols), BF16),
    ]
    kern = functools.partial(_mixer_kernel, seq_len=seq_len, ctx_len=ctx_len,
                             use_rope=use_rope, emit_kv=emit_kv, mod_row=mod_row,
                             n_side=len(side))
    return pl.pallas_call(
        kern,
        grid=(n_blocks,),
        in_specs=in_specs,
        out_specs=out_specs,
        out_shape=out_shape,
        scratch_shapes=scratch,
        compiler_params=pltpu.CompilerParams(
            dimension_semantics=("arbitrary",),
            vmem_limit_bytes=_vmem_limit(
                [(s, a.dtype) for s, a in zip(in_specs, args)]
                + [(s, o.dtype) for s, o in zip(out_specs, out_shape)],
                scratch)),
        name="mixer_ctx" if emit_kv else "mixer_latent",
    )(*args)


def _ffn_rows(x_ref, mod, nffn_ref, wg_ref, wu_ref, wd_ref, out_ref, wait=None):
    shift = mod[:, 0:D_MODEL]
    scale1 = 1.0 + mod[:, D_MODEL:2 * D_MODEL]
    gate = mod[:, 2 * D_MODEL:3 * D_MODEL]
    x = x_ref[...]
    hb = ((_rms_rows(x) * nffn_ref[...]) * scale1 + shift).astype(BF16)
    acc = None
    for c, (c0, cw) in enumerate(FF_CHUNKS):
        if wait is not None:
            wait(c, 0)
            wait(c, 1)
        gt = jnp.dot(hb, wg_ref[:, c0:c0 + cw], preferred_element_type=F32)
        up = jnp.dot(hb, wu_ref[:, c0:c0 + cw], preferred_element_type=F32)
        act = ((gt * jax.nn.sigmoid(gt)) * up).astype(BF16)
        if wait is not None:
            wait(c, 2)
        part = jnp.dot(act, wd_ref[c0:c0 + cw, :], preferred_element_type=F32)
        acc = part if acc is None else acc + part
    out_ref[...] = x + gate * acc


def _weight_copies(c, w_hbm, w_s, sem):
    c0, cw = FF_CHUNKS[c]
    cols, rows = (slice(None), pl.ds(c0, cw)), (pl.ds(c0, cw), slice(None))
    return [pltpu.make_async_copy(h.at[idx], s.at[idx], sem.at[c, j])
            for j, (h, s, idx) in enumerate(zip(w_hbm, w_s, (cols, cols, rows)))]


def _ffn_kernel(xa_ref, xb_ref, mod_ref, nffn_ref, wg_hbm, wu_hbm, wd_hbm, ya_ref, yb_ref,
                wg_s, wu_s, wd_s, sem, *, n_a, mod_row_a, tiles_per_mod_row_b):
    i = pl.program_id(0)
    w_s = (wg_s, wu_s, wd_s)
    copies = [_weight_copies(c, (wg_hbm, wu_hbm, wd_hbm), w_s, sem)
              for c in range(len(FF_CHUNKS))]

    def rows_a(wait=None):
        mod = mod_ref[mod_row_a % MOD_ROWS:mod_row_a % MOD_ROWS + 1, :]
        _ffn_rows(xa_ref, mod, nffn_ref, *w_s, ya_ref, wait=wait)

    @pl.when(i == 0)
    def _():
        for chunk in copies:
            for copy in chunk:
                copy.start()
        rows_a(wait=lambda c, j: copies[c][j].wait())

    @pl.when((i > 0) & (i < n_a))
    def _():
        rows_a()

    @pl.when(i >= n_a)
    def _():
        mod = mod_ref[pl.ds((i - n_a) // tiles_per_mod_row_b, 1), :]
        _ffn_rows(xb_ref, mod, nffn_ref, *w_s, yb_ref)


def _ffn_call(xa_rows, xb_rows, mod, mod_row_a, rows_per_mod_row_b, norm_ffn, w_gate, w_up, w_down):
    n_a, n_b = xa_rows.shape[0] // FFN_ROWS, xb_rows.shape[0] // FFN_ROWS
    tiles_per_row_b = rows_per_mod_row_b // FFN_ROWS
    assert n_a > 0 and n_b // tiles_per_row_b <= MOD_ROWS
    spec_a = pl.BlockSpec((FFN_ROWS, D_MODEL), lambda i: (jnp.minimum(i, n_a - 1), 0))
    spec_b = pl.BlockSpec((FFN_ROWS, D_MODEL), lambda i: (jnp.maximum(i - n_a, 0), 0))
    mod_spec = pl.BlockSpec((MOD_ROWS, 3 * D_MODEL),
                            lambda i: (jnp.where(i < n_a, mod_row_a // MOD_ROWS, 0), 1))
    hbm = pl.BlockSpec(memory_space=pl.ANY)
    in_specs = [spec_a, spec_b, mod_spec, _const_spec(norm_ffn.shape), hbm, hbm, hbm]
    weights = (w_gate, w_up, w_down)
    assert all(w.dtype == BF16 for w in weights)
    scratch = [pltpu.VMEM(w.shape, BF16) for w in weights]
    scratch.append(pltpu.SemaphoreType.DMA((len(FF_CHUNKS), len(weights))))
    return pl.pallas_call(
        functools.partial(_ffn_kernel, n_a=n_a, mod_row_a=mod_row_a,
                          tiles_per_mod_row_b=tiles_per_row_b),
        grid=(n_a + n_b,),
        in_specs=in_specs,
        out_specs=[spec_a, spec_b],
        out_shape=[jax.ShapeDtypeStruct(xa_rows.shape, F32),
                   jax.ShapeDtypeStruct(xb_rows.shape, F32)],
        scratch_shapes=scratch,
        compiler_params=pltpu.CompilerParams(
            dimension_semantics=("arbitrary",),
            vmem_limit_bytes=_vmem_limit(
                [(s, F32) for s in in_specs + [spec_a, spec_b]], scratch)),
        name="ffn",
    )(xa_rows, xb_rows, mod, norm_ffn, *weights)


def _rope_tables(n_tokens):
    rows = n_tokens // GRID_W
    row = jnp.repeat(jnp.arange(rows, dtype=F32), GRID_W)
    col = jnp.tile(jnp.arange(GRID_W, dtype=F32), rows)
    inv = 1.0 / (ROPE_THETA ** (jnp.arange(ROT_PAIRS, dtype=F32) / ROT_PAIRS))
    ang = jnp.stack([row[:, None] * inv, col[:, None] * inv], axis=1)
    cos, sin = jnp.cos(ang), jnp.sin(ang)
    cos_h = jnp.concatenate([cos, cos], axis=-1).reshape(n_tokens, HEAD_DIM)
    sin_h = jnp.concatenate([-sin, sin], axis=-1).reshape(n_tokens, HEAD_DIM)
    reps = LANES // HEAD_DIM
    return jnp.tile(cos_h, (1, reps)), jnp.tile(sin_h, (1, reps))


def _keys_minor(kv):
    return jnp.transpose(kv, (0, 1, 3, 4, 2))


def _group_mean_matrix():
    idx = np.arange(MXU_DIM) // HEAD_DIM
    g = (idx[:, None] == idx[None, :]).astype(np.float32) / HEAD_DIM
    return jnp.asarray(g, dtype=BF16)


def kernel(x_prompt, x_sample, c, cache_k, cache_v, c_ctx, norm_mix, norm_ffn, w_ada, b_ada,
           w_in, q_norm, k_norm, conv_w, attn_out_norm, conv_out_norm, w_out, w_gate_up, w_down):
    depth = w_in.shape[0]
    assert depth == 1
    n_prompt, seq, _ = x_prompt.shape
    n_sample, dec_seq, _ = x_sample.shape
    assert dec_seq == BLOCK_ROWS and BLOCK_ROWS % seq == 0 and n_sample <= CTX_ROW

    cond = jnp.zeros((COND_ROWS, D_MODEL), F32)
    cond = cond.at[0:n_sample].set(c).at[CTX_ROW].set(c_ctx)
    mod = _ada_call(cond, w_ada[0], b_ada[0][None, :])

    consts = (
        norm_mix[0][None, :],
        w_in[0].astype(BF16),
        _group_mean_matrix(),
        jnp.concatenate([jnp.tile(q_norm[0], N_HEADS), jnp.tile(k_norm[0], N_KV)])[None, :],
        conv_w[0],
        attn_out_norm[0][None, :],
        conv_out_norm[0][None, :],
        w_out[0].astype(BF16),
    )
    nffn = norm_ffn[0][None, :]


    xs1, w_gate = _mixer_call(x_sample, mod, None, _keys_minor(cache_k), _keys_minor(cache_v),
                              _rope_tables(dec_seq), consts, seq_len=dec_seq, emit_kv=False,
                              side=((w_gate_up[0], 2, 0),))

    per_block = BLOCK_ROWS // seq
    xp_blocks = x_prompt.reshape(n_prompt // per_block, BLOCK_ROWS, D_MODEL)
    xp1, k_new, v_new, w_up, w_down_b = _mixer_call(
        xp_blocks, mod, CTX_ROW, None, None, None, consts, seq_len=seq, emit_kv=True,
        side=((w_gate_up[0], 2, 1), (w_down[0], 1, 0)))

    yp, ys = _ffn_call(xp1.reshape(-1, D_MODEL), xs1.reshape(-1, D_MODEL), mod, CTX_ROW, dec_seq,
                       nffn, w_gate, w_up, w_down_b)

    return (yp.reshape(n_prompt, seq, D_MODEL),
            ys.reshape(n_sample, dec_seq, D_MODEL),
            jnp.transpose(k_new, (0, 1, 4, 2, 3)),
            jnp.transpose(v_new, (0, 1, 4, 2, 3)))
```

```python
import functools

import numpy as np
import jax
import jax.numpy as jnp
from jax import lax
from jax.experimental import pallas as pl
from jax.experimental.pallas import tpu as pltpu

D_MODEL = 1024
HEAD_DIM = 64
ATTN_DIM = 512
N_HEADS = 8
N_KV = 2
KV_GROUP = N_HEADS // N_KV
KV_DIM = N_KV * HEAD_DIM
CONV_DIM = 512
D_FF = 2816
QK_DIM = ATTN_DIM + KV_DIM
QKV_DIM = ATTN_DIM + 2 * KV_DIM
IN_DIM = QKV_DIM + 3 * CONV_DIM
GRID_W = 64
ROT_PAIRS = HEAD_DIM // 4
ROPE_THETA = 10000.0
RMS_EPS = 1e-6
LOG2_E = 1.4426950408889634
Q_SCALE = HEAD_DIM ** -0.5 * LOG2_E
V_ROWS = HEAD_DIM + 16
MAX_UNSHIFTED_SCORE = 64.0

LANES = 128
MXU_DIM = 256
BLOCK_ROWS = 1024
CHUNK_ROWS = 512
LATENT_Q_ROWS = 256
P_ROWS = 16
FFN_ROWS = 512
FF_CHUNKS = ((0, 1024), (1024, 1024), (2048, 768))
COND_ROWS = 16
MOD_ROWS = 8
CTX_ROW = 8
ADA_COLS = 2048
VMEM_TEMP_BYTES = 10 * 1024 * 1024
VMEM_MAX_BYTES = 60 * 1024 * 1024

F32 = jnp.float32
BF16 = jnp.bfloat16


def _vmem_limit(windows, scratch=()):
    total = VMEM_TEMP_BYTES
    for spec, dtype in windows:
        if spec.block_shape is None:
            continue
        n = int(np.prod([1 if d is None else d for d in spec.block_shape]))
        total += n * jnp.dtype(dtype).itemsize * (1 if spec.pipeline_mode is not None else 2)
    for s in scratch:
        if s.memory_space == pltpu.VMEM:
            total += int(np.prod(s.shape)) * jnp.dtype(s.dtype).itemsize
    return min(total, VMEM_MAX_BYTES)


def _mod_spec(row, half):
    blk = 0 if row is None else row // MOD_ROWS
    return pl.BlockSpec((MOD_ROWS, 3 * D_MODEL), lambda i: (blk, half))


def _mod_row(mod_ref, row):
    r = pl.program_id(0) if row is None else row % MOD_ROWS
    return mod_ref[pl.ds(r, 1), :]


def _const_spec(shape):
    nd = len(shape)
    return pl.BlockSpec(shape, lambda *_: (0,) * nd, pipeline_mode=pl.Buffered(1))


def _ada_kernel(cond_ref, w_ref, b_ref, out_ref):
    c = cond_ref[...]
    s = (c * jax.nn.sigmoid(c)).astype(BF16)
    out_ref[...] = jnp.dot(s, w_ref[...].astype(BF16), preferred_element_type=F32) + b_ref[...]


def _ada_call(cond, w_ada, b_ada):
    n = w_ada.shape[1]
    in_specs = [
        pl.BlockSpec((COND_ROWS, D_MODEL), lambda j: (0, 0)),
        pl.BlockSpec((D_MODEL, ADA_COLS), lambda j: (0, j)),
        pl.BlockSpec((1, ADA_COLS), lambda j: (0, j)),
    ]
    out_spec = pl.BlockSpec((COND_ROWS, ADA_COLS), lambda j: (0, j))
    return pl.pallas_call(
        _ada_kernel,
        grid=(n // ADA_COLS,),
        in_specs=in_specs,
        out_specs=out_spec,
        out_shape=jax.ShapeDtypeStruct((COND_ROWS, n), F32),
        compiler_params=pltpu.CompilerParams(
            dimension_semantics=("arbitrary",),
            vmem_limit_bytes=_vmem_limit([(s, F32) for s in in_specs + [out_spec]])),
        name="ada_rows",
    )(cond, w_ada, b_ada)


def _rms_rows(x):
    return x * lax.rsqrt(jnp.mean(x * x, axis=-1, keepdims=True) + RMS_EPS)


def _mixer_kernel(*refs, seq_len, ctx_len, use_rope, emit_kv, mod_row, n_side):
    it = iter(refs)
    x_ref, mod_ref = next(it), next(it)
    if ctx_len:
        ck_ref, cv_ref = next(it), next(it)
    if use_rope:
        cos_ref, sin_ref = next(it), next(it)
    (nmix_ref, win_ref, gsum_ref, qkg_ref, convw_ref, ga_ref, gc_ref, wout_ref) = (
        next(it) for _ in range(8))
    side_in = [next(it) for _ in range(n_side)]
    out_ref = next(it)
    if emit_kv:
        ko_ref, vo_ref = next(it), next(it)
    side_out = [next(it) for _ in range(n_side)]
    (q_s, k_s, vt_s, attn_s, t_s, gb_s, s_ref, p_ref) = (next(it) for _ in range(8))

    for src, dst in zip(side_in, side_out):
        dst[...] = src[...].astype(BF16)

    n_chunks = BLOCK_ROWS // CHUNK_ROWS
    if ctx_len:
        q_rows, n_keys, key_blk = LATENT_Q_ROWS, ctx_len + seq_len, BLOCK_ROWS
    else:
        q_rows, n_keys, key_blk = seq_len, seq_len, seq_len
    n_qb = BLOCK_ROWS // q_rows

    mod = _mod_row(mod_ref, mod_row)
    shift = mod[:, 0:D_MODEL]
    scale1 = 1.0 + mod[:, D_MODEL:2 * D_MODEL]
    gate = mod[:, 2 * D_MODEL:3 * D_MODEL]

    def put_values_t(blk, off, vt):
        vt = vt.astype(BF16)
        for g in range(N_KV):
            vt_s[blk, g, 0:HEAD_DIM, off:off + vt.shape[1]] = vt[g * HEAD_DIM:(g + 1) * HEAD_DIM, :]

    tail = (lax.broadcasted_iota(jnp.int32, (V_ROWS - HEAD_DIM, n_keys), 0) == 0).astype(BF16)
    for blk in range(vt_s.shape[0]):
        for g in range(N_KV):
            vt_s[blk, g, HEAD_DIM:V_ROWS, :] = tail

    if ctx_len:
        ck = ck_ref[...].reshape(KV_DIM, ctx_len).T.astype(BF16)
        for g in range(N_KV):
            k_s[g, 0:ctx_len, :] = ck[:, g * HEAD_DIM:(g + 1) * HEAD_DIM]
        put_values_t(0, 0, cv_ref[...].reshape(KV_DIM, ctx_len))

    if use_rope:
        lane = lax.broadcasted_iota(jnp.int32, (CHUNK_ROWS, LANES), 1)
        first_half = (lane % (2 * ROT_PAIRS)) < ROT_PAIRS

    t_s[0:8, :] = jnp.zeros((8, CONV_DIM), F32)
    t_s[8 + BLOCK_ROWS:16 + BLOCK_ROWS, :] = jnp.zeros((8, CONV_DIM), F32)
    w0 = convw_ref[0:1, :]
    w1 = convw_ref[1:2, :]
    w2 = convw_ref[2:3, :]

    def conv_rows(a, n):
        slab = t_s[a:a + n + 16, :]
        t_prev = pltpu.roll(slab, 1, axis=0)[8:8 + n, :]
        t_mid = slab[8:8 + n, :]
        t_next = pltpu.roll(slab, n + 15, axis=0)[8:8 + n, :]
        if seq_len < BLOCK_ROWS:
            pos = (lax.broadcasted_iota(jnp.int32, (n, 1), 0) + a) % seq_len
            t_prev = jnp.where(pos == 0, 0.0, t_prev)
            t_next = jnp.where(pos == seq_len - 1, 0.0, t_next)
        y = gb_s[a:a + n, :] * (w0 * t_prev + w1 * t_mid + w2 * t_next)
        return (_rms_rows(y) * gc_ref[...]).astype(BF16)

    def project_conv(r0, hb):
        cvp = jnp.dot(hb, win_ref[:, QKV_DIM:IN_DIM], preferred_element_type=F32)
        gb_s[r0:r0 + CHUNK_ROWS, :] = cvp[:, 0:CONV_DIM]
        t_s[8 + r0:8 + r0 + CHUNK_ROWS, :] = (cvp[:, CONV_DIM:2 * CONV_DIM]
                                              * cvp[:, 2 * CONV_DIM:3 * CONV_DIM])

    for c in range(n_chunks):
        r0 = c * CHUNK_ROWS
        x = x_ref[r0:r0 + CHUNK_ROWS, :]
        h = (_rms_rows(x) * nmix_ref[...]) * scale1 + shift
        hb = h.astype(BF16)
        qkv = jnp.dot(hb, win_ref[:, 0:QKV_DIM], preferred_element_type=F32)

        groups = []
        for g0 in range(0, QKV_DIM, MXU_DIM):
            sq = qkv[:, g0:g0 + MXU_DIM]
            groups.append(jnp.dot((sq * sq).astype(BF16), gsum_ref[...],
                                  preferred_element_type=F32))
        ms = jnp.concatenate(groups, axis=-1)[:, 0:QK_DIM]
        qk = (qkv[:, 0:QK_DIM] * lax.rsqrt(ms + RMS_EPS)) * qkg_ref[...]
        vv = qkv[:, QK_DIM:QKV_DIM]

        if emit_kv:
            for r1 in range(0, CHUNK_ROWS, seq_len):
                kt = qk[r1:r1 + seq_len, ATTN_DIM:QK_DIM].T
                ko_ref[(r0 + r1) // seq_len] = kt.reshape(N_KV, HEAD_DIM, seq_len)

        for cg in range(QK_DIM // LANES):
            xg = qk[:, cg * LANES:(cg + 1) * LANES]
            if use_rope:
                cs = cos_ref[r0:r0 + CHUNK_ROWS, :]
                sn = sin_ref[r0:r0 + CHUNK_ROWS, :]
                partner = jnp.where(first_half,
                                    pltpu.roll(xg, LANES - ROT_PAIRS, axis=1),
                                    pltpu.roll(xg, ROT_PAIRS, axis=1))
                xg = xg * cs + partner * sn
            if cg < ATTN_DIM // LANES:
                xb = (xg * Q_SCALE).astype(BF16)
                q_s[2 * cg, r0:r0 + CHUNK_ROWS, :] = xb[:, 0:HEAD_DIM]
                q_s[2 * cg + 1, r0:r0 + CHUNK_ROWS, :] = xb[:, HEAD_DIM:LANES]
            else:
                xb = xg.astype(BF16)
                for g in range(N_KV):
                    k_s[g, ctx_len + r0:ctx_len + r0 + CHUNK_ROWS, :] = (
                        xb[:, g * HEAD_DIM:(g + 1) * HEAD_DIM])
        w = min(key_blk, CHUNK_ROWS)
        for r1 in range(r0, r0 + CHUNK_ROWS, w):
            blk, off = (0, ctx_len + r1) if ctx_len else (r1 // key_blk, 0)
            vt = vv[r1 - r0:r1 - r0 + w, :].T
            put_values_t(blk, off, vt)
            if emit_kv:
                vo_ref[blk] = vt.reshape(N_KV, HEAD_DIM, seq_len)

        project_conv(r0, hb)

    def scores_t(qb, g):
        r0 = pl.multiple_of(qb * q_rows, q_rows)
        k0 = 0 if ctx_len else r0
        qs = jnp.concatenate(
            [q_s[KV_GROUP * g + j, pl.ds(r0, q_rows), :] for j in range(KV_GROUP)], axis=0)
        kk = k_s[g, pl.ds(k0, n_keys), :]
        return lax.dot_general(kk, qs, (((1,), (1,)), ((), ())),
                               preferred_element_type=F32)

    def values_out(qb, g, p):
        r0 = pl.multiple_of(qb * q_rows, q_rows)
        blk = 0 if ctx_len else qb
        ot = jnp.dot(vt_s[blk, g], p, preferred_element_type=F32)
        ot = ot[0:HEAD_DIM, :] / ot[HEAD_DIM:HEAD_DIM + 1, :]
        for jj in range(KV_GROUP // 2):
            pair_t = jnp.concatenate(
                [ot[:, (2 * jj) * q_rows:(2 * jj + 1) * q_rows],
                 ot[:, (2 * jj + 1) * q_rows:(2 * jj + 2) * q_rows]], axis=0)
            col = (KV_GROUP * g + 2 * jj) * HEAD_DIM
            attn_s[pl.ds(r0, q_rows), col:col + LANES] = pair_t.T

    def pipeline_pairs(step):
        assert n_qb % 2 == 0
        step(-1, 1, first=True)
        step(0, 0)

        def body(t, carry):
            step(2 * t + 1, 1)
            step(2 * t + 2, 0)
            return carry

        lax.fori_loop(0, (n_qb - 2) // 2, body, 0)
        step(n_qb - 1, 1, last=True)

    def shifted_block(qb, carry):
        for g in range(N_KV):
            s = scores_t(qb, g)
            s_ref[...] = s
            m = jnp.max(s, axis=0, keepdims=True)
            for k1 in range(0, n_keys, P_ROWS):
                p_ref[0, g, k1:k1 + P_ROWS, :] = jnp.exp2(
                    s_ref[k1:k1 + P_ROWS, :] - m).astype(BF16)
            values_out(qb, g, p_ref[0, g])
        return carry

    def unshifted_step(j, par, first=False, last=False):
        for g in range(N_KV):
            if not last:
                p_ref[1 - par, g] = jnp.exp2(scores_t(j + 1, g)).astype(BF16)
            if not first:
                values_out(j, g, p_ref[par, g])

    gains = jnp.abs(qkg_ref[...])
    q_bound = (Q_SCALE * Q_SCALE * HEAD_DIM) * jnp.max(gains[:, 0:ATTN_DIM]) ** 2
    k_bound = HEAD_DIM * jnp.max(gains[:, ATTN_DIM:QK_DIM]) ** 2
    if ctx_len:
        ck2 = ck_ref[...]
        k_bound = jnp.maximum(k_bound, jnp.max(jnp.sum(ck2 * ck2, axis=1)))
    small_scores = q_bound * k_bound <= MAX_UNSHIFTED_SCORE ** 2
    pl.when(small_scores)(lambda: pipeline_pairs(unshifted_step))

    @pl.when(jnp.logical_not(small_scores))
    def _():
        lax.fori_loop(0, n_qb, shifted_block, 0)

    for r0 in range(0, BLOCK_ROWS, CHUNK_ROWS):
        rows = slice(r0, r0 + CHUNK_ROWS)
        an = (_rms_rows(attn_s[rows, :]) * ga_ref[...]).astype(BF16)
        merged = jnp.concatenate([an, conv_rows(r0, CHUNK_ROWS)], axis=-1)
        mix = jnp.dot(merged, wout_ref[...], preferred_element_type=F32)
        out_ref[rows, :] = x_ref[rows, :] + gate * mix


def _mixer_call(x_blocks, mod, mod_row, ctx_k, ctx_v, rope, consts, *, seq_len, emit_kv, side=()):
    n_blocks = x_blocks.shape[0]
    ctx_len = 0 if ctx_k is None else ctx_k.shape[4]
    use_rope = rope is not None
    if ctx_len:
        assert seq_len == BLOCK_ROWS
        q_rows, n_keys, n_key_blocks = LATENT_Q_ROWS, ctx_len + BLOCK_ROWS, 1
    else:
        q_rows, n_keys, n_key_blocks = seq_len, seq_len, BLOCK_ROWS // seq_len
    n_q_cols = KV_GROUP * q_rows

    blk = lambda cols: pl.BlockSpec((None, BLOCK_ROWS, cols), lambda b: (b, 0, 0))
    assert mod_row is not None or n_blocks <= MOD_ROWS
    args = [x_blocks, mod]
    in_specs = [blk(D_MODEL), _mod_spec(mod_row, 0)]
    if ctx_len:
        args += [ctx_k, ctx_v]
        in_specs += [pl.BlockSpec((None, None, N_KV, HEAD_DIM, ctx_len),
                                  lambda b: (b, 0, 0, 0, 0))] * 2
    if use_rope:
        args += list(rope)
        in_specs += [_const_spec((BLOCK_ROWS, LANES))] * 2
    args += list(consts)
    in_specs += [_const_spec(a.shape) for a in consts]
    side_specs, side_shapes = [], []
    for w, col_blocks, col_block in side:
        rows, cols = w.shape[0] // n_blocks, w.shape[1] // col_blocks
        assert rows * n_blocks == w.shape[0] and cols * col_blocks == w.shape[1]
        side_specs.append(pl.BlockSpec((rows, cols), lambda b, col_block=col_block: (b, col_block)))
        side_shapes.append(jax.ShapeDtypeStruct((w.shape[0], cols), BF16))
    args += [w for w, _, _ in side]
    in_specs += side_specs

    out_shape = [jax.ShapeDtypeStruct((n_blocks, BLOCK_ROWS, D_MODEL), F32)]
    out_specs = [blk(D_MODEL)]
    if emit_kv:
        per_block = BLOCK_ROWS // seq_len
        out_shape += [jax.ShapeDtypeStruct(
            (n_blocks * per_block, 1, N_KV, HEAD_DIM, seq_len), F32)] * 2
        out_specs += [pl.BlockSpec((per_block, None, N_KV, HEAD_DIM, seq_len),
                                   lambda b: (b, 0, 0, 0, 0))] * 2
    out_shape += side_shapes
    out_specs += [pl.BlockSpec(s.block_shape, lambda b: (b, 0)) for s in side_specs]

    scratch = [
        pltpu.VMEM((N_HEADS, BLOCK_ROWS, HEAD_DIM), BF16),
        pltpu.VMEM((N_KV, ctx_len + BLOCK_ROWS, HEAD_DIM), BF16),
        pltpu.VMEM((n_key_blocks, N_KV, V_ROWS, n_keys), BF16),
        pltpu.VMEM((BLOCK_ROWS, ATTN_DIM), F32),
        pltpu.VMEM((BLOCK_ROWS + 16, CONV_DIM), F32),
        pltpu.VMEM((BLOCK_ROWS, CONV_DIM), F32),
        pltpu.VMEM((n_keys, n_q_cols), F32),
        pltpu.VMEM((2, N_KV, n_keys, n_q_cols), BF16),
    ]
    kern = functools.partial(_mixer_kernel, seq_len=seq_len, ctx_len=ctx_len,
                             use_rope=use_rope, emit_kv=emit_kv, mod_row=mod_row,
                             n_side=len(side))
    return pl.pallas_call(
        kern,
        grid=(n_blocks,),
        in_specs=in_specs,
        out_specs=out_specs,
        out_shape=out_shape,
        scratch_shapes=scratch,
        compiler_params=pltpu.CompilerParams(
            dimension_semantics=("arbitrary",),
            vmem_limit_bytes=_vmem_limit(
                [(s, a.dtype) for s, a in zip(in_specs, args)]
                + [(s, o.dtype) for s, o in zip(out_specs, out_shape)],
                scratch)),
        name="mixer_ctx" if emit_kv else "mixer_latent",
    )(*args)


def _ffn_rows(x_ref, mod, nffn_ref, wg_ref, wu_ref, wd_ref, out_ref):
    shift = mod[:, 0:D_MODEL]
    scale1 = 1.0 + mod[:, D_MODEL:2 * D_MODEL]
    gate = mod[:, 2 * D_MODEL:3 * D_MODEL]
    x = x_ref[...]
    hb = ((_rms_rows(x) * nffn_ref[...]) * scale1 + shift).astype(BF16)
    acc = None
    for c0, cw in FF_CHUNKS:
        gt = jnp.dot(hb, wg_ref[:, c0:c0 + cw], preferred_element_type=F32)
        up = jnp.dot(hb, wu_ref[:, c0:c0 + cw], preferred_element_type=F32)
        act = ((gt * jax.nn.sigmoid(gt)) * up).astype(BF16)
        part = jnp.dot(act, wd_ref[c0:c0 + cw, :], preferred_element_type=F32)
        acc = part if acc is None else acc + part
    out_ref[...] = x + gate * acc


def _ffn_kernel(xa_ref, xb_ref, mod_ref, nffn_ref, wg_ref, wu_ref, wd_ref, ya_ref, yb_ref,
                *, n_a, mod_row_a, tiles_per_mod_row_b):
    i = pl.program_id(0)

    @pl.when(i < n_a)
    def _():
        mod = mod_ref[mod_row_a % MOD_ROWS:mod_row_a % MOD_ROWS + 1, :]
        _ffn_rows(xa_ref, mod, nffn_ref, wg_ref, wu_ref, wd_ref, ya_ref)

    @pl.when(i >= n_a)
    def _():
        mod = mod_ref[pl.ds((i - n_a) // tiles_per_mod_row_b, 1), :]
        _ffn_rows(xb_ref, mod, nffn_ref, wg_ref, wu_ref, wd_ref, yb_ref)


def _ffn_call(xa_rows, xb_rows, mod, mod_row_a, rows_per_mod_row_b, norm_ffn, w_gate, w_up, w_down):
    n_a, n_b = xa_rows.shape[0] // FFN_ROWS, xb_rows.shape[0] // FFN_ROWS
    tiles_per_row_b = rows_per_mod_row_b // FFN_ROWS
    assert n_b // tiles_per_row_b <= MOD_ROWS
    spec_a = pl.BlockSpec((FFN_ROWS, D_MODEL), lambda i: (jnp.minimum(i, n_a - 1), 0))
    spec_b = pl.BlockSpec((FFN_ROWS, D_MODEL), lambda i: (jnp.maximum(i - n_a, 0), 0))
    mod_spec = pl.BlockSpec((MOD_ROWS, 3 * D_MODEL),
                            lambda i: (jnp.where(i < n_a, mod_row_a // MOD_ROWS, 0), 1))
    args = (xa_rows, xb_rows, mod, norm_ffn, w_gate, w_up, w_down)
    assert all(w.dtype == BF16 for w in args[4:])
    in_specs = [spec_a, spec_b, mod_spec] + [_const_spec(a.shape) for a in args[3:]]
    return pl.pallas_call(
        functools.partial(_ffn_kernel, n_a=n_a, mod_row_a=mod_row_a,
                          tiles_per_mod_row_b=tiles_per_row_b),
        grid=(n_a + n_b,),
        in_specs=in_specs,
        out_specs=[spec_a, spec_b],
        out_shape=[jax.ShapeDtypeStruct(xa_rows.shape, F32),
                   jax.ShapeDtypeStruct(xb_rows.shape, F32)],
        compiler_params=pltpu.CompilerParams(
            dimension_semantics=("arbitrary",),
            vmem_limit_bytes=_vmem_limit(
                [(s, a.dtype) for s, a in zip(in_specs, args)]
                + [(spec_a, F32), (spec_b, F32)])),
        name="ffn",
    )(*args)


def _rope_tables(n_tokens):
    rows = n_tokens // GRID_W
    row = jnp.repeat(jnp.arange(rows, dtype=F32), GRID_W)
    col = jnp.tile(jnp.arange(GRID_W, dtype=F32), rows)
    inv = 1.0 / (ROPE_THETA ** (jnp.arange(ROT_PAIRS, dtype=F32) / ROT_PAIRS))
    ang = jnp.stack([row[:, None] * inv, col[:, None] * inv], axis=1)
    cos, sin = jnp.cos(ang), jnp.sin(ang)
    cos_h = jnp.concatenate([cos, cos], axis=-1).reshape(n_tokens, HEAD_DIM)
    sin_h = jnp.concatenate([-sin, sin], axis=-1).reshape(n_tokens, HEAD_DIM)
    reps = LANES // HEAD_DIM
    return jnp.tile(cos_h, (1, reps)), jnp.tile(sin_h, (1, reps))


def _keys_minor(kv):
    return jnp.transpose(kv, (0, 1, 3, 4, 2))


def _group_mean_matrix():
    idx = np.arange(MXU_DIM) // HEAD_DIM
    g = (idx[:, None] == idx[None, :]).astype(np.float32) / HEAD_DIM
    return jnp.asarray(g, dtype=BF16)


def kernel(x_prompt, x_sample, c, cache_k, cache_v, c_ctx, norm_mix, norm_ffn, w_ada, b_ada,
           w_in, q_norm, k_norm, conv_w, attn_out_norm, conv_out_norm, w_out, w_gate_up, w_down):
    depth = w_in.shape[0]
    assert depth == 1
    n_prompt, seq, _ = x_prompt.shape
    n_sample, dec_seq, _ = x_sample.shape
    assert dec_seq == BLOCK_ROWS and BLOCK_ROWS % seq == 0 and n_sample <= CTX_ROW

    cond = jnp.zeros((COND_ROWS, D_MODEL), F32)
    cond = cond.at[0:n_sample].set(c).at[CTX_ROW].set(c_ctx)
    mod = _ada_call(cond, w_ada[0], b_ada[0][None, :])

    consts = (
        norm_mix[0][None, :],
        w_in[0].astype(BF16),
        _group_mean_matrix(),
        jnp.concatenate([jnp.tile(q_norm[0], N_HEADS), jnp.tile(k_norm[0], N_KV)])[None, :],
        conv_w[0],
        attn_out_norm[0][None, :],
        conv_out_norm[0][None, :],
        w_out[0].astype(BF16),
    )
    nffn = norm_ffn[0][None, :]


    xs1, w_gate = _mixer_call(x_sample, mod, None, _keys_minor(cache_k), _keys_minor(cache_v),
                              _rope_tables(dec_seq), consts, seq_len=dec_seq, emit_kv=False,
                              side=((w_gate_up[0], 2, 0),))

    per_block = BLOCK_ROWS // seq
    xp_blocks = x_prompt.reshape(n_prompt // per_block, BLOCK_ROWS, D_MODEL)
    xp1, k_new, v_new, w_up, w_down_b = _mixer_call(
        xp_blocks, mod, CTX_ROW, None, None, None, consts, seq_len=seq, emit_kv=True,
        side=((w_gate_up[0], 2, 1), (w_down[0], 1, 0)))

    yp, ys = _ffn_call(xp1.reshape(-1, D_MODEL), xs1.reshape(-1, D_MODEL), mod, CTX_ROW, dec_seq,
                       nffn, w_gate, w_up, w_down_b)

    return (yp.reshape(n_prompt, seq, D_MODEL),
            ys.reshape(n_sample, dec_seq, D_MODEL),
            jnp.transpose(k_new, (0, 1, 4, 2, 3)),
            jnp.transpose(v_new, (0, 1, 4, 2, 3)))
```

```python
import functools

import numpy as np
import jax
import jax.numpy as jnp
from jax import lax
from jax.experimental import pallas as pl
from jax.experimental.pallas import tpu as pltpu

D_MODEL = 1024
HEAD_DIM = 64
ATTN_DIM = 512
N_HEADS = 8
N_KV = 2
KV_GROUP = N_HEADS // N_KV
KV_DIM = N_KV * HEAD_DIM
CONV_DIM = 512
D_FF = 2816
QK_DIM = ATTN_DIM + KV_DIM
QKV_DIM = ATTN_DIM + 2 * KV_DIM
IN_DIM = QKV_DIM + 3 * CONV_DIM
GRID_W = 64
ROT_PAIRS = HEAD_DIM // 4
ROPE_THETA = 10000.0
RMS_EPS = 1e-6
LOG2_E = 1.4426950408889634
Q_SCALE = HEAD_DIM ** -0.5 * LOG2_E
V_ROWS = HEAD_DIM + 16
MAX_UNSHIFTED_SCORE = 64.0

LANES = 128
MXU_DIM = 256
BLOCK_ROWS = 1024
CHUNK_ROWS = 512
LATENT_Q_ROWS = 256
P_ROWS = 16
FFN_ROWS = 512
FF_CHUNKS = ((0, 1024), (1024, 1024), (2048, 768))
COND_ROWS = 16
MOD_ROWS = 8
CTX_ROW = 8
ADA_COLS = 2048
VMEM_TEMP_BYTES = 8 * 1024 * 1024
VMEM_MAX_BYTES = 60 * 1024 * 1024

F32 = jnp.float32
BF16 = jnp.bfloat16


def _vmem_limit(windows, scratch=()):
    total = VMEM_TEMP_BYTES
    for spec, dtype in windows:
        if spec.block_shape is None:
            continue
        n = int(np.prod([1 if d is None else d for d in spec.block_shape]))
        total += n * jnp.dtype(dtype).itemsize * (1 if spec.pipeline_mode is not None else 2)
    for s in scratch:
        if s.memory_space == pltpu.VMEM:
            total += int(np.prod(s.shape)) * jnp.dtype(s.dtype).itemsize
    return min(total, VMEM_MAX_BYTES)


def _mod_spec(row, half):
    blk = 0 if row is None else row // MOD_ROWS
    return pl.BlockSpec((MOD_ROWS, 3 * D_MODEL), lambda i: (blk, half))


def _mod_row(mod_ref, row):
    r = pl.program_id(0) if row is None else row % MOD_ROWS
    return mod_ref[pl.ds(r, 1), :]


def _const_spec(shape):
    nd = len(shape)
    return pl.BlockSpec(shape, lambda *_: (0,) * nd, pipeline_mode=pl.Buffered(1))


def _ada_kernel(cond_ref, w_ref, b_ref, out_ref):
    c = cond_ref[...]
    s = (c * jax.nn.sigmoid(c)).astype(BF16)
    out_ref[...] = jnp.dot(s, w_ref[...].astype(BF16), preferred_element_type=F32) + b_ref[...]


def _ada_call(cond, w_ada, b_ada):
    n = w_ada.shape[1]
    in_specs = [
        pl.BlockSpec((COND_ROWS, D_MODEL), lambda j: (0, 0)),
        pl.BlockSpec((D_MODEL, ADA_COLS), lambda j: (0, j)),
        pl.BlockSpec((1, ADA_COLS), lambda j: (0, j)),
    ]
    out_spec = pl.BlockSpec((COND_ROWS, ADA_COLS), lambda j: (0, j))
    return pl.pallas_call(
        _ada_kernel,
        grid=(n // ADA_COLS,),
        in_specs=in_specs,
        out_specs=out_spec,
        out_shape=jax.ShapeDtypeStruct((COND_ROWS, n), F32),
        compiler_params=pltpu.CompilerParams(
            dimension_semantics=("arbitrary",),
            vmem_limit_bytes=_vmem_limit([(s, F32) for s in in_specs + [out_spec]])),
        name="ada_rows",
    )(cond, w_ada, b_ada)


def _rms_rows(x):
    return x * lax.rsqrt(jnp.mean(x * x, axis=-1, keepdims=True) + RMS_EPS)


def _mixer_kernel(*refs, seq_len, ctx_len, use_rope, emit_kv, mod_row, n_side):
    it = iter(refs)
    x_ref, mod_ref = next(it), next(it)
    if ctx_len:
        ck_ref, cv_ref = next(it), next(it)
    if use_rope:
        cos_ref, sin_ref = next(it), next(it)
    (nmix_ref, win_ref, gsum_ref, qkg_ref, convw_ref, ga_ref, gc_ref, wout_ref) = (
        next(it) for _ in range(8))
    side_in = [next(it) for _ in range(n_side)]
    out_ref = next(it)
    if emit_kv:
        ko_ref, vo_ref = next(it), next(it)
    side_out = [next(it) for _ in range(n_side)]
    (q_s, k_s, vt_s, attn_s, t_s, gb_s, s_ref, p_ref) = (next(it) for _ in range(8))

    for src, dst in zip(side_in, side_out):
        dst[...] = src[...].astype(BF16)

    n_chunks = BLOCK_ROWS // CHUNK_ROWS
    if ctx_len:
        q_rows, n_keys, key_blk = LATENT_Q_ROWS, ctx_len + seq_len, BLOCK_ROWS
    else:
        q_rows, n_keys, key_blk = seq_len, seq_len, seq_len
    n_qb = BLOCK_ROWS // q_rows

    mod = _mod_row(mod_ref, mod_row)
    shift = mod[:, 0:D_MODEL]
    scale1 = 1.0 + mod[:, D_MODEL:2 * D_MODEL]
    gate = mod[:, 2 * D_MODEL:3 * D_MODEL]

    def put_values_t(blk, off, vt):
        vt = vt.astype(BF16)
        for g in range(N_KV):
            vt_s[blk, g, 0:HEAD_DIM, off:off + vt.shape[1]] = vt[g * HEAD_DIM:(g + 1) * HEAD_DIM, :]

    tail = (lax.broadcasted_iota(jnp.int32, (V_ROWS - HEAD_DIM, n_keys), 0) == 0).astype(BF16)
    for blk in range(vt_s.shape[0]):
        for g in range(N_KV):
            vt_s[blk, g, HEAD_DIM:V_ROWS, :] = tail

    if ctx_len:
        ck = ck_ref[...].reshape(KV_DIM, ctx_len).T.astype(BF16)
        for g in range(N_KV):
            k_s[g, 0:ctx_len, :] = ck[:, g * HEAD_DIM:(g + 1) * HEAD_DIM]
        put_values_t(0, 0, cv_ref[...].reshape(KV_DIM, ctx_len))

    if use_rope:
        lane = lax.broadcasted_iota(jnp.int32, (CHUNK_ROWS, LANES), 1)
        first_half = (lane % (2 * ROT_PAIRS)) < ROT_PAIRS

    t_s[0:8, :] = jnp.zeros((8, CONV_DIM), F32)
    t_s[8 + BLOCK_ROWS:16 + BLOCK_ROWS, :] = jnp.zeros((8, CONV_DIM), F32)
    w0 = convw_ref[0:1, :]
    w1 = convw_ref[1:2, :]
    w2 = convw_ref[2:3, :]

    def conv_rows(a, n):
        slab = t_s[a:a + n + 16, :]
        t_prev = pltpu.roll(slab, 1, axis=0)[8:8 + n, :]
        t_mid = slab[8:8 + n, :]
        t_next = pltpu.roll(slab, n + 15, axis=0)[8:8 + n, :]
        if seq_len < BLOCK_ROWS:
            pos = (lax.broadcasted_iota(jnp.int32, (n, 1), 0) + a) % seq_len
            t_prev = jnp.where(pos == 0, 0.0, t_prev)
            t_next = jnp.where(pos == seq_len - 1, 0.0, t_next)
        y = gb_s[a:a + n, :] * (w0 * t_prev + w1 * t_mid + w2 * t_next)
        return (_rms_rows(y) * gc_ref[...]).astype(BF16)

    def project_conv(r0, hb):
        cvp = jnp.dot(hb, win_ref[:, QKV_DIM:IN_DIM], preferred_element_type=F32)
        gb_s[r0:r0 + CHUNK_ROWS, :] = cvp[:, 0:CONV_DIM]
        t_s[8 + r0:8 + r0 + CHUNK_ROWS, :] = (cvp[:, CONV_DIM:2 * CONV_DIM]
                                              * cvp[:, 2 * CONV_DIM:3 * CONV_DIM])

    for c in range(n_chunks):
        r0 = c * CHUNK_ROWS
        x = x_ref[r0:r0 + CHUNK_ROWS, :]
        h = (_rms_rows(x) * nmix_ref[...]) * scale1 + shift
        hb = h.astype(BF16)
        qkv = jnp.dot(hb, win_ref[:, 0:QKV_DIM], preferred_element_type=F32)

        groups = []
        for g0 in range(0, QKV_DIM, MXU_DIM):
            sq = qkv[:, g0:g0 + MXU_DIM]
            groups.append(jnp.dot((sq * sq).astype(BF16), gsum_ref[...],
                                  preferred_element_type=F32))
        ms = jnp.concatenate(groups, axis=-1)[:, 0:QK_DIM]
        qk = (qkv[:, 0:QK_DIM] * lax.rsqrt(ms + RMS_EPS)) * qkg_ref[...]
        vv = qkv[:, QK_DIM:QKV_DIM]

        if emit_kv:
            for r1 in range(0, CHUNK_ROWS, seq_len):
                kt = qk[r1:r1 + seq_len, ATTN_DIM:QK_DIM].T
                ko_ref[(r0 + r1) // seq_len] = kt.reshape(N_KV, HEAD_DIM, seq_len)

        for cg in range(QK_DIM // LANES):
            xg = qk[:, cg * LANES:(cg + 1) * LANES]
            if use_rope:
                cs = cos_ref[r0:r0 + CHUNK_ROWS, :]
                sn = sin_ref[r0:r0 + CHUNK_ROWS, :]
                partner = jnp.where(first_half,
                                    pltpu.roll(xg, LANES - ROT_PAIRS, axis=1),
                                    pltpu.roll(xg, ROT_PAIRS, axis=1))
                xg = xg * cs + partner * sn
            if cg < ATTN_DIM // LANES:
                xb = (xg * Q_SCALE).astype(BF16)
                q_s[2 * cg, r0:r0 + CHUNK_ROWS, :] = xb[:, 0:HEAD_DIM]
                q_s[2 * cg + 1, r0:r0 + CHUNK_ROWS, :] = xb[:, HEAD_DIM:LANES]
            else:
                xb = xg.astype(BF16)
                for g in range(N_KV):
                    k_s[g, ctx_len + r0:ctx_len + r0 + CHUNK_ROWS, :] = (
                        xb[:, g * HEAD_DIM:(g + 1) * HEAD_DIM])
        w = min(key_blk, CHUNK_ROWS)
        for r1 in range(r0, r0 + CHUNK_ROWS, w):
            blk, off = (0, ctx_len + r1) if ctx_len else (r1 // key_blk, 0)
            vt = vv[r1 - r0:r1 - r0 + w, :].T
            put_values_t(blk, off, vt)
            if emit_kv:
                vo_ref[blk] = vt.reshape(N_KV, HEAD_DIM, seq_len)

        project_conv(r0, hb)

    def scores_t(qb, g):
        r0 = pl.multiple_of(qb * q_rows, q_rows)
        k0 = 0 if ctx_len else r0
        qs = jnp.concatenate(
            [q_s[KV_GROUP * g + j, pl.ds(r0, q_rows), :] for j in range(KV_GROUP)], axis=0)
        kk = k_s[g, pl.ds(k0, n_keys), :]
        return lax.dot_general(kk, qs, (((1,), (1,)), ((), ())),
                               preferred_element_type=F32)

    def values_out(qb, g, p):
        r0 = pl.multiple_of(qb * q_rows, q_rows)
        blk = 0 if ctx_len else qb
        ot = jnp.dot(vt_s[blk, g], p, preferred_element_type=F32)
        ot = ot[0:HEAD_DIM, :] / ot[HEAD_DIM:HEAD_DIM + 1, :]
        for jj in range(KV_GROUP // 2):
            pair_t = jnp.concatenate(
                [ot[:, (2 * jj) * q_rows:(2 * jj + 1) * q_rows],
                 ot[:, (2 * jj + 1) * q_rows:(2 * jj + 2) * q_rows]], axis=0)
            col = (KV_GROUP * g + 2 * jj) * HEAD_DIM
            attn_s[pl.ds(r0, q_rows), col:col + LANES] = pair_t.T

    def pipeline_pairs(step):
        assert n_qb % 2 == 0
        step(-1, 1, first=True)
        step(0, 0)

        def body(t, carry):
            step(2 * t + 1, 1)
            step(2 * t + 2, 0)
            return carry

        lax.fori_loop(0, (n_qb - 2) // 2, body, 0)
        step(n_qb - 1, 1, last=True)

    def shifted_block(qb, carry):
        for g in range(N_KV):
            s = scores_t(qb, g)
            s_ref[...] = s
            m = jnp.max(s, axis=0, keepdims=True)
            for k1 in range(0, n_keys, P_ROWS):
                p_ref[0, g, k1:k1 + P_ROWS, :] = jnp.exp2(
                    s_ref[k1:k1 + P_ROWS, :] - m).astype(BF16)
            values_out(qb, g, p_ref[0, g])
        return carry

    def unshifted_step(j, par, first=False, last=False):
        for g in range(N_KV):
            if not last:
                p_ref[1 - par, g] = jnp.exp2(scores_t(j + 1, g)).astype(BF16)
            if not first:
                values_out(j, g, p_ref[par, g])

    gains = jnp.abs(qkg_ref[...])
    q_bound = (Q_SCALE * Q_SCALE * HEAD_DIM) * jnp.max(gains[:, 0:ATTN_DIM]) ** 2
    k_bound = HEAD_DIM * jnp.max(gains[:, ATTN_DIM:QK_DIM]) ** 2
    if ctx_len:
        ck2 = ck_ref[...]
        k_bound = jnp.maximum(k_bound, jnp.max(jnp.sum(ck2 * ck2, axis=1)))
    small_scores = q_bound * k_bound <= MAX_UNSHIFTED_SCORE ** 2
    pl.when(small_scores)(lambda: pipeline_pairs(unshifted_step))

    @pl.when(jnp.logical_not(small_scores))
    def _():
        lax.fori_loop(0, n_qb, shifted_block, 0)

    for r0 in range(0, BLOCK_ROWS, CHUNK_ROWS):
        rows = slice(r0, r0 + CHUNK_ROWS)
        an = (_rms_rows(attn_s[rows, :]) * ga_ref[...]).astype(BF16)
        merged = jnp.concatenate([an, conv_rows(r0, CHUNK_ROWS)], axis=-1)
        mix = jnp.dot(merged, wout_ref[...], preferred_element_type=F32)
        out_ref[rows, :] = x_ref[rows, :] + gate * mix


def _mixer_call(x_blocks, mod, mod_row, ctx_k, ctx_v, rope, consts, *, seq_len, emit_kv, side=()):
    n_blocks = x_blocks.shape[0]
    ctx_len = 0 if ctx_k is None else ctx_k.shape[4]
    use_rope = rope is not None
    if ctx_len:
        assert seq_len == BLOCK_ROWS
        q_rows, n_keys, n_key_blocks = LATENT_Q_ROWS, ctx_len + BLOCK_ROWS, 1
    else:
        q_rows, n_keys, n_key_blocks = seq_len, seq_len, BLOCK_ROWS // seq_len
    n_q_cols = KV_GROUP * q_rows

    blk = lambda cols: pl.BlockSpec((None, BLOCK_ROWS, cols), lambda b: (b, 0, 0))
    assert mod_row is not None or n_blocks <= MOD_ROWS
    args = [x_blocks, mod]
    in_specs = [blk(D_MODEL), _mod_spec(mod_row, 0)]
    if ctx_len:
        args += [ctx_k, ctx_v]
        in_specs += [pl.BlockSpec((None, None, N_KV, HEAD_DIM, ctx_len),
                                  lambda b: (b, 0, 0, 0, 0))] * 2
    if use_rope:
        args += list(rope)
        in_specs += [_const_spec((BLOCK_ROWS, LANES))] * 2
    args += list(consts)
    in_specs += [_const_spec(a.shape) for a in consts]
    side_specs, side_shapes = [], []
    for w, col_blocks, col_block in side:
        rows, cols = w.shape[0] // n_blocks, w.shape[1] // col_blocks
        assert rows * n_blocks == w.shape[0] and cols * col_blocks == w.shape[1]
        side_specs.append(pl.BlockSpec((rows, cols), lambda b, col_block=col_block: (b, col_block)))
        side_shapes.append(jax.ShapeDtypeStruct((w.shape[0], cols), BF16))
    args += [w for w, _, _ in side]
    in_specs += side_specs

    out_shape = [jax.ShapeDtypeStruct((n_blocks, BLOCK_ROWS, D_MODEL), F32)]
    out_specs = [blk(D_MODEL)]
    if emit_kv:
        per_block = BLOCK_ROWS // seq_len
        out_shape += [jax.ShapeDtypeStruct(
            (n_blocks * per_block, 1, N_KV, HEAD_DIM, seq_len), F32)] * 2
        out_specs += [pl.BlockSpec((per_block, None, N_KV, HEAD_DIM, seq_len),
                                   lambda b: (b, 0, 0, 0, 0))] * 2
    out_shape += side_shapes
    out_specs += [pl.BlockSpec(s.block_shape, lambda b: (b, 0)) for s in side_specs]

    scratch = [
        pltpu.VMEM((N_HEADS, BLOCK_ROWS, HEAD_DIM), BF16),
        pltpu.VMEM((N_KV, ctx_len + BLOCK_ROWS, HEAD_DIM), BF16),
        pltpu.VMEM((n_key_blocks, N_KV, V_ROWS, n_keys), BF16),
        pltpu.VMEM((BLOCK_ROWS, ATTN_DIM), F32),
        pltpu.VMEM((BLOCK_ROWS + 16, CONV_DIM), F32),
        pltpu.VMEM((BLOCK_ROWS, CONV_DIM), F32),
        pltpu.VMEM((n_keys, n_q_cols), F32),
        pltpu.VMEM((2, N_KV, n_keys, n_q_cols), BF16),
    ]
    kern = functools.partial(_mixer_kernel, seq_len=seq_len, ctx_len=ctx_len,
                             use_rope=use_rope, emit_kv=emit_kv, mod_row=mod_row,
                             n_side=len(side))
    return pl.pallas_call(
        kern,
        grid=(n_blocks,),
        in_specs=in_specs,
        out_specs=out_specs,
        out_shape=out_shape,
        scratch_shapes=scratch,
        compiler_params=pltpu.CompilerParams(
            dimension_semantics=("arbitrary",),
            vmem_limit_bytes=_vmem_limit(
                [(s, a.dtype) for s, a in zip(in_specs, args)]
                + [(s, o.dtype) for s, o in zip(out_specs, out_shape)],
                scratch)),
        name="mixer_ctx" if emit_kv else "mixer_latent",
    )(*args)


def _ffn_rows(x_ref, mod, nffn_ref, wg_ref, wu_ref, wd_ref, out_ref):
    shift = mod[:, 0:D_MODEL]
    scale1 = 1.0 + mod[:, D_MODEL:2 * D_MODEL]
    gate = mod[:, 2 * D_MODEL:3 * D_MODEL]
    x = x_ref[...]
    hb = ((_rms_rows(x) * nffn_ref[...]) * scale1 + shift).astype(BF16)
    acc = None
    for c0, cw in FF_CHUNKS:
        gt = jnp.dot(hb, wg_ref[:, c0:c0 + cw], preferred_element_type=F32)
        up = jnp.dot(hb, wu_ref[:, c0:c0 + cw], preferred_element_type=F32)
        act = ((gt * jax.nn.sigmoid(gt)) * up).astype(BF16)
        part = jnp.dot(act, wd_ref[c0:c0 + cw, :], preferred_element_type=F32)
        acc = part if acc is None else acc + part
    out_ref[...] = x + gate * acc


def _ffn_kernel(xa_ref, xb_ref, mod_ref, nffn_ref, wg_ref, wu_ref, wd_ref, ya_ref, yb_ref,
                *, n_a, mod_row_a, tiles_per_mod_row_b):
    i = pl.program_id(0)

    @pl.when(i < n_a)
    def _():
        mod = mod_ref[mod_row_a % MOD_ROWS:mod_row_a % MOD_ROWS + 1, :]
        _ffn_rows(xa_ref, mod, nffn_ref, wg_ref, wu_ref, wd_ref, ya_ref)

    @pl.when(i >= n_a)
    def _():
        mod = mod_ref[pl.ds((i - n_a) // tiles_per_mod_row_b, 1), :]
        _ffn_rows(xb_ref, mod, nffn_ref, wg_ref, wu_ref, wd_ref, yb_ref)


def _ffn_call(xa_rows, xb_rows, mod, mod_row_a, rows_per_mod_row_b, norm_ffn, w_gate, w_up, w_down):
    n_a, n_b = xa_rows.shape[0] // FFN_ROWS, xb_rows.shape[0] // FFN_ROWS
    tiles_per_row_b = rows_per_mod_row_b // FFN_ROWS
    assert n_b // tiles_per_row_b <= MOD_ROWS
    spec_a = pl.BlockSpec((FFN_ROWS, D_MODEL), lambda i: (jnp.minimum(i, n_a - 1), 0))
    spec_b = pl.BlockSpec((FFN_ROWS, D_MODEL), lambda i: (jnp.maximum(i - n_a, 0), 0))
    mod_spec = pl.BlockSpec((MOD_ROWS, 3 * D_MODEL),
                            lambda i: (jnp.where(i < n_a, mod_row_a // MOD_ROWS, 0), 1))
    args = (xa_rows, xb_rows, mod, norm_ffn, w_gate, w_up, w_down)
    assert all(w.dtype == BF16 for w in args[4:])
    in_specs = [spec_a, spec_b, mod_spec] + [_const_spec(a.shape) for a in args[3:]]
    return pl.pallas_call(
        functools.partial(_ffn_kernel, n_a=n_a, mod_row_a=mod_row_a,
                          tiles_per_mod_row_b=tiles_per_row_b),
        grid=(n_a + n_b,),
        in_specs=in_specs,
        out_specs=[spec_a, spec_b],
        out_shape=[jax.ShapeDtypeStruct(xa_rows.shape, F32),
                   jax.ShapeDtypeStruct(xb_rows.shape, F32)],
        compiler_params=pltpu.CompilerParams(
            dimension_semantics=("arbitrary",),
            vmem_limit_bytes=_vmem_limit(
                [(s, a.dtype) for s, a in zip(in_specs, args)]
                + [(spec_a, F32), (spec_b, F32)])),
        name="ffn",
    )(*args)


def _rope_tables(n_tokens):
    rows = n_tokens // GRID_W
    row = jnp.repeat(jnp.arange(rows, dtype=F32), GRID_W)
    col = jnp.tile(jnp.arange(GRID_W, dtype=F32), rows)
    inv = 1.0 / (ROPE_THETA ** (jnp.arange(ROT_PAIRS, dtype=F32) / ROT_PAIRS))
    ang = jnp.stack([row[:, None] * inv, col[:, None] * inv], axis=1)
    cos, sin = jnp.cos(ang), jnp.sin(ang)
    cos_h = jnp.concatenate([cos, cos], axis=-1).reshape(n_tokens, HEAD_DIM)
    sin_h = jnp.concatenate([-sin, sin], axis=-1).reshape(n_tokens, HEAD_DIM)
    reps = LANES // HEAD_DIM
    return jnp.tile(cos_h, (1, reps)), jnp.tile(sin_h, (1, reps))


def _keys_minor(kv):
    return jnp.transpose(kv, (0, 1, 3, 4, 2))


def _group_mean_matrix():
    idx = np.arange(MXU_DIM) // HEAD_DIM
    g = (idx[:, None] == idx[None, :]).astype(np.float32) / HEAD_DIM
    return jnp.asarray(g, dtype=BF16)


def kernel(x_prompt, x_sample, c, cache_k, cache_v, c_ctx, norm_mix, norm_ffn, w_ada, b_ada,
           w_in, q_norm, k_norm, conv_w, attn_out_norm, conv_out_norm, w_out, w_gate_up, w_down):
    depth = w_in.shape[0]
    assert depth == 1
    n_prompt, seq, _ = x_prompt.shape
    n_sample, dec_seq, _ = x_sample.shape
    assert dec_seq == BLOCK_ROWS and BLOCK_ROWS % seq == 0 and n_sample <= CTX_ROW

    cond = jnp.zeros((COND_ROWS, D_MODEL), F32)
    cond = cond.at[0:n_sample].set(c).at[CTX_ROW].set(c_ctx)
    mod = _ada_call(cond, w_ada[0], b_ada[0][None, :])

    consts = (
        norm_mix[0][None, :],
        w_in[0].astype(BF16),
        _group_mean_matrix(),
        jnp.concatenate([jnp.tile(q_norm[0], N_HEADS), jnp.tile(k_norm[0], N_KV)])[None, :],
        conv_w[0],
        attn_out_norm[0][None, :],
        conv_out_norm[0][None, :],
        w_out[0].astype(BF16),
    )
    nffn = norm_ffn[0][None, :]


    xs1, w_gate, w_up = _mixer_call(
        x_sample, mod, None, _keys_minor(cache_k), _keys_minor(cache_v), _rope_tables(dec_seq),
        consts, seq_len=dec_seq, emit_kv=False, side=((w_gate_up[0], 2, 0), (w_gate_up[0], 2, 1)))

    per_block = BLOCK_ROWS // seq
    xp_blocks = x_prompt.reshape(n_prompt // per_block, BLOCK_ROWS, D_MODEL)
    xp1, k_new, v_new, w_down_b = _mixer_call(
        xp_blocks, mod, CTX_ROW, None, None, None, consts, seq_len=seq, emit_kv=True,
        side=((w_down[0], 1, 0),))

    yp, ys = _ffn_call(xp1.reshape(-1, D_MODEL), xs1.reshape(-1, D_MODEL), mod, CTX_ROW, dec_seq,
                       nffn, w_gate, w_up, w_down_b)

    return (yp.reshape(n_prompt, seq, D_MODEL),
            ys.reshape(n_sample, dec_seq, D_MODEL),
            jnp.transpose(k_new, (0, 1, 4, 2, 3)),
            jnp.transpose(v_new, (0, 1, 4, 2, 3)))
```

```python
import functools

import numpy as np
import jax
import jax.numpy as jnp
from jax import lax
from jax.experimental import pallas as pl
from jax.experimental.pallas import tpu as pltpu

D_MODEL = 1024
HEAD_DIM = 64
ATTN_DIM = 512
N_HEADS = 8
N_KV = 2
KV_GROUP = N_HEADS // N_KV
KV_DIM = N_KV * HEAD_DIM
CONV_DIM = 512
D_FF = 2816
QK_DIM = ATTN_DIM + KV_DIM
QKV_DIM = ATTN_DIM + 2 * KV_DIM
IN_DIM = QKV_DIM + 3 * CONV_DIM
GRID_W = 64
ROT_PAIRS = HEAD_DIM // 4
ROPE_THETA = 10000.0
RMS_EPS = 1e-6
LOG2_E = 1.4426950408889634
Q_SCALE = HEAD_DIM ** -0.5 * LOG2_E
V_ROWS = HEAD_DIM + 16
MAX_UNSHIFTED_SCORE = 64.0

LANES = 128
MXU_DIM = 256
BLOCK_ROWS = 1024
CHUNK_ROWS = 512
LATENT_Q_ROWS = 256
P_ROWS = 16
FFN_ROWS = 512
GU_CAST_ROWS = 128
DOWN_CAST_ROWS = 352
FF_CHUNKS = ((0, 1024), (1024, 1024), (2048, 768))
COND_ROWS = 16
MOD_ROWS = 8
CTX_ROW = 8
ADA_COLS = 512
VMEM_TEMP_BYTES = 8 * 1024 * 1024
VMEM_MAX_BYTES = 56 * 1024 * 1024

F32 = jnp.float32
BF16 = jnp.bfloat16


def _vmem_limit(windows, scratch=()):
    total = VMEM_TEMP_BYTES
    for spec, dtype in windows:
        if spec.block_shape is None:
            continue
        n = int(np.prod([1 if d is None else d for d in spec.block_shape]))
        total += n * jnp.dtype(dtype).itemsize * (1 if spec.pipeline_mode is not None else 2)
    for s in scratch:
        if s.memory_space == pltpu.VMEM:
            total += int(np.prod(s.shape)) * jnp.dtype(s.dtype).itemsize
    return min(total, VMEM_MAX_BYTES)


def _mod_spec(row, half):
    blk = 0 if row is None else row // MOD_ROWS
    return pl.BlockSpec((MOD_ROWS, 3 * D_MODEL), lambda i: (blk, half))


def _mod_row(mod_ref, row):
    r = pl.program_id(0) if row is None else row % MOD_ROWS
    return mod_ref[pl.ds(r, 1), :]


def _const_spec(shape):
    nd = len(shape)
    return pl.BlockSpec(shape, lambda *_: (0,) * nd, pipeline_mode=pl.Buffered(1))


def _ada_kernel(cond_ref, w_ref, b_ref, out_ref):
    c = cond_ref[...]
    s = (c * jax.nn.sigmoid(c)).astype(BF16)
    out_ref[...] = jnp.dot(s, w_ref[...].astype(BF16), preferred_element_type=F32) + b_ref[...]


def _ada_call(cond, w_ada, b_ada):
    n = w_ada.shape[1]
    in_specs = [
        pl.BlockSpec((COND_ROWS, D_MODEL), lambda j: (0, 0)),
        pl.BlockSpec((D_MODEL, ADA_COLS), lambda j: (0, j)),
        pl.BlockSpec((1, ADA_COLS), lambda j: (0, j)),
    ]
    out_spec = pl.BlockSpec((COND_ROWS, ADA_COLS), lambda j: (0, j))
    return pl.pallas_call(
        _ada_kernel,
        grid=(n // ADA_COLS,),
        in_specs=in_specs,
        out_specs=out_spec,
        out_shape=jax.ShapeDtypeStruct((COND_ROWS, n), F32),
        compiler_params=pltpu.CompilerParams(
            dimension_semantics=("arbitrary",),
            vmem_limit_bytes=VMEM_MAX_BYTES),
        name="ada_rows",
    )(cond, w_ada, b_ada)


def _rms_rows(x):
    return x * lax.rsqrt(jnp.mean(x * x, axis=-1, keepdims=True) + RMS_EPS)


def _mixer_kernel(*refs, seq_len, ctx_len, use_rope, emit_kv, mod_row):
    it = iter(refs)
    x_ref, mod_ref = next(it), next(it)
    if ctx_len:
        ck_ref, cv_ref = next(it), next(it)
    if use_rope:
        cos_ref, sin_ref = next(it), next(it)
    (nmix_ref, win_ref, gsum_ref, qkg_ref, convw_ref, ga_ref, gc_ref, wout_ref) = (
        next(it) for _ in range(8))
    out_ref = next(it)
    if emit_kv:
        ko_ref, vo_ref = next(it), next(it)
    (q_s, k_s, vt_s, attn_s, t_s, gb_s, s_ref, p_ref) = (next(it) for _ in range(8))

    n_chunks = BLOCK_ROWS // CHUNK_ROWS
    if ctx_len:
        q_rows, n_keys, key_blk = LATENT_Q_ROWS, ctx_len + seq_len, BLOCK_ROWS
    else:
        q_rows, n_keys, key_blk = seq_len, seq_len, seq_len
    n_qb = BLOCK_ROWS // q_rows

    mod = _mod_row(mod_ref, mod_row)
    shift = mod[:, 0:D_MODEL]
    scale1 = 1.0 + mod[:, D_MODEL:2 * D_MODEL]
    gate = mod[:, 2 * D_MODEL:3 * D_MODEL]

    def put_values_t(blk, off, vt):
        vt = vt.astype(BF16)
        for g in range(N_KV):
            vt_s[blk, g, 0:HEAD_DIM, off:off + vt.shape[1]] = vt[g * HEAD_DIM:(g + 1) * HEAD_DIM, :]

    tail = (lax.broadcasted_iota(jnp.int32, (V_ROWS - HEAD_DIM, n_keys), 0) == 0).astype(BF16)
    for blk in range(vt_s.shape[0]):
        for g in range(N_KV):
            vt_s[blk, g, HEAD_DIM:V_ROWS, :] = tail

    if ctx_len:
        ck = ck_ref[...].reshape(KV_DIM, ctx_len).T.astype(BF16)
        for g in range(N_KV):
            k_s[g, 0:ctx_len, :] = ck[:, g * HEAD_DIM:(g + 1) * HEAD_DIM]
        put_values_t(0, 0, cv_ref[...].reshape(KV_DIM, ctx_len))

    if use_rope:
        lane = lax.broadcasted_iota(jnp.int32, (CHUNK_ROWS, LANES), 1)
        first_half = (lane % (2 * ROT_PAIRS)) < ROT_PAIRS

    t_s[0:8, :] = jnp.zeros((8, CONV_DIM), F32)
    t_s[8 + BLOCK_ROWS:16 + BLOCK_ROWS, :] = jnp.zeros((8, CONV_DIM), F32)
    w0 = convw_ref[0:1, :]
    w1 = convw_ref[1:2, :]
    w2 = convw_ref[2:3, :]

    def conv_rows(a, n):
        slab = t_s[a:a + n + 16, :]
        t_prev = pltpu.roll(slab, 1, axis=0)[8:8 + n, :]
        t_mid = slab[8:8 + n, :]
        t_next = pltpu.roll(slab, n + 15, axis=0)[8:8 + n, :]
        if seq_len < BLOCK_ROWS:
            pos = (lax.broadcasted_iota(jnp.int32, (n, 1), 0) + a) % seq_len
            t_prev = jnp.where(pos == 0, 0.0, t_prev)
            t_next = jnp.where(pos == seq_len - 1, 0.0, t_next)
        y = gb_s[a:a + n, :] * (w0 * t_prev + w1 * t_mid + w2 * t_next)
        return (_rms_rows(y) * gc_ref[...]).astype(BF16)

    def project_conv(r0, hb):
        cvp = jnp.dot(hb, win_ref[:, QKV_DIM:IN_DIM], preferred_element_type=F32)
        gb_s[r0:r0 + CHUNK_ROWS, :] = cvp[:, 0:CONV_DIM]
        t_s[8 + r0:8 + r0 + CHUNK_ROWS, :] = (cvp[:, CONV_DIM:2 * CONV_DIM]
                                              * cvp[:, 2 * CONV_DIM:3 * CONV_DIM])

    for c in range(n_chunks):
        r0 = c * CHUNK_ROWS
        x = x_ref[r0:r0 + CHUNK_ROWS, :]
        h = (_rms_rows(x) * nmix_ref[...]) * scale1 + shift
        hb = h.astype(BF16)
        qkv = jnp.dot(hb, win_ref[:, 0:QKV_DIM], preferred_element_type=F32)

        groups = []
        for g0 in range(0, QKV_DIM, MXU_DIM):
            sq = qkv[:, g0:g0 + MXU_DIM]
            groups.append(jnp.dot((sq * sq).astype(BF16), gsum_ref[...],
                                  preferred_element_type=F32))
        ms = jnp.concatenate(groups, axis=-1)[:, 0:QK_DIM]
        qk = (qkv[:, 0:QK_DIM] * lax.rsqrt(ms + RMS_EPS)) * qkg_ref[...]
        vv = qkv[:, QK_DIM:QKV_DIM]

        if emit_kv:
            for r1 in range(0, CHUNK_ROWS, seq_len):
                kt = qk[r1:r1 + seq_len, ATTN_DIM:QK_DIM].T
                ko_ref[(r0 + r1) // seq_len] = kt.reshape(N_KV, HEAD_DIM, seq_len)

        for cg in range(QK_DIM // LANES):
            xg = qk[:, cg * LANES:(cg + 1) * LANES]
            if use_rope:
                cs = cos_ref[r0:r0 + CHUNK_ROWS, :]
                sn = sin_ref[r0:r0 + CHUNK_ROWS, :]
                partner = jnp.where(first_half,
                                    pltpu.roll(xg, LANES - ROT_PAIRS, axis=1),
                                    pltpu.roll(xg, ROT_PAIRS, axis=1))
                xg = xg * cs + partner * sn
            if cg < ATTN_DIM // LANES:
                xb = (xg * Q_SCALE).astype(BF16)
                q_s[2 * cg, r0:r0 + CHUNK_ROWS, :] = xb[:, 0:HEAD_DIM]
                q_s[2 * cg + 1, r0:r0 + CHUNK_ROWS, :] = xb[:, HEAD_DIM:LANES]
            else:
                xb = xg.astype(BF16)
                for g in range(N_KV):
                    k_s[g, ctx_len + r0:ctx_len + r0 + CHUNK_ROWS, :] = (
                        xb[:, g * HEAD_DIM:(g + 1) * HEAD_DIM])
        w = min(key_blk, CHUNK_ROWS)
        for r1 in range(r0, r0 + CHUNK_ROWS, w):
            blk, off = (0, ctx_len + r1) if ctx_len else (r1 // key_blk, 0)
            vt = vv[r1 - r0:r1 - r0 + w, :].T
            put_values_t(blk, off, vt)
            if emit_kv:
                vo_ref[blk] = vt.reshape(N_KV, HEAD_DIM, seq_len)

        project_conv(r0, hb)

    def scores_t(qb, g):
        r0 = pl.multiple_of(qb * q_rows, q_rows)
        k0 = 0 if ctx_len else r0
        qs = jnp.concatenate(
            [q_s[KV_GROUP * g + j, pl.ds(r0, q_rows), :] for j in range(KV_GROUP)], axis=0)
        kk = k_s[g, pl.ds(k0, n_keys), :]
        return lax.dot_general(kk, qs, (((1,), (1,)), ((), ())),
                               preferred_element_type=F32)

    def values_out(qb, g, p):
        r0 = pl.multiple_of(qb * q_rows, q_rows)
        blk = 0 if ctx_len else qb
        ot = jnp.dot(vt_s[blk, g], p, preferred_element_type=F32)
        ot = ot[0:HEAD_DIM, :] / ot[HEAD_DIM:HEAD_DIM + 1, :]
        for jj in range(KV_GROUP // 2):
            pair_t = jnp.concatenate(
                [ot[:, (2 * jj) * q_rows:(2 * jj + 1) * q_rows],
                 ot[:, (2 * jj + 1) * q_rows:(2 * jj + 2) * q_rows]], axis=0)
            col = (KV_GROUP * g + 2 * jj) * HEAD_DIM
            attn_s[pl.ds(r0, q_rows), col:col + LANES] = pair_t.T

    def pipeline_pairs(step):
        assert n_qb % 2 == 0
        step(-1, 1, first=True)
        step(0, 0)

        def body(t, carry):
            step(2 * t + 1, 1)
            step(2 * t + 2, 0)
            return carry

        lax.fori_loop(0, (n_qb - 2) // 2, body, 0)
        step(n_qb - 1, 1, last=True)

    def shifted_block(qb, carry):
        for g in range(N_KV):
            s = scores_t(qb, g)
            s_ref[...] = s
            m = jnp.max(s, axis=0, keepdims=True)
            for k1 in range(0, n_keys, P_ROWS):
                p_ref[0, g, k1:k1 + P_ROWS, :] = jnp.exp2(
                    s_ref[k1:k1 + P_ROWS, :] - m).astype(BF16)
            values_out(qb, g, p_ref[0, g])
        return carry

    def unshifted_step(j, par, first=False, last=False):
        for g in range(N_KV):
            if not last:
                p_ref[1 - par, g] = jnp.exp2(scores_t(j + 1, g)).astype(BF16)
            if not first:
                values_out(j, g, p_ref[par, g])

    gains = jnp.abs(qkg_ref[...])
    q_bound = (Q_SCALE * Q_SCALE * HEAD_DIM) * jnp.max(gains[:, 0:ATTN_DIM]) ** 2
    k_bound = HEAD_DIM * jnp.max(gains[:, ATTN_DIM:QK_DIM]) ** 2
    if ctx_len:
        ck2 = ck_ref[...]
        k_bound = jnp.maximum(k_bound, jnp.max(jnp.sum(ck2 * ck2, axis=1)))
    small_scores = q_bound * k_bound <= MAX_UNSHIFTED_SCORE ** 2
    pl.when(small_scores)(lambda: pipeline_pairs(unshifted_step))

    @pl.when(jnp.logical_not(small_scores))
    def _():
        lax.fori_loop(0, n_qb, shifted_block, 0)

    for r0 in range(0, BLOCK_ROWS, CHUNK_ROWS):
        rows = slice(r0, r0 + CHUNK_ROWS)
        an = (_rms_rows(attn_s[rows, :]) * ga_ref[...]).astype(BF16)
        merged = jnp.concatenate([an, conv_rows(r0, CHUNK_ROWS)], axis=-1)
        mix = jnp.dot(merged, wout_ref[...], preferred_element_type=F32)
        out_ref[rows, :] = x_ref[rows, :] + gate * mix


def _mixer_call(x_blocks, mod, mod_row, ctx_k, ctx_v, rope, consts, *, seq_len, emit_kv):
    n_blocks = x_blocks.shape[0]
    ctx_len = 0 if ctx_k is None else ctx_k.shape[4]
    use_rope = rope is not None
    if ctx_len:
        assert seq_len == BLOCK_ROWS
        q_rows, n_keys, n_key_blocks = LATENT_Q_ROWS, ctx_len + BLOCK_ROWS, 1
    else:
        q_rows, n_keys, n_key_blocks = seq_len, seq_len, BLOCK_ROWS // seq_len
    n_q_cols = KV_GROUP * q_rows

    blk = lambda cols: pl.BlockSpec((None, BLOCK_ROWS, cols), lambda b: (b, 0, 0))
    assert mod_row is not None or n_blocks <= MOD_ROWS
    args = [x_blocks, mod]
    in_specs = [blk(D_MODEL), _mod_spec(mod_row, 0)]
    if ctx_len:
        args += [ctx_k, ctx_v]
        in_specs += [pl.BlockSpec((None, None, N_KV, HEAD_DIM, ctx_len),
                                  lambda b: (b, 0, 0, 0, 0))] * 2
    if use_rope:
        args += list(rope)
        in_specs += [_const_spec((BLOCK_ROWS, LANES))] * 2
    args += list(consts)
    in_specs += [_const_spec(a.shape) for a in consts]

    out_shape = [jax.ShapeDtypeStruct((n_blocks, BLOCK_ROWS, D_MODEL), F32)]
    out_specs = [blk(D_MODEL)]
    if emit_kv:
        per_block = BLOCK_ROWS // seq_len
        out_shape += [jax.ShapeDtypeStruct(
            (n_blocks * per_block, 1, N_KV, HEAD_DIM, seq_len), F32)] * 2
        out_specs += [pl.BlockSpec((per_block, None, N_KV, HEAD_DIM, seq_len),
                                   lambda b: (b, 0, 0, 0, 0))] * 2

    scratch = [
        pltpu.VMEM((N_HEADS, BLOCK_ROWS, HEAD_DIM), BF16),
        pltpu.VMEM((N_KV, ctx_len + BLOCK_ROWS, HEAD_DIM), BF16),
        pltpu.VMEM((n_key_blocks, N_KV, V_ROWS, n_keys), BF16),
        pltpu.VMEM((BLOCK_ROWS, ATTN_DIM), F32),
        pltpu.VMEM((BLOCK_ROWS + 16, CONV_DIM), F32),
        pltpu.VMEM((BLOCK_ROWS, CONV_DIM), F32),
        pltpu.VMEM((n_keys, n_q_cols), F32),
        pltpu.VMEM((2, N_KV, n_keys, n_q_cols), BF16),
    ]
    kern = functools.partial(_mixer_kernel, seq_len=seq_len, ctx_len=ctx_len,
                             use_rope=use_rope, emit_kv=emit_kv, mod_row=mod_row)
    return pl.pallas_call(
        kern,
        grid=(n_blocks,),
        in_specs=in_specs,
        out_specs=out_specs,
        out_shape=out_shape,
        scratch_shapes=scratch,
        compiler_params=pltpu.CompilerParams(
            dimension_semantics=("arbitrary",),
            vmem_limit_bytes=_vmem_limit(
                [(s, a.dtype) for s, a in zip(in_specs, args)] + [(s, F32) for s in out_specs],
                scratch)),
        name="mixer_ctx" if emit_kv else "mixer_latent",
    )(*args)


def _ffn_rows(x_ref, mod, nffn_ref, wgu_ref, wd_ref, out_ref):
    shift = mod[:, 0:D_MODEL]
    scale1 = 1.0 + mod[:, D_MODEL:2 * D_MODEL]
    gate = mod[:, 2 * D_MODEL:3 * D_MODEL]
    x = x_ref[...]
    hb = ((_rms_rows(x) * nffn_ref[...]) * scale1 + shift).astype(BF16)
    acc = None
    for c0, cw in FF_CHUNKS:
        gt = jnp.dot(hb, wgu_ref[:, c0:c0 + cw], preferred_element_type=F32)
        up = jnp.dot(hb, wgu_ref[:, D_FF + c0:D_FF + c0 + cw], preferred_element_type=F32)
        act = ((gt * jax.nn.sigmoid(gt)) * up).astype(BF16)
        part = jnp.dot(act, wd_ref[c0:c0 + cw, :], preferred_element_type=F32)
        acc = part if acc is None else acc + part
    out_ref[...] = x + gate * acc


def _cast_rows(src_hbm, dst, stage, sem, chunk_rows):
    n = src_hbm.shape[0] // chunk_rows

    def fetch(c):
        return pltpu.make_async_copy(src_hbm.at[pl.ds(c * chunk_rows, chunk_rows), :],
                                     stage.at[c % 2], sem.at[c % 2])

    fetch(0).start()
    for c in range(n):
        if c + 1 < n:
            fetch(c + 1).start()
        fetch(c).wait()
        dst[c * chunk_rows:(c + 1) * chunk_rows, :] = stage[c % 2].astype(BF16)


def _ffn_kernel(xa_ref, xb_ref, mod_ref, nffn_ref, wgu_hbm, wd_hbm, ya_ref, yb_ref,
                wgu_s, wd_s, stage_gu, stage_d, sem, *, n_a, mod_row_a, tiles_per_mod_row_b):
    i = pl.program_id(0)

    @pl.when(i == 0)
    def _():
        _cast_rows(wgu_hbm, wgu_s, stage_gu, sem, GU_CAST_ROWS)
        _cast_rows(wd_hbm, wd_s, stage_d, sem, DOWN_CAST_ROWS)

    @pl.when(i < n_a)
    def _():
        mod = mod_ref[mod_row_a % MOD_ROWS:mod_row_a % MOD_ROWS + 1, :]
        _ffn_rows(xa_ref, mod, nffn_ref, wgu_s, wd_s, ya_ref)

    @pl.when(i >= n_a)
    def _():
        mod = mod_ref[pl.ds((i - n_a) // tiles_per_mod_row_b, 1), :]
        _ffn_rows(xb_ref, mod, nffn_ref, wgu_s, wd_s, yb_ref)


def _ffn_call(xa_rows, xb_rows, mod, mod_row_a, rows_per_mod_row_b, norm_ffn, w_gate_up, w_down):
    n_a, n_b = xa_rows.shape[0] // FFN_ROWS, xb_rows.shape[0] // FFN_ROWS
    tiles_per_row_b = rows_per_mod_row_b // FFN_ROWS
    assert n_b // tiles_per_row_b <= MOD_ROWS
    spec_a = pl.BlockSpec((FFN_ROWS, D_MODEL), lambda i: (jnp.minimum(i, n_a - 1), 0))
    spec_b = pl.BlockSpec((FFN_ROWS, D_MODEL), lambda i: (jnp.maximum(i - n_a, 0), 0))
    mod_spec = pl.BlockSpec((MOD_ROWS, 3 * D_MODEL),
                            lambda i: (jnp.where(i < n_a, mod_row_a // MOD_ROWS, 0), 1))
    hbm = pl.BlockSpec(memory_space=pl.ANY)
    in_specs = [spec_a, spec_b, mod_spec, _const_spec(norm_ffn.shape), hbm, hbm]
    scratch = [pltpu.VMEM(w_gate_up.shape, BF16), pltpu.VMEM(w_down.shape, BF16),
               pltpu.VMEM((2, GU_CAST_ROWS, w_gate_up.shape[1]), F32),
               pltpu.VMEM((2, DOWN_CAST_ROWS, w_down.shape[1]), F32),
               pltpu.SemaphoreType.DMA((2,))]
    return pl.pallas_call(
        functools.partial(_ffn_kernel, n_a=n_a, mod_row_a=mod_row_a,
                          tiles_per_mod_row_b=tiles_per_row_b),
        grid=(n_a + n_b,),
        in_specs=in_specs,
        out_specs=[spec_a, spec_b],
        out_shape=[jax.ShapeDtypeStruct(xa_rows.shape, F32),
                   jax.ShapeDtypeStruct(xb_rows.shape, F32)],
        scratch_shapes=scratch,
        compiler_params=pltpu.CompilerParams(
            dimension_semantics=("arbitrary",),
            vmem_limit_bytes=_vmem_limit(
                [(s, F32) for s in in_specs + [spec_a, spec_b]], scratch)),
        name="ffn",
    )(xa_rows, xb_rows, mod, norm_ffn, w_gate_up, w_down)


def _rope_tables(n_tokens):
    rows = n_tokens // GRID_W
    row = jnp.repeat(jnp.arange(rows, dtype=F32), GRID_W)
    col = jnp.tile(jnp.arange(GRID_W, dtype=F32), rows)
    inv = 1.0 / (ROPE_THETA ** (jnp.arange(ROT_PAIRS, dtype=F32) / ROT_PAIRS))
    ang = jnp.stack([row[:, None] * inv, col[:, None] * inv], axis=1)
    cos, sin = jnp.cos(ang), jnp.sin(ang)
    cos_h = jnp.concatenate([cos, cos], axis=-1).reshape(n_tokens, HEAD_DIM)
    sin_h = jnp.concatenate([-sin, sin], axis=-1).reshape(n_tokens, HEAD_DIM)
    reps = LANES // HEAD_DIM
    return jnp.tile(cos_h, (1, reps)), jnp.tile(sin_h, (1, reps))


def _keys_minor(kv):
    return jnp.transpose(kv, (0, 1, 3, 4, 2))


def _group_mean_matrix():
    idx = np.arange(MXU_DIM) // HEAD_DIM
    g = (idx[:, None] == idx[None, :]).astype(np.float32) / HEAD_DIM
    return jnp.asarray(g, dtype=BF16)


def kernel(x_prompt, x_sample, c, cache_k, cache_v, c_ctx, norm_mix, norm_ffn, w_ada, b_ada,
           w_in, q_norm, k_norm, conv_w, attn_out_norm, conv_out_norm, w_out, w_gate_up, w_down):
    depth = w_in.shape[0]
    assert depth == 1
    n_prompt, seq, _ = x_prompt.shape
    n_sample, dec_seq, _ = x_sample.shape
    assert dec_seq == BLOCK_ROWS and BLOCK_ROWS % seq == 0 and n_sample <= CTX_ROW

    cond = jnp.zeros((COND_ROWS, D_MODEL), F32)
    cond = cond.at[0:n_sample].set(c).at[CTX_ROW].set(c_ctx)
    mod = _ada_call(cond, w_ada[0], b_ada[0][None, :])

    consts = (
        norm_mix[0][None, :],
        w_in[0].astype(BF16),
        _group_mean_matrix(),
        jnp.concatenate([jnp.tile(q_norm[0], N_HEADS), jnp.tile(k_norm[0], N_KV)])[None, :],
        conv_w[0],
        attn_out_norm[0][None, :],
        conv_out_norm[0][None, :],
        w_out[0].astype(BF16),
    )
    nffn = norm_ffn[0][None, :]

    (xs1,) = _mixer_call(x_sample, mod, None, _keys_minor(cache_k), _keys_minor(cache_v),
                         _rope_tables(dec_seq), consts, seq_len=dec_seq, emit_kv=False)

    per_block = BLOCK_ROWS // seq
    xp_blocks = x_prompt.reshape(n_prompt // per_block, BLOCK_ROWS, D_MODEL)
    xp1, k_new, v_new = _mixer_call(xp_blocks, mod, CTX_ROW, None, None, None, consts,
                                    seq_len=seq, emit_kv=True)

    yp, ys = _ffn_call(xp1.reshape(-1, D_MODEL), xs1.reshape(-1, D_MODEL), mod, CTX_ROW, dec_seq,
                       nffn, w_gate_up[0], w_down[0])

    return (yp.reshape(n_prompt, seq, D_MODEL),
            ys.reshape(n_sample, dec_seq, D_MODEL),
            jnp.transpose(k_new, (0, 1, 4, 2, 3)),
            jnp.transpose(v_new, (0, 1, 4, 2, 3)))
```

```python
import functools

import numpy as np
import jax
import jax.numpy as jnp
from jax import lax
from jax.experimental import pallas as pl
from jax.experimental.pallas import tpu as pltpu

D_MODEL = 1024
HEAD_DIM = 64
ATTN_DIM = 512
N_HEADS = 8
N_KV = 2
KV_GROUP = N_HEADS // N_KV
KV_DIM = N_KV * HEAD_DIM
CONV_DIM = 512
D_FF = 2816
QK_DIM = ATTN_DIM + KV_DIM
QKV_DIM = ATTN_DIM + 2 * KV_DIM
IN_DIM = QKV_DIM + 3 * CONV_DIM
GRID_W = 64
ROT_PAIRS = HEAD_DIM // 4
ROPE_THETA = 10000.0
RMS_EPS = 1e-6
LOG2_E = 1.4426950408889634
Q_SCALE = HEAD_DIM ** -0.5 * LOG2_E
V_ROWS = HEAD_DIM + 16
MAX_UNSHIFTED_SCORE = 64.0

LANES = 128
MXU_DIM = 256
BLOCK_ROWS = 1024
CHUNK_ROWS = 512
LATENT_Q_ROWS = 256
P_ROWS = 16
FFN_ROWS = 512
GU_CAST_ROWS = 128
DOWN_CAST_ROWS = 352
FF_CHUNKS = ((0, 1024), (1024, 1024), (2048, 768))
COND_ROWS = 16
MOD_ROWS = 8
CTX_ROW = 8
ADA_COLS = 2048
VMEM_TEMP_BYTES = 8 * 1024 * 1024
VMEM_MAX_BYTES = 56 * 1024 * 1024

F32 = jnp.float32
BF16 = jnp.bfloat16


def _vmem_limit(windows, scratch=()):
    total = VMEM_TEMP_BYTES
    for spec, dtype in windows:
        if spec.block_shape is None:
            continue
        n = int(np.prod([1 if d is None else d for d in spec.block_shape]))
        total += n * jnp.dtype(dtype).itemsize * (1 if spec.pipeline_mode is not None else 2)
    for s in scratch:
        if s.memory_space == pltpu.VMEM:
            total += int(np.prod(s.shape)) * jnp.dtype(s.dtype).itemsize
    return min(total, VMEM_MAX_BYTES)


def _mod_spec(row, half):
    blk = 0 if row is None else row // MOD_ROWS
    return pl.BlockSpec((MOD_ROWS, 3 * D_MODEL), lambda i: (blk, half))


def _mod_row(mod_ref, row):
    r = pl.program_id(0) if row is None else row % MOD_ROWS
    return mod_ref[pl.ds(r, 1), :]


def _const_spec(shape):
    nd = len(shape)
    return pl.BlockSpec(shape, lambda *_: (0,) * nd, pipeline_mode=pl.Buffered(1))


def _ada_kernel(cond_ref, w_ref, b_ref, out_ref):
    c = cond_ref[...]
    s = (c * jax.nn.sigmoid(c)).astype(BF16)
    out_ref[...] = jnp.dot(s, w_ref[...].astype(BF16), preferred_element_type=F32) + b_ref[...]


def _ada_call(cond, w_ada, b_ada):
    n = w_ada.shape[1]
    in_specs = [
        pl.BlockSpec((COND_ROWS, D_MODEL), lambda j: (0, 0)),
        pl.BlockSpec((D_MODEL, ADA_COLS), lambda j: (0, j)),
        pl.BlockSpec((1, ADA_COLS), lambda j: (0, j)),
    ]
    out_spec = pl.BlockSpec((COND_ROWS, ADA_COLS), lambda j: (0, j))
    return pl.pallas_call(
        _ada_kernel,
        grid=(n // ADA_COLS,),
        in_specs=in_specs,
        out_specs=out_spec,
        out_shape=jax.ShapeDtypeStruct((COND_ROWS, n), F32),
        compiler_params=pltpu.CompilerParams(
            dimension_semantics=("arbitrary",),
            vmem_limit_bytes=_vmem_limit([(s, F32) for s in in_specs + [out_spec]])),
        name="ada_rows",
    )(cond, w_ada, b_ada)


def _rms_rows(x):
    return x * lax.rsqrt(jnp.mean(x * x, axis=-1, keepdims=True) + RMS_EPS)


def _mixer_kernel(*refs, seq_len, ctx_len, use_rope, emit_kv, mod_row):
    it = iter(refs)
    x_ref, mod_ref = next(it), next(it)
    if ctx_len:
        ck_ref, cv_ref = next(it), next(it)
    if use_rope:
        cos_ref, sin_ref = next(it), next(it)
    (nmix_ref, win_ref, gsum_ref, qkg_ref, convw_ref, ga_ref, gc_ref, wout_ref) = (
        next(it) for _ in range(8))
    out_ref = next(it)
    if emit_kv:
        ko_ref, vo_ref = next(it), next(it)
    (q_s, k_s, vt_s, attn_s, t_s, gb_s, s_ref, p_ref) = (next(it) for _ in range(8))

    n_chunks = BLOCK_ROWS // CHUNK_ROWS
    if ctx_len:
        q_rows, n_keys, key_blk = LATENT_Q_ROWS, ctx_len + seq_len, BLOCK_ROWS
    else:
        q_rows, n_keys, key_blk = seq_len, seq_len, seq_len
    n_qb = BLOCK_ROWS // q_rows

    mod = _mod_row(mod_ref, mod_row)
    shift = mod[:, 0:D_MODEL]
    scale1 = 1.0 + mod[:, D_MODEL:2 * D_MODEL]
    gate = mod[:, 2 * D_MODEL:3 * D_MODEL]

    def put_values_t(blk, off, vt):
        vt = vt.astype(BF16)
        for g in range(N_KV):
            vt_s[blk, g, 0:HEAD_DIM, off:off + vt.shape[1]] = vt[g * HEAD_DIM:(g + 1) * HEAD_DIM, :]

    tail = (lax.broadcasted_iota(jnp.int32, (V_ROWS - HEAD_DIM, n_keys), 0) == 0).astype(BF16)
    for blk in range(vt_s.shape[0]):
        for g in range(N_KV):
            vt_s[blk, g, HEAD_DIM:V_ROWS, :] = tail

    if ctx_len:
        ck = ck_ref[...].reshape(KV_DIM, ctx_len).T.astype(BF16)
        for g in range(N_KV):
            k_s[g, 0:ctx_len, :] = ck[:, g * HEAD_DIM:(g + 1) * HEAD_DIM]
        put_values_t(0, 0, cv_ref[...].reshape(KV_DIM, ctx_len))

    if use_rope:
        lane = lax.broadcasted_iota(jnp.int32, (CHUNK_ROWS, LANES), 1)
        first_half = (lane % (2 * ROT_PAIRS)) < ROT_PAIRS

    t_s[0:8, :] = jnp.zeros((8, CONV_DIM), F32)
    t_s[8 + BLOCK_ROWS:16 + BLOCK_ROWS, :] = jnp.zeros((8, CONV_DIM), F32)
    w0 = convw_ref[0:1, :]
    w1 = convw_ref[1:2, :]
    w2 = convw_ref[2:3, :]

    def conv_rows(a, n):
        slab = t_s[a:a + n + 16, :]
        t_prev = pltpu.roll(slab, 1, axis=0)[8:8 + n, :]
        t_mid = slab[8:8 + n, :]
        t_next = pltpu.roll(slab, n + 15, axis=0)[8:8 + n, :]
        if seq_len < BLOCK_ROWS:
            pos = (lax.broadcasted_iota(jnp.int32, (n, 1), 0) + a) % seq_len
            t_prev = jnp.where(pos == 0, 0.0, t_prev)
            t_next = jnp.where(pos == seq_len - 1, 0.0, t_next)
        y = gb_s[a:a + n, :] * (w0 * t_prev + w1 * t_mid + w2 * t_next)
        return (_rms_rows(y) * gc_ref[...]).astype(BF16)

    def project_conv(r0, hb):
        cvp = jnp.dot(hb, win_ref[:, QKV_DIM:IN_DIM], preferred_element_type=F32)
        gb_s[r0:r0 + CHUNK_ROWS, :] = cvp[:, 0:CONV_DIM]
        t_s[8 + r0:8 + r0 + CHUNK_ROWS, :] = (cvp[:, CONV_DIM:2 * CONV_DIM]
                                              * cvp[:, 2 * CONV_DIM:3 * CONV_DIM])

    for c in range(n_chunks):
        r0 = c * CHUNK_ROWS
        x = x_ref[r0:r0 + CHUNK_ROWS, :]
        h = (_rms_rows(x) * nmix_ref[...]) * scale1 + shift
        hb = h.astype(BF16)
        qkv = jnp.dot(hb, win_ref[:, 0:QKV_DIM], preferred_element_type=F32)

        groups = []
        for g0 in range(0, QKV_DIM, MXU_DIM):
            sq = qkv[:, g0:g0 + MXU_DIM]
            groups.append(jnp.dot((sq * sq).astype(BF16), gsum_ref[...],
                                  preferred_element_type=F32))
        ms = jnp.concatenate(groups, axis=-1)[:, 0:QK_DIM]
        qk = (qkv[:, 0:QK_DIM] * lax.rsqrt(ms + RMS_EPS)) * qkg_ref[...]
        vv = qkv[:, QK_DIM:QKV_DIM]

        if emit_kv:
            for r1 in range(0, CHUNK_ROWS, seq_len):
                kt = qk[r1:r1 + seq_len, ATTN_DIM:QK_DIM].T
                ko_ref[(r0 + r1) // seq_len] = kt.reshape(N_KV, HEAD_DIM, seq_len)

        for cg in range(QK_DIM // LANES):
            xg = qk[:, cg * LANES:(cg + 1) * LANES]
            if use_rope:
                cs = cos_ref[r0:r0 + CHUNK_ROWS, :]
                sn = sin_ref[r0:r0 + CHUNK_ROWS, :]
                partner = jnp.where(first_half,
                                    pltpu.roll(xg, LANES - ROT_PAIRS, axis=1),
                                    pltpu.roll(xg, ROT_PAIRS, axis=1))
                xg = xg * cs + partner * sn
            if cg < ATTN_DIM // LANES:
                xb = (xg * Q_SCALE).astype(BF16)
                q_s[2 * cg, r0:r0 + CHUNK_ROWS, :] = xb[:, 0:HEAD_DIM]
                q_s[2 * cg + 1, r0:r0 + CHUNK_ROWS, :] = xb[:, HEAD_DIM:LANES]
            else:
                xb = xg.astype(BF16)
                for g in range(N_KV):
                    k_s[g, ctx_len + r0:ctx_len + r0 + CHUNK_ROWS, :] = (
                        xb[:, g * HEAD_DIM:(g + 1) * HEAD_DIM])
        w = min(key_blk, CHUNK_ROWS)
        for r1 in range(r0, r0 + CHUNK_ROWS, w):
            blk, off = (0, ctx_len + r1) if ctx_len else (r1 // key_blk, 0)
            vt = vv[r1 - r0:r1 - r0 + w, :].T
            put_values_t(blk, off, vt)
            if emit_kv:
                vo_ref[blk] = vt.reshape(N_KV, HEAD_DIM, seq_len)

        project_conv(r0, hb)

    def scores_t(qb, g):
        r0 = pl.multiple_of(qb * q_rows, q_rows)
        k0 = 0 if ctx_len else r0
        qs = jnp.concatenate(
            [q_s[KV_GROUP * g + j, pl.ds(r0, q_rows), :] for j in range(KV_GROUP)], axis=0)
        kk = k_s[g, pl.ds(k0, n_keys), :]
        return lax.dot_general(kk, qs, (((1,), (1,)), ((), ())),
                               preferred_element_type=F32)

    def values_out(qb, g, p):
        r0 = pl.multiple_of(qb * q_rows, q_rows)
        blk = 0 if ctx_len else qb
        ot = jnp.dot(vt_s[blk, g], p, preferred_element_type=F32)
        ot = ot[0:HEAD_DIM, :] / ot[HEAD_DIM:HEAD_DIM + 1, :]
        for jj in range(KV_GROUP // 2):
            pair_t = jnp.concatenate(
                [ot[:, (2 * jj) * q_rows:(2 * jj + 1) * q_rows],
                 ot[:, (2 * jj + 1) * q_rows:(2 * jj + 2) * q_rows]], axis=0)
            col = (KV_GROUP * g + 2 * jj) * HEAD_DIM
            attn_s[pl.ds(r0, q_rows), col:col + LANES] = pair_t.T

    def pipeline_pairs(step):
        assert n_qb % 2 == 0
        step(-1, 1, first=True)
        step(0, 0)

        def body(t, carry):
            step(2 * t + 1, 1)
            step(2 * t + 2, 0)
            return carry

        lax.fori_loop(0, (n_qb - 2) // 2, body, 0)
        step(n_qb - 1, 1, last=True)

    def shifted_block(qb, carry):
        for g in range(N_KV):
            s = scores_t(qb, g)
            s_ref[...] = s
            m = jnp.max(s, axis=0, keepdims=True)
            for k1 in range(0, n_keys, P_ROWS):
                p_ref[0, g, k1:k1 + P_ROWS, :] = jnp.exp2(
                    s_ref[k1:k1 + P_ROWS, :] - m).astype(BF16)
            values_out(qb, g, p_ref[0, g])
        return carry

    def unshifted_step(j, par, first=False, last=False):
        for g in range(N_KV):
            if not last:
                p_ref[1 - par, g] = jnp.exp2(scores_t(j + 1, g)).astype(BF16)
            if not first:
                values_out(j, g, p_ref[par, g])

    gains = jnp.abs(qkg_ref[...])
    q_bound = (Q_SCALE * Q_SCALE * HEAD_DIM) * jnp.max(gains[:, 0:ATTN_DIM]) ** 2
    k_bound = HEAD_DIM * jnp.max(gains[:, ATTN_DIM:QK_DIM]) ** 2
    if ctx_len:
        ck2 = ck_ref[...]
        k_bound = jnp.maximum(k_bound, jnp.max(jnp.sum(ck2 * ck2, axis=1)))
    small_scores = q_bound * k_bound <= MAX_UNSHIFTED_SCORE ** 2
    pl.when(small_scores)(lambda: pipeline_pairs(unshifted_step))

    @pl.when(jnp.logical_not(small_scores))
    def _():
        lax.fori_loop(0, n_qb, shifted_block, 0)

    for r0 in range(0, BLOCK_ROWS, CHUNK_ROWS):
        rows = slice(r0, r0 + CHUNK_ROWS)
        an = (_rms_rows(attn_s[rows, :]) * ga_ref[...]).astype(BF16)
        merged = jnp.concatenate([an, conv_rows(r0, CHUNK_ROWS)], axis=-1)
        mix = jnp.dot(merged, wout_ref[...], preferred_element_type=F32)
        out_ref[rows, :] = x_ref[rows, :] + gate * mix


def _mixer_call(x_blocks, mod, mod_row, ctx_k, ctx_v, rope, consts, *, seq_len, emit_kv):
    n_blocks = x_blocks.shape[0]
    ctx_len = 0 if ctx_k is None else ctx_k.shape[4]
    use_rope = rope is not None
    if ctx_len:
        assert seq_len == BLOCK_ROWS
        q_rows, n_keys, n_key_blocks = LATENT_Q_ROWS, ctx_len + BLOCK_ROWS, 1
    else:
        q_rows, n_keys, n_key_blocks = seq_len, seq_len, BLOCK_ROWS // seq_len
    n_q_cols = KV_GROUP * q_rows

    blk = lambda cols: pl.BlockSpec((None, BLOCK_ROWS, cols), lambda b: (b, 0, 0))
    assert mod_row is not None or n_blocks <= MOD_ROWS
    args = [x_blocks, mod]
    in_specs = [blk(D_MODEL), _mod_spec(mod_row, 0)]
    if ctx_len:
        args += [ctx_k, ctx_v]
        in_specs += [pl.BlockSpec((None, None, N_KV, HEAD_DIM, ctx_len),
                                  lambda b: (b, 0, 0, 0, 0))] * 2
    if use_rope:
        args += list(rope)
        in_specs += [_const_spec((BLOCK_ROWS, LANES))] * 2
    args += list(consts)
    in_specs += [_const_spec(a.shape) for a in consts]

    out_shape = [jax.ShapeDtypeStruct((n_blocks, BLOCK_ROWS, D_MODEL), F32)]
    out_specs = [blk(D_MODEL)]
    if emit_kv:
        per_block = BLOCK_ROWS // seq_len
        out_shape += [jax.ShapeDtypeStruct(
            (n_blocks * per_block, 1, N_KV, HEAD_DIM, seq_len), F32)] * 2
        out_specs += [pl.BlockSpec((per_block, None, N_KV, HEAD_DIM, seq_len),
                                   lambda b: (b, 0, 0, 0, 0))] * 2

    scratch = [
        pltpu.VMEM((N_HEADS, BLOCK_ROWS, HEAD_DIM), BF16),
        pltpu.VMEM((N_KV, ctx_len + BLOCK_ROWS, HEAD_DIM), BF16),
        pltpu.VMEM((n_key_blocks, N_KV, V_ROWS, n_keys), BF16),
        pltpu.VMEM((BLOCK_ROWS, ATTN_DIM), F32),
        pltpu.VMEM((BLOCK_ROWS + 16, CONV_DIM), F32),
        pltpu.VMEM((BLOCK_ROWS, CONV_DIM), F32),
        pltpu.VMEM((n_keys, n_q_cols), F32),
        pltpu.VMEM((2, N_KV, n_keys, n_q_cols), BF16),
    ]
    kern = functools.partial(_mixer_kernel, seq_len=seq_len, ctx_len=ctx_len,
                             use_rope=use_rope, emit_kv=emit_kv, mod_row=mod_row)
    return pl.pallas_call(
        kern,
        grid=(n_blocks,),
        in_specs=in_specs,
        out_specs=out_specs,
        out_shape=out_shape,
        scratch_shapes=scratch,
        compiler_params=pltpu.CompilerParams(
            dimension_semantics=("arbitrary",),
            allow_input_fusion=[a.dtype == BF16 and a.size >= D_MODEL * D_MODEL for a in args],
            vmem_limit_bytes=_vmem_limit(
                [(s, a.dtype) for s, a in zip(in_specs, args)] + [(s, F32) for s in out_specs],
                scratch)),
        name="mixer_ctx" if emit_kv else "mixer_latent",
    )(*args)


def _ffn_rows(x_ref, mod, nffn_ref, wgu_ref, wd_ref, out_ref):
    shift = mod[:, 0:D_MODEL]
    scale1 = 1.0 + mod[:, D_MODEL:2 * D_MODEL]
    gate = mod[:, 2 * D_MODEL:3 * D_MODEL]
    x = x_ref[...]
    hb = ((_rms_rows(x) * nffn_ref[...]) * scale1 + shift).astype(BF16)
    acc = None
    for c0, cw in FF_CHUNKS:
        gt = jnp.dot(hb, wgu_ref[:, c0:c0 + cw], preferred_element_type=F32)
        up = jnp.dot(hb, wgu_ref[:, D_FF + c0:D_FF + c0 + cw], preferred_element_type=F32)
        act = ((gt * jax.nn.sigmoid(gt)) * up).astype(BF16)
        part = jnp.dot(act, wd_ref[c0:c0 + cw, :], preferred_element_type=F32)
        acc = part if acc is None else acc + part
    out_ref[...] = x + gate * acc


def _cast_rows(src_hbm, dst, stage, sem, chunk_rows):
    n = src_hbm.shape[0] // chunk_rows

    def fetch(c):
        return pltpu.make_async_copy(src_hbm.at[pl.ds(c * chunk_rows, chunk_rows), :],
                                     stage.at[c % 2], sem.at[c % 2])

    fetch(0).start()
    for c in range(n):
        if c + 1 < n:
            fetch(c + 1).start()
        fetch(c).wait()
        dst[c * chunk_rows:(c + 1) * chunk_rows, :] = stage[c % 2].astype(BF16)


def _ffn_kernel(xa_ref, xb_ref, mod_ref, nffn_ref, wgu_hbm, wd_hbm, ya_ref, yb_ref,
                wgu_s, wd_s, stage_gu, stage_d, sem, *, n_a, mod_row_a, tiles_per_mod_row_b):
    i = pl.program_id(0)

    @pl.when(i == 0)
    def _():
        _cast_rows(wgu_hbm, wgu_s, stage_gu, sem, GU_CAST_ROWS)
        _cast_rows(wd_hbm, wd_s, stage_d, sem, DOWN_CAST_ROWS)

    @pl.when(i < n_a)
    def _():
        mod = mod_ref[mod_row_a % MOD_ROWS:mod_row_a % MOD_ROWS + 1, :]
        _ffn_rows(xa_ref, mod, nffn_ref, wgu_s, wd_s, ya_ref)

    @pl.when(i >= n_a)
    def _():
        mod = mod_ref[pl.ds((i - n_a) // tiles_per_mod_row_b, 1), :]
        _ffn_rows(xb_ref, mod, nffn_ref, wgu_s, wd_s, yb_ref)


def _ffn_call(xa_rows, xb_rows, mod, mod_row_a, rows_per_mod_row_b, norm_ffn, w_gate_up, w_down):
    n_a, n_b = xa_rows.shape[0] // FFN_ROWS, xb_rows.shape[0] // FFN_ROWS
    tiles_per_row_b = rows_per_mod_row_b // FFN_ROWS
    assert n_b // tiles_per_row_b <= MOD_ROWS
    spec_a = pl.BlockSpec((FFN_ROWS, D_MODEL), lambda i: (jnp.minimum(i, n_a - 1), 0))
    spec_b = pl.BlockSpec((FFN_ROWS, D_MODEL), lambda i: (jnp.maximum(i - n_a, 0), 0))
    mod_spec = pl.BlockSpec((MOD_ROWS, 3 * D_MODEL),
                            lambda i: (jnp.where(i < n_a, mod_row_a // MOD_ROWS, 0), 1))
    hbm = pl.BlockSpec(memory_space=pl.ANY)
    in_specs = [spec_a, spec_b, mod_spec, _const_spec(norm_ffn.shape), hbm, hbm]
    scratch = [pltpu.VMEM(w_gate_up.shape, BF16), pltpu.VMEM(w_down.shape, BF16),
               pltpu.VMEM((2, GU_CAST_ROWS, w_gate_up.shape[1]), F32),
               pltpu.VMEM((2, DOWN_CAST_ROWS, w_down.shape[1]), F32),
               pltpu.SemaphoreType.DMA((2,))]
    return pl.pallas_call(
        functools.partial(_ffn_kernel, n_a=n_a, mod_row_a=mod_row_a,
                          tiles_per_mod_row_b=tiles_per_row_b),
        grid=(n_a + n_b,),
        in_specs=in_specs,
        out_specs=[spec_a, spec_b],
        out_shape=[jax.ShapeDtypeStruct(xa_rows.shape, F32),
                   jax.ShapeDtypeStruct(xb_rows.shape, F32)],
        scratch_shapes=scratch,
        compiler_params=pltpu.CompilerParams(
            dimension_semantics=("arbitrary",),
            vmem_limit_bytes=_vmem_limit(
                [(s, F32) for s in in_specs + [spec_a, spec_b]], scratch)),
        name="ffn",
    )(xa_rows, xb_rows, mod, norm_ffn, w_gate_up, w_down)


def _rope_tables(n_tokens):
    rows = n_tokens // GRID_W
    row = jnp.repeat(jnp.arange(rows, dtype=F32), GRID_W)
    col = jnp.tile(jnp.arange(GRID_W, dtype=F32), rows)
    inv = 1.0 / (ROPE_THETA ** (jnp.arange(ROT_PAIRS, dtype=F32) / ROT_PAIRS))
    ang = jnp.stack([row[:, None] * inv, col[:, None] * inv], axis=1)
    cos, sin = jnp.cos(ang), jnp.sin(ang)
    cos_h = jnp.concatenate([cos, cos], axis=-1).reshape(n_tokens, HEAD_DIM)
    sin_h = jnp.concatenate([-sin, sin], axis=-1).reshape(n_tokens, HEAD_DIM)
    reps = LANES // HEAD_DIM
    return jnp.tile(cos_h, (1, reps)), jnp.tile(sin_h, (1, reps))


def _keys_minor(kv):
    return jnp.transpose(kv, (0, 1, 3, 4, 2))


def _group_mean_matrix():
    idx = np.arange(MXU_DIM) // HEAD_DIM
    g = (idx[:, None] == idx[None, :]).astype(np.float32) / HEAD_DIM
    return jnp.asarray(g, dtype=BF16)


def kernel(x_prompt, x_sample, c, cache_k, cache_v, c_ctx, norm_mix, norm_ffn, w_ada, b_ada,
           w_in, q_norm, k_norm, conv_w, attn_out_norm, conv_out_norm, w_out, w_gate_up, w_down):
    depth = w_in.shape[0]
    assert depth == 1
    n_prompt, seq, _ = x_prompt.shape
    n_sample, dec_seq, _ = x_sample.shape
    assert dec_seq == BLOCK_ROWS and BLOCK_ROWS % seq == 0 and n_sample <= CTX_ROW

    cond = jnp.zeros((COND_ROWS, D_MODEL), F32)
    cond = cond.at[0:n_sample].set(c).at[CTX_ROW].set(c_ctx)
    mod = _ada_call(cond, w_ada[0], b_ada[0][None, :])

    consts = (
        norm_mix[0][None, :],
        w_in[0].astype(BF16),
        _group_mean_matrix(),
        jnp.concatenate([jnp.tile(q_norm[0], N_HEADS), jnp.tile(k_norm[0], N_KV)])[None, :],
        conv_w[0],
        attn_out_norm[0][None, :],
        conv_out_norm[0][None, :],
        w_out[0].astype(BF16),
    )
    nffn = norm_ffn[0][None, :]

    (xs1,) = _mixer_call(x_sample, mod, None, _keys_minor(cache_k), _keys_minor(cache_v),
                         _rope_tables(dec_seq), consts, seq_len=dec_seq, emit_kv=False)

    per_block = BLOCK_ROWS // seq
    xp_blocks = x_prompt.reshape(n_prompt // per_block, BLOCK_ROWS, D_MODEL)
    xp1, k_new, v_new = _mixer_call(xp_blocks, mod, CTX_ROW, None, None, None, consts,
                                    seq_len=seq, emit_kv=True)

    yp, ys = _ffn_call(xp1.reshape(-1, D_MODEL), xs1.reshape(-1, D_MODEL), mod, CTX_ROW, dec_seq,
                       nffn, w_gate_up[0], w_down[0])

    return (yp.reshape(n_prompt, seq, D_MODEL),
            ys.reshape(n_sample, dec_seq, D_MODEL),
            jnp.transpose(k_new, (0, 1, 4, 2, 3)),
            jnp.transpose(v_new, (0, 1, 4, 2, 3)))
```
